```python
import jax
import jax.numpy as jnp
from jax import lax
import numpy as np

D_MODEL = 1024
BATCH = 8
SEQ = 2048
DEPTH = 2
DEC_BATCH = 32
DEC_SEQ = 4
PAST_LEN = 16384
PAGE_SIZE = 128

N_EVEN = (DEPTH + 1) // 2
N_ODD = DEPTH // 2
POOL_DIM = D_MODEL // 2
POOL_WINDOWS = (2, 4, 8, 16)
POOL_GROUP = POOL_DIM // len(POOL_WINDOWS)
POOL_HIST = max(POOL_WINDOWS) - 1
RET_HEADS = 4
RET_DK = 128
RET_DV = 128
RET_CHUNK = 128
EVEN_IN = POOL_DIM + 2 * RET_HEADS * RET_DK + 2 * RET_HEADS * RET_DV
EVEN_OUT = POOL_DIM + RET_HEADS * RET_DV
MLA_HEADS = 8
QK_NOPE = 128
QK_ROPE = 64
V_DIM = 128
Q_LORA = 512
KV_LORA = 256
MLA_SCALE = (QK_NOPE + QK_ROPE) ** -0.5
ATTN_Q_BLOCK = 128
D_FF = 2816
CONV_W = 3
ALPHA = (2.0 * DEPTH) ** 0.25
BETA = (8.0 * DEPTH) ** -0.25
ROPE_THETA = 10000.0
LN_EPS = 1e-5
RMS_EPS = 1e-6
GN_EPS = 1e-6

kernel_name = 'hybrid_pool_retention_mla_convffn_step'

F32 = jnp.float32


def layer_norm(x, g, b):
    xf = x.astype(F32)
    mu = jnp.mean(xf, axis=-1, keepdims=True)
    var = jnp.mean(jnp.square(xf - mu), axis=-1, keepdims=True)
    return ((xf - mu) * lax.rsqrt(var + LN_EPS) * g.astype(F32) + b.astype(F32)).astype(x.dtype)


def rms_norm(x, g):
    xf = x.astype(F32)
    ms = jnp.mean(jnp.square(xf), axis=-1, keepdims=True)
    return (xf * lax.rsqrt(ms + RMS_EPS) * g.astype(F32)).astype(x.dtype)


def rope(x, pos):
    half = x.shape[-1] // 2
    inv_freq = ROPE_THETA ** (-jnp.arange(half, dtype=F32) / half)
    ang = pos[:, None] * inv_freq[None, :]
    shape = (pos.shape[0],) + (1,) * (x.ndim - 3) + (half,)
    cos = jnp.cos(ang).reshape(shape)
    sin = jnp.sin(ang).reshape(shape)
    xf = x.astype(F32)
    x1, x2 = xf[..., :half], xf[..., half:]
    return jnp.concatenate([x1 * cos - x2 * sin, x2 * cos + x1 * sin], axis=-1).astype(x.dtype)


def pool_mixer(u, hist, pos, pool_w, pool_scale):
    B, L, _ = u.shape
    u_ext = jnp.concatenate([hist.astype(u.dtype), u], axis=1)
    cs = jnp.cumsum(u_ext.astype(F32), axis=1)
    cs = jnp.concatenate([jnp.zeros_like(cs[:, :1]), cs], axis=1)
    end = cs[:, POOL_HIST + 1:]
    outs = []
    for gi, w in enumerate(POOL_WINDOWS):
        sl = slice(gi * POOL_GROUP, (gi + 1) * POOL_GROUP)
        start = cs[:, POOL_HIST + 1 - w:POOL_HIST + 1 - w + L, sl]
        cnt = jnp.minimum(float(w), pos + 1.0)[None, :, None]
        outs.append((end[..., sl] - start) / cnt - u[..., sl].astype(F32))
    pooled = jnp.stack(outs, axis=2).astype(u.dtype)
    mixed = jnp.einsum('blgc,gcd->blgd', pooled, pool_w).reshape(B, L, POOL_DIM)
    return mixed * pool_scale, u_ext[:, -POOL_HIST:]


def retention_chunk(q, k, v, S, log_gamma):
    C = q.shape[1]
    idx = jnp.arange(C, dtype=F32)
    diff = idx[:, None] - idx[None, :]
    decay = jnp.where(diff >= 0, jnp.exp(jnp.maximum(diff, 0.0)[None] * log_gamma[:, None, None]), 0.0).astype(q.dtype)
    scores = jnp.einsum('blhk,bmhk->bhlm', q, k) * decay
    o = jnp.einsum('bhlm,bmhv->blhv', scores, v)
    q_dec = jnp.exp((idx + 1.0)[:, None] * log_gamma[None, :]).astype(q.dtype)
    o = o + jnp.einsum('blhk,bhkv->blhv', q * q_dec[None, :, :, None], S)
    k_dec = jnp.exp((C - 1.0 - idx)[:, None] * log_gamma[None, :]).astype(q.dtype)
    S_new = jnp.exp(C * log_gamma)[None, :, None, None].astype(S.dtype) * S + jnp.einsum('blhk,blhv->bhkv', k * k_dec[None, :, :, None], v)
    return o, S_new.astype(S.dtype)


def retention(q, k, v, S0, log_gamma):
    B, L = q.shape[:2]
    C = RET_CHUNK if L % RET_CHUNK == 0 else L
    nc = L // C

    def to_chunks(t):
        return jnp.moveaxis(t.reshape((B, nc, C) + t.shape[2:]), 1, 0)

    def step(S, blk):
        qc, kc, vc = blk
        o, S = retention_chunk(qc, kc, vc, S, log_gamma)
        return S, o

    S, o = lax.scan(step, S0, (to_chunks(q), to_chunks(k), to_chunks(v)))
    o = jnp.moveaxis(o, 0, 1).reshape(B, L, RET_HEADS, RET_DV)
    return o, S


def even_mixer(x, pos, pool_hist, ret_state, w_in, pool_w, pool_scale, gn_g, w_o):
    B, L, _ = x.shape
    h = x @ w_in
    qk = RET_HEADS * RET_DK
    vd = RET_HEADS * RET_DV
    o0 = POOL_DIM
    u = h[..., :o0]
    q = h[..., o0:o0 + qk].reshape(B, L, RET_HEADS, RET_DK)
    k = h[..., o0 + qk:o0 + 2 * qk].reshape(B, L, RET_HEADS, RET_DK)
    v = h[..., o0 + 2 * qk:o0 + 2 * qk + vd].reshape(B, L, RET_HEADS, RET_DV)
    g = h[..., o0 + 2 * qk + vd:]
    pool_out, pool_hist_new = pool_mixer(u, pool_hist, pos, pool_w, pool_scale)
    q = rope(q, pos)
    k = rope(k, pos) * (RET_DK ** -0.5)
    log_gamma = jnp.log(1.0 - 2.0 ** (-5.0 - jnp.arange(RET_HEADS, dtype=F32)))
    o, S = retention(q, k, v, ret_state.astype(x.dtype), log_gamma)
    of = o.astype(F32)
    mu = jnp.mean(of, axis=-1, keepdims=True)
    var = jnp.mean(jnp.square(of - mu), axis=-1, keepdims=True)
    on = (of - mu) * lax.rsqrt(var + GN_EPS) * gn_g.astype(F32).reshape(RET_HEADS, RET_DV)
    ret_out = (jax.nn.silu(g.astype(F32)) * on.reshape(B, L, vd)).astype(x.dtype)
    y = jnp.concatenate([pool_out.astype(x.dtype), ret_out], axis=-1) @ w_o
    return y, pool_hist_new, S


def mla_project(x, pos, w_dq, q_norm_g, w_uq, w_dkv, kv_norm_g):
    B, L, _ = x.shape
    q = (rms_norm(x @ w_dq, q_norm_g) @ w_uq).reshape(B, L, MLA_HEADS, QK_NOPE + QK_ROPE)
    q_nope = q[..., :QK_NOPE]
    q_pe = rope(q[..., QK_NOPE:], pos)
    kv = x @ w_dkv
    c_kv = rms_norm(kv[..., :KV_LORA], kv_norm_g)
    k_pe = rope(kv[..., KV_LORA:], pos)
    return q_nope, q_pe, c_kv, k_pe


def mla_prompt(x, pos, w_dq, q_norm_g, w_uq, w_dkv, kv_norm_g, w_uk, w_uv, w_o):
    B, L, _ = x.shape
    q_nope, q_pe, c_kv, k_pe = mla_project(x, pos, w_dq, q_norm_g, w_uq, w_dkv, kv_norm_g)
    k_nope = jnp.einsum('bsc,chd->bshd', c_kv, w_uk)
    v = jnp.einsum('bsc,chd->bshd', c_kv, w_uv)
    QB = ATTN_Q_BLOCK if L % ATTN_Q_BLOCK == 0 else L
    nb = L // QB
    key_pos = jnp.arange(L)

    def to_blocks(t):
        return jnp.moveaxis(t.reshape((B, nb, QB) + t.shape[2:]), 1, 0)

    def block(args):
        qn, qp, i = args
        s = jnp.einsum('bqhd,bkhd->bhqk', qn, k_nope) + jnp.einsum('bqhr,bkr->bhqk', qp, k_pe)
        s = s.astype(F32) * MLA_SCALE
        q_pos = i * QB + jnp.arange(QB)
        s = jnp.where(key_pos[None, :] <= q_pos[:, None], s, -jnp.inf)
        p = jax.nn.softmax(s, axis=-1).astype(v.dtype)
        return jnp.einsum('bhqk,bkhd->bqhd', p, v)

    o = lax.map(block, (to_blocks(q_nope), to_blocks(q_pe), jnp.arange(nb)))
    o = jnp.moveaxis(o, 0, 1).reshape(B, L, MLA_HEADS * V_DIM)
    return o @ w_o, c_kv, k_pe


def mla_sample(x, pos, cache_ckv, cache_kpe, layer_idx, page_table, w_dq, q_norm_g, w_uq, w_dkv, kv_norm_g, w_uk, w_uv, w_o):
    B, L, _ = x.shape
    q_nope, q_pe, c_kv, k_pe = mla_project(x, pos, w_dq, q_norm_g, w_uq, w_dkv, kv_norm_g)
    n_pages = page_table.shape[1]
    P = n_pages * PAGE_SIZE
    ckv_past = cache_ckv[layer_idx, page_table].reshape(B, P, KV_LORA).astype(x.dtype)
    kpe_past = cache_kpe[layer_idx, page_table].reshape(B, P, QK_ROPE).astype(x.dtype)
    q_lat = jnp.einsum('bqhd,chd->bqhc', q_nope, w_uk)
    s_past = jnp.einsum('bqhc,bkc->bhqk', q_lat, ckv_past) + jnp.einsum('bqhr,bkr->bhqk', q_pe, kpe_past)
    s_new = jnp.einsum('bqhc,bkc->bhqk', q_lat, c_kv) + jnp.einsum('bqhr,bkr->bhqk', q_pe, k_pe)
    s = jnp.concatenate([s_past, s_new], axis=-1).astype(F32) * MLA_SCALE
    causal = jnp.arange(L)[None, :] <= jnp.arange(L)[:, None]
    mask = jnp.concatenate([jnp.ones((L, P), dtype=bool), causal], axis=-1)
    s = jnp.where(mask, s, -jnp.inf)
    p = jax.nn.softmax(s, axis=-1).astype(x.dtype)
    o_lat = jnp.einsum('bhqk,bkc->bqhc', p[..., :P], ckv_past) + jnp.einsum('bhqk,bkc->bqhc', p[..., P:], c_kv)
    o = jnp.einsum('bqhc,chd->bqhd', o_lat, w_uv).reshape(B, L, MLA_HEADS * V_DIM)
    return o @ w_o, c_kv, k_pe


def conv_ffn(x, hist, w_up, conv_w, conv_b, w_down):
    L = x.shape[1]
    h = x @ w_up
    a, b = h[..., :D_FF], h[..., D_FF:]
    a_ext = jnp.concatenate([hist.astype(a.dtype), a], axis=1)
    conv = conv_b
    for j in range(CONV_W):
        conv = conv + conv_w[j] * a_ext[:, j:j + L]
    y = (jax.nn.silu(conv) * b) @ w_down
    return y, a_ext[:, L:]


def setup_inputs(seed: int = 0) -> dict:
    key = jax.random.key(seed)
    ks = jax.random.split(key, 40)

    def nrm(i, shape, scale):
        return jax.random.normal(ks[i], shape, F32) * scale

    n_pages = PAST_LEN // PAGE_SIZE
    n_pool = (DEC_BATCH * n_pages * 5) // 4
    page_table = jax.random.permutation(ks[0], n_pool)[:DEC_BATCH * n_pages].reshape(DEC_BATCH, n_pages).astype(jnp.int32)
    return {
        'x_prompt': nrm(1, (BATCH, SEQ, D_MODEL), 1.0),
        'x_sample': nrm(2, (DEC_BATCH, DEC_SEQ, D_MODEL), 1.0),
        'state_pool': nrm(3, (N_EVEN, DEC_BATCH, POOL_HIST, POOL_DIM), 1.0),
        'state_ret': nrm(4, (N_EVEN, DEC_BATCH, RET_HEADS, RET_DK, RET_DV), 0.3),
        'cache_ckv': nrm(5, (N_ODD, n_pool, PAGE_SIZE, KV_LORA), 1.0),
        'cache_kpe': nrm(6, (N_ODD, n_pool, PAGE_SIZE, QK_ROPE), 1.0),
        'state_conv': nrm(7, (DEPTH, DEC_BATCH, CONV_W - 1, D_FF), 1.0),
        'page_table': page_table,
        'w_in_even': nrm(8, (N_EVEN, D_MODEL, EVEN_IN), D_MODEL ** -0.5),
        'pool_w': nrm(9, (N_EVEN, len(POOL_WINDOWS), POOL_GROUP, POOL_GROUP), POOL_GROUP ** -0.5),
        'pool_scale': 1.0 + nrm(10, (N_EVEN, POOL_DIM), 0.1),
        'ret_gn_g': 1.0 + nrm(11, (N_EVEN, RET_HEADS * RET_DV), 0.1),
        'w_o_even': nrm(12, (N_EVEN, EVEN_OUT, D_MODEL), EVEN_OUT ** -0.5 * BETA),
        'w_dq': nrm(13, (N_ODD, D_MODEL, Q_LORA), D_MODEL ** -0.5),
        'q_norm_g': 1.0 + nrm(14, (N_ODD, Q_LORA), 0.1),
        'w_uq': nrm(15, (N_ODD, Q_LORA, MLA_HEADS * (QK_NOPE + QK_ROPE)), Q_LORA ** -0.5),
        'w_dkv': nrm(16, (N_ODD, D_MODEL, KV_LORA + QK_ROPE), D_MODEL ** -0.5),
        'kv_norm_g': 1.0 + nrm(17, (N_ODD, KV_LORA), 0.1),
        'w_uk': nrm(18, (N_ODD, KV_LORA, MLA_HEADS, QK_NOPE), KV_LORA ** -0.5),
        'w_uv': nrm(19, (N_ODD, KV_LORA, MLA_HEADS, V_DIM), KV_LORA ** -0.5),
        'w_o_mla': nrm(20, (N_ODD, MLA_HEADS * V_DIM, D_MODEL), (MLA_HEADS * V_DIM) ** -0.5 * BETA),
        'w_up': nrm(21, (DEPTH, D_MODEL, 2 * D_FF), D_MODEL ** -0.5),
        'conv_w': nrm(22, (DEPTH, CONV_W, D_FF), 0.5),
        'conv_b': nrm(23, (DEPTH, D_FF), 0.02),
        'w_down': nrm(24, (DEPTH, D_FF, D_MODEL), D_FF ** -0.5 * BETA),
        'ln_mix_g': 1.0 + nrm(25, (DEPTH, D_MODEL), 0.05),
        'ln_mix_b': nrm(26, (DEPTH, D_MODEL), 0.02),
        'ln_ffn_g': 1.0 + nrm(27, (DEPTH, D_MODEL), 0.05),
        'ln_ffn_b': nrm(28, (DEPTH, D_MODEL), 0.02),
    }


def reference(x_prompt, x_sample, state_pool, state_ret, cache_ckv, cache_kpe, state_conv, page_table,
              w_in_even, pool_w, pool_scale, ret_gn_g, w_o_even,
              w_dq, q_norm_g, w_uq, w_dkv, kv_norm_g, w_uk, w_uv, w_o_mla,
              w_up, conv_w, conv_b, w_down, ln_mix_g, ln_mix_b, ln_ffn_g, ln_ffn_b):
    xp, xs = x_prompt, x_sample
    Bp, Lp, _ = xp.shape
    past_len = page_table.shape[1] * PAGE_SIZE
    pos_p = jnp.arange(Lp, dtype=F32)
    pos_s = past_len + jnp.arange(xs.shape[1], dtype=F32)
    pool_p, pool_s, ret_p, ret_s = [], [], [], []
    ckv_p, ckv_s, kpe_p, kpe_s = [], [], [], []
    conv_p, conv_s = [], []
    for layer in range(DEPTH):
        if layer % 2 == 0:
            e = layer // 2
            hist0 = jnp.zeros((Bp, POOL_HIST, POOL_DIM), xp.dtype)
            S0 = jnp.zeros((Bp, RET_HEADS, RET_DK, RET_DV), xp.dtype)
            mp, hp, sp = even_mixer(xp, pos_p, hist0, S0, w_in_even[e], pool_w[e], pool_scale[e], ret_gn_g[e], w_o_even[e])
            ms, hs, ss = even_mixer(xs, pos_s, state_pool[e], state_ret[e], w_in_even[e], pool_w[e], pool_scale[e], ret_gn_g[e], w_o_even[e])
            pool_p.append(hp)
            pool_s.append(hs)
            ret_p.append(sp)
            ret_s.append(ss)
        else:
            o = layer // 2
            mp, cp, kp = mla_prompt(xp, pos_p, w_dq[o], q_norm_g[o], w_uq[o], w_dkv[o], kv_norm_g[o], w_uk[o], w_uv[o], w_o_mla[o])
            ms, cs, ksmp = mla_sample(xs, pos_s, cache_ckv, cache_kpe, o, page_table, w_dq[o], q_norm_g[o], w_uq[o], w_dkv[o], kv_norm_g[o], w_uk[o], w_uv[o], w_o_mla[o])
            ckv_p.append(cp)
            ckv_s.append(cs)
            kpe_p.append(kp)
            kpe_s.append(ksmp)
        xp = layer_norm(ALPHA * xp + mp, ln_mix_g[layer], ln_mix_b[layer])
        xs = layer_norm(ALPHA * xs + ms, ln_mix_g[layer], ln_mix_b[layer])
        conv0 = jnp.zeros((Bp, CONV_W - 1, D_FF), xp.dtype)
        fp, hcp = conv_ffn(xp, conv0, w_up[layer], conv_w[layer], conv_b[layer], w_down[layer])
        fs, hcs = conv_ffn(xs, state_conv[layer], w_up[layer], conv_w[layer], conv_b[layer], w_down[layer])
        conv_p.append(hcp)
        conv_s.append(hcs)
        xp = layer_norm(ALPHA * xp + fp, ln_ffn_g[layer], ln_ffn_b[layer])
        xs = layer_norm(ALPHA * xs + fs, ln_ffn_g[layer], ln_ffn_b[layer])
    return (xp, xs,
            jnp.stack(pool_p), jnp.stack(pool_s),
            jnp.stack(ret_p), jnp.stack(ret_s),
            jnp.stack(ckv_p), jnp.stack(ckv_s),
            jnp.stack(kpe_p), jnp.stack(kpe_s),
            jnp.stack(conv_p), jnp.stack(conv_s))
```

```python
import functools

import jax
import jax.numpy as jnp
from jax import lax
from jax.experimental import pallas as pl
from jax.experimental.pallas import tpu as pltpu

F32 = jnp.float32
BF16 = jnp.bfloat16

PAGE_SIZE = 128
POOL_WINDOWS = (2, 4, 8, 16)
POOL_HIST = max(POOL_WINDOWS) - 1
RET_HEADS = 4
RET_DK = 128
RET_CHUNK = 128
MLA_HEADS = 8
QK_NOPE = 128
QK_ROPE = 64
V_DIM = 128
KV_LORA = 256
CONV_W = 3
DEPTH = 2
ALPHA = (2.0 * DEPTH) ** 0.25
ROPE_THETA = 10000.0
LN_EPS = 1e-5
RMS_EPS = 1e-6
GN_EPS = 1e-6
MLA_SCALE = (QK_NOPE + QK_ROPE) ** -0.5

LANES = 128
SUBLANES = 8
MIB = 1024 * 1024


def _cparams(sem, vmem_mib):
    return pltpu.CompilerParams(dimension_semantics=sem, vmem_limit_bytes=int(vmem_mib * MIB))


def _resident(shape):
    nd = len(shape)
    return pl.BlockSpec(shape, lambda *_: (0,) * nd, pipeline_mode=pl.Buffered(1))


def _whole(shape):
    nd = len(shape)
    return pl.BlockSpec(shape, lambda *_: (0,) * nd)


def _dot(a, b):
    return jnp.dot(a, b, preferred_element_type=F32)


def _dot_nt(a, b):
    return lax.dot_general(a, b, (((1,), (1,)), ((), ())), preferred_element_type=F32)


def _dot_tn(a, b):
    return lax.dot_general(a, b, (((0,), (0,)), ((), ())), preferred_element_type=F32)


def _layer_norm(z, g, b):
    mu = jnp.mean(z, axis=-1, keepdims=True)
    d = z - mu
    var = jnp.mean(d * d, axis=-1, keepdims=True)
    return d * lax.rsqrt(var + LN_EPS) * g + b


def _silu(x):
    return x * jax.nn.sigmoid(x)


def _mm_body(x_ref, w_ref, o_ref):
    o_ref[...] = _dot(x_ref[...].astype(BF16), w_ref[...]).astype(o_ref.dtype)


def _matmul(x, w, tm, out_dtype=F32):
    m, k = x.shape
    n = w.shape[1]
    return pl.pallas_call(
        _mm_body,
        grid=(m // tm,),
        in_specs=[pl.BlockSpec((tm, k), lambda i: (i, 0)), _resident((k, n))],
        out_specs=pl.BlockSpec((tm, n), lambda i: (i, 0)),
        out_shape=jax.ShapeDtypeStruct((m, n), out_dtype),
        compiler_params=_cparams(("parallel",), 40),
        name="matmul",
    )(x, w)


def _mm_ln_body(a_ref, w_ref, r_ref, g_ref, b_ref, o_ref):
    y = _dot(a_ref[...].astype(BF16), w_ref[...])
    o_ref[...] = _layer_norm(ALPHA * r_ref[...] + y, g_ref[...], b_ref[...])


def _matmul_res_ln(a, w, res, g, b, tm):
    m, k = a.shape
    n = w.shape[1]
    return pl.pallas_call(
        _mm_ln_body,
        grid=(m // tm,),
        in_specs=[pl.BlockSpec((tm, k), lambda i: (i, 0)), _resident((k, n)),
                  pl.BlockSpec((tm, n), lambda i: (i, 0)), _resident((1, n)), _resident((1, n))],
        out_specs=pl.BlockSpec((tm, n), lambda i: (i, 0)),
        out_shape=jax.ShapeDtypeStruct((m, n), F32),
        compiler_params=_cparams(("parallel",), 32),
        name="matmul_res_ln",
    )(a, w, res, g.reshape(1, n), b.reshape(1, n))


def _rope_full(x, c2, s2):
    return x * c2 + pltpu.roll(x, 64, axis=1) * s2


def _group_norm_gate(o, gate, gn_row):
    mu = jnp.mean(o, axis=-1, keepdims=True)
    d = o - mu
    var = jnp.mean(d * d, axis=-1, keepdims=True)
    return _silu(gate) * (d * lax.rsqrt(var + GN_EPS) * gn_row)


def _even_mix_prompt_body(h_ref, c2_ref, s2_ref, dec_ref, qd_ref, kd_ref, gc_ref, pw_ref, ps_ref,
                          gn_ref, mix_ref, pst_ref, rst_ref, ext_ref, s_ref, *, tm, n_j):
    j = pl.program_id(1)
    pd = len(POOL_WINDOWS) * LANES

    @pl.when(j == 0)
    def _():
        ext_ref[0:16, :] = jnp.zeros((16, pd), F32)
        s_ref[...] = jnp.zeros(s_ref.shape, F32)

    @pl.when(j > 0)
    def _():
        ext_ref[0:16, :] = ext_ref[tm:tm + 16, :]

    ext_ref[16:16 + tm, :] = h_ref[:, 0:pd]

    pos = (j * tm + lax.broadcasted_iota(jnp.int32, (tm, 1), 0)).astype(F32)
    for g, w in enumerate(POOL_WINDOWS):
        cols = slice(g * LANES, (g + 1) * LANES)
        e = ext_ref[:, cols]
        u = e[16:, :]
        s = 1
        while s < w:
            e = e + pltpu.roll(e, s, axis=0)
            s *= 2
        cnt = jnp.minimum(float(w), pos + 1.0)
        pooled = e[16:, :] / cnt - u
        mixed = _dot(pooled.astype(BF16), pw_ref[g]) * ps_ref[:, cols]
        mix_ref[:, cols] = mixed.astype(BF16)

    c = RET_CHUNK
    k_scale = RET_DK ** -0.5
    for ci in range(tm // c):
        rows = slice(ci * c, (ci + 1) * c)
        c2 = c2_ref[rows, :]
        s2 = s2_ref[rows, :]
        for hd in range(RET_HEADS):
            def col(part, hd=hd):
                return slice(pd + (part * RET_HEADS + hd) * LANES, pd + (part * RET_HEADS + hd + 1) * LANES)
            q = _rope_full(h_ref[rows, col(0)], c2, s2)
            k = _rope_full(h_ref[rows, col(1)], c2, s2) * k_scale
            vb = h_ref[rows, col(2)].astype(BF16)
            gate = h_ref[rows, col(3)]
            st = s_ref[hd]
            sc = _dot_nt(q.astype(BF16), k.astype(BF16)) * dec_ref[hd]
            o = _dot(sc.astype(BF16), vb)
            o = o + _dot((q * qd_ref[hd]).astype(BF16), st.astype(BF16))
            s_ref[hd] = gc_ref[hd] * st + _dot_tn((k * kd_ref[hd]).astype(BF16), vb)
            ret = _group_norm_gate(o, gate, gn_ref[:, hd * LANES:(hd + 1) * LANES])
            mix_ref[rows, pd + hd * LANES:pd + (hd + 1) * LANES] = ret.astype(BF16)

    @pl.when(j == n_j - 1)
    def _():
        pst_ref[0] = ext_ref[pl.ds(tm + 1, POOL_HIST), :]
        rst_ref[0] = s_ref[...]


def _even_mix_prompt(h, bsz, seq, tabs, pool_w, pool_scale, gn_g, tm):
    n_j = seq // tm
    pd = pool_scale.shape[0]
    ed = pd + RET_HEADS * LANES
    body = functools.partial(_even_mix_prompt_body, tm=tm, n_j=n_j)
    return pl.pallas_call(
        body,
        grid=(bsz, n_j),
        in_specs=[
            pl.BlockSpec((tm, h.shape[1]), lambda b, j: (b * n_j + j, 0)),
            pl.BlockSpec((tm, LANES), lambda b, j: (j, 0)),
            pl.BlockSpec((tm, LANES), lambda b, j: (j, 0)),
            _resident((RET_HEADS, RET_CHUNK, RET_CHUNK)),
            _resident((RET_HEADS, RET_CHUNK, LANES)),
            _resident((RET_HEADS, RET_CHUNK, LANES)),
            _resident((RET_HEADS, 1, LANES)),
            _resident(pool_w.shape),
            _resident((1, pd)),
            _resident((1, RET_HEADS * LANES)),
        ],
        out_specs=[
            pl.BlockSpec((tm, ed), lambda b, j: (b * n_j + j, 0)),
            pl.BlockSpec((1, POOL_HIST, pd), lambda b, j: (b, 0, 0)),
            pl.BlockSpec((1, RET_HEADS, RET_DK, LANES), lambda b, j: (b, 0, 0, 0)),
        ],
        out_shape=[
            jax.ShapeDtypeStruct((bsz * seq, ed), BF16),
            jax.ShapeDtypeStruct((bsz, POOL_HIST, pd), F32),
            jax.ShapeDtypeStruct((bsz, RET_HEADS, RET_DK, LANES), F32),
        ],
        scratch_shapes=[pltpu.VMEM((tm + 16, pd), F32), pltpu.VMEM((RET_HEADS, RET_DK, LANES), F32)],
        compiler_params=_cparams(("parallel", "arbitrary"), 32),
        name="even_mix_prompt",
    )(h, tabs["c2"], tabs["s2"], tabs["dec"], tabs["qd"], tabs["kd"], tabs["gc"],
      pool_w, pool_scale.reshape(1, pd), gn_g.reshape(1, -1))


def _even_mix_sample_body(h_ref, hist_ref, s0_ref, c2_ref, s2_ref, dm_ref, qd_ref, kd_ref, gc_ref,
                          pw_ref, ps_ref, gn_ref, mix_ref, hist_o_ref, s_o_ref, oc_ref,
                          *, nb, ls, cnts):
    pd = len(POOL_WINDOWS) * LANES
    ext = [hist_ref[i] for i in range(POOL_HIST)]
    ext += [h_ref[l * nb:(l + 1) * nb, 0:pd] for l in range(ls)]
    for i in range(POOL_HIST):
        hist_o_ref[i] = ext[ls + i]
    for g, w in enumerate(POOL_WINDOWS):
        cols = slice(g * LANES, (g + 1) * LANES)
        outs = []
        for l in range(ls):
            top = POOL_HIST + l
            acc = ext[top][:, cols]
            for jj in range(1, w):
                acc = acc + ext[top - jj][:, cols]
            outs.append(acc / cnts[g][l] - ext[top][:, cols])
        pooled = jnp.concatenate(outs, axis=0)
        mixed = _dot(pooled.astype(BF16), pw_ref[g]) * ps_ref[:, cols]
        mix_ref[:, cols] = mixed.astype(BF16)

    rows_b = lax.broadcasted_iota(jnp.int32, (ls * nb, 1), 0) % nb
    k_scale = RET_DK ** -0.5
    c2 = c2_ref[...]
    s2 = s2_ref[...]
    qs, ks, vs = [], [], []
    for hd in range(RET_HEADS):
        def col(part, hd=hd):
            return slice(pd + (part * RET_HEADS + hd) * LANES, pd + (part * RET_HEADS + hd + 1) * LANES)
        q = _rope_full(h_ref[:, col(0)], c2, s2)
        k = _rope_full(h_ref[:, col(1)], c2, s2) * k_scale
        vb = h_ref[:, col(2)].astype(BF16)
        sc = _dot_nt(q.astype(BF16), k.astype(BF16)) * dm_ref[hd]
        oc_ref[hd] = _dot(sc.astype(BF16), vb)
        qs.append(q * qd_ref[hd])
        ks.append(k * kd_ref[hd])
        vs.append(vb)

    def per_batch(b, carry):
        sel = rows_b == b
        for hd in range(RET_HEADS):
            st = s0_ref[b, hd]
            qm = jnp.where(sel, qs[hd], 0.0).astype(BF16)
            km = jnp.where(sel, ks[hd], 0.0).astype(BF16)
            oc_ref[hd] += _dot(qm, st.astype(BF16))
            s_o_ref[b, hd] = gc_ref[hd] * st + _dot_tn(km, vs[hd])
        return carry

    lax.fori_loop(0, nb, per_batch, 0)

    for hd in range(RET_HEADS):
        gate = h_ref[:, pd + (3 * RET_HEADS + hd) * LANES:pd + (3 * RET_HEADS + hd + 1) * LANES]
        ret = _group_norm_gate(oc_ref[hd], gate, gn_ref[:, hd * LANES:(hd + 1) * LANES])
        mix_ref[:, pd + hd * LANES:pd + (hd + 1) * LANES] = ret.astype(BF16)


def _even_mix_sample(h, hist, s0, tabs, pool_w, pool_scale, gn_g, nb, ls, past_len):
    pd = pool_scale.shape[0]
    ed = pd + RET_HEADS * LANES
    cnts = tuple(tuple(float(min(w, past_len + l + 1)) for l in range(ls)) for w in POOL_WINDOWS)
    body = functools.partial(_even_mix_sample_body, nb=nb, ls=ls, cnts=cnts)
    n = ls * nb
    args = (h, hist, s0, tabs["c2"], tabs["s2"], tabs["dm"], tabs["qd"], tabs["kd"], tabs["gc"],
            pool_w, pool_scale.reshape(1, pd), gn_g.reshape(1, -1))
    return pl.pallas_call(
        body,
        grid=(1,),
        in_specs=[_resident(a.shape) for a in args],
        out_specs=[_whole((n, ed)), _whole(hist.shape), _whole(s0.shape)],
        out_shape=[jax.ShapeDtypeStruct((n, ed), BF16),
                   jax.ShapeDtypeStruct(hist.shape, F32),
                   jax.ShapeDtypeStruct(s0.shape, F32)],
        scratch_shapes=[pltpu.VMEM((RET_HEADS, n, LANES), F32)],
        compiler_params=_cparams(("arbitrary",), 48),
        name="even_mix_sample",
    )(*args)


def _ffn_chunk(xb, wup_ref, wd_ref, cw_ref, cb_ref, c, tf, dff, shift_fn):
    cols = slice(c * tf, (c + 1) * tf)
    a = _dot(xb, wup_ref[:, cols])
    gate_in = _dot(xb, wup_ref[:, dff + c * tf:dff + (c + 1) * tf])
    a1, a2 = shift_fn(a, c)
    conv = cb_ref[:, cols] + cw_ref[0:1, cols] * a2
    conv = conv + cw_ref[1:2, cols] * a1
    conv = conv + cw_ref[2:3, cols] * a
    act = (_silu(conv) * gate_in).astype(BF16)
    return a, _dot(act, wd_ref[cols, :])


def _ffn_prompt_body(x_ref, wup_ref, wd_ref, cw_ref, cb_ref, g_ref, b_ref, o_ref, st_ref, carry_ref,
                     *, tm, n_j, tf, dff):
    j = pl.program_id(1)

    @pl.when(j == 0)
    def _():
        carry_ref[...] = jnp.zeros(carry_ref.shape, F32)

    x = x_ref[...]
    xb = x.astype(BF16)
    row = lax.broadcasted_iota(jnp.int32, (tm, tf), 0)

    def shift_fn(a, c):
        prev = carry_ref[:, c * tf:(c + 1) * tf]
        a1 = jnp.where(row == 0, prev[7:8, :], pltpu.roll(a, 1, axis=0))
        a2 = jnp.where(row == 0, prev[6:7, :], jnp.where(row == 1, prev[7:8, :], pltpu.roll(a, 2, axis=0)))
        return a1, a2

    acc = None
    for c in range(dff // tf):
        a, y = _ffn_chunk(xb, wup_ref, wd_ref, cw_ref, cb_ref, c, tf, dff, shift_fn)
        acc = y if acc is None else acc + y
        tail = a[tm - SUBLANES:tm, :]
        carry_ref[:, c * tf:(c + 1) * tf] = tail
        st_ref[0, :, c * tf:(c + 1) * tf] = tail
    o_ref[...] = _layer_norm(ALPHA * x + acc, g_ref[...], b_ref[...])


def _ffn_prompt(x, bsz, seq, wup, wd, conv_w, conv_b, g, b, tm, tf):
    d = x.shape[1]
    dff = wd.shape[0]
    n_j = seq // tm
    body = functools.partial(_ffn_prompt_body, tm=tm, n_j=n_j, tf=tf, dff=dff)
    return pl.pallas_call(
        body,
        grid=(bsz, n_j),
        in_specs=[pl.BlockSpec((tm, d), lambda bi, j: (bi * n_j + j, 0)),
                  _resident(wup.shape), _resident(wd.shape), _resident(conv_w.shape),
                  _resident((1, dff)), _resident((1, d)), _resident((1, d))],
        out_specs=[pl.BlockSpec((tm, d), lambda bi, j: (bi * n_j + j, 0)),
                   pl.BlockSpec((1, SUBLANES, dff), lambda bi, j: (bi, 0, 0))],
        out_shape=[jax.ShapeDtypeStruct(x.shape, F32),
                   jax.ShapeDtypeStruct((bsz, SUBLANES, dff), F32)],
        scratch_shapes=[pltpu.VMEM((SUBLANES, dff), F32)],
        compiler_params=_cparams(("parallel", "arbitrary"), 52),
        name="ffn_prompt",
    )(x, wup, wd, conv_w, conv_b.reshape(1, dff), g.reshape(1, d), b.reshape(1, d))


def _ffn_sample_body(x_ref, st_ref, wup_ref, wd_ref, cw_ref, cb_ref, g_ref, b_ref, o_ref, st_o_ref,
                     *, nb, ls, tf, dff):
    x = x_ref[...]
    xb = x.astype(BF16)
    nh = CONV_W - 1

    def shift_fn(a, c):
        cols = slice(c * tf, (c + 1) * tf)
        ext = [st_ref[i, :, cols] for i in range(nh)] + [a[l * nb:(l + 1) * nb, :] for l in range(ls)]
        a1 = jnp.concatenate([ext[nh + l - 1] for l in range(ls)], axis=0)
        a2 = jnp.concatenate([ext[nh + l - 2] for l in range(ls)], axis=0)
        for i in range(nh):
            st_o_ref[i, :, cols] = ext[ls + i]
        return a1, a2

    acc = None
    for c in range(dff // tf):
        _, y = _ffn_chunk(xb, wup_ref, wd_ref, cw_ref, cb_ref, c, tf, dff, shift_fn)
        acc = y if acc is None else acc + y
    o_ref[...] = _layer_norm(ALPHA * x + acc, g_ref[...], b_ref[...])


def _ffn_sample(x, st, wup, wd, conv_w, conv_b, g, b, nb, ls, tf):
    d = x.shape[1]
    dff = wd.shape[0]
    body = functools.partial(_ffn_sample_body, nb=nb, ls=ls, tf=tf, dff=dff)
    args = (x, st, wup, wd, conv_w, conv_b.reshape(1, dff), g.reshape(1, d), b.reshape(1, d))
    return pl.pallas_call(
        body,
        grid=(1,),
        in_specs=[_resident(a.shape) for a in args],
        out_specs=[_whole(x.shape), _whole(st.shape)],
        out_shape=[jax.ShapeDtypeStruct(x.shape, F32), jax.ShapeDtypeStruct(st.shape, F32)],
        compiler_params=_cparams(("arbitrary",), 40),
        name="ffn_sample",
    )(*args)


def _rope_pe(blk, cc, s1, s2):
    return blk * cc + pltpu.roll(blk, 96, axis=1) * s1 + pltpu.roll(blk, 32, axis=1) * s2


def _rms_norm(x, g):
    ms = jnp.mean(x * x, axis=-1, keepdims=True)
    return x * lax.rsqrt(ms + RMS_EPS) * g


def _mla_proj_body(x_ref, cc_ref, s1_ref, s2_ref, wdq_ref, qg_ref, wuq_ref, wdkv_ref, kvg_ref, *rest,
                   decode):
    hw = 2 * LANES
    cc = cc_ref[...]
    s1 = s1_ref[...]
    s2 = s2_ref[...]
    xb = x_ref[...].astype(BF16)
    cq = _rms_norm(_dot(xb, wdq_ref[...]), qg_ref[...])
    q = _dot(cq.astype(BF16), wuq_ref[...])
    kv = _dot(xb, wdkv_ref[...])
    ckv = _rms_norm(kv[:, 0:KV_LORA], kvg_ref[...])
    kpe = _rope_pe(kv[:, KV_LORA:KV_LORA + LANES], cc, s1, s2)
    if decode:
        wukt_ref, ql_ref, qp_ref, ckv_ref, kpe_ref = rest
        for h in range(MLA_HEADS):
            qn = q[:, h * hw:h * hw + LANES].astype(BF16)
            ql_ref[:, h * KV_LORA:(h + 1) * KV_LORA] = _dot(qn, wukt_ref[h]).astype(BF16)
            qp = _rope_pe(q[:, h * hw + LANES:(h + 1) * hw], cc, s1, s2)
            qp_ref[:, h * LANES:(h + 1) * LANES] = qp.astype(BF16)
    else:
        wuk_ref, wuv_ref, qo_ref, ko_ref, vo_ref, ckv_ref, kpe_ref = rest
        cb = ckv.astype(BF16)
        kn = _dot(cb, wuk_ref[...])
        vo_ref[...] = _dot(cb, wuv_ref[...]).astype(BF16)
        kpb = kpe.astype(BF16)
        for h in range(MLA_HEADS):
            qo_ref[:, h * hw:h * hw + LANES] = q[:, h * hw:h * hw + LANES].astype(BF16)
            qp = _rope_pe(q[:, h * hw + LANES:(h + 1) * hw], cc, s1, s2)
            qo_ref[:, h * hw + LANES:(h + 1) * hw] = qp.astype(BF16)
            ko_ref[:, h * hw:h * hw + LANES] = kn[:, h * LANES:(h + 1) * LANES].astype(BF16)
            ko_ref[:, h * hw + LANES:(h + 1) * hw] = kpb
    ckv_ref[...] = ckv
    kpe_ref[...] = kpe[:, 0:QK_ROPE]


def _mla_proj(x, tabs, w, tm, n_pos_blocks, decode):
    t, d = x.shape
    hw = 2 * LANES
    body = functools.partial(_mla_proj_body, decode=decode)
    row = lambda i: (i, 0)
    tab = lambda i: (i % n_pos_blocks, 0)
    ins = [x, tabs["cc"], tabs["s1"], tabs["s2"], w["dq"], w["qg"], w["uq"], w["dkv"], w["kvg"]]
    in_specs = [pl.BlockSpec((tm, d), row)] + [pl.BlockSpec((tm, LANES), tab)] * 3
    in_specs += [_resident(a.shape) for a in ins[4:]]
    if decode:
        ins += [w["ukt"]]
        in_specs += [_resident(w["ukt"].shape)]
        outs = [(MLA_HEADS * KV_LORA, BF16), (MLA_HEADS * LANES, BF16)]
    else:
        ins += [w["uk"], w["uv"]]
        in_specs += [_resident(w["uk"].shape), _resident(w["uv"].shape)]
        outs = [(MLA_HEADS * hw, BF16), (MLA_HEADS * hw, BF16), (MLA_HEADS * V_DIM, BF16)]
    outs += [(KV_LORA, F32), (QK_ROPE, F32)]
    return pl.pallas_call(
        body,
        grid=(t // tm,),
        in_specs=in_specs,
        out_specs=[pl.BlockSpec((tm, n), row) for n, _ in outs],
        out_shape=[jax.ShapeDtypeStruct((t, n), dt) for n, dt in outs],
        compiler_params=_cparams(("parallel",), 40),
        name="mla_proj_decode" if decode else "mla_proj",
    )(*ins)


def _flash_body(q_ref, k_ref, v_ref, o_ref, m_ref, l_ref, acc_ref, *, tq):
    qi = pl.program_id(1)
    hw = 2 * LANES
    row = lax.broadcasted_iota(jnp.int32, (tq, tq), 0)
    colm = lax.broadcasted_iota(jnp.int32, (tq, tq), 1)
    for h in range(MLA_HEADS):
        q = q_ref[:, h * hw:(h + 1) * hw]
        m_ref[...] = jnp.full(m_ref.shape, -jnp.inf, F32)
        l_ref[...] = jnp.zeros(l_ref.shape, F32)
        acc_ref[...] = jnp.zeros(acc_ref.shape, F32)

        def step(kj, masked, h=h, q=q):
            start = pl.multiple_of(kj * tq, tq)
            k = k_ref[pl.ds(start, tq), h * hw:(h + 1) * hw]
            v = v_ref[pl.ds(start, tq), h * V_DIM:(h + 1) * V_DIM]
            s = _dot_nt(q, k) * MLA_SCALE
            if masked:
                s = jnp.where(colm <= row, s, -jnp.inf)
            m_prev = m_ref[...]
            m_new = jnp.maximum(m_prev, jnp.max(s, axis=-1, keepdims=True))
            alpha = jnp.exp(m_prev - m_new)
            p = jnp.exp(s - m_new)
            l_ref[...] = alpha * l_ref[...] + jnp.sum(p, axis=-1, keepdims=True)
            acc_ref[...] = alpha * acc_ref[...] + _dot(p.astype(BF16), v)
            m_ref[...] = m_new

        def loop_body(kj, carry):
            step(kj, False)
            return carry

        lax.fori_loop(0, qi, loop_body, 0)
        step(qi, True)
        o_ref[:, h * V_DIM:(h + 1) * V_DIM] = (acc_ref[...] / l_ref[...]).astype(BF16)


def _flash_prompt(qp, kp, vp, bsz, seq, tq):
    nq = seq // tq
    body = functools.partial(_flash_body, tq=tq)
    return pl.pallas_call(
        body,
        grid=(bsz, nq),
        in_specs=[pl.BlockSpec((tq, qp.shape[1]), lambda b, i: (b * nq + i, 0)),
                  pl.BlockSpec((seq, kp.shape[1]), lambda b, i: (b, 0)),
                  pl.BlockSpec((seq, vp.shape[1]), lambda b, i: (b, 0))],
        out_specs=pl.BlockSpec((tq, vp.shape[1]), lambda b, i: (b * nq + i, 0)),
        out_shape=jax.ShapeDtypeStruct(vp.shape, BF16),
        scratch_shapes=[pltpu.VMEM((tq, 1), F32), pltpu.VMEM((tq, 1), F32), pltpu.VMEM((tq, V_DIM), F32)],
        compiler_params=_cparams(("parallel", "arbitrary"), 48),
        name="flash_prompt",
    )(qp, kp, vp)


def _decode_body(pt_ref, ql_ref, qp_ref, cn_ref, kn_ref, *rest, gp, ls):
    del pt_ref
    ckv_refs = rest[:gp]
    kpe_refs = rest[gp:2 * gp]
    o_ref, m_ref, l_ref, acc_ref = rest[2 * gp:]
    g = pl.program_id(1)
    n_g = pl.num_programs(1)

    @pl.when(g == 0)
    def _():
        m_ref[...] = jnp.full(m_ref.shape, -jnp.inf, F32)
        l_ref[...] = jnp.zeros(l_ref.shape, F32)
        acc_ref[...] = jnp.zeros(acc_ref.shape, F32)

    ql = ql_ref[0]
    qp = qp_ref[0]

    def update(s_list, k_list):
        s = jnp.concatenate(s_list, axis=1) if len(s_list) > 1 else s_list[0]
        m_prev = m_ref[...]
        m_new = jnp.maximum(m_prev, jnp.max(s, axis=-1, keepdims=True))
        alpha = jnp.exp(m_prev - m_new)
        p = jnp.exp(s - m_new)
        l_ref[...] = alpha * l_ref[...] + jnp.sum(p, axis=-1, keepdims=True)
        pv = None
        for i, kc in enumerate(k_list):
            y = _dot(p[:, i * PAGE_SIZE:(i + 1) * PAGE_SIZE].astype(BF16), kc)
            pv = y if pv is None else pv + y
        acc_ref[...] = alpha * acc_ref[...] + pv
        m_ref[...] = m_new

    s_list, k_list = [], []
    for i in range(gp):
        kc = ckv_refs[i][...].astype(BF16)
        kr = kpe_refs[i][...].astype(BF16)
        s_list.append((_dot_nt(ql, kc) + _dot_nt(qp, kr)) * MLA_SCALE)
        k_list.append(kc)
    update(s_list, k_list)

    @pl.when(g == n_g - 1)
    def _():
        kc = cn_ref[0].astype(BF16)
        kr = kn_ref[0].astype(BF16)
        s = (_dot_nt(ql, kc) + _dot_nt(qp, kr)) * MLA_SCALE
        r = lax.broadcasted_iota(jnp.int32, s.shape, 0) // MLA_HEADS
        cidx = lax.broadcasted_iota(jnp.int32, s.shape, 1)
        s = jnp.where((cidx <= r) & (cidx < ls), s, -jnp.inf)
        update([s], [kc])
        o_ref[0] = acc_ref[...] / l_ref[...]


def _decode_attention(page_table, ql, qp, cn, kn, cache_ckv, cache_kpe, layer, gp):
    nb, rows, _ = ql.shape
    n_pages = page_table.shape[1]
    ls = rows // MLA_HEADS
    body = functools.partial(_decode_body, gp=gp, ls=ls)

    def page_spec(width, i):
        return pl.BlockSpec((None, None, PAGE_SIZE, width),
                            lambda b, g, pt, i=i: (layer, pt[b, g * gp + i], 0, 0))

    per_b = lambda b, g, pt: (b, 0, 0)
    in_specs = [pl.BlockSpec((1, rows, KV_LORA), per_b), pl.BlockSpec((1, rows, QK_ROPE), per_b),
                pl.BlockSpec((1, PAGE_SIZE, KV_LORA), per_b), pl.BlockSpec((1, PAGE_SIZE, QK_ROPE), per_b)]
    in_specs += [page_spec(KV_LORA, i) for i in range(gp)]
    in_specs += [page_spec(QK_ROPE, i) for i in range(gp)]
    grid_spec = pltpu.PrefetchScalarGridSpec(
        num_scalar_prefetch=1,
        grid=(nb, n_pages // gp),
        in_specs=in_specs,
        out_specs=pl.BlockSpec((1, rows, KV_LORA), per_b),
        scratch_shapes=[pltpu.VMEM((rows, 1), F32), pltpu.VMEM((rows, 1), F32), pltpu.VMEM((rows, KV_LORA), F32)],
    )
    return pl.pallas_call(
        body,
        grid_spec=grid_spec,
        out_shape=jax.ShapeDtypeStruct((nb, rows, KV_LORA), F32),
        compiler_params=_cparams(("parallel", "arbitrary"), 32),
        name="decode_attention",
    )(page_table, ql, qp, cn, kn, *([cache_ckv] * gp), *([cache_kpe] * gp))


def _decode_out_body(ol_ref, wuv_ref, wo_ref, r_ref, g_ref, b_ref, o_ref):
    y = None
    for h in range(MLA_HEADS):
        oh = _dot(ol_ref[h].astype(BF16), wuv_ref[h]).astype(BF16)
        t = _dot(oh, wo_ref[h * V_DIM:(h + 1) * V_DIM, :])
        y = t if y is None else y + t
    o_ref[...] = _layer_norm(ALPHA * r_ref[...] + y, g_ref[...], b_ref[...])


def _decode_out(ol, wuv3, wo, res, g, b):
    d = res.shape[1]
    args = (ol, wuv3, wo, res, g.reshape(1, d), b.reshape(1, d))
    return pl.pallas_call(
        _decode_out_body,
        grid=(1,),
        in_specs=[_resident(a.shape) for a in args],
        out_specs=_whole(res.shape),
        out_shape=jax.ShapeDtypeStruct(res.shape, F32),
        compiler_params=_cparams(("arbitrary",), 24),
        name="decode_out",
    )(*args)


def _rope_angles(pos, half):
    inv_freq = ROPE_THETA ** (-jnp.arange(half, dtype=F32) / half)
    ang = pos[:, None] * inv_freq[None, :]
    return jnp.cos(ang), jnp.sin(ang)


def _ret_tables(pos, chunk):
    cos, sin = _rope_angles(pos, RET_DK // 2)
    log_gamma = jnp.log(1.0 - 2.0 ** (-5.0 - jnp.arange(RET_HEADS, dtype=F32)))
    idx = jnp.arange(chunk, dtype=F32)
    diff = idx[:, None] - idx[None, :]
    dec = jnp.where(diff >= 0, jnp.exp(jnp.maximum(diff, 0.0)[None] * log_gamma[:, None, None]), 0.0)
    q_dec = jnp.exp((idx + 1.0)[None, :] * log_gamma[:, None])
    k_dec = jnp.exp((chunk - 1.0 - idx)[None, :] * log_gamma[:, None])
    gc = jnp.exp(chunk * log_gamma)
    return {
        "c2": jnp.concatenate([cos, cos], axis=1),
        "s2": jnp.concatenate([-sin, sin], axis=1),
        "dec": dec.astype(F32),
        "q_dec": q_dec,
        "k_dec": k_dec,
        "gc": jnp.broadcast_to(gc[:, None, None], (RET_HEADS, 1, LANES)).astype(F32),
    }


def _pe_tables(pos):
    cos, sin = _rope_angles(pos, QK_ROPE // 2)
    z = jnp.zeros_like(cos)
    return {"cc": jnp.concatenate([cos, cos, z, z], axis=1),
            "s1": jnp.concatenate([-sin, z, z, z], axis=1),
            "s2": jnp.concatenate([z, sin, z, z], axis=1)}


def _pad_heads(w, nope, rope):
    k = w.shape[0]
    w3 = w.reshape(k, MLA_HEADS, nope + rope)
    pad = jnp.zeros((k, MLA_HEADS, 2 * LANES - nope - rope), w.dtype)
    return jnp.concatenate([w3, pad], axis=2).reshape(k, MLA_HEADS * 2 * LANES)


def _tiles(seq):
    def fit(t):
        while seq % t:
            t //= 2
        return t
    return {"tok": fit(512), "mix": fit(256), "ffn": fit(512), "attn": fit(512), "ff_chunk": 256, "pages": 16}


def kernel(x_prompt, x_sample, state_pool, state_ret, cache_ckv, cache_kpe, state_conv, page_table,
           w_in_even, pool_w, pool_scale, ret_gn_g, w_o_even,
           w_dq, q_norm_g, w_uq, w_dkv, kv_norm_g, w_uk, w_uv, w_o_mla,
           w_up, conv_w, conv_b, w_down, ln_mix_g, ln_mix_b, ln_ffn_g, ln_ffn_b):
    bp, lp, d = x_prompt.shape
    bs, ls, _ = x_sample.shape
    past_len = page_table.shape[1] * PAGE_SIZE
    depth = w_up.shape[0]
    dff = w_down.shape[1]
    tl = _tiles(lp)
    ns = bs * ls
    assert lp % RET_CHUNK == 0 and ls % RET_CHUNK != 0 and tl["mix"] % RET_CHUNK == 0
    assert dff % tl["ff_chunk"] == 0 and page_table.shape[1] % tl["pages"] == 0 and ls <= PAGE_SIZE

    xp = x_prompt.reshape(bp * lp, d)
    xs = jnp.swapaxes(x_sample, 0, 1).reshape(ns, d)
    pos_p = jnp.arange(lp, dtype=F32)
    pos_s = past_len + jnp.arange(ls, dtype=F32)
    pos_s_rows = jnp.repeat(pos_s, bs)

    outs = {k: [] for k in ("pool_p", "pool_s", "ret_p", "ret_s", "ckv_p", "ckv_s", "kpe_p", "kpe_s",
                            "conv_p", "conv_s")}
    for layer in range(depth):
        if layer % 2 == 0:
            e = layer // 2
            w_in = w_in_even[e].astype(BF16)
            w_o = w_o_even[e].astype(BF16)
            pw = pool_w[e].astype(BF16)
            tp = _ret_tables(pos_p, RET_CHUNK)
            tp["qd"] = jnp.broadcast_to(tp["q_dec"][:, :, None], (RET_HEADS, RET_CHUNK, LANES))
            tp["kd"] = jnp.broadcast_to(tp["k_dec"][:, :, None], (RET_HEADS, RET_CHUNK, LANES))
            hp = _matmul(xp, w_in, tl["tok"])
            mix_p, pst, rst = _even_mix_prompt(hp, bp, lp, tp, pw, pool_scale[e], ret_gn_g[e], tl["mix"])
            outs["pool_p"].append(pst)
            outs["ret_p"].append(rst)
            xp = _matmul_res_ln(mix_p, w_o, xp, ln_mix_g[layer], ln_mix_b[layer], tl["tok"])
            ts = _ret_tables(pos_s_rows, ls)
            rl = jnp.arange(ns) // bs
            rb = jnp.arange(ns) % bs
            same = rb[:, None] == rb[None, :]
            ts["dm"] = jnp.where(same[None], ts["dec"][:, rl[:, None], rl[None, :]], 0.0)
            ts["qd"] = jnp.broadcast_to(ts["q_dec"][:, rl, None], (RET_HEADS, ns, LANES))
            ts["kd"] = jnp.broadcast_to(ts["k_dec"][:, rl, None], (RET_HEADS, ns, LANES))
            hs = _matmul(xs, w_in, ns)
            hist = jnp.swapaxes(state_pool[e], 0, 1)
            mix_s, hist_new, s_new = _even_mix_sample(hs, hist, state_ret[e], ts, pw, pool_scale[e],
                                                      ret_gn_g[e], bs, ls, past_len)
            outs["pool_s"].append(jnp.swapaxes(hist_new, 0, 1))
            outs["ret_s"].append(s_new)
            xs = _matmul_res_ln(mix_s, w_o, xs, ln_mix_g[layer], ln_mix_b[layer], ns)
        else:
            o = layer // 2
            hw = 2 * LANES
            w = {
                "dq": w_dq[o].astype(BF16),
                "qg": q_norm_g[o].reshape(1, -1),
                "uq": _pad_heads(w_uq[o], QK_NOPE, QK_ROPE).astype(BF16),
                "dkv": jnp.pad(w_dkv[o], ((0, 0), (0, KV_LORA + LANES - w_dkv.shape[2]))).astype(BF16),
                "kvg": kv_norm_g[o].reshape(1, -1),
                "uk": w_uk[o].reshape(KV_LORA, MLA_HEADS * QK_NOPE).astype(BF16),
                "uv": w_uv[o].reshape(KV_LORA, MLA_HEADS * V_DIM).astype(BF16),
                "ukt": jnp.transpose(w_uk[o], (1, 2, 0)).astype(BF16),
            }
            w_o = w_o_mla[o].astype(BF16)
            qp, kp, vp, ckv_p, kpe_p = _mla_proj(xp, _pe_tables(pos_p), w, tl["tok"], lp // tl["tok"], False)
            att = _flash_prompt(qp, kp, vp, bp, lp, tl["attn"])
            outs["ckv_p"].append(ckv_p.reshape(bp, lp, KV_LORA))
            outs["kpe_p"].append(kpe_p.reshape(bp, lp, QK_ROPE))
            xp = _matmul_res_ln(att, w_o, xp, ln_mix_g[layer], ln_mix_b[layer], tl["tok"])
            ql, qpe, ckv_s, kpe_s = _mla_proj(xs, _pe_tables(pos_s_rows), w, ns, 1, True)
            rows = ls * MLA_HEADS

            def per_batch(a, width):
                return jnp.transpose(a.reshape(ls, bs, MLA_HEADS, width), (1, 0, 2, 3)).reshape(bs, rows, width)

            ql_b = per_batch(ql, KV_LORA)
            qp_b = per_batch(qpe, LANES)[:, :, 0:QK_ROPE]
            ckv_sb = jnp.swapaxes(ckv_s.reshape(ls, bs, KV_LORA), 0, 1)
            kpe_sb = jnp.swapaxes(kpe_s.reshape(ls, bs, QK_ROPE), 0, 1)
            cn = jnp.pad(ckv_sb, ((0, 0), (0, PAGE_SIZE - ls), (0, 0)))
            kn = jnp.pad(kpe_sb, ((0, 0), (0, PAGE_SIZE - ls), (0, 0)))
            o_lat = _decode_attention(page_table, ql_b, qp_b, cn, kn, cache_ckv, cache_kpe, o, tl["pages"])
            outs["ckv_s"].append(ckv_sb)
            outs["kpe_s"].append(kpe_sb)
            ol = jnp.transpose(o_lat.reshape(bs, ls, MLA_HEADS, KV_LORA), (2, 1, 0, 3)).reshape(MLA_HEADS, ns, KV_LORA)
            wuv3 = jnp.transpose(w_uv[o], (1, 0, 2)).astype(BF16)
            xs = _decode_out(ol, wuv3, w_o, xs, ln_mix_g[layer], ln_mix_b[layer])
        wup = w_up[layer].astype(BF16)
        wd = w_down[layer].astype(BF16)
        xp, st_p = _ffn_prompt(xp, bp, lp, wup, wd, conv_w[layer], conv_b[layer],
                               ln_ffn_g[layer], ln_ffn_b[layer], tl["ffn"], tl["ff_chunk"])
        outs["conv_p"].append(st_p[:, SUBLANES - (CONV_W - 1):, :])
        st_s = jnp.swapaxes(state_conv[layer], 0, 1)
        xs, st_s_new = _ffn_sample(xs, st_s, wup, wd, conv_w[layer], conv_b[layer],
                                   ln_ffn_g[layer], ln_ffn_b[layer], bs, ls, tl["ff_chunk"])
        outs["conv_s"].append(jnp.swapaxes(st_s_new, 0, 1))

    y_p = xp.reshape(bp, lp, d)
    y_s = jnp.swapaxes(xs.reshape(ls, bs, d), 0, 1)
    return (y_p, y_s,
            jnp.stack(outs["pool_p"]), jnp.stack(outs["pool_s"]),
            jnp.stack(outs["ret_p"]), jnp.stack(outs["ret_s"]),
            jnp.stack(outs["ckv_p"]), jnp.stack(outs["ckv_s"]),
            jnp.stack(outs["kpe_p"]), jnp.stack(outs["kpe_s"]),
            jnp.stack(outs["conv_p"]), jnp.stack(outs["conv_s"]))
```

```python
import functools

import jax
import jax.numpy as jnp
from jax import lax
from jax.experimental import pallas as pl
from jax.experimental.pallas import tpu as pltpu

F32 = jnp.float32
BF16 = jnp.bfloat16

PAGE_SIZE = 128
POOL_WINDOWS = (2, 4, 8, 16)
POOL_HIST = max(POOL_WINDOWS) - 1
RET_HEADS = 4
RET_DK = 128
RET_CHUNK = 128
MLA_HEADS = 8
QK_NOPE = 128
QK_ROPE = 64
V_DIM = 128
KV_LORA = 256
CONV_W = 3
DEPTH = 2
ALPHA = (2.0 * DEPTH) ** 0.25
ROPE_THETA = 10000.0
LN_EPS = 1e-5
RMS_EPS = 1e-6
GN_EPS = 1e-6
MLA_SCALE = (QK_NOPE + QK_ROPE) ** -0.5

LANES = 128
SUBLANES = 8
MIB = 1024 * 1024


def _cparams(sem, vmem_mib):
    return pltpu.CompilerParams(dimension_semantics=sem, vmem_limit_bytes=int(vmem_mib * MIB))


def _resident(shape):
    nd = len(shape)
    return pl.BlockSpec(shape, lambda *_: (0,) * nd, pipeline_mode=pl.Buffered(1))


def _whole(shape):
    nd = len(shape)
    return pl.BlockSpec(shape, lambda *_: (0,) * nd)


def _dot(a, b):
    return jnp.dot(a, b, preferred_element_type=F32)


def _dot_nt(a, b):
    return lax.dot_general(a, b, (((1,), (1,)), ((), ())), preferred_element_type=F32)


def _dot_tn(a, b):
    return lax.dot_general(a, b, (((0,), (0,)), ((), ())), preferred_element_type=F32)


def _layer_norm(z, g, b):
    mu = jnp.mean(z, axis=-1, keepdims=True)
    d = z - mu
    var = jnp.mean(d * d, axis=-1, keepdims=True)
    return d * lax.rsqrt(var + LN_EPS) * g + b


def _silu(x):
    return x * jax.nn.sigmoid(x)


def _mm_body(x_ref, w_ref, o_ref):
    o_ref[...] = _dot(x_ref[...].astype(BF16), w_ref[...]).astype(o_ref.dtype)


def _matmul(x, w, tm, out_dtype=F32):
    m, k = x.shape
    n = w.shape[1]
    return pl.pallas_call(
        _mm_body,
        grid=(m // tm,),
        in_specs=[pl.BlockSpec((tm, k), lambda i: (i, 0)), _resident((k, n))],
        out_specs=pl.BlockSpec((tm, n), lambda i: (i, 0)),
        out_shape=jax.ShapeDtypeStruct((m, n), out_dtype),
        compiler_params=_cparams(("parallel",), 40),
        name="matmul",
    )(x, w)


def _mm_ln_body(a_ref, w_ref, r_ref, g_ref, b_ref, o_ref):
    y = _dot(a_ref[...].astype(BF16), w_ref[...])
    o_ref[...] = _layer_norm(ALPHA * r_ref[...] + y, g_ref[...], b_ref[...])


def _matmul_res_ln(a, w, res, g, b, tm):
    m, k = a.shape
    n = w.shape[1]
    return pl.pallas_call(
        _mm_ln_body,
        grid=(m // tm,),
        in_specs=[pl.BlockSpec((tm, k), lambda i: (i, 0)), _resident((k, n)),
                  pl.BlockSpec((tm, n), lambda i: (i, 0)), _resident((1, n)), _resident((1, n))],
        out_specs=pl.BlockSpec((tm, n), lambda i: (i, 0)),
        out_shape=jax.ShapeDtypeStruct((m, n), F32),
        compiler_params=_cparams(("parallel",), 32),
        name="matmul_res_ln",
    )(a, w, res, g.reshape(1, n), b.reshape(1, n))


def _rope_full(x, c2, s2):
    return x * c2 + pltpu.roll(x, 64, axis=1) * s2


def _group_norm_gate(o, gate, gn_row):
    mu = jnp.mean(o, axis=-1, keepdims=True)
    d = o - mu
    var = jnp.mean(d * d, axis=-1, keepdims=True)
    return _silu(gate) * (d * lax.rsqrt(var + GN_EPS) * gn_row)


def _even_mix_prompt_body(h_ref, c2_ref, s2_ref, dec_ref, qd_ref, kd_ref, gc_ref, pw_ref, ps_ref,
                          gn_ref, mix_ref, pst_ref, rst_ref, ext_ref, s_ref, *, tm, n_j):
    j = pl.program_id(1)
    pd = len(POOL_WINDOWS) * LANES

    @pl.when(j == 0)
    def _():
        ext_ref[0:16, :] = jnp.zeros((16, pd), F32)
        s_ref[...] = jnp.zeros(s_ref.shape, F32)

    @pl.when(j > 0)
    def _():
        ext_ref[0:16, :] = ext_ref[tm:tm + 16, :]

    ext_ref[16:16 + tm, :] = h_ref[:, 0:pd]

    pos = (j * tm + lax.broadcasted_iota(jnp.int32, (tm, 1), 0)).astype(F32)
    for g, w in enumerate(POOL_WINDOWS):
        cols = slice(g * LANES, (g + 1) * LANES)
        e = ext_ref[:, cols]
        u = e[16:, :]
        s = 1
        while s < w:
            e = e + pltpu.roll(e, s, axis=0)
            s *= 2
        cnt = jnp.minimum(float(w), pos + 1.0)
        pooled = e[16:, :] / cnt - u
        mixed = _dot(pooled.astype(BF16), pw_ref[g]) * ps_ref[:, cols]
        mix_ref[:, cols] = mixed.astype(BF16)

    c = RET_CHUNK
    k_scale = RET_DK ** -0.5
    for ci in range(tm // c):
        rows = slice(ci * c, (ci + 1) * c)
        c2 = c2_ref[rows, :]
        s2 = s2_ref[rows, :]
        for hd in range(RET_HEADS):
            def col(part, hd=hd):
                return slice(pd + (part * RET_HEADS + hd) * LANES, pd + (part * RET_HEADS + hd + 1) * LANES)
            q = _rope_full(h_ref[rows, col(0)], c2, s2)
            k = _rope_full(h_ref[rows, col(1)], c2, s2) * k_scale
            vb = h_ref[rows, col(2)].astype(BF16)
            gate = h_ref[rows, col(3)]
            st = s_ref[hd]
            sc = _dot_nt(q.astype(BF16), k.astype(BF16)) * dec_ref[hd]
            o = _dot(sc.astype(BF16), vb)
            o = o + _dot((q * qd_ref[hd]).astype(BF16), st.astype(BF16))
            s_ref[hd] = gc_ref[hd] * st + _dot_tn((k * kd_ref[hd]).astype(BF16), vb)
            ret = _group_norm_gate(o, gate, gn_ref[:, hd * LANES:(hd + 1) * LANES])
            mix_ref[rows, pd + hd * LANES:pd + (hd + 1) * LANES] = ret.astype(BF16)

    @pl.when(j == n_j - 1)
    def _():
        pst_ref[0] = ext_ref[pl.ds(tm + 1, POOL_HIST), :]
        rst_ref[0] = s_ref[...]


def _even_mix_prompt(h, bsz, seq, tabs, pool_w, pool_scale, gn_g, tm):
    n_j = seq // tm
    pd = pool_scale.shape[0]
    ed = pd + RET_HEADS * LANES
    body = functools.partial(_even_mix_prompt_body, tm=tm, n_j=n_j)
    return pl.pallas_call(
        body,
        grid=(bsz, n_j),
        in_specs=[
            pl.BlockSpec((tm, h.shape[1]), lambda b, j: (b * n_j + j, 0)),
            pl.BlockSpec((tm, LANES), lambda b, j: (j, 0)),
            pl.BlockSpec((tm, LANES), lambda b, j: (j, 0)),
            _resident((RET_HEADS, RET_CHUNK, RET_CHUNK)),
            _resident((RET_HEADS, RET_CHUNK, LANES)),
            _resident((RET_HEADS, RET_CHUNK, LANES)),
            _resident((RET_HEADS, 1, LANES)),
            _resident(pool_w.shape),
            _resident((1, pd)),
            _resident((1, RET_HEADS * LANES)),
        ],
        out_specs=[
            pl.BlockSpec((tm, ed), lambda b, j: (b * n_j + j, 0)),
            pl.BlockSpec((1, POOL_HIST, pd), lambda b, j: (b, 0, 0)),
            pl.BlockSpec((1, RET_HEADS, RET_DK, LANES), lambda b, j: (b, 0, 0, 0)),
        ],
        out_shape=[
            jax.ShapeDtypeStruct((bsz * seq, ed), BF16),
            jax.ShapeDtypeStruct((bsz, POOL_HIST, pd), F32),
            jax.ShapeDtypeStruct((bsz, RET_HEADS, RET_DK, LANES), F32),
        ],
        scratch_shapes=[pltpu.VMEM((tm + 16, pd), F32), pltpu.VMEM((RET_HEADS, RET_DK, LANES), F32)],
        compiler_params=_cparams(("parallel", "arbitrary"), 32),
        name="even_mix_prompt",
    )(h, tabs["c2"], tabs["s2"], tabs["dec"], tabs["qd"], tabs["kd"], tabs["gc"],
      pool_w, pool_scale.reshape(1, pd), gn_g.reshape(1, -1))


def _even_mix_sample_body(h_ref, hist_ref, s0_ref, c2_ref, s2_ref, dm_ref, qd_ref, kd_ref, gc_ref,
                          pw_ref, ps_ref, gn_ref, mix_ref, hist_o_ref, s_o_ref, oc_ref,
                          *, nb, ls, cnts):
    pd = len(POOL_WINDOWS) * LANES
    ext = [hist_ref[i] for i in range(POOL_HIST)]
    ext += [h_ref[l * nb:(l + 1) * nb, 0:pd] for l in range(ls)]
    for i in range(POOL_HIST):
        hist_o_ref[i] = ext[ls + i]
    for g, w in enumerate(POOL_WINDOWS):
        cols = slice(g * LANES, (g + 1) * LANES)
        outs = []
        for l in range(ls):
            top = POOL_HIST + l
            acc = ext[top][:, cols]
            for jj in range(1, w):
                acc = acc + ext[top - jj][:, cols]
            outs.append(acc / cnts[g][l] - ext[top][:, cols])
        pooled = jnp.concatenate(outs, axis=0)
        mixed = _dot(pooled.astype(BF16), pw_ref[g]) * ps_ref[:, cols]
        mix_ref[:, cols] = mixed.astype(BF16)

    rows_b = lax.broadcasted_iota(jnp.int32, (ls * nb, 1), 0) % nb
    k_scale = RET_DK ** -0.5
    c2 = c2_ref[...]
    s2 = s2_ref[...]
    qs, ks, vs = [], [], []
    for hd in range(RET_HEADS):
        def col(part, hd=hd):
            return slice(pd + (part * RET_HEADS + hd) * LANES, pd + (part * RET_HEADS + hd + 1) * LANES)
        q = _rope_full(h_ref[:, col(0)], c2, s2)
        k = _rope_full(h_ref[:, col(1)], c2, s2) * k_scale
        vb = h_ref[:, col(2)].astype(BF16)
        sc = _dot_nt(q.astype(BF16), k.astype(BF16)) * dm_ref[hd]
        oc_ref[hd] = _dot(sc.astype(BF16), vb)
        qs.append(q * qd_ref[hd])
        ks.append(k * kd_ref[hd])
        vs.append(vb)

    def per_batch(b, carry):
        sel = rows_b == b
        for hd in range(RET_HEADS):
            st = s0_ref[b, hd]
            qm = jnp.where(sel, qs[hd], 0.0).astype(BF16)
            km = jnp.where(sel, ks[hd], 0.0).astype(BF16)
            oc_ref[hd] += _dot(qm, st.astype(BF16))
            s_o_ref[b, hd] = gc_ref[hd] * st + _dot_tn(km, vs[hd])
        return carry

    lax.fori_loop(0, nb, per_batch, 0)

    for hd in range(RET_HEADS):
        gate = h_ref[:, pd + (3 * RET_HEADS + hd) * LANES:pd + (3 * RET_HEADS + hd + 1) * LANES]
        ret = _group_norm_gate(oc_ref[hd], gate, gn_ref[:, hd * LANES:(hd + 1) * LANES])
        mix_ref[:, pd + hd * LANES:pd + (hd + 1) * LANES] = ret.astype(BF16)


def _even_mix_sample(h, hist, s0, tabs, pool_w, pool_scale, gn_g, nb, ls, past_len):
    pd = pool_scale.shape[0]
    ed = pd + RET_HEADS * LANES
    cnts = tuple(tuple(float(min(w, past_len + l + 1)) for l in range(ls)) for w in POOL_WINDOWS)
    body = functools.partial(_even_mix_sample_body, nb=nb, ls=ls, cnts=cnts)
    n = ls * nb
    args = (h, hist, s0, tabs["c2"], tabs["s2"], tabs["dm"], tabs["qd"], tabs["kd"], tabs["gc"],
            pool_w, pool_scale.reshape(1, pd), gn_g.reshape(1, -1))
    return pl.pallas_call(
        body,
        grid=(1,),
        in_specs=[_resident(a.shape) for a in args],
        out_specs=[_whole((n, ed)), _whole(hist.shape), _whole(s0.shape)],
        out_shape=[jax.ShapeDtypeStruct((n, ed), BF16),
                   jax.ShapeDtypeStruct(hist.shape, F32),
                   jax.ShapeDtypeStruct(s0.shape, F32)],
        scratch_shapes=[pltpu.VMEM((RET_HEADS, n, LANES), F32)],
        compiler_params=_cparams(("arbitrary",), 48),
        name="even_mix_sample",
    )(*args)


def _ffn_chunk(xb, wup_ref, wd_ref, cw_ref, cb_ref, c, tf, dff, shift_fn):
    cols = slice(c * tf, (c + 1) * tf)
    a = _dot(xb, wup_ref[:, cols])
    gate_in = _dot(xb, wup_ref[:, dff + c * tf:dff + (c + 1) * tf])
    a1, a2 = shift_fn(a, c)
    conv = cb_ref[:, cols] + cw_ref[0:1, cols] * a2
    conv = conv + cw_ref[1:2, cols] * a1
    conv = conv + cw_ref[2:3, cols] * a
    act = (_silu(conv) * gate_in).astype(BF16)
    return a, _dot(act, wd_ref[cols, :])


def _ffn_prompt_body(x_ref, wup_ref, wd_ref, cw_ref, cb_ref, g_ref, b_ref, o_ref, st_ref, carry_ref,
                     *, tm, n_j, tf, dff):
    j = pl.program_id(1)

    @pl.when(j == 0)
    def _():
        carry_ref[...] = jnp.zeros(carry_ref.shape, F32)

    x = x_ref[...]
    xb = x.astype(BF16)
    row = lax.broadcasted_iota(jnp.int32, (tm, tf), 0)

    def shift_fn(a, c):
        prev = carry_ref[:, c * tf:(c + 1) * tf]
        a1 = jnp.where(row == 0, prev[7:8, :], pltpu.roll(a, 1, axis=0))
        a2 = jnp.where(row == 0, prev[6:7, :], jnp.where(row == 1, prev[7:8, :], pltpu.roll(a, 2, axis=0)))
        return a1, a2

    acc = None
    for c in range(dff // tf):
        a, y = _ffn_chunk(xb, wup_ref, wd_ref, cw_ref, cb_ref, c, tf, dff, shift_fn)
        acc = y if acc is None else acc + y
        tail = a[tm - SUBLANES:tm, :]
        carry_ref[:, c * tf:(c + 1) * tf] = tail
        st_ref[0, :, c * tf:(c + 1) * tf] = tail
    o_ref[...] = _layer_norm(ALPHA * x + acc, g_ref[...], b_ref[...])


def _ffn_prompt(x, bsz, seq, wup, wd, conv_w, conv_b, g, b, tm, tf):
    d = x.shape[1]
    dff = wd.shape[0]
    n_j = seq // tm
    body = functools.partial(_ffn_prompt_body, tm=tm, n_j=n_j, tf=tf, dff=dff)
    return pl.pallas_call(
        body,
        grid=(bsz, n_j),
        in_specs=[pl.BlockSpec((tm, d), lambda bi, j: (bi * n_j + j, 0)),
                  _resident(wup.shape), _resident(wd.shape), _resident(conv_w.shape),
                  _resident((1, dff)), _resident((1, d)), _resident((1, d))],
        out_specs=[pl.BlockSpec((tm, d), lambda bi, j: (bi * n_j + j, 0)),
                   pl.BlockSpec((1, SUBLANES, dff), lambda bi, j: (bi, 0, 0))],
        out_shape=[jax.ShapeDtypeStruct(x.shape, F32),
                   jax.ShapeDtypeStruct((bsz, SUBLANES, dff), F32)],
        scratch_shapes=[pltpu.VMEM((SUBLANES, dff), F32)],
        compiler_params=_cparams(("parallel", "arbitrary"), 52),
        name="ffn_prompt",
    )(x, wup, wd, conv_w, conv_b.reshape(1, dff), g.reshape(1, d), b.reshape(1, d))


def _ffn_sample_body(x_ref, st_ref, wup_ref, wd_ref, cw_ref, cb_ref, g_ref, b_ref, o_ref, st_o_ref,
                     *, nb, ls, tf, dff):
    x = x_ref[...]
    xb = x.astype(BF16)
    nh = CONV_W - 1

    def shift_fn(a, c):
        cols = slice(c * tf, (c + 1) * tf)
        ext = [st_ref[i, :, cols] for i in range(nh)] + [a[l * nb:(l + 1) * nb, :] for l in range(ls)]
        a1 = jnp.concatenate([ext[nh + l - 1] for l in range(ls)], axis=0)
        a2 = jnp.concatenate([ext[nh + l - 2] for l in range(ls)], axis=0)
        for i in range(nh):
            st_o_ref[i, :, cols] = ext[ls + i]
        return a1, a2

    acc = None
    for c in range(dff // tf):
        _, y = _ffn_chunk(xb, wup_ref, wd_ref, cw_ref, cb_ref, c, tf, dff, shift_fn)
        acc = y if acc is None else acc + y
    o_ref[...] = _layer_norm(ALPHA * x + acc, g_ref[...], b_ref[...])


def _ffn_sample(x, st, wup, wd, conv_w, conv_b, g, b, nb, ls, tf):
    d = x.shape[1]
    dff = wd.shape[0]
    body = functools.partial(_ffn_sample_body, nb=nb, ls=ls, tf=tf, dff=dff)
    args = (x, st, wup, wd, conv_w, conv_b.reshape(1, dff), g.reshape(1, d), b.reshape(1, d))
    return pl.pallas_call(
        body,
        grid=(1,),
        in_specs=[_resident(a.shape) for a in args],
        out_specs=[_whole(x.shape), _whole(st.shape)],
        out_shape=[jax.ShapeDtypeStruct(x.shape, F32), jax.ShapeDtypeStruct(st.shape, F32)],
        compiler_params=_cparams(("arbitrary",), 40),
        name="ffn_sample",
    )(*args)


def _rope_pe(blk, cc, s1, s2):
    return blk * cc + pltpu.roll(blk, 96, axis=1) * s1 + pltpu.roll(blk, 32, axis=1) * s2


def _rms_norm(x, g):
    ms = jnp.mean(x * x, axis=-1, keepdims=True)
    return x * lax.rsqrt(ms + RMS_EPS) * g


def _mla_proj_body(x_ref, cc_ref, s1_ref, s2_ref, wdq_ref, qg_ref, wuq_ref, wdkv_ref, kvg_ref, *rest,
                   decode):
    hw = 2 * LANES
    cc = cc_ref[...]
    s1 = s1_ref[...]
    s2 = s2_ref[...]
    xb = x_ref[...].astype(BF16)
    cq = _rms_norm(_dot(xb, wdq_ref[...]), qg_ref[...])
    q = _dot(cq.astype(BF16), wuq_ref[...])
    kv = _dot(xb, wdkv_ref[...])
    ckv = _rms_norm(kv[:, 0:KV_LORA], kvg_ref[...])
    kpe = _rope_pe(kv[:, KV_LORA:KV_LORA + LANES], cc, s1, s2)
    if decode:
        wukt_ref, ql_ref, qp_ref, ckv_ref, kpe_ref = rest
        for h in range(MLA_HEADS):
            qn = q[:, h * hw:h * hw + LANES].astype(BF16)
            ql_ref[:, h * KV_LORA:(h + 1) * KV_LORA] = _dot(qn, wukt_ref[h]).astype(BF16)
            qp = _rope_pe(q[:, h * hw + LANES:(h + 1) * hw], cc, s1, s2)
            qp_ref[:, h * LANES:(h + 1) * LANES] = qp.astype(BF16)
    else:
        wuk_ref, wuv_ref, qo_ref, ko_ref, vo_ref, ckv_ref, kpe_ref = rest
        cb = ckv.astype(BF16)
        kn = _dot(cb, wuk_ref[...])
        vo_ref[...] = _dot(cb, wuv_ref[...]).astype(BF16)
        kpb = kpe.astype(BF16)
        for h in range(MLA_HEADS):
            qo_ref[:, h * hw:h * hw + LANES] = q[:, h * hw:h * hw + LANES].astype(BF16)
            qp = _rope_pe(q[:, h * hw + LANES:(h + 1) * hw], cc, s1, s2)
            qo_ref[:, h * hw + LANES:(h + 1) * hw] = qp.astype(BF16)
            ko_ref[:, h * hw:h * hw + LANES] = kn[:, h * LANES:(h + 1) * LANES].astype(BF16)
            ko_ref[:, h * hw + LANES:(h + 1) * hw] = kpb
    ckv_ref[...] = ckv
    kpe_ref[...] = kpe[:, 0:QK_ROPE]


def _mla_proj(x, tabs, w, tm, n_pos_blocks, decode):
    t, d = x.shape
    hw = 2 * LANES
    body = functools.partial(_mla_proj_body, decode=decode)
    row = lambda i: (i, 0)
    tab = lambda i: (i % n_pos_blocks, 0)
    ins = [x, tabs["cc"], tabs["s1"], tabs["s2"], w["dq"], w["qg"], w["uq"], w["dkv"], w["kvg"]]
    in_specs = [pl.BlockSpec((tm, d), row)] + [pl.BlockSpec((tm, LANES), tab)] * 3
    in_specs += [_resident(a.shape) for a in ins[4:]]
    if decode:
        ins += [w["ukt"]]
        in_specs += [_resident(w["ukt"].shape)]
        outs = [(MLA_HEADS * KV_LORA, BF16), (MLA_HEADS * LANES, BF16)]
    else:
        ins += [w["uk"], w["uv"]]
        in_specs += [_resident(w["uk"].shape), _resident(w["uv"].shape)]
        outs = [(MLA_HEADS * hw, BF16), (MLA_HEADS * hw, BF16), (MLA_HEADS * V_DIM, BF16)]
    outs += [(KV_LORA, F32), (QK_ROPE, F32)]
    return pl.pallas_call(
        body,
        grid=(t // tm,),
        in_specs=in_specs,
        out_specs=[pl.BlockSpec((tm, n), row) for n, _ in outs],
        out_shape=[jax.ShapeDtypeStruct((t, n), dt) for n, dt in outs],
        compiler_params=_cparams(("parallel",), 40),
        name="mla_proj_decode" if decode else "mla_proj",
    )(*ins)


def _online_softmax_update(s, m_ref, l_ref, acc_ref, idx, pv_fn):
    n = s.shape[1] // LANES
    tiles = [s[:, j * LANES:(j + 1) * LANES] for j in range(n)]
    mx = tiles[0]
    for t in tiles[1:]:
        mx = jnp.maximum(mx, t)
    m_prev = m_ref[idx]
    m_new = jnp.maximum(m_prev, jnp.max(mx, axis=-1, keepdims=True))
    alpha = jnp.exp(m_prev - m_new)
    ps = [jnp.exp(t - m_new) for t in tiles]
    rs = ps[0]
    for t in ps[1:]:
        rs = rs + t
    l_ref[idx] = alpha * l_ref[idx] + jnp.sum(rs, axis=-1, keepdims=True)
    p = jnp.concatenate(ps, axis=1) if n > 1 else ps[0]
    pv = pv_fn(p.astype(BF16))
    a_w = alpha if acc_ref.shape[-1] == LANES else jnp.concatenate([alpha] * (acc_ref.shape[-1] // LANES), axis=1)
    acc_ref[idx] = a_w * acc_ref[idx] + pv
    m_ref[idx] = m_new


def _flash_body(q_ref, k_ref, v_ref, o_ref, m_ref, l_ref, acc_ref, *, tq):
    qi = pl.program_id(1)
    hw = 2 * LANES
    m_ref[...] = jnp.full(m_ref.shape, -jnp.inf, F32)
    l_ref[...] = jnp.zeros(l_ref.shape, F32)
    acc_ref[...] = jnp.zeros(acc_ref.shape, F32)

    def step(h, kj, masked):
        start = pl.multiple_of(kj * tq, tq)
        q = q_ref[:, h * hw:(h + 1) * hw]
        k = k_ref[pl.ds(start, tq), h * hw:(h + 1) * hw]
        v = v_ref[pl.ds(start, tq), h * V_DIM:(h + 1) * V_DIM]
        s = _dot_nt(q, k) * MLA_SCALE
        if masked:
            row = lax.broadcasted_iota(jnp.int32, (tq, tq), 0)
            colm = lax.broadcasted_iota(jnp.int32, (tq, tq), 1)
            s = jnp.where(colm <= row, s, -jnp.inf)
        _online_softmax_update(s, m_ref, l_ref, acc_ref, h, lambda p: _dot(p, v))

    def loop_body(kj, carry):
        for h in range(MLA_HEADS):
            step(h, kj, False)
        return carry

    lax.fori_loop(0, qi, loop_body, 0)
    for h in range(MLA_HEADS):
        step(h, qi, True)
        o_ref[:, h * V_DIM:(h + 1) * V_DIM] = (acc_ref[h] / l_ref[h]).astype(BF16)


def _flash_prompt(qp, kp, vp, bsz, seq, tq):
    nq = seq // tq
    body = functools.partial(_flash_body, tq=tq)
    return pl.pallas_call(
        body,
        grid=(bsz, nq),
        in_specs=[pl.BlockSpec((tq, qp.shape[1]), lambda b, i: (b * nq + i, 0)),
                  pl.BlockSpec((seq, kp.shape[1]), lambda b, i: (b, 0)),
                  pl.BlockSpec((seq, vp.shape[1]), lambda b, i: (b, 0))],
        out_specs=pl.BlockSpec((tq, vp.shape[1]), lambda b, i: (b * nq + i, 0)),
        out_shape=jax.ShapeDtypeStruct(vp.shape, BF16),
        scratch_shapes=[pltpu.VMEM((MLA_HEADS, tq, LANES), F32), pltpu.VMEM((MLA_HEADS, tq, LANES), F32),
                        pltpu.VMEM((MLA_HEADS, tq, V_DIM), F32)],
        compiler_params=_cparams(("parallel", "arbitrary"), 56),
        name="flash_prompt",
    )(qp, kp, vp)


def _decode_body(pt_ref, ql_ref, qp_ref, cn_ref, kn_ref, *rest, gp, ls):
    del pt_ref
    ckv_refs = rest[:gp]
    kpe_refs = rest[gp:2 * gp]
    o_ref, kbuf_ref, rbuf_ref, m_ref, l_ref, acc_ref = rest[2 * gp:]
    g = pl.program_id(1)
    n_g = pl.num_programs(1)

    @pl.when(g == 0)
    def _():
        m_ref[...] = jnp.full(m_ref.shape, -jnp.inf, F32)
        l_ref[...] = jnp.zeros(l_ref.shape, F32)
        acc_ref[...] = jnp.zeros(acc_ref.shape, F32)

    ql = ql_ref[0]
    qp = qp_ref[0]
    for i in range(gp):
        kbuf_ref[i * PAGE_SIZE:(i + 1) * PAGE_SIZE, :] = ckv_refs[i][...].astype(BF16)
        rbuf_ref[:, i * PAGE_SIZE:(i + 1) * PAGE_SIZE] = kpe_refs[i][...].astype(BF16)
    kb = kbuf_ref[...]
    s = (_dot_nt(ql, kb) + _dot(qp, rbuf_ref[...])) * MLA_SCALE
    _online_softmax_update(s, m_ref, l_ref, acc_ref, 0, lambda p: _dot(p, kb))

    @pl.when(g == n_g - 1)
    def _():
        kc = cn_ref[0].astype(BF16)
        s = (_dot_nt(ql, kc) + _dot(qp, kn_ref[0].astype(BF16))) * MLA_SCALE
        r = lax.broadcasted_iota(jnp.int32, s.shape, 0) // MLA_HEADS
        cidx = lax.broadcasted_iota(jnp.int32, s.shape, 1)
        s = jnp.where((cidx <= r) & (cidx < ls), s, -jnp.inf)
        _online_softmax_update(s, m_ref, l_ref, acc_ref, 0, lambda p: _dot(p, kc))
        l_w = jnp.concatenate([l_ref[0]] * (KV_LORA // LANES), axis=1)
        o_ref[0] = acc_ref[0] / l_w


def _decode_attention(page_table, ql, qp, cn, knt, cache_ckv, cache_kpet, layer, gp):
    nb, rows, _ = ql.shape
    n_pages = page_table.shape[1]
    ls = rows // MLA_HEADS
    body = functools.partial(_decode_body, gp=gp, ls=ls)

    def page_spec(shape, i):
        return pl.BlockSpec((None, None) + shape, lambda b, g, pt, i=i: (layer, pt[b, g * gp + i], 0, 0))

    per_b = lambda b, g, pt: (b, 0, 0)
    in_specs = [pl.BlockSpec((1, rows, KV_LORA), per_b), pl.BlockSpec((1, rows, QK_ROPE), per_b),
                pl.BlockSpec((1, PAGE_SIZE, KV_LORA), per_b), pl.BlockSpec((1, QK_ROPE, PAGE_SIZE), per_b)]
    in_specs += [page_spec((PAGE_SIZE, KV_LORA), i) for i in range(gp)]
    in_specs += [page_spec((QK_ROPE, PAGE_SIZE), i) for i in range(gp)]
    grid_spec = pltpu.PrefetchScalarGridSpec(
        num_scalar_prefetch=1,
        grid=(nb, n_pages // gp),
        in_specs=in_specs,
        out_specs=pl.BlockSpec((1, rows, KV_LORA), per_b),
        scratch_shapes=[pltpu.VMEM((gp * PAGE_SIZE, KV_LORA), BF16), pltpu.VMEM((QK_ROPE, gp * PAGE_SIZE), BF16),
                        pltpu.VMEM((1, rows, LANES), F32), pltpu.VMEM((1, rows, LANES), F32),
                        pltpu.VMEM((1, rows, KV_LORA), F32)],
    )
    return pl.pallas_call(
        body,
        grid_spec=grid_spec,
        out_shape=jax.ShapeDtypeStruct((nb, rows, KV_LORA), F32),
        compiler_params=_cparams(("parallel", "arbitrary"), 32),
        name="decode_attention",
    )(page_table, ql, qp, cn, knt, *([cache_ckv] * gp), *([cache_kpet] * gp))


def _decode_out_body(ol_ref, wuv_ref, wo_ref, r_ref, g_ref, b_ref, o_ref):
    y = None
    for h in range(MLA_HEADS):
        oh = _dot(ol_ref[h].astype(BF16), wuv_ref[h]).astype(BF16)
        t = _dot(oh, wo_ref[h * V_DIM:(h + 1) * V_DIM, :])
        y = t if y is None else y + t
    o_ref[...] = _layer_norm(ALPHA * r_ref[...] + y, g_ref[...], b_ref[...])


def _decode_out(ol, wuv3, wo, res, g, b):
    d = res.shape[1]
    args = (ol, wuv3, wo, res, g.reshape(1, d), b.reshape(1, d))
    return pl.pallas_call(
        _decode_out_body,
        grid=(1,),
        in_specs=[_resident(a.shape) for a in args],
        out_specs=_whole(res.shape),
        out_shape=jax.ShapeDtypeStruct(res.shape, F32),
        compiler_params=_cparams(("arbitrary",), 24),
        name="decode_out",
    )(*args)


def _rope_angles(pos, half):
    inv_freq = ROPE_THETA ** (-jnp.arange(half, dtype=F32) / half)
    ang = pos[:, None] * inv_freq[None, :]
    return jnp.cos(ang), jnp.sin(ang)


def _ret_tables(pos, chunk):
    cos, sin = _rope_angles(pos, RET_DK // 2)
    log_gamma = jnp.log(1.0 - 2.0 ** (-5.0 - jnp.arange(RET_HEADS, dtype=F32)))
    idx = jnp.arange(chunk, dtype=F32)
    diff = idx[:, None] - idx[None, :]
    dec = jnp.where(diff >= 0, jnp.exp(jnp.maximum(diff, 0.0)[None] * log_gamma[:, None, None]), 0.0)
    q_dec = jnp.exp((idx + 1.0)[None, :] * log_gamma[:, None])
    k_dec = jnp.exp((chunk - 1.0 - idx)[None, :] * log_gamma[:, None])
    gc = jnp.exp(chunk * log_gamma)
    return {
        "c2": jnp.concatenate([cos, cos], axis=1),
        "s2": jnp.concatenate([-sin, sin], axis=1),
        "dec": dec.astype(F32),
        "q_dec": q_dec,
        "k_dec": k_dec,
        "gc": jnp.broadcast_to(gc[:, None, None], (RET_HEADS, 1, LANES)).astype(F32),
    }


def _pe_tables(pos):
    cos, sin = _rope_angles(pos, QK_ROPE // 2)
    z = jnp.zeros_like(cos)
    return {"cc": jnp.concatenate([cos, cos, z, z], axis=1),
            "s1": jnp.concatenate([-sin, z, z, z], axis=1),
            "s2": jnp.concatenate([z, sin, z, z], axis=1)}


def _pad_heads(w, nope, rope):
    k = w.shape[0]
    w3 = w.reshape(k, MLA_HEADS, nope + rope)
    pad = jnp.zeros((k, MLA_HEADS, 2 * LANES - nope - rope), w.dtype)
    return jnp.concatenate([w3, pad], axis=2).reshape(k, MLA_HEADS * 2 * LANES)


def _tiles(seq):
    def fit(t):
        while seq % t:
            t //= 2
        return t
    return {"tok": fit(512), "mix": fit(256), "ffn": fit(512), "attn": fit(512), "ff_chunk": 256, "pages": 16}


def kernel(x_prompt, x_sample, state_pool, state_ret, cache_ckv, cache_kpe, state_conv, page_table,
           w_in_even, pool_w, pool_scale, ret_gn_g, w_o_even,
           w_dq, q_norm_g, w_uq, w_dkv, kv_norm_g, w_uk, w_uv, w_o_mla,
           w_up, conv_w, conv_b, w_down, ln_mix_g, ln_mix_b, ln_ffn_g, ln_ffn_b):
    bp, lp, d = x_prompt.shape
    bs, ls, _ = x_sample.shape
    past_len = page_table.shape[1] * PAGE_SIZE
    depth = w_up.shape[0]
    dff = w_down.shape[1]
    tl = _tiles(lp)
    ns = bs * ls
    assert lp % RET_CHUNK == 0 and ls % RET_CHUNK != 0 and tl["mix"] % RET_CHUNK == 0
    assert dff % tl["ff_chunk"] == 0 and page_table.shape[1] % tl["pages"] == 0 and ls <= PAGE_SIZE

    xp = x_prompt.reshape(bp * lp, d)
    xs = jnp.swapaxes(x_sample, 0, 1).reshape(ns, d)
    pos_p = jnp.arange(lp, dtype=F32)
    pos_s = past_len + jnp.arange(ls, dtype=F32)
    pos_s_rows = jnp.repeat(pos_s, bs)

    outs = {k: [] for k in ("pool_p", "pool_s", "ret_p", "ret_s", "ckv_p", "ckv_s", "kpe_p", "kpe_s",
                            "conv_p", "conv_s")}
    for layer in range(depth):
        if layer % 2 == 0:
            e = layer // 2
            w_in = w_in_even[e].astype(BF16)
            w_o = w_o_even[e].astype(BF16)
            pw = pool_w[e].astype(BF16)
            tp = _ret_tables(pos_p, RET_CHUNK)
            tp["qd"] = jnp.broadcast_to(tp["q_dec"][:, :, None], (RET_HEADS, RET_CHUNK, LANES))
            tp["kd"] = jnp.broadcast_to(tp["k_dec"][:, :, None], (RET_HEADS, RET_CHUNK, LANES))
            hp = _matmul(xp, w_in, tl["tok"])
            mix_p, pst, rst = _even_mix_prompt(hp, bp, lp, tp, pw, pool_scale[e], ret_gn_g[e], tl["mix"])
            outs["pool_p"].append(pst)
            outs["ret_p"].append(rst)
            xp = _matmul_res_ln(mix_p, w_o, xp, ln_mix_g[layer], ln_mix_b[layer], tl["tok"])
            ts = _ret_tables(pos_s_rows, ls)
            rb = jnp.arange(ns) % bs
            same = rb[:, None] == rb[None, :]
            dec_rows = jnp.repeat(jnp.repeat(ts["dec"], bs, axis=1), bs, axis=2)
            ts["dm"] = jnp.where(same[None], dec_rows, 0.0)
            ts["qd"] = jnp.broadcast_to(jnp.repeat(ts["q_dec"], bs, axis=1)[:, :, None], (RET_HEADS, ns, LANES))
            ts["kd"] = jnp.broadcast_to(jnp.repeat(ts["k_dec"], bs, axis=1)[:, :, None], (RET_HEADS, ns, LANES))
            hs = _matmul(xs, w_in, ns)
            hist = jnp.swapaxes(state_pool[e], 0, 1)
            mix_s, hist_new, s_new = _even_mix_sample(hs, hist, state_ret[e], ts, pw, pool_scale[e],
                                                      ret_gn_g[e], bs, ls, past_len)
            outs["pool_s"].append(jnp.swapaxes(hist_new, 0, 1))
            outs["ret_s"].append(s_new)
            xs = _matmul_res_ln(mix_s, w_o, xs, ln_mix_g[layer], ln_mix_b[layer], ns)
        else:
            o = layer // 2
            hw = 2 * LANES
            w = {
                "dq": w_dq[o].astype(BF16),
                "qg": q_norm_g[o].reshape(1, -1),
                "uq": _pad_heads(w_uq[o], QK_NOPE, QK_ROPE).astype(BF16),
                "dkv": jnp.pad(w_dkv[o], ((0, 0), (0, KV_LORA + LANES - w_dkv.shape[2]))).astype(BF16),
                "kvg": kv_norm_g[o].reshape(1, -1),
                "uk": w_uk[o].reshape(KV_LORA, MLA_HEADS * QK_NOPE).astype(BF16),
                "uv": w_uv[o].reshape(KV_LORA, MLA_HEADS * V_DIM).astype(BF16),
                "ukt": jnp.transpose(w_uk[o], (1, 2, 0)).astype(BF16),
            }
            w_o = w_o_mla[o].astype(BF16)
            qp, kp, vp, ckv_p, kpe_p = _mla_proj(xp, _pe_tables(pos_p), w, tl["tok"], lp // tl["tok"], False)
            att = _flash_prompt(qp, kp, vp, bp, lp, tl["attn"])
            outs["ckv_p"].append(ckv_p.reshape(bp, lp, KV_LORA))
            outs["kpe_p"].append(kpe_p.reshape(bp, lp, QK_ROPE))
            xp = _matmul_res_ln(att, w_o, xp, ln_mix_g[layer], ln_mix_b[layer], tl["tok"])
            ql, qpe, ckv_s, kpe_s = _mla_proj(xs, _pe_tables(pos_s_rows), w, ns, 1, True)
            rows = ls * MLA_HEADS

            def per_batch(a, width):
                return jnp.transpose(a.reshape(ls, bs, MLA_HEADS, width), (1, 0, 2, 3)).reshape(bs, rows, width)

            ql_b = per_batch(ql, KV_LORA)
            qp_b = per_batch(qpe, LANES)[:, :, 0:QK_ROPE]
            ckv_sb = jnp.swapaxes(ckv_s.reshape(ls, bs, KV_LORA), 0, 1)
            kpe_sb = jnp.swapaxes(kpe_s.reshape(ls, bs, QK_ROPE), 0, 1)
            cn = jnp.pad(ckv_sb, ((0, 0), (0, PAGE_SIZE - ls), (0, 0)))
            knt = jnp.swapaxes(jnp.pad(kpe_sb, ((0, 0), (0, PAGE_SIZE - ls), (0, 0))), 1, 2)
            o_lat = _decode_attention(page_table, ql_b, qp_b, cn, knt, cache_ckv,
                                      jnp.swapaxes(cache_kpe, 2, 3), o, tl["pages"])
            outs["ckv_s"].append(ckv_sb)
            outs["kpe_s"].append(kpe_sb)
            ol = jnp.transpose(o_lat.reshape(bs, ls, MLA_HEADS, KV_LORA), (2, 1, 0, 3)).reshape(MLA_HEADS, ns, KV_LORA)
            wuv3 = jnp.transpose(w_uv[o], (1, 0, 2)).astype(BF16)
            xs = _decode_out(ol, wuv3, w_o, xs, ln_mix_g[layer], ln_mix_b[layer])
        wup = w_up[layer].astype(BF16)
        wd = w_down[layer].astype(BF16)
        xp, st_p = _ffn_prompt(xp, bp, lp, wup, wd, conv_w[layer], conv_b[layer],
                               ln_ffn_g[layer], ln_ffn_b[layer], tl["ffn"], tl["ff_chunk"])
        outs["conv_p"].append(st_p[:, SUBLANES - (CONV_W - 1):, :])
        st_s = jnp.swapaxes(state_conv[layer], 0, 1)
        xs, st_s_new = _ffn_sample(xs, st_s, wup, wd, conv_w[layer], conv_b[layer],
                                   ln_ffn_g[layer], ln_ffn_b[layer], bs, ls, tl["ff_chunk"])
        outs["conv_s"].append(jnp.swapaxes(st_s_new, 0, 1))

    y_p = xp.reshape(bp, lp, d)
    y_s = jnp.swapaxes(xs.reshape(ls, bs, d), 0, 1)
    return (y_p, y_s,
            jnp.stack(outs["pool_p"]), jnp.stack(outs["pool_s"]),
            jnp.stack(outs["ret_p"]), jnp.stack(outs["ret_s"]),
            jnp.stack(outs["ckv_p"]), jnp.stack(outs["ckv_s"]),
            jnp.stack(outs["kpe_p"]), jnp.stack(outs["kpe_s"]),
            jnp.stack(outs["conv_p"]), jnp.stack(outs["conv_s"]))
```

```python
import functools

import jax
import jax.numpy as jnp
from jax import lax
from jax.experimental import pallas as pl
from jax.experimental.pallas import tpu as pltpu

F32 = jnp.float32
BF16 = jnp.bfloat16

PAGE_SIZE = 128
POOL_WINDOWS = (2, 4, 8, 16)
POOL_HIST = max(POOL_WINDOWS) - 1
RET_HEADS = 4
RET_DK = 128
RET_CHUNK = 128
MLA_HEADS = 8
QK_NOPE = 128
QK_ROPE = 64
V_DIM = 128
KV_LORA = 256
CONV_W = 3
DEPTH = 2
ALPHA = (2.0 * DEPTH) ** 0.25
ROPE_THETA = 10000.0
LN_EPS = 1e-5
RMS_EPS = 1e-6
GN_EPS = 1e-6
MLA_SCALE = (QK_NOPE + QK_ROPE) ** -0.5

LANES = 128
SUBLANES = 8
MIB = 1024 * 1024


def _cparams(sem, vmem_mib):
    return pltpu.CompilerParams(dimension_semantics=sem, vmem_limit_bytes=int(vmem_mib * MIB))


def _resident(shape):
    nd = len(shape)
    return pl.BlockSpec(shape, lambda *_: (0,) * nd, pipeline_mode=pl.Buffered(1))


def _resident_layer(stacked_shape, layer):
    nd = len(stacked_shape) - 1
    return pl.BlockSpec((None,) + tuple(stacked_shape[1:]), lambda *_: (layer,) + (0,) * nd,
                        pipeline_mode=pl.Buffered(1))


def _whole(shape):
    nd = len(shape)
    return pl.BlockSpec(shape, lambda *_: (0,) * nd)


def _dot(a, b):
    return jnp.dot(a, b, preferred_element_type=F32)


def _dot_nt(a, b):
    return lax.dot_general(a, b, (((1,), (1,)), ((), ())), preferred_element_type=F32)


def _dot_tn(a, b):
    return lax.dot_general(a, b, (((0,), (0,)), ((), ())), preferred_element_type=F32)


def _layer_norm(z, g, b):
    mu = jnp.mean(z, axis=-1, keepdims=True)
    d = z - mu
    var = jnp.mean(d * d, axis=-1, keepdims=True)
    return d * lax.rsqrt(var + LN_EPS) * g + b


def _silu(x):
    return x * jax.nn.sigmoid(x)


def _mm_body(x_ref, w_ref, o_ref):
    o_ref[...] = _dot(x_ref[...].astype(BF16), w_ref[...]).astype(o_ref.dtype)


def _matmul(x, w, tm, out_dtype=F32):
    m, k = x.shape
    n = w.shape[1]
    return pl.pallas_call(
        _mm_body,
        grid=(m // tm,),
        in_specs=[pl.BlockSpec((tm, k), lambda i: (i, 0)), _resident((k, n))],
        out_specs=pl.BlockSpec((tm, n), lambda i: (i, 0)),
        out_shape=jax.ShapeDtypeStruct((m, n), out_dtype),
        compiler_params=_cparams(("parallel",), 40),
        name="matmul",
    )(x, w)


def _mm_ln_body(a_ref, w_ref, r_ref, g_ref, b_ref, o_ref):
    y = _dot(a_ref[...].astype(BF16), w_ref[...])
    o_ref[...] = _layer_norm(ALPHA * r_ref[...] + y, g_ref[...], b_ref[...])


def _matmul_res_ln(a, w, res, g, b, tm):
    m, k = a.shape
    n = w.shape[1]
    return pl.pallas_call(
        _mm_ln_body,
        grid=(m // tm,),
        in_specs=[pl.BlockSpec((tm, k), lambda i: (i, 0)), _resident((k, n)),
                  pl.BlockSpec((tm, n), lambda i: (i, 0)), _resident((1, n)), _resident((1, n))],
        out_specs=pl.BlockSpec((tm, n), lambda i: (i, 0)),
        out_shape=jax.ShapeDtypeStruct((m, n), F32),
        compiler_params=_cparams(("parallel",), 32),
        name="matmul_res_ln",
    )(a, w, res, g.reshape(1, n), b.reshape(1, n))


def _rope_full(x, c2, s2):
    return x * c2 + pltpu.roll(x, 64, axis=1) * s2


def _group_norm_gate(o, gate, gn_row):
    mu = jnp.mean(o, axis=-1, keepdims=True)
    d = o - mu
    var = jnp.mean(d * d, axis=-1, keepdims=True)
    return _silu(gate) * (d * lax.rsqrt(var + GN_EPS) * gn_row)


def _even_layer_prompt_body(x_ref, win_ref, c2_ref, s2_ref, dec_ref, qd_ref, kd_ref, gc_ref, pw_ref, ps_ref,
                            gn_ref, wo_ref, lg_ref, lb_ref, o_ref, pst_ref, rst_ref,
                            h_ref, mix_ref, ext_ref, s_ref, *, tm, n_j):
    j = pl.program_id(1)
    pd = len(POOL_WINDOWS) * LANES

    x = x_ref[...]
    xb = x.astype(BF16)
    nblk = 4 * LANES
    for c0 in range(0, h_ref.shape[1], nblk):
        h_ref[:, c0:c0 + nblk] = _dot(xb, win_ref[:, c0:c0 + nblk])

    @pl.when(j == 0)
    def _():
        ext_ref[0:16, :] = jnp.zeros((16, pd), F32)
        s_ref[...] = jnp.zeros(s_ref.shape, F32)

    @pl.when(j > 0)
    def _():
        ext_ref[0:16, :] = ext_ref[tm:tm + 16, :]

    ext_ref[16:16 + tm, :] = h_ref[:, 0:pd]

    pos = (j * tm + lax.broadcasted_iota(jnp.int32, (tm, 1), 0)).astype(F32)
    for g, w in enumerate(POOL_WINDOWS):
        cols = slice(g * LANES, (g + 1) * LANES)
        e = ext_ref[:, cols]
        u = e[16:, :]
        s = 1
        while s < w:
            e = e + pltpu.roll(e, s, axis=0)
            s *= 2
        cnt = jnp.minimum(float(w), pos + 1.0)
        pooled = e[16:, :] / cnt - u
        mixed = _dot(pooled.astype(BF16), pw_ref[g]) * ps_ref[:, cols]
        mix_ref[:, cols] = mixed.astype(BF16)

    c = RET_CHUNK
    k_scale = RET_DK ** -0.5
    for ci in range(tm // c):
        rows = slice(ci * c, (ci + 1) * c)
        c2 = c2_ref[rows, :]
        s2 = s2_ref[rows, :]
        for hd in range(RET_HEADS):
            def col(part, hd=hd):
                return slice(pd + (part * RET_HEADS + hd) * LANES, pd + (part * RET_HEADS + hd + 1) * LANES)
            q = _rope_full(h_ref[rows, col(0)], c2, s2)
            k = _rope_full(h_ref[rows, col(1)], c2, s2) * k_scale
            vb = h_ref[rows, col(2)].astype(BF16)
            gate = h_ref[rows, col(3)]
            st = s_ref[hd]
            sc = _dot_nt(q.astype(BF16), k.astype(BF16)) * dec_ref[hd]
            o = _dot(sc.astype(BF16), vb)
            o = o + _dot((q * qd_ref[hd]).astype(BF16), st.astype(BF16))
            s_ref[hd] = gc_ref[hd] * st + _dot_tn((k * kd_ref[hd]).astype(BF16), vb)
            ret = _group_norm_gate(o, gate, gn_ref[:, hd * LANES:(hd + 1) * LANES])
            mix_ref[rows, pd + hd * LANES:pd + (hd + 1) * LANES] = ret.astype(BF16)

    @pl.when(j == n_j - 1)
    def _():
        pst_ref[0] = ext_ref[pl.ds(tm + 1, POOL_HIST), :]
        rst_ref[0] = s_ref[...]

    y = _dot(mix_ref[...], wo_ref[...])
    o_ref[...] = _layer_norm(ALPHA * x + y, lg_ref[...], lb_ref[...])


def _even_layer_prompt(x, bsz, seq, w_in, tabs, pool_w, pool_scale, gn_g, w_o, ln_g, ln_b, tm):
    n_j = seq // tm
    d = x.shape[1]
    pd = pool_scale.shape[0]
    ed = pd + RET_HEADS * LANES
    body = functools.partial(_even_layer_prompt_body, tm=tm, n_j=n_j)
    return pl.pallas_call(
        body,
        grid=(bsz, n_j),
        in_specs=[
            pl.BlockSpec((tm, d), lambda b, j: (b * n_j + j, 0)),
            _resident(w_in.shape),
            pl.BlockSpec((tm, LANES), lambda b, j: (j, 0)),
            pl.BlockSpec((tm, LANES), lambda b, j: (j, 0)),
            _resident((RET_HEADS, RET_CHUNK, RET_CHUNK)),
            _resident((RET_HEADS, RET_CHUNK, LANES)),
            _resident((RET_HEADS, RET_CHUNK, LANES)),
            _resident((RET_HEADS, 1, LANES)),
            _resident(pool_w.shape),
            _resident((1, pd)),
            _resident((1, RET_HEADS * LANES)),
            _resident(w_o.shape),
            _resident((1, d)),
            _resident((1, d)),
        ],
        out_specs=[
            pl.BlockSpec((tm, d), lambda b, j: (b * n_j + j, 0)),
            pl.BlockSpec((1, POOL_HIST, pd), lambda b, j: (b, 0, 0)),
            pl.BlockSpec((1, RET_HEADS, RET_DK, LANES), lambda b, j: (b, 0, 0, 0)),
        ],
        out_shape=[
            jax.ShapeDtypeStruct((bsz * seq, d), F32),
            jax.ShapeDtypeStruct((bsz, POOL_HIST, pd), F32),
            jax.ShapeDtypeStruct((bsz, RET_HEADS, RET_DK, LANES), F32),
        ],
        scratch_shapes=[pltpu.VMEM((tm, w_in.shape[1]), F32), pltpu.VMEM((tm, ed), BF16),
                        pltpu.VMEM((tm + 16, pd), F32), pltpu.VMEM((RET_HEADS, RET_DK, LANES), F32)],
        compiler_params=_cparams(("parallel", "arbitrary"), 40),
        name="even_layer_prompt",
    )(x, w_in, tabs["c2"], tabs["s2"], tabs["dec"], tabs["qd"], tabs["kd"], tabs["gc"],
      pool_w, pool_scale.reshape(1, pd), gn_g.reshape(1, -1), w_o, ln_g.reshape(1, d), ln_b.reshape(1, d))


def _even_mix_sample_body(h_ref, hist_ref, s0_ref, c2_ref, s2_ref, dm_ref, qd_ref, kd_ref, gc_ref,
                          pw_ref, ps_ref, gn_ref, mix_ref, hist_o_ref, s_o_ref, oc_ref,
                          *, nb, ls, cnts):
    pd = len(POOL_WINDOWS) * LANES
    ext = [hist_ref[i] for i in range(POOL_HIST)]
    ext += [h_ref[l * nb:(l + 1) * nb, 0:pd] for l in range(ls)]
    for i in range(POOL_HIST):
        hist_o_ref[i] = ext[ls + i]
    for g, w in enumerate(POOL_WINDOWS):
        cols = slice(g * LANES, (g + 1) * LANES)
        outs = []
        for l in range(ls):
            top = POOL_HIST + l
            acc = ext[top][:, cols]
            for jj in range(1, w):
                acc = acc + ext[top - jj][:, cols]
            outs.append(acc / cnts[g][l] - ext[top][:, cols])
        pooled = jnp.concatenate(outs, axis=0)
        mixed = _dot(pooled.astype(BF16), pw_ref[g]) * ps_ref[:, cols]
        mix_ref[:, cols] = mixed.astype(BF16)

    rows_b = lax.broadcasted_iota(jnp.int32, (ls * nb, 1), 0) % nb
    k_scale = RET_DK ** -0.5
    c2 = c2_ref[...]
    s2 = s2_ref[...]
    qs, ks, vs = [], [], []
    for hd in range(RET_HEADS):
        def col(part, hd=hd):
            return slice(pd + (part * RET_HEADS + hd) * LANES, pd + (part * RET_HEADS + hd + 1) * LANES)
        q = _rope_full(h_ref[:, col(0)], c2, s2)
        k = _rope_full(h_ref[:, col(1)], c2, s2) * k_scale
        vb = h_ref[:, col(2)].astype(BF16)
        sc = _dot_nt(q.astype(BF16), k.astype(BF16)) * dm_ref[hd]
        oc_ref[hd] = _dot(sc.astype(BF16), vb)
        qs.append(q * qd_ref[hd])
        ks.append(k * kd_ref[hd])
        vs.append(vb)

    def per_batch(b, carry):
        sel = rows_b == b
        for hd in range(RET_HEADS):
            st = s0_ref[b, hd]
            qm = jnp.where(sel, qs[hd], 0.0).astype(BF16)
            km = jnp.where(sel, ks[hd], 0.0).astype(BF16)
            oc_ref[hd] += _dot(qm, st.astype(BF16))
            s_o_ref[b, hd] = gc_ref[hd] * st + _dot_tn(km, vs[hd])
        return carry

    lax.fori_loop(0, nb, per_batch, 0)

    for hd in range(RET_HEADS):
        gate = h_ref[:, pd + (3 * RET_HEADS + hd) * LANES:pd + (3 * RET_HEADS + hd + 1) * LANES]
        ret = _group_norm_gate(oc_ref[hd], gate, gn_ref[:, hd * LANES:(hd + 1) * LANES])
        mix_ref[:, pd + hd * LANES:pd + (hd + 1) * LANES] = ret.astype(BF16)


def _even_mix_sample(h, hist, s0, tabs, pool_w, pool_scale, gn_g, nb, ls, past_len):
    pd = pool_scale.shape[0]
    ed = pd + RET_HEADS * LANES
    cnts = tuple(tuple(float(min(w, past_len + l + 1)) for l in range(ls)) for w in POOL_WINDOWS)
    body = functools.partial(_even_mix_sample_body, nb=nb, ls=ls, cnts=cnts)
    n = ls * nb
    args = (h, hist, s0, tabs["c2"], tabs["s2"], tabs["dm"], tabs["qd"], tabs["kd"], tabs["gc"],
            pool_w, pool_scale.reshape(1, pd), gn_g.reshape(1, -1))
    return pl.pallas_call(
        body,
        grid=(1,),
        in_specs=[_resident(a.shape) for a in args],
        out_specs=[_whole((n, ed)), _whole(hist.shape), _whole(s0.shape)],
        out_shape=[jax.ShapeDtypeStruct((n, ed), BF16),
                   jax.ShapeDtypeStruct(hist.shape, F32),
                   jax.ShapeDtypeStruct(s0.shape, F32)],
        scratch_shapes=[pltpu.VMEM((RET_HEADS, n, LANES), F32)],
        compiler_params=_cparams(("arbitrary",), 48),
        name="even_mix_sample",
    )(*args)


def _ffn_chunk(xb, wup_ref, wd_ref, cw_ref, cb_ref, c, tf, dff, shift_fn):
    cols = slice(c * tf, (c + 1) * tf)
    a = _dot(xb, wup_ref[:, cols])
    gate_in = _dot(xb, wup_ref[:, dff + c * tf:dff + (c + 1) * tf])
    a1, a2 = shift_fn(a, c)
    conv = cb_ref[:, cols] + cw_ref[0:1, cols] * a2
    conv = conv + cw_ref[1:2, cols] * a1
    conv = conv + cw_ref[2:3, cols] * a
    act = (_silu(conv) * gate_in).astype(BF16)
    return a, _dot(act, wd_ref[cols, :])


def _ffn_prompt_body(x_ref, wup_ref, wd_ref, cw_ref, cb_ref, g_ref, b_ref, o_ref, st_ref, carry_ref,
                     *, tm, n_j, tf, dff):
    j = pl.program_id(1)

    @pl.when(j == 0)
    def _():
        carry_ref[...] = jnp.zeros(carry_ref.shape, F32)

    x = x_ref[...]
    xb = x.astype(BF16)
    row = lax.broadcasted_iota(jnp.int32, (tm, tf), 0)

    def shift_fn(a, c):
        prev = carry_ref[:, c * tf:(c + 1) * tf]
        a1 = jnp.where(row == 0, prev[7:8, :], pltpu.roll(a, 1, axis=0))
        a2 = jnp.where(row == 0, prev[6:7, :], jnp.where(row == 1, prev[7:8, :], pltpu.roll(a, 2, axis=0)))
        return a1, a2

    acc = None
    for c in range(dff // tf):
        a, y = _ffn_chunk(xb, wup_ref, wd_ref, cw_ref, cb_ref, c, tf, dff, shift_fn)
        acc = y if acc is None else acc + y
        tail = a[tm - SUBLANES:tm, :]
        carry_ref[:, c * tf:(c + 1) * tf] = tail
        st_ref[0, :, c * tf:(c + 1) * tf] = tail
    o_ref[...] = _layer_norm(ALPHA * x + acc, g_ref[...], b_ref[...])


def _ffn_prompt(x, bsz, seq, layer, wup, wd, conv_w, conv_b, g, b, tm, tf):
    d = x.shape[1]
    dff = wd.shape[1]
    n_j = seq // tm
    body = functools.partial(_ffn_prompt_body, tm=tm, n_j=n_j, tf=tf, dff=dff)
    return pl.pallas_call(
        body,
        grid=(bsz, n_j),
        in_specs=[pl.BlockSpec((tm, d), lambda bi, j: (bi * n_j + j, 0)),
                  _resident_layer(wup.shape, layer), _resident_layer(wd.shape, layer), _resident(conv_w.shape),
                  _resident((1, dff)), _resident((1, d)), _resident((1, d))],
        out_specs=[pl.BlockSpec((tm, d), lambda bi, j: (bi * n_j + j, 0)),
                   pl.BlockSpec((1, SUBLANES, dff), lambda bi, j: (bi, 0, 0))],
        out_shape=[jax.ShapeDtypeStruct(x.shape, F32),
                   jax.ShapeDtypeStruct((bsz, SUBLANES, dff), F32)],
        scratch_shapes=[pltpu.VMEM((SUBLANES, dff), F32)],
        compiler_params=_cparams(("parallel", "arbitrary"), 52),
        name="ffn_prompt",
    )(x, wup, wd, conv_w, conv_b.reshape(1, dff), g.reshape(1, d), b.reshape(1, d))


def _ffn_sample_body(x_ref, st_ref, wup_ref, wd_ref, cw_ref, cb_ref, g_ref, b_ref, o_ref, st_o_ref,
                     *, nb, ls, tf, dff):
    x = x_ref[...]
    xb = x.astype(BF16)
    nh = CONV_W - 1

    def shift_fn(a, c):
        cols = slice(c * tf, (c + 1) * tf)
        ext = [st_ref[i, :, cols] for i in range(nh)] + [a[l * nb:(l + 1) * nb, :] for l in range(ls)]
        a1 = jnp.concatenate([ext[nh + l - 1] for l in range(ls)], axis=0)
        a2 = jnp.concatenate([ext[nh + l - 2] for l in range(ls)], axis=0)
        for i in range(nh):
            st_o_ref[i, :, cols] = ext[ls + i]
        return a1, a2

    acc = None
    for c in range(dff // tf):
        _, y = _ffn_chunk(xb, wup_ref, wd_ref, cw_ref, cb_ref, c, tf, dff, shift_fn)
        acc = y if acc is None else acc + y
    o_ref[...] = _layer_norm(ALPHA * x + acc, g_ref[...], b_ref[...])


def _ffn_sample(x, st, layer, wup, wd, conv_w, conv_b, g, b, nb, ls, tf):
    d = x.shape[1]
    dff = wd.shape[1]
    body = functools.partial(_ffn_sample_body, nb=nb, ls=ls, tf=tf, dff=dff)
    args = (x, st, wup, wd, conv_w, conv_b.reshape(1, dff), g.reshape(1, d), b.reshape(1, d))
    in_specs = [_resident(a.shape) for a in args]
    in_specs[2] = _resident_layer(wup.shape, layer)
    in_specs[3] = _resident_layer(wd.shape, layer)
    return pl.pallas_call(
        body,
        grid=(1,),
        in_specs=in_specs,
        out_specs=[_whole(x.shape), _whole(st.shape)],
        out_shape=[jax.ShapeDtypeStruct(x.shape, F32), jax.ShapeDtypeStruct(st.shape, F32)],
        compiler_params=_cparams(("arbitrary",), 40),
        name="ffn_sample",
    )(*args)


def _rope_pe(blk, cc, s1, s2):
    return blk * cc + pltpu.roll(blk, 96, axis=1) * s1 + pltpu.roll(blk, 32, axis=1) * s2


def _rms_norm(x, g):
    ms = jnp.mean(x * x, axis=-1, keepdims=True)
    return x * lax.rsqrt(ms + RMS_EPS) * g


def _mla_proj_body(x_ref, cc_ref, s1_ref, s2_ref, wdq_ref, qg_ref, wuq_ref, wdkv_ref, kvg_ref, *rest,
                   decode):
    hw = 2 * LANES
    cc = cc_ref[...]
    s1 = s1_ref[...]
    s2 = s2_ref[...]
    xb = x_ref[...].astype(BF16)
    cq = _rms_norm(_dot(xb, wdq_ref[...]), qg_ref[...])
    q = _dot(cq.astype(BF16), wuq_ref[...])
    kv = _dot(xb, wdkv_ref[...])
    ckv = _rms_norm(kv[:, 0:KV_LORA], kvg_ref[...])
    kpe = _rope_pe(kv[:, KV_LORA:KV_LORA + LANES], cc, s1, s2)
    if decode:
        wukt_ref, ql_ref, qp_ref, ckv_ref, kpe_ref = rest
        for h in range(MLA_HEADS):
            qn = q[:, h * hw:h * hw + LANES].astype(BF16)
            ql_ref[:, h * KV_LORA:(h + 1) * KV_LORA] = _dot(qn, wukt_ref[h]).astype(BF16)
            qp = _rope_pe(q[:, h * hw + LANES:(h + 1) * hw], cc, s1, s2)
            qp_ref[:, h * LANES:(h + 1) * LANES] = qp.astype(BF16)
    else:
        wuk_ref, wuv_ref, qo_ref, ko_ref, vo_ref, ckv_ref, kpe_ref = rest
        cb = ckv.astype(BF16)
        kn = _dot(cb, wuk_ref[...])
        vo_ref[...] = _dot(cb, wuv_ref[...]).astype(BF16)
        kpb = kpe.astype(BF16)
        for h in range(MLA_HEADS):
            qo_ref[:, h * hw:h * hw + LANES] = q[:, h * hw:h * hw + LANES].astype(BF16)
            qp = _rope_pe(q[:, h * hw + LANES:(h + 1) * hw], cc, s1, s2)
            qo_ref[:, h * hw + LANES:(h + 1) * hw] = qp.astype(BF16)
            ko_ref[:, h * hw:h * hw + LANES] = kn[:, h * LANES:(h + 1) * LANES].astype(BF16)
            ko_ref[:, h * hw + LANES:(h + 1) * hw] = kpb
    ckv_ref[...] = ckv
    kpe_ref[...] = kpe[:, 0:QK_ROPE]


def _mla_proj(x, tabs, w, tm, n_pos_blocks, decode):
    t, d = x.shape
    hw = 2 * LANES
    body = functools.partial(_mla_proj_body, decode=decode)
    row = lambda i: (i, 0)
    tab = lambda i: (i % n_pos_blocks, 0)
    ins = [x, tabs["cc"], tabs["s1"], tabs["s2"], w["dq"], w["qg"], w["uq"], w["dkv"], w["kvg"]]
    in_specs = [pl.BlockSpec((tm, d), row)] + [pl.BlockSpec((tm, LANES), tab)] * 3
    in_specs += [_resident(a.shape) for a in ins[4:]]
    if decode:
        ins += [w["ukt"]]
        in_specs += [_resident(w["ukt"].shape)]
        outs = [(MLA_HEADS * KV_LORA, BF16), (MLA_HEADS * LANES, BF16)]
    else:
        ins += [w["uk"], w["uv"]]
        in_specs += [_resident(w["uk"].shape), _resident(w["uv"].shape)]
        outs = [(MLA_HEADS * hw, BF16), (MLA_HEADS * hw, BF16), (MLA_HEADS * V_DIM, BF16)]
    outs += [(KV_LORA, F32), (QK_ROPE, F32)]
    return pl.pallas_call(
        body,
        grid=(t // tm,),
        in_specs=in_specs,
        out_specs=[pl.BlockSpec((tm, n), row) for n, _ in outs],
        out_shape=[jax.ShapeDtypeStruct((t, n), dt) for n, dt in outs],
        compiler_params=_cparams(("parallel",), 40),
        name="mla_proj_decode" if decode else "mla_proj",
    )(*ins)


def _online_softmax_update(s, m_ref, l_ref, acc_ref, idx, pv_fn):
    n = s.shape[1] // LANES
    tiles = [s[:, j * LANES:(j + 1) * LANES] for j in range(n)]
    mx = tiles[0]
    for t in tiles[1:]:
        mx = jnp.maximum(mx, t)
    m_prev = m_ref[idx]
    m_new = jnp.maximum(m_prev, jnp.max(mx, axis=-1, keepdims=True))
    alpha = jnp.exp(m_prev - m_new)
    ps = [jnp.exp(t - m_new) for t in tiles]
    rs = ps[0]
    for t in ps[1:]:
        rs = rs + t
    l_ref[idx] = alpha * l_ref[idx] + jnp.sum(rs, axis=-1, keepdims=True)
    p = jnp.concatenate(ps, axis=1) if n > 1 else ps[0]
    pv = pv_fn(p.astype(BF16))
    a_w = alpha if acc_ref.shape[-1] == LANES else jnp.concatenate([alpha] * (acc_ref.shape[-1] // LANES), axis=1)
    acc_ref[idx] = a_w * acc_ref[idx] + pv
    m_ref[idx] = m_new


def _flash_body(q_ref, k_ref, v_ref, o_ref, m_ref, l_ref, acc_ref, *, tq):
    qi = pl.program_id(1)
    hw = 2 * LANES
    m_ref[...] = jnp.full(m_ref.shape, -jnp.inf, F32)
    l_ref[...] = jnp.zeros(l_ref.shape, F32)
    acc_ref[...] = jnp.zeros(acc_ref.shape, F32)

    def step(h, kj, masked):
        start = pl.multiple_of(kj * tq, tq)
        q = q_ref[:, h * hw:(h + 1) * hw]
        k = k_ref[pl.ds(start, tq), h * hw:(h + 1) * hw]
        v = v_ref[pl.ds(start, tq), h * V_DIM:(h + 1) * V_DIM]
        s = _dot_nt(q, k) * MLA_SCALE
        if masked:
            row = lax.broadcasted_iota(jnp.int32, (tq, tq), 0)
            colm = lax.broadcasted_iota(jnp.int32, (tq, tq), 1)
            s = jnp.where(colm <= row, s, -jnp.inf)
        _online_softmax_update(s, m_ref, l_ref, acc_ref, h, lambda p: _dot(p, v))

    def loop_body(kj, carry):
        for h in range(MLA_HEADS):
            step(h, kj, False)
        return carry

    lax.fori_loop(0, qi, loop_body, 0)
    for h in range(MLA_HEADS):
        step(h, qi, True)
        o_ref[:, h * V_DIM:(h + 1) * V_DIM] = (acc_ref[h] / l_ref[h]).astype(BF16)


def _flash_prompt(qp, kp, vp, bsz, seq, tq):
    nq = seq // tq
    body = functools.partial(_flash_body, tq=tq)
    return pl.pallas_call(
        body,
        grid=(bsz, nq),
        in_specs=[pl.BlockSpec((tq, qp.shape[1]), lambda b, i: (b * nq + i, 0)),
                  pl.BlockSpec((seq, kp.shape[1]), lambda b, i: (b, 0)),
                  pl.BlockSpec((seq, vp.shape[1]), lambda b, i: (b, 0))],
        out_specs=pl.BlockSpec((tq, vp.shape[1]), lambda b, i: (b * nq + i, 0)),
        out_shape=jax.ShapeDtypeStruct(vp.shape, BF16),
        scratch_shapes=[pltpu.VMEM((MLA_HEADS, tq, LANES), F32), pltpu.VMEM((MLA_HEADS, tq, LANES), F32),
                        pltpu.VMEM((MLA_HEADS, tq, V_DIM), F32)],
        compiler_params=_cparams(("parallel", "arbitrary"), 56),
        name="flash_prompt",
    )(qp, kp, vp)


def _decode_body(pt_ref, ql_ref, qp_ref, cn_ref, kn_ref, ckv_hbm, kpe_hbm, o_ref,
                 cbuf_ref, rbuf_ref, sem_ref, m_ref, l_ref, acc_ref, *, gp, ls, layer):
    b = pl.program_id(0)
    g = pl.program_id(1)
    n_b = pl.num_programs(0)
    n_g = pl.num_programs(1)

    def page_copies(bb, grp, slot):
        cps = []
        for i in range(gp):
            pg = pt_ref[bb, grp * gp + i]
            keys = pl.ds(i * PAGE_SIZE, PAGE_SIZE)
            cps.append(pltpu.make_async_copy(ckv_hbm.at[layer, pg], cbuf_ref.at[slot, keys, :], sem_ref.at[slot, 0]))
            cps.append(pltpu.make_async_copy(kpe_hbm.at[layer, pg], rbuf_ref.at[slot, :, keys], sem_ref.at[slot, 1]))
        return cps

    def start(bb, grp, slot):
        for cp in page_copies(bb, grp, slot):
            cp.start()

    def wait(slot):
        for cp in page_copies(b, 2 * g + slot, slot):
            cp.wait()

    @pl.when((b == 0) & (g == 0))
    def _():
        start(b, 0, 0)

    start(b, 2 * g + 1, 1)

    @pl.when(g == 0)
    def _():
        m_ref[...] = jnp.full(m_ref.shape, -jnp.inf, F32)
        l_ref[...] = jnp.zeros(l_ref.shape, F32)
        acc_ref[...] = jnp.zeros(acc_ref.shape, F32)

    ql = ql_ref[0]
    qp = qp_ref[0]

    def consume(slot):
        kb = cbuf_ref[slot].astype(BF16)
        s = (_dot_nt(ql, kb) + _dot(qp, rbuf_ref[slot].astype(BF16))) * MLA_SCALE
        _online_softmax_update(s, m_ref, l_ref, acc_ref, 0, lambda p: _dot(p, kb))

    wait(0)
    consume(0)

    @pl.when(g < n_g - 1)
    def _():
        start(b, 2 * g + 2, 0)

    @pl.when((g == n_g - 1) & (b < n_b - 1))
    def _():
        start(b + 1, 0, 0)

    wait(1)
    consume(1)

    @pl.when(g == n_g - 1)
    def _():
        kc = cn_ref[0].astype(BF16)
        s = (_dot_nt(ql, kc) + _dot(qp, kn_ref[0].astype(BF16))) * MLA_SCALE
        r = lax.broadcasted_iota(jnp.int32, s.shape, 0) // MLA_HEADS
        cidx = lax.broadcasted_iota(jnp.int32, s.shape, 1)
        s = jnp.where((cidx <= r) & (cidx < ls), s, -jnp.inf)
        _online_softmax_update(s, m_ref, l_ref, acc_ref, 0, lambda p: _dot(p, kc))
        l_w = jnp.concatenate([l_ref[0]] * (KV_LORA // LANES), axis=1)
        o_ref[0] = acc_ref[0] / l_w


def _decode_attention(page_table, ql, qp, cn, knt, cache_ckv, cache_kpet, layer, gp):
    nb, rows, _ = ql.shape
    n_pages = page_table.shape[1]
    ls = rows // MLA_HEADS
    body = functools.partial(_decode_body, gp=gp, ls=ls, layer=layer)
    per_b = lambda b, g, pt: (b, 0, 0)
    in_specs = [pl.BlockSpec((1, rows, KV_LORA), per_b), pl.BlockSpec((1, rows, QK_ROPE), per_b),
                pl.BlockSpec((1, PAGE_SIZE, KV_LORA), per_b), pl.BlockSpec((1, QK_ROPE, PAGE_SIZE), per_b),
                pl.BlockSpec(memory_space=pl.ANY), pl.BlockSpec(memory_space=pl.ANY)]
    n_buf = 2
    grid_spec = pltpu.PrefetchScalarGridSpec(
        num_scalar_prefetch=1,
        grid=(nb, n_pages // (n_buf * gp)),
        in_specs=in_specs,
        out_specs=pl.BlockSpec((1, rows, KV_LORA), per_b),
        scratch_shapes=[pltpu.VMEM((n_buf, gp * PAGE_SIZE, KV_LORA), F32),
                        pltpu.VMEM((n_buf, QK_ROPE, gp * PAGE_SIZE), F32),
                        pltpu.SemaphoreType.DMA((n_buf, 2)),
                        pltpu.VMEM((1, rows, LANES), F32), pltpu.VMEM((1, rows, LANES), F32),
                        pltpu.VMEM((1, rows, KV_LORA), F32)],
    )
    return pl.pallas_call(
        body,
        grid_spec=grid_spec,
        out_shape=jax.ShapeDtypeStruct((nb, rows, KV_LORA), F32),
        compiler_params=_cparams(("arbitrary", "arbitrary"), 32),
        name="decode_attention",
    )(page_table, ql, qp, cn, knt, cache_ckv, cache_kpet)


def _decode_out_body(ol_ref, wuv_ref, wo_ref, r_ref, g_ref, b_ref, o_ref):
    y = None
    for h in range(MLA_HEADS):
        oh = _dot(ol_ref[h].astype(BF16), wuv_ref[h]).astype(BF16)
        t = _dot(oh, wo_ref[h * V_DIM:(h + 1) * V_DIM, :])
        y = t if y is None else y + t
    o_ref[...] = _layer_norm(ALPHA * r_ref[...] + y, g_ref[...], b_ref[...])


def _decode_out(ol, wuv3, wo, res, g, b):
    d = res.shape[1]
    args = (ol, wuv3, wo, res, g.reshape(1, d), b.reshape(1, d))
    return pl.pallas_call(
        _decode_out_body,
        grid=(1,),
        in_specs=[_resident(a.shape) for a in args],
        out_specs=_whole(res.shape),
        out_shape=jax.ShapeDtypeStruct(res.shape, F32),
        compiler_params=_cparams(("arbitrary",), 24),
        name="decode_out",
    )(*args)


def _rope_angles(pos, half):
    inv_freq = ROPE_THETA ** (-jnp.arange(half, dtype=F32) / half)
    ang = pos[:, None] * inv_freq[None, :]
    return jnp.cos(ang), jnp.sin(ang)


def _ret_tables(pos, chunk):
    cos, sin = _rope_angles(pos, RET_DK // 2)
    log_gamma = jnp.log(1.0 - 2.0 ** (-5.0 - jnp.arange(RET_HEADS, dtype=F32)))
    idx = jnp.arange(chunk, dtype=F32)
    diff = idx[:, None] - idx[None, :]
    dec = jnp.where(diff >= 0, jnp.exp(jnp.maximum(diff, 0.0)[None] * log_gamma[:, None, None]), 0.0)
    q_dec = jnp.exp((idx + 1.0)[None, :] * log_gamma[:, None])
    k_dec = jnp.exp((chunk - 1.0 - idx)[None, :] * log_gamma[:, None])
    gc = jnp.exp(chunk * log_gamma)
    return {
        "c2": jnp.concatenate([cos, cos], axis=1),
        "s2": jnp.concatenate([-sin, sin], axis=1),
        "dec": dec.astype(F32),
        "q_dec": q_dec,
        "k_dec": k_dec,
        "gc": jnp.broadcast_to(gc[:, None, None], (RET_HEADS, 1, LANES)).astype(F32),
    }


def _pe_tables(pos):
    cos, sin = _rope_angles(pos, QK_ROPE // 2)
    z = jnp.zeros_like(cos)
    return {"cc": jnp.concatenate([cos, cos, z, z], axis=1),
            "s1": jnp.concatenate([-sin, z, z, z], axis=1),
            "s2": jnp.concatenate([z, sin, z, z], axis=1)}


def _pad_heads(w, nope, rope):
    k = w.shape[0]
    w3 = w.reshape(k, MLA_HEADS, nope + rope)
    pad = jnp.zeros((k, MLA_HEADS, 2 * LANES - nope - rope), w.dtype)
    return jnp.concatenate([w3, pad], axis=2).reshape(k, MLA_HEADS * 2 * LANES)


def _tiles(seq, n_pages, dff):
    def fit(t, n=seq):
        while n % t:
            t //= 2
        return t
    return {"tok": fit(512), "mix": fit(512), "ffn": fit(512), "attn": fit(512), "ff_chunk": dff,
            "pages": fit(32, n_pages // 2)}


def kernel(x_prompt, x_sample, state_pool, state_ret, cache_ckv, cache_kpe, state_conv, page_table,
           w_in_even, pool_w, pool_scale, ret_gn_g, w_o_even,
           w_dq, q_norm_g, w_uq, w_dkv, kv_norm_g, w_uk, w_uv, w_o_mla,
           w_up, conv_w, conv_b, w_down, ln_mix_g, ln_mix_b, ln_ffn_g, ln_ffn_b):
    bp, lp, d = x_prompt.shape
    bs, ls, _ = x_sample.shape
    past_len = page_table.shape[1] * PAGE_SIZE
    depth = w_up.shape[0]
    dff = w_down.shape[1]
    tl = _tiles(lp, page_table.shape[1], dff)
    ns = bs * ls
    assert lp % RET_CHUNK == 0 and ls % RET_CHUNK != 0 and tl["mix"] % RET_CHUNK == 0
    assert dff % tl["ff_chunk"] == 0 and page_table.shape[1] % (2 * tl["pages"]) == 0 and ls <= PAGE_SIZE

    xp = x_prompt.reshape(bp * lp, d)
    xs = jnp.swapaxes(x_sample, 0, 1).reshape(ns, d)
    pos_p = jnp.arange(lp, dtype=F32)
    pos_s = past_len + jnp.arange(ls, dtype=F32)
    pos_s_rows = jnp.repeat(pos_s, bs)

    wup_all = w_up.astype(BF16)
    wd_all = w_down.astype(BF16)
    outs = {k: [] for k in ("pool_p", "pool_s", "ret_p", "ret_s", "ckv_p", "ckv_s", "kpe_p", "kpe_s",
                            "conv_p", "conv_s")}
    for layer in range(depth):
        if layer % 2 == 0:
            e = layer // 2
            w_in = w_in_even[e].astype(BF16)
            w_o = w_o_even[e].astype(BF16)
            pw = pool_w[e].astype(BF16)
            tp = _ret_tables(pos_p, RET_CHUNK)
            tp["qd"] = jnp.broadcast_to(tp["q_dec"][:, :, None], (RET_HEADS, RET_CHUNK, LANES))
            tp["kd"] = jnp.broadcast_to(tp["k_dec"][:, :, None], (RET_HEADS, RET_CHUNK, LANES))
            xp, pst, rst = _even_layer_prompt(xp, bp, lp, w_in, tp, pw, pool_scale[e], ret_gn_g[e], w_o,
                                              ln_mix_g[layer], ln_mix_b[layer], tl["mix"])
            outs["pool_p"].append(pst)
            outs["ret_p"].append(rst)
            ts = _ret_tables(pos_s_rows, ls)
            rb = jnp.arange(ns) % bs
            same = rb[:, None] == rb[None, :]
            dec_rows = jnp.repeat(jnp.repeat(ts["dec"], bs, axis=1), bs, axis=2)
            ts["dm"] = jnp.where(same[None], dec_rows, 0.0)
            ts["qd"] = jnp.broadcast_to(jnp.repeat(ts["q_dec"], bs, axis=1)[:, :, None], (RET_HEADS, ns, LANES))
            ts["kd"] = jnp.broadcast_to(jnp.repeat(ts["k_dec"], bs, axis=1)[:, :, None], (RET_HEADS, ns, LANES))
            hs = _matmul(xs, w_in, ns)
            hist = jnp.swapaxes(state_pool[e], 0, 1)
            mix_s, hist_new, s_new = _even_mix_sample(hs, hist, state_ret[e], ts, pw, pool_scale[e],
                                                      ret_gn_g[e], bs, ls, past_len)
            outs["pool_s"].append(jnp.swapaxes(hist_new, 0, 1))
            outs["ret_s"].append(s_new)
            xs = _matmul_res_ln(mix_s, w_o, xs, ln_mix_g[layer], ln_mix_b[layer], ns)
        else:
            o = layer // 2
            hw = 2 * LANES
            w = {
                "dq": w_dq[o].astype(BF16),
                "qg": q_norm_g[o].reshape(1, -1),
                "uq": _pad_heads(w_uq[o], QK_NOPE, QK_ROPE).astype(BF16),
                "dkv": jnp.pad(w_dkv[o], ((0, 0), (0, KV_LORA + LANES - w_dkv.shape[2]))).astype(BF16),
                "kvg": kv_norm_g[o].reshape(1, -1),
                "uk": w_uk[o].reshape(KV_LORA, MLA_HEADS * QK_NOPE).astype(BF16),
                "uv": w_uv[o].reshape(KV_LORA, MLA_HEADS * V_DIM).astype(BF16),
                "ukt": jnp.transpose(w_uk[o], (1, 2, 0)).astype(BF16),
            }
            w_o = w_o_mla[o].astype(BF16)
            qp, kp, vp, ckv_p, kpe_p = _mla_proj(xp, _pe_tables(pos_p), w, tl["tok"], lp // tl["tok"], False)
            att = _flash_prompt(qp, kp, vp, bp, lp, tl["attn"])
            outs["ckv_p"].append(ckv_p.reshape(bp, lp, KV_LORA))
            outs["kpe_p"].append(kpe_p.reshape(bp, lp, QK_ROPE))
            xp = _matmul_res_ln(att, w_o, xp, ln_mix_g[layer], ln_mix_b[layer], tl["tok"])
            ql, qpe, ckv_s, kpe_s = _mla_proj(xs, _pe_tables(pos_s_rows), w, ns, 1, True)
            rows = ls * MLA_HEADS

            def per_batch(a, width):
                return jnp.transpose(a.reshape(ls, bs, MLA_HEADS, width), (1, 0, 2, 3)).reshape(bs, rows, width)

            ql_b = per_batch(ql, KV_LORA)
            qp_b = per_batch(qpe, LANES)[:, :, 0:QK_ROPE]
            ckv_sb = jnp.swapaxes(ckv_s.reshape(ls, bs, KV_LORA), 0, 1)
            kpe_sb = jnp.swapaxes(kpe_s.reshape(ls, bs, QK_ROPE), 0, 1)
            cn = jnp.pad(ckv_sb, ((0, 0), (0, PAGE_SIZE - ls), (0, 0)))
            knt = jnp.swapaxes(jnp.pad(kpe_sb, ((0, 0), (0, PAGE_SIZE - ls), (0, 0))), 1, 2)
            o_lat = _decode_attention(page_table, ql_b, qp_b, cn, knt, cache_ckv,
                                      jnp.swapaxes(cache_kpe, 2, 3), o, tl["pages"])
            outs["ckv_s"].append(ckv_sb)
            outs["kpe_s"].append(kpe_sb)
            ol = jnp.transpose(o_lat.reshape(bs, ls, MLA_HEADS, KV_LORA), (2, 1, 0, 3)).reshape(MLA_HEADS, ns, KV_LORA)
            wuv3 = jnp.transpose(w_uv[o], (1, 0, 2)).astype(BF16)
            xs = _decode_out(ol, wuv3, w_o, xs, ln_mix_g[layer], ln_mix_b[layer])
        xp, st_p = _ffn_prompt(xp, bp, lp, layer, wup_all, wd_all, conv_w[layer], conv_b[layer],
                               ln_ffn_g[layer], ln_ffn_b[layer], tl["ffn"], tl["ff_chunk"])
        outs["conv_p"].append(st_p[:, SUBLANES - (CONV_W - 1):, :])
        st_s = jnp.swapaxes(state_conv[layer], 0, 1)
        xs, st_s_new = _ffn_sample(xs, st_s, layer, wup_all, wd_all, conv_w[layer], conv_b[layer],
                                   ln_ffn_g[layer], ln_ffn_b[layer], bs, ls, tl["ff_chunk"])
        outs["conv_s"].append(jnp.swapaxes(st_s_new, 0, 1))

    y_p = xp.reshape(bp, lp, d)
    y_s = jnp.swapaxes(xs.reshape(ls, bs, d), 0, 1)
    return (y_p, y_s,
            jnp.stack(outs["pool_p"]), jnp.stack(outs["pool_s"]),
            jnp.stack(outs["ret_p"]), jnp.stack(outs["ret_s"]),
            jnp.stack(outs["ckv_p"]), jnp.stack(outs["ckv_s"]),
            jnp.stack(outs["kpe_p"]), jnp.stack(outs["kpe_s"]),
            jnp.stack(outs["conv_p"]), jnp.stack(outs["conv_s"]))
```

```python
import functools

import jax
import jax.numpy as jnp
from jax import lax
from jax.experimental import pallas as pl
from jax.experimental.pallas import tpu as pltpu

F32 = jnp.float32
BF16 = jnp.bfloat16

PAGE_SIZE = 128
POOL_WINDOWS = (2, 4, 8, 16)
POOL_HIST = max(POOL_WINDOWS) - 1
RET_HEADS = 4
RET_DK = 128
RET_CHUNK = 128
MLA_HEADS = 8
QK_NOPE = 128
QK_ROPE = 64
V_DIM = 128
KV_LORA = 256
CONV_W = 3
DEPTH = 2
ALPHA = (2.0 * DEPTH) ** 0.25
ROPE_THETA = 10000.0
LN_EPS = 1e-5
RMS_EPS = 1e-6
GN_EPS = 1e-6
MLA_SCALE = (QK_NOPE + QK_ROPE) ** -0.5

LANES = 128
SUBLANES = 8
MIB = 1024 * 1024


def _cparams(sem, vmem_mib):
    return pltpu.CompilerParams(dimension_semantics=sem, vmem_limit_bytes=int(vmem_mib * MIB))


def _resident(shape):
    nd = len(shape)
    return pl.BlockSpec(shape, lambda *_: (0,) * nd, pipeline_mode=pl.Buffered(1))


def _resident_layer(stacked_shape, layer):
    nd = len(stacked_shape) - 1
    return pl.BlockSpec((None,) + tuple(stacked_shape[1:]), lambda *_: (layer,) + (0,) * nd,
                        pipeline_mode=pl.Buffered(1))


def _whole(shape):
    nd = len(shape)
    return pl.BlockSpec(shape, lambda *_: (0,) * nd)


def _dot(a, b):
    return jnp.dot(a, b, preferred_element_type=F32)


def _dot_nt(a, b):
    return lax.dot_general(a, b, (((1,), (1,)), ((), ())), preferred_element_type=F32)


def _dot_tn(a, b):
    return lax.dot_general(a, b, (((0,), (0,)), ((), ())), preferred_element_type=F32)


def _layer_norm(z, g, b):
    mu = jnp.mean(z, axis=-1, keepdims=True)
    d = z - mu
    var = jnp.mean(d * d, axis=-1, keepdims=True)
    return d * lax.rsqrt(var + LN_EPS) * g + b


def _silu(x):
    return x * jax.nn.sigmoid(x)


def _mm_body(x_ref, w_ref, o_ref):
    o_ref[...] = _dot(x_ref[...].astype(BF16), w_ref[...]).astype(o_ref.dtype)


def _matmul(x, w, tm, out_dtype=F32):
    m, k = x.shape
    n = w.shape[1]
    return pl.pallas_call(
        _mm_body,
        grid=(m // tm,),
        in_specs=[pl.BlockSpec((tm, k), lambda i: (i, 0)), _resident((k, n))],
        out_specs=pl.BlockSpec((tm, n), lambda i: (i, 0)),
        out_shape=jax.ShapeDtypeStruct((m, n), out_dtype),
        compiler_params=_cparams(("parallel",), 40),
        name="matmul",
    )(x, w)


def _mm_ln_body(a_ref, w_ref, r_ref, g_ref, b_ref, o_ref):
    y = _dot(a_ref[...].astype(BF16), w_ref[...])
    o_ref[...] = _layer_norm(ALPHA * r_ref[...] + y, g_ref[...], b_ref[...])


def _matmul_res_ln(a, w, res, g, b, tm):
    m, k = a.shape
    n = w.shape[1]
    return pl.pallas_call(
        _mm_ln_body,
        grid=(m // tm,),
        in_specs=[pl.BlockSpec((tm, k), lambda i: (i, 0)), _resident((k, n)),
                  pl.BlockSpec((tm, n), lambda i: (i, 0)), _resident((1, n)), _resident((1, n))],
        out_specs=pl.BlockSpec((tm, n), lambda i: (i, 0)),
        out_shape=jax.ShapeDtypeStruct((m, n), F32),
        compiler_params=_cparams(("parallel",), 32),
        name="matmul_res_ln",
    )(a, w, res, g.reshape(1, n), b.reshape(1, n))


def _rope_full(x, c2, s2):
    return x * c2 + pltpu.roll(x, 64, axis=1) * s2


def _group_norm_gate(o, gate, gn_row):
    mu = jnp.mean(o, axis=-1, keepdims=True)
    d = o - mu
    var = jnp.mean(d * d, axis=-1, keepdims=True)
    return _silu(gate) * (d * lax.rsqrt(var + GN_EPS) * gn_row)


def _even_layer_prompt_body(x_ref, win_ref, c2_ref, s2_ref, dec_ref, qd_ref, kd_ref, gc_ref, pw_ref, ps_ref,
                            gn_ref, wo_ref, lg_ref, lb_ref, o_ref, pst_ref, rst_ref,
                            h_ref, mix_ref, ext_ref, s_ref, *, tm, n_j, sub):
    j = pl.program_id(1)
    pd = len(POOL_WINDOWS) * LANES

    @pl.when(j == 0)
    def _():
        ext_ref[0:16, :] = jnp.zeros((16, pd), F32)
        s_ref[...] = jnp.zeros(s_ref.shape, F32)

    @pl.when(j > 0)
    def _():
        ext_ref[0:16, :] = ext_ref[tm:tm + 16, :]

    c = RET_CHUNK
    k_scale = RET_DK ** -0.5
    nblk = 4 * LANES
    for r0 in range(0, tm, sub):
        rb = slice(r0, r0 + sub)
        x = x_ref[rb, :]
        xb = x.astype(BF16)
        for c0 in range(0, h_ref.shape[1], nblk):
            h_ref[rb, c0:c0 + nblk] = _dot(xb, win_ref[:, c0:c0 + nblk])
        ext_ref[16 + r0:16 + r0 + sub, :] = h_ref[rb, 0:pd]

        pos = (j * tm + r0 + lax.broadcasted_iota(jnp.int32, (sub, 1), 0)).astype(F32)
        for g, w in enumerate(POOL_WINDOWS):
            cols = slice(g * LANES, (g + 1) * LANES)
            e = ext_ref[r0:r0 + sub + 16, cols]
            u = e[16:, :]
            s = 1
            while s < w:
                e = e + pltpu.roll(e, s, axis=0)
                s *= 2
            cnt = jnp.minimum(float(w), pos + 1.0)
            pooled = e[16:, :] / cnt - u
            mixed = _dot(pooled.astype(BF16), pw_ref[g]) * ps_ref[:, cols]
            mix_ref[rb, cols] = mixed.astype(BF16)

        for ci in range(r0 // c, (r0 + sub) // c):
            rows = slice(ci * c, (ci + 1) * c)
            c2 = c2_ref[rows, :]
            s2 = s2_ref[rows, :]
            for hd in range(RET_HEADS):
                def col(part, hd=hd):
                    return slice(pd + (part * RET_HEADS + hd) * LANES, pd + (part * RET_HEADS + hd + 1) * LANES)
                q = _rope_full(h_ref[rows, col(0)], c2, s2)
                k = _rope_full(h_ref[rows, col(1)], c2, s2) * k_scale
                vb = h_ref[rows, col(2)].astype(BF16)
                gate = h_ref[rows, col(3)]
                st = s_ref[hd]
                sc = _dot_nt(q.astype(BF16), k.astype(BF16)) * dec_ref[hd]
                o = _dot(sc.astype(BF16), vb)
                o = o + _dot((q * qd_ref[hd]).astype(BF16), st.astype(BF16))
                s_ref[hd] = gc_ref[hd] * st + _dot_tn((k * kd_ref[hd]).astype(BF16), vb)
                ret = _group_norm_gate(o, gate, gn_ref[:, hd * LANES:(hd + 1) * LANES])
                mix_ref[rows, pd + hd * LANES:pd + (hd + 1) * LANES] = ret.astype(BF16)

        y = _dot(mix_ref[rb, :], wo_ref[...])
        o_ref[rb, :] = _layer_norm(ALPHA * x + y, lg_ref[...], lb_ref[...])

    @pl.when(j == n_j - 1)
    def _():
        pst_ref[0] = ext_ref[pl.ds(tm + 1, POOL_HIST), :]
        rst_ref[0] = s_ref[...]


def _even_layer_prompt(x, bsz, seq, w_in, tabs, pool_w, pool_scale, gn_g, w_o, ln_g, ln_b, tm):
    n_j = seq // tm
    d = x.shape[1]
    pd = pool_scale.shape[0]
    ed = pd + RET_HEADS * LANES
    body = functools.partial(_even_layer_prompt_body, tm=tm, n_j=n_j, sub=min(tm, 2 * RET_CHUNK))
    return pl.pallas_call(
        body,
        grid=(bsz, n_j),
        in_specs=[
            pl.BlockSpec((tm, d), lambda b, j: (b * n_j + j, 0)),
            _resident(w_in.shape),
            pl.BlockSpec((tm, LANES), lambda b, j: (j, 0)),
            pl.BlockSpec((tm, LANES), lambda b, j: (j, 0)),
            _resident((RET_HEADS, RET_CHUNK, RET_CHUNK)),
            _resident((RET_HEADS, RET_CHUNK, LANES)),
            _resident((RET_HEADS, RET_CHUNK, LANES)),
            _resident((RET_HEADS, 1, LANES)),
            _resident(pool_w.shape),
            _resident((1, pd)),
            _resident((1, RET_HEADS * LANES)),
            _resident(w_o.shape),
            _resident((1, d)),
            _resident((1, d)),
        ],
        out_specs=[
            pl.BlockSpec((tm, d), lambda b, j: (b * n_j + j, 0)),
            pl.BlockSpec((1, POOL_HIST, pd), lambda b, j: (b, 0, 0)),
            pl.BlockSpec((1, RET_HEADS, RET_DK, LANES), lambda b, j: (b, 0, 0, 0)),
        ],
        out_shape=[
            jax.ShapeDtypeStruct((bsz * seq, d), F32),
            jax.ShapeDtypeStruct((bsz, POOL_HIST, pd), F32),
            jax.ShapeDtypeStruct((bsz, RET_HEADS, RET_DK, LANES), F32),
        ],
        scratch_shapes=[pltpu.VMEM((tm, w_in.shape[1]), F32), pltpu.VMEM((tm, ed), BF16),
                        pltpu.VMEM((tm + 16, pd), F32), pltpu.VMEM((RET_HEADS, RET_DK, LANES), F32)],
        compiler_params=_cparams(("parallel", "arbitrary"), 40),
        name="even_layer_prompt",
    )(x, w_in, tabs["c2"], tabs["s2"], tabs["dec"], tabs["qd"], tabs["kd"], tabs["gc"],
      pool_w, pool_scale.reshape(1, pd), gn_g.reshape(1, -1), w_o, ln_g.reshape(1, d), ln_b.reshape(1, d))


def _even_mix_sample_body(h_ref, hist_ref, s0_ref, c2_ref, s2_ref, dm_ref, qd_ref, kd_ref, gc_ref,
                          pw_ref, ps_ref, gn_ref, mix_ref, hist_o_ref, s_o_ref, oc_ref,
                          *, nb, ls, cnts):
    pd = len(POOL_WINDOWS) * LANES
    ext = [hist_ref[i] for i in range(POOL_HIST)]
    ext += [h_ref[l * nb:(l + 1) * nb, 0:pd] for l in range(ls)]
    for i in range(POOL_HIST):
        hist_o_ref[i] = ext[ls + i]
    for g, w in enumerate(POOL_WINDOWS):
        cols = slice(g * LANES, (g + 1) * LANES)
        outs = []
        for l in range(ls):
            top = POOL_HIST + l
            acc = ext[top][:, cols]
            for jj in range(1, w):
                acc = acc + ext[top - jj][:, cols]
            outs.append(acc / cnts[g][l] - ext[top][:, cols])
        pooled = jnp.concatenate(outs, axis=0)
        mixed = _dot(pooled.astype(BF16), pw_ref[g]) * ps_ref[:, cols]
        mix_ref[:, cols] = mixed.astype(BF16)

    rows_b = lax.broadcasted_iota(jnp.int32, (ls * nb, 1), 0) % nb
    k_scale = RET_DK ** -0.5
    c2 = c2_ref[...]
    s2 = s2_ref[...]
    qs, ks, vs = [], [], []
    for hd in range(RET_HEADS):
        def col(part, hd=hd):
            return slice(pd + (part * RET_HEADS + hd) * LANES, pd + (part * RET_HEADS + hd + 1) * LANES)
        q = _rope_full(h_ref[:, col(0)], c2, s2)
        k = _rope_full(h_ref[:, col(1)], c2, s2) * k_scale
        vb = h_ref[:, col(2)].astype(BF16)
        sc = _dot_nt(q.astype(BF16), k.astype(BF16)) * dm_ref[hd]
        oc_ref[hd] = _dot(sc.astype(BF16), vb)
        qs.append(q * qd_ref[hd])
        ks.append(k * kd_ref[hd])
        vs.append(vb)

    def per_batch(b, carry):
        sel = rows_b == b
        for hd in range(RET_HEADS):
            st = s0_ref[b, hd]
            qm = jnp.where(sel, qs[hd], 0.0).astype(BF16)
            km = jnp.where(sel, ks[hd], 0.0).astype(BF16)
            oc_ref[hd] += _dot(qm, st.astype(BF16))
            s_o_ref[b, hd] = gc_ref[hd] * st + _dot_tn(km, vs[hd])
        return carry

    lax.fori_loop(0, nb, per_batch, 0)

    for hd in range(RET_HEADS):
        gate = h_ref[:, pd + (3 * RET_HEADS + hd) * LANES:pd + (3 * RET_HEADS + hd + 1) * LANES]
        ret = _group_norm_gate(oc_ref[hd], gate, gn_ref[:, hd * LANES:(hd + 1) * LANES])
        mix_ref[:, pd + hd * LANES:pd + (hd + 1) * LANES] = ret.astype(BF16)


def _even_mix_sample(h, hist, s0, tabs, pool_w, pool_scale, gn_g, nb, ls, past_len):
    pd = pool_scale.shape[0]
    ed = pd + RET_HEADS * LANES
    cnts = tuple(tuple(float(min(w, past_len + l + 1)) for l in range(ls)) for w in POOL_WINDOWS)
    body = functools.partial(_even_mix_sample_body, nb=nb, ls=ls, cnts=cnts)
    n = ls * nb
    args = (h, hist, s0, tabs["c2"], tabs["s2"], tabs["dm"], tabs["qd"], tabs["kd"], tabs["gc"],
            pool_w, pool_scale.reshape(1, pd), gn_g.reshape(1, -1))
    return pl.pallas_call(
        body,
        grid=(1,),
        in_specs=[_resident(a.shape) for a in args],
        out_specs=[_whole((n, ed)), _whole(hist.shape), _whole(s0.shape)],
        out_shape=[jax.ShapeDtypeStruct((n, ed), BF16),
                   jax.ShapeDtypeStruct(hist.shape, F32),
                   jax.ShapeDtypeStruct(s0.shape, F32)],
        scratch_shapes=[pltpu.VMEM((RET_HEADS, n, LANES), F32)],
        compiler_params=_cparams(("arbitrary",), 48),
        name="even_mix_sample",
    )(*args)


def _ffn_chunk(xb, wup_ref, wd_ref, cw_ref, cb_ref, c, tf, dff, shift_fn):
    cols = slice(c * tf, (c + 1) * tf)
    a = _dot(xb, wup_ref[:, cols])
    gate_in = _dot(xb, wup_ref[:, dff + c * tf:dff + (c + 1) * tf])
    a1, a2 = shift_fn(a, c)
    conv = cb_ref[:, cols] + cw_ref[0:1, cols] * a2
    conv = conv + cw_ref[1:2, cols] * a1
    conv = conv + cw_ref[2:3, cols] * a
    act = (_silu(conv) * gate_in).astype(BF16)
    return a, _dot(act, wd_ref[cols, :])


def _ffn_prompt_body(x_ref, wup_ref, wd_ref, cw_ref, cb_ref, g_ref, b_ref, o_ref, st_ref, carry_ref,
                     *, tm, n_j, tf, dff):
    j = pl.program_id(1)

    @pl.when(j == 0)
    def _():
        carry_ref[...] = jnp.zeros(carry_ref.shape, F32)

    x = x_ref[...]
    xb = x.astype(BF16)
    row = lax.broadcasted_iota(jnp.int32, (tm, tf), 0)

    def shift_fn(a, c):
        prev = carry_ref[:, c * tf:(c + 1) * tf]
        a1 = jnp.where(row == 0, prev[7:8, :], pltpu.roll(a, 1, axis=0))
        a2 = jnp.where(row == 0, prev[6:7, :], jnp.where(row == 1, prev[7:8, :], pltpu.roll(a, 2, axis=0)))
        return a1, a2

    acc = None
    for c in range(dff // tf):
        a, y = _ffn_chunk(xb, wup_ref, wd_ref, cw_ref, cb_ref, c, tf, dff, shift_fn)
        acc = y if acc is None else acc + y
        tail = a[tm - SUBLANES:tm, :]
        carry_ref[:, c * tf:(c + 1) * tf] = tail
        st_ref[0, :, c * tf:(c + 1) * tf] = tail
    o_ref[...] = _layer_norm(ALPHA * x + acc, g_ref[...], b_ref[...])


def _ffn_prompt(x, bsz, seq, layer, wup, wd, conv_w, conv_b, g, b, tm, tf):
    d = x.shape[1]
    dff = wd.shape[1]
    n_j = seq // tm
    body = functools.partial(_ffn_prompt_body, tm=tm, n_j=n_j, tf=tf, dff=dff)
    return pl.pallas_call(
        body,
        grid=(bsz, n_j),
        in_specs=[pl.BlockSpec((tm, d), lambda bi, j: (bi * n_j + j, 0)),
                  _resident_layer(wup.shape, layer), _resident_layer(wd.shape, layer), _resident(conv_w.shape),
                  _resident((1, dff)), _resident((1, d)), _resident((1, d))],
        out_specs=[pl.BlockSpec((tm, d), lambda bi, j: (bi * n_j + j, 0)),
                   pl.BlockSpec((1, SUBLANES, dff), lambda bi, j: (bi, 0, 0))],
        out_shape=[jax.ShapeDtypeStruct(x.shape, F32),
                   jax.ShapeDtypeStruct((bsz, SUBLANES, dff), F32)],
        scratch_shapes=[pltpu.VMEM((SUBLANES, dff), F32)],
        compiler_params=_cparams(("parallel", "arbitrary"), 52),
        name="ffn_prompt",
    )(x, wup, wd, conv_w, conv_b.reshape(1, dff), g.reshape(1, d), b.reshape(1, d))


def _ffn_sample_body(x_ref, st_ref, wup_ref, wd_ref, cw_ref, cb_ref, g_ref, b_ref, o_ref, st_o_ref,
                     *, nb, ls, tf, dff):
    x = x_ref[...]
    xb = x.astype(BF16)
    nh = CONV_W - 1

    def shift_fn(a, c):
        cols = slice(c * tf, (c + 1) * tf)
        ext = [st_ref[i, :, cols] for i in range(nh)] + [a[l * nb:(l + 1) * nb, :] for l in range(ls)]
        a1 = jnp.concatenate([ext[nh + l - 1] for l in range(ls)], axis=0)
        a2 = jnp.concatenate([ext[nh + l - 2] for l in range(ls)], axis=0)
        for i in range(nh):
            st_o_ref[i, :, cols] = ext[ls + i]
        return a1, a2

    acc = None
    for c in range(dff // tf):
        _, y = _ffn_chunk(xb, wup_ref, wd_ref, cw_ref, cb_ref, c, tf, dff, shift_fn)
        acc = y if acc is None else acc + y
    o_ref[...] = _layer_norm(ALPHA * x + acc, g_ref[...], b_ref[...])


def _ffn_sample(x, st, layer, wup, wd, conv_w, conv_b, g, b, nb, ls, tf):
    d = x.shape[1]
    dff = wd.shape[1]
    body = functools.partial(_ffn_sample_body, nb=nb, ls=ls, tf=tf, dff=dff)
    args = (x, st, wup, wd, conv_w, conv_b.reshape(1, dff), g.reshape(1, d), b.reshape(1, d))
    in_specs = [_resident(a.shape) for a in args]
    in_specs[2] = _resident_layer(wup.shape, layer)
    in_specs[3] = _resident_layer(wd.shape, layer)
    return pl.pallas_call(
        body,
        grid=(1,),
        in_specs=in_specs,
        out_specs=[_whole(x.shape), _whole(st.shape)],
        out_shape=[jax.ShapeDtypeStruct(x.shape, F32), jax.ShapeDtypeStruct(st.shape, F32)],
        compiler_params=_cparams(("arbitrary",), 40),
        name="ffn_sample",
    )(*args)


def _rope_pe(blk, cc, s1, s2):
    return blk * cc + pltpu.roll(blk, 96, axis=1) * s1 + pltpu.roll(blk, 32, axis=1) * s2


def _rms_norm(x, g):
    ms = jnp.mean(x * x, axis=-1, keepdims=True)
    return x * lax.rsqrt(ms + RMS_EPS) * g


def _mla_proj_body(x_ref, cc_ref, s1_ref, s2_ref, wdq_ref, qg_ref, wuq_ref, wdkv_ref, kvg_ref, *rest,
                   decode, sub):
    hw = 2 * LANES
    for r0 in range(0, x_ref.shape[0], sub):
        rb = slice(r0, r0 + sub)
        cc = cc_ref[rb, :]
        s1 = s1_ref[rb, :]
        s2 = s2_ref[rb, :]
        xb = x_ref[rb, :].astype(BF16)
        cq = _rms_norm(_dot(xb, wdq_ref[...]), qg_ref[...])
        q = _dot(cq.astype(BF16), wuq_ref[...])
        kv = _dot(xb, wdkv_ref[...])
        ckv = _rms_norm(kv[:, 0:KV_LORA], kvg_ref[...])
        kpe = _rope_pe(kv[:, KV_LORA:KV_LORA + LANES], cc, s1, s2)
        if decode:
            wukt_ref, ql_ref, qp_ref, ckv_ref, kpe_ref = rest
            for h in range(MLA_HEADS):
                qn = q[:, h * hw:h * hw + LANES].astype(BF16)
                ql_ref[rb, h * KV_LORA:(h + 1) * KV_LORA] = _dot(qn, wukt_ref[h]).astype(BF16)
                qp = _rope_pe(q[:, h * hw + LANES:(h + 1) * hw], cc, s1, s2)
                qp_ref[rb, h * LANES:(h + 1) * LANES] = qp.astype(BF16)
        else:
            wuk_ref, wuv_ref, qo_ref, ko_ref, vo_ref, ckv_ref, kpe_ref = rest
            cb = ckv.astype(BF16)
            kn = _dot(cb, wuk_ref[...])
            vo_ref[rb, :] = _dot(cb, wuv_ref[...]).astype(BF16)
            kpb = kpe.astype(BF16)
            for h in range(MLA_HEADS):
                qo_ref[rb, h * hw:h * hw + LANES] = q[:, h * hw:h * hw + LANES].astype(BF16)
                qp = _rope_pe(q[:, h * hw + LANES:(h + 1) * hw], cc, s1, s2)
                qo_ref[rb, h * hw + LANES:(h + 1) * hw] = qp.astype(BF16)
                ko_ref[rb, h * hw:h * hw + LANES] = kn[:, h * LANES:(h + 1) * LANES].astype(BF16)
                ko_ref[rb, h * hw + LANES:(h + 1) * hw] = kpb
        ckv_ref[rb, :] = ckv
        kpe_ref[rb, :] = kpe[:, 0:QK_ROPE]


def _mla_proj(x, tabs, w, tm, n_pos_blocks, decode):
    t, d = x.shape
    hw = 2 * LANES
    body = functools.partial(_mla_proj_body, decode=decode, sub=tm)
    row = lambda i: (i, 0)
    tab = lambda i: (i % n_pos_blocks, 0)
    ins = [x, tabs["cc"], tabs["s1"], tabs["s2"], w["dq"], w["qg"], w["uq"], w["dkv"], w["kvg"]]
    in_specs = [pl.BlockSpec((tm, d), row)] + [pl.BlockSpec((tm, LANES), tab)] * 3
    in_specs += [_resident(a.shape) for a in ins[4:]]
    if decode:
        ins += [w["ukt"]]
        in_specs += [_resident(w["ukt"].shape)]
        outs = [(MLA_HEADS * KV_LORA, BF16), (MLA_HEADS * LANES, BF16)]
    else:
        ins += [w["uk"], w["uv"]]
        in_specs += [_resident(w["uk"].shape), _resident(w["uv"].shape)]
        outs = [(MLA_HEADS * hw, BF16), (MLA_HEADS * hw, BF16), (MLA_HEADS * V_DIM, BF16)]
    outs += [(KV_LORA, F32), (QK_ROPE, F32)]
    return pl.pallas_call(
        body,
        grid=(t // tm,),
        in_specs=in_specs,
        out_specs=[pl.BlockSpec((tm, n), row) for n, _ in outs],
        out_shape=[jax.ShapeDtypeStruct((t, n), dt) for n, dt in outs],
        compiler_params=_cparams(("parallel",), 40),
        name="mla_proj_decode" if decode else "mla_proj",
    )(*ins)


def _online_softmax_update(s, m_ref, l_ref, acc_ref, idx, pv_fn):
    n = s.shape[1] // LANES
    tiles = [s[:, j * LANES:(j + 1) * LANES] for j in range(n)]
    mx = tiles[0]
    for t in tiles[1:]:
        mx = jnp.maximum(mx, t)
    m_prev = m_ref[idx]
    m_new = jnp.maximum(m_prev, jnp.max(mx, axis=-1, keepdims=True))
    alpha = jnp.exp(m_prev - m_new)
    ps = [jnp.exp(t - m_new) for t in tiles]
    rs = ps[0]
    for t in ps[1:]:
        rs = rs + t
    l_ref[idx] = alpha * l_ref[idx] + jnp.sum(rs, axis=-1, keepdims=True)
    p = jnp.concatenate(ps, axis=1) if n > 1 else ps[0]
    pv = pv_fn(p.astype(BF16))
    a_w = alpha if acc_ref.shape[-1] == LANES else jnp.concatenate([alpha] * (acc_ref.shape[-1] // LANES), axis=1)
    acc_ref[idx] = a_w * acc_ref[idx] + pv
    m_ref[idx] = m_new


def _flash_body(q_ref, k_ref, v_ref, o_ref, m_ref, l_ref, acc_ref, *, tq):
    qi = pl.program_id(1)
    hw = 2 * LANES
    m_ref[...] = jnp.full(m_ref.shape, -jnp.inf, F32)
    l_ref[...] = jnp.zeros(l_ref.shape, F32)
    acc_ref[...] = jnp.zeros(acc_ref.shape, F32)

    def step(h, rows, key0, n_keys, masked):
        start = pl.multiple_of(key0, n_keys)
        q = q_ref[rows, h * hw:(h + 1) * hw]
        k = k_ref[pl.ds(start, n_keys), h * hw:(h + 1) * hw]
        v = v_ref[pl.ds(start, n_keys), h * V_DIM:(h + 1) * V_DIM]
        s = _dot_nt(q, k) * MLA_SCALE
        if masked:
            row = lax.broadcasted_iota(jnp.int32, s.shape, 0)
            colm = lax.broadcasted_iota(jnp.int32, s.shape, 1)
            s = jnp.where(colm <= row, s, -jnp.inf)
        _online_softmax_update(s, m_ref, l_ref, acc_ref, (h, rows), lambda p: _dot(p, v))

    def loop_body(kj, carry):
        for h in range(MLA_HEADS):
            step(h, slice(0, tq), kj * tq, tq, False)
        return carry

    lax.fori_loop(0, qi, loop_body, 0)
    for h in range(MLA_HEADS):
        step(h, slice(0, tq), qi * tq, tq, True)
        o_ref[:, h * V_DIM:(h + 1) * V_DIM] = (acc_ref[h] / l_ref[h]).astype(BF16)


def _flash_prompt(qp, kp, vp, bsz, seq, tq):
    nq = seq // tq
    body = functools.partial(_flash_body, tq=tq)
    return pl.pallas_call(
        body,
        grid=(bsz, nq),
        in_specs=[pl.BlockSpec((tq, qp.shape[1]), lambda b, i: (b * nq + i, 0)),
                  pl.BlockSpec((seq, kp.shape[1]), lambda b, i: (b, 0)),
                  pl.BlockSpec((seq, vp.shape[1]), lambda b, i: (b, 0))],
        out_specs=pl.BlockSpec((tq, vp.shape[1]), lambda b, i: (b * nq + i, 0)),
        out_shape=jax.ShapeDtypeStruct(vp.shape, BF16),
        scratch_shapes=[pltpu.VMEM((MLA_HEADS, tq, LANES), F32), pltpu.VMEM((MLA_HEADS, tq, LANES), F32),
                        pltpu.VMEM((MLA_HEADS, tq, V_DIM), F32)],
        compiler_params=_cparams(("parallel", "arbitrary"), 56),
        name="flash_prompt",
    )(qp, kp, vp)


def _decode_body(pt_ref, ql_ref, qp_ref, cn_ref, kn_ref, ckv_hbm, kpe_hbm, o_ref,
                 cbuf_ref, rbuf_ref, sem_ref, m_ref, l_ref, acc_ref, *, gp, ls, layer, n_split):
    b = pl.program_id(0)
    g = pl.program_id(1)
    n_b = pl.num_programs(0)
    n_g = pl.num_programs(1)
    slot = lax.rem(b * n_g + g, 2)
    other = 1 - slot

    def page_copies(bb, grp, slot):
        cps = []
        for i in range(gp):
            pg = pt_ref[bb, grp * gp + i]
            keys = pl.ds(i * PAGE_SIZE, PAGE_SIZE)
            cps.append(pltpu.make_async_copy(ckv_hbm.at[layer, pg], cbuf_ref.at[slot, keys, :], sem_ref.at[slot, 0]))
            cps.append(pltpu.make_async_copy(kpe_hbm.at[layer, pg], rbuf_ref.at[slot, :, keys], sem_ref.at[slot, 1]))
        return cps

    def start(bb, grp, slot):
        for cp in page_copies(bb, grp, slot):
            cp.start()

    @pl.when((b == 0) & (g == 0))
    def _():
        start(b, g, slot)

    @pl.when(g < n_g - 1)
    def _():
        start(b, g + 1, other)

    @pl.when((g == n_g - 1) & (b < n_b - 1))
    def _():
        start(b + 1, 0, other)

    @pl.when(g == 0)
    def _():
        m_ref[...] = jnp.full(m_ref.shape, -jnp.inf, F32)
        l_ref[...] = jnp.zeros(l_ref.shape, F32)
        acc_ref[...] = jnp.zeros(acc_ref.shape, F32)

    ql = ql_ref[0]
    qp = qp_ref[0]

    for cp in page_copies(b, g, slot):
        cp.wait()

    kc_len = gp * PAGE_SIZE // n_split
    kbs, scores = [], []
    for c in range(n_split):
        ks = slice(c * kc_len, (c + 1) * kc_len)
        kb = cbuf_ref[slot, ks, :].astype(BF16)
        scores.append((_dot_nt(ql, kb) + _dot(qp, rbuf_ref[slot, :, ks].astype(BF16))) * MLA_SCALE)
        kbs.append(kb)
    for c in range(n_split):
        _online_softmax_update(scores[c], m_ref, l_ref, acc_ref, 0, lambda p, kb=kbs[c]: _dot(p, kb))

    @pl.when(g == n_g - 1)
    def _():
        kc = cn_ref[0].astype(BF16)
        s = (_dot_nt(ql, kc) + _dot(qp, kn_ref[0].astype(BF16))) * MLA_SCALE
        r = lax.broadcasted_iota(jnp.int32, s.shape, 0) // MLA_HEADS
        cidx = lax.broadcasted_iota(jnp.int32, s.shape, 1)
        s = jnp.where((cidx <= r) & (cidx < ls), s, -jnp.inf)
        _online_softmax_update(s, m_ref, l_ref, acc_ref, 0, lambda p: _dot(p, kc))
        l_w = jnp.concatenate([l_ref[0]] * (KV_LORA // LANES), axis=1)
        o_ref[0] = acc_ref[0] / l_w


def _decode_attention(page_table, ql, qp, cn, knt, cache_ckv, cache_kpet, layer, gp):
    nb, rows, _ = ql.shape
    n_pages = page_table.shape[1]
    ls = rows // MLA_HEADS
    n_split = 4 if gp % 4 == 0 else 1
    body = functools.partial(_decode_body, gp=gp, ls=ls, layer=layer, n_split=n_split)
    per_b = lambda b, g, pt: (b, 0, 0)
    in_specs = [pl.BlockSpec((1, rows, KV_LORA), per_b), pl.BlockSpec((1, rows, QK_ROPE), per_b),
                pl.BlockSpec((1, PAGE_SIZE, KV_LORA), per_b), pl.BlockSpec((1, QK_ROPE, PAGE_SIZE), per_b),
                pl.BlockSpec(memory_space=pl.ANY), pl.BlockSpec(memory_space=pl.ANY)]
    n_buf = 2
    grid_spec = pltpu.PrefetchScalarGridSpec(
        num_scalar_prefetch=1,
        grid=(nb, n_pages // gp),
        in_specs=in_specs,
        out_specs=pl.BlockSpec((1, rows, KV_LORA), per_b),
        scratch_shapes=[pltpu.VMEM((n_buf, gp * PAGE_SIZE, KV_LORA), F32),
                        pltpu.VMEM((n_buf, QK_ROPE, gp * PAGE_SIZE), F32),
                        pltpu.SemaphoreType.DMA((n_buf, 2)),
                        pltpu.VMEM((1, rows, LANES), F32), pltpu.VMEM((1, rows, LANES), F32),
                        pltpu.VMEM((1, rows, KV_LORA), F32)],
    )
    return pl.pallas_call(
        body,
        grid_spec=grid_spec,
        out_shape=jax.ShapeDtypeStruct((nb, rows, KV_LORA), F32),
        compiler_params=_cparams(("arbitrary", "arbitrary"), 48),
        name="decode_attention",
    )(page_table, ql, qp, cn, knt, cache_ckv, cache_kpet)


def _decode_out_body(ol_ref, wuv_ref, wo_ref, r_ref, g_ref, b_ref, o_ref):
    y = None
    for h in range(MLA_HEADS):
        oh = _dot(ol_ref[h].astype(BF16), wuv_ref[h]).astype(BF16)
        t = _dot(oh, wo_ref[h * V_DIM:(h + 1) * V_DIM, :])
        y = t if y is None else y + t
    o_ref[...] = _layer_norm(ALPHA * r_ref[...] + y, g_ref[...], b_ref[...])


def _decode_out(ol, wuv3, wo, res, g, b):
    d = res.shape[1]
    args = (ol, wuv3, wo, res, g.reshape(1, d), b.reshape(1, d))
    return pl.pallas_call(
        _decode_out_body,
        grid=(1,),
        in_specs=[_resident(a.shape) for a in args],
        out_specs=_whole(res.shape),
        out_shape=jax.ShapeDtypeStruct(res.shape, F32),
        compiler_params=_cparams(("arbitrary",), 24),
        name="decode_out",
    )(*args)


def _rope_angles(pos, half):
    inv_freq = ROPE_THETA ** (-jnp.arange(half, dtype=F32) / half)
    ang = pos[:, None] * inv_freq[None, :]
    return jnp.cos(ang), jnp.sin(ang)


def _ret_tables(pos, chunk):
    cos, sin = _rope_angles(pos, RET_DK // 2)
    log_gamma = jnp.log(1.0 - 2.0 ** (-5.0 - jnp.arange(RET_HEADS, dtype=F32)))
    idx = jnp.arange(chunk, dtype=F32)
    diff = idx[:, None] - idx[None, :]
    dec = jnp.where(diff >= 0, jnp.exp(jnp.maximum(diff, 0.0)[None] * log_gamma[:, None, None]), 0.0)
    q_dec = jnp.exp((idx + 1.0)[None, :] * log_gamma[:, None])
    k_dec = jnp.exp((chunk - 1.0 - idx)[None, :] * log_gamma[:, None])
    gc = jnp.exp(chunk * log_gamma)
    return {
        "c2": jnp.concatenate([cos, cos], axis=1),
        "s2": jnp.concatenate([-sin, sin], axis=1),
        "dec": dec.astype(F32),
        "q_dec": q_dec,
        "k_dec": k_dec,
        "gc": jnp.broadcast_to(gc[:, None, None], (RET_HEADS, 1, LANES)).astype(F32),
    }


def _pe_tables(pos):
    cos, sin = _rope_angles(pos, QK_ROPE // 2)
    z = jnp.zeros_like(cos)
    return {"cc": jnp.concatenate([cos, cos, z, z], axis=1),
            "s1": jnp.concatenate([-sin, z, z, z], axis=1),
            "s2": jnp.concatenate([z, sin, z, z], axis=1)}


def _pad_heads(w, nope, rope):
    k = w.shape[0]
    w3 = w.reshape(k, MLA_HEADS, nope + rope)
    pad = jnp.zeros((k, MLA_HEADS, 2 * LANES - nope - rope), w.dtype)
    return jnp.concatenate([w3, pad], axis=2).reshape(k, MLA_HEADS * 2 * LANES)


def _tiles(seq, n_pages, dff):
    def fit(t, n=seq):
        while n % t:
            t //= 2
        return t
    return {"tok": fit(512), "mix": fit(512), "ffn": fit(512), "attn": fit(512), "ff_chunk": dff,
            "pages": fit(64, n_pages)}


def kernel(x_prompt, x_sample, state_pool, state_ret, cache_ckv, cache_kpe, state_conv, page_table,
           w_in_even, pool_w, pool_scale, ret_gn_g, w_o_even,
           w_dq, q_norm_g, w_uq, w_dkv, kv_norm_g, w_uk, w_uv, w_o_mla,
           w_up, conv_w, conv_b, w_down, ln_mix_g, ln_mix_b, ln_ffn_g, ln_ffn_b):
    bp, lp, d = x_prompt.shape
    bs, ls, _ = x_sample.shape
    past_len = page_table.shape[1] * PAGE_SIZE
    depth = w_up.shape[0]
    dff = w_down.shape[1]
    tl = _tiles(lp, page_table.shape[1], dff)
    ns = bs * ls
    assert lp % RET_CHUNK == 0 and ls % RET_CHUNK != 0 and tl["mix"] % RET_CHUNK == 0
    assert dff % tl["ff_chunk"] == 0 and page_table.shape[1] % tl["pages"] == 0 and ls <= PAGE_SIZE

    xp = x_prompt.reshape(bp * lp, d)
    xs = jnp.swapaxes(x_sample, 0, 1).reshape(ns, d)
    pos_p = jnp.arange(lp, dtype=F32)
    pos_s = past_len + jnp.arange(ls, dtype=F32)
    pos_s_rows = jnp.repeat(pos_s, bs)

    wup_all = w_up.astype(BF16)
    wd_all = w_down.astype(BF16)
    outs = {k: [] for k in ("pool_p", "pool_s", "ret_p", "ret_s", "ckv_p", "ckv_s", "kpe_p", "kpe_s",
                            "conv_p", "conv_s")}
    for layer in range(depth):
        if layer % 2 == 0:
            e = layer // 2
            w_in = w_in_even[e].astype(BF16)
            w_o = w_o_even[e].astype(BF16)
            pw = pool_w[e].astype(BF16)
            tp = _ret_tables(pos_p, RET_CHUNK)
            tp["qd"] = jnp.broadcast_to(tp["q_dec"][:, :, None], (RET_HEADS, RET_CHUNK, LANES))
            tp["kd"] = jnp.broadcast_to(tp["k_dec"][:, :, None], (RET_HEADS, RET_CHUNK, LANES))
            xp, pst, rst = _even_layer_prompt(xp, bp, lp, w_in, tp, pw, pool_scale[e], ret_gn_g[e], w_o,
                                              ln_mix_g[layer], ln_mix_b[layer], tl["mix"])
            outs["pool_p"].append(pst)
            outs["ret_p"].append(rst)
            ts = _ret_tables(pos_s_rows, ls)
            rb = jnp.arange(ns) % bs
            same = rb[:, None] == rb[None, :]
            dec_rows = jnp.repeat(jnp.repeat(ts["dec"], bs, axis=1), bs, axis=2)
            ts["dm"] = jnp.where(same[None], dec_rows, 0.0)
            ts["qd"] = jnp.broadcast_to(jnp.repeat(ts["q_dec"], bs, axis=1)[:, :, None], (RET_HEADS, ns, LANES))
            ts["kd"] = jnp.broadcast_to(jnp.repeat(ts["k_dec"], bs, axis=1)[:, :, None], (RET_HEADS, ns, LANES))
            hs = _matmul(xs, w_in, ns)
            hist = jnp.swapaxes(state_pool[e], 0, 1)
            mix_s, hist_new, s_new = _even_mix_sample(hs, hist, state_ret[e], ts, pw, pool_scale[e],
                                                      ret_gn_g[e], bs, ls, past_len)
            outs["pool_s"].append(jnp.swapaxes(hist_new, 0, 1))
            outs["ret_s"].append(s_new)
            xs = _matmul_res_ln(mix_s, w_o, xs, ln_mix_g[layer], ln_mix_b[layer], ns)
        else:
            o = layer // 2
            hw = 2 * LANES
            w = {
                "dq": w_dq[o].astype(BF16),
                "qg": q_norm_g[o].reshape(1, -1),
                "uq": _pad_heads(w_uq[o], QK_NOPE, QK_ROPE).astype(BF16),
                "dkv": jnp.pad(w_dkv[o], ((0, 0), (0, KV_LORA + LANES - w_dkv.shape[2]))).astype(BF16),
                "kvg": kv_norm_g[o].reshape(1, -1),
                "uk": w_uk[o].reshape(KV_LORA, MLA_HEADS * QK_NOPE).astype(BF16),
                "uv": w_uv[o].reshape(KV_LORA, MLA_HEADS * V_DIM).astype(BF16),
                "ukt": jnp.transpose(w_uk[o], (1, 2, 0)).astype(BF16),
            }
            w_o = w_o_mla[o].astype(BF16)
            qp, kp, vp, ckv_p, kpe_p = _mla_proj(xp, _pe_tables(pos_p), w, tl["tok"], lp // tl["tok"], False)
            att = _flash_prompt(qp, kp, vp, bp, lp, tl["attn"])
            outs["ckv_p"].append(ckv_p.reshape(bp, lp, KV_LORA))
            outs["kpe_p"].append(kpe_p.reshape(bp, lp, QK_ROPE))
            xp = _matmul_res_ln(att, w_o, xp, ln_mix_g[layer], ln_mix_b[layer], tl["tok"])
            ql, qpe, ckv_s, kpe_s = _mla_proj(xs, _pe_tables(pos_s_rows), w, ns, 1, True)
            rows = ls * MLA_HEADS

            def per_batch(a, width):
                return jnp.transpose(a.reshape(ls, bs, MLA_HEADS, width), (1, 0, 2, 3)).reshape(bs, rows, width)

            ql_b = per_batch(ql, KV_LORA)
            qp_b = per_batch(qpe, LANES)[:, :, 0:QK_ROPE]
            ckv_sb = jnp.swapaxes(ckv_s.reshape(ls, bs, KV_LORA), 0, 1)
            kpe_sb = jnp.swapaxes(kpe_s.reshape(ls, bs, QK_ROPE), 0, 1)
            cn = jnp.pad(ckv_sb, ((0, 0), (0, PAGE_SIZE - ls), (0, 0)))
            knt = jnp.swapaxes(jnp.pad(kpe_sb, ((0, 0), (0, PAGE_SIZE - ls), (0, 0))), 1, 2)
            o_lat = _decode_attention(page_table, ql_b, qp_b, cn, knt, cache_ckv,
                                      jnp.swapaxes(cache_kpe, 2, 3), o, tl["pages"])
            outs["ckv_s"].append(ckv_sb)
            outs["kpe_s"].append(kpe_sb)
            ol = jnp.transpose(o_lat.reshape(bs, ls, MLA_HEADS, KV_LORA), (2, 1, 0, 3)).reshape(MLA_HEADS, ns, KV_LORA)
            wuv3 = jnp.transpose(w_uv[o], (1, 0, 2)).astype(BF16)
            xs = _decode_out(ol, wuv3, w_o, xs, ln_mix_g[layer], ln_mix_b[layer])
        xp, st_p = _ffn_prompt(xp, bp, lp, layer, wup_all, wd_all, conv_w[layer], conv_b[layer],
                               ln_ffn_g[layer], ln_ffn_b[layer], tl["ffn"], tl["ff_chunk"])
        outs["conv_p"].append(st_p[:, SUBLANES - (CONV_W - 1):, :])
        st_s = jnp.swapaxes(state_conv[layer], 0, 1)
        xs, st_s_new = _ffn_sample(xs, st_s, layer, wup_all, wd_all, conv_w[layer], conv_b[layer],
                                   ln_ffn_g[layer], ln_ffn_b[layer], bs, ls, tl["ff_chunk"])
        outs["conv_s"].append(jnp.swapaxes(st_s_new, 0, 1))

    y_p = xp.reshape(bp, lp, d)
    y_s = jnp.swapaxes(xs.reshape(ls, bs, d), 0, 1)
    return (y_p, y_s,
            jnp.stack(outs["pool_p"]), jnp.stack(outs["pool_s"]),
            jnp.stack(outs["ret_p"]), jnp.stack(outs["ret_s"]),
            jnp.stack(outs["ckv_p"]), jnp.stack(outs["ckv_s"]),
            jnp.stack(outs["kpe_p"]), jnp.stack(outs["kpe_s"]),
            jnp.stack(outs["conv_p"]), jnp.stack(outs["conv_s"]))
```

```python
import functools

import jax
import jax.numpy as jnp
import numpy as np
from jax import lax
from jax.experimental import pallas as pl
from jax.experimental.pallas import tpu as pltpu

F32 = jnp.float32
BF16 = jnp.bfloat16

PAGE_SIZE = 128
POOL_WINDOWS = (2, 4, 8, 16)
POOL_HIST = max(POOL_WINDOWS) - 1
RET_HEADS = 4
RET_DK = 128
RET_CHUNK = 128
MLA_HEADS = 8
QK_NOPE = 128
QK_ROPE = 64
V_DIM = 128
KV_LORA = 256
CONV_W = 3
DEPTH = 2
ALPHA = (2.0 * DEPTH) ** 0.25
ROPE_THETA = 10000.0
LN_EPS = 1e-5
RMS_EPS = 1e-6
GN_EPS = 1e-6
MLA_SCALE = (QK_NOPE + QK_ROPE) ** -0.5
LOG2E = 1.4426950408889634
SCALE_LOG2E = MLA_SCALE * LOG2E

LANES = 128
SUBLANES = 8
MIB = 1024 * 1024


def _cparams(sem, vmem_mib):
    return pltpu.CompilerParams(dimension_semantics=sem, vmem_limit_bytes=int(vmem_mib * MIB))


def _resident(shape):
    nd = len(shape)
    return pl.BlockSpec(shape, lambda *_: (0,) * nd, pipeline_mode=pl.Buffered(1))


def _resident_layer(stacked_shape, layer):
    nd = len(stacked_shape) - 1
    return pl.BlockSpec((None,) + tuple(stacked_shape[1:]), lambda *_: (layer,) + (0,) * nd,
                        pipeline_mode=pl.Buffered(1))


def _whole(shape):
    nd = len(shape)
    return pl.BlockSpec(shape, lambda *_: (0,) * nd)


def _dot(a, b):
    return jnp.dot(a, b, preferred_element_type=F32)


def _dot_nt(a, b):
    return lax.dot_general(a, b, (((1,), (1,)), ((), ())), preferred_element_type=F32)


def _dot_tn(a, b):
    return lax.dot_general(a, b, (((0,), (0,)), ((), ())), preferred_element_type=F32)


def _layer_norm(z, g, b):
    mu = jnp.mean(z, axis=-1, keepdims=True)
    d = z - mu
    var = jnp.mean(d * d, axis=-1, keepdims=True)
    return d * lax.rsqrt(var + LN_EPS) * g + b


def _silu(x):
    return x * jax.nn.sigmoid(x)


def _mm_body(x_ref, w_ref, o_ref):
    o_ref[...] = _dot(x_ref[...].astype(BF16), w_ref[...]).astype(o_ref.dtype)


def _matmul(x, w, tm, out_dtype=F32):
    m, k = x.shape
    n = w.shape[1]
    return pl.pallas_call(
        _mm_body,
        grid=(m // tm,),
        in_specs=[pl.BlockSpec((tm, k), lambda i: (i, 0)), _resident((k, n))],
        out_specs=pl.BlockSpec((tm, n), lambda i: (i, 0)),
        out_shape=jax.ShapeDtypeStruct((m, n), out_dtype),
        compiler_params=_cparams(("parallel",), 40),
        name="matmul",
    )(x, w)


def _mm_ln_body(a_ref, w_ref, r_ref, g_ref, b_ref, o_ref):
    y = _dot(a_ref[...].astype(BF16), w_ref[...])
    o_ref[...] = _layer_norm(ALPHA * r_ref[...] + y, g_ref[...], b_ref[...])


def _matmul_res_ln(a, w, res, g, b, tm):
    m, k = a.shape
    n = w.shape[1]
    return pl.pallas_call(
        _mm_ln_body,
        grid=(m // tm,),
        in_specs=[pl.BlockSpec((tm, k), lambda i: (i, 0)), _resident((k, n)),
                  pl.BlockSpec((tm, n), lambda i: (i, 0)), _resident((1, n)), _resident((1, n))],
        out_specs=pl.BlockSpec((tm, n), lambda i: (i, 0)),
        out_shape=jax.ShapeDtypeStruct((m, n), F32),
        compiler_params=_cparams(("parallel",), 32),
        name="matmul_res_ln",
    )(a, w, res, g.reshape(1, n), b.reshape(1, n))


def _rope_full(x, c2, s2):
    return x * c2 + pltpu.roll(x, 64, axis=1) * s2


def _group_norm_gate(o, gate, gn_row):
    mu = jnp.mean(o, axis=-1, keepdims=True)
    d = o - mu
    var = jnp.mean(d * d, axis=-1, keepdims=True)
    return _silu(gate) * (d * lax.rsqrt(var + GN_EPS) * gn_row)


def _even_layer_prompt_body(x_ref, win_ref, c2_ref, s2_ref, dec_ref, qd_ref, kd_ref, gc_ref, pw_ref, ps_ref,
                            gn_ref, wo_ref, lg_ref, lb_ref, o_ref, pst_ref, rst_ref,
                            h_ref, mix_ref, ext_ref, s_ref, *, tm, n_j, sub):
    j = pl.program_id(1)
    pd = len(POOL_WINDOWS) * LANES

    @pl.when(j == 0)
    def _():
        ext_ref[0:16, :] = jnp.zeros((16, pd), F32)
        s_ref[...] = jnp.zeros(s_ref.shape, F32)

    @pl.when(j > 0)
    def _():
        ext_ref[0:16, :] = ext_ref[tm:tm + 16, :]

    c = RET_CHUNK
    k_scale = RET_DK ** -0.5
    nblk = 4 * LANES
    for r0 in range(0, tm, sub):
        rb = slice(r0, r0 + sub)
        x = x_ref[rb, :]
        xb = x.astype(BF16)
        for c0 in range(0, h_ref.shape[1], nblk):
            h_ref[rb, c0:c0 + nblk] = _dot(xb, win_ref[:, c0:c0 + nblk])
        ext_ref[16 + r0:16 + r0 + sub, :] = h_ref[rb, 0:pd]

        pos = (j * tm + r0 + lax.broadcasted_iota(jnp.int32, (sub, 1), 0)).astype(F32)
        for g, w in enumerate(POOL_WINDOWS):
            cols = slice(g * LANES, (g + 1) * LANES)
            e = ext_ref[r0:r0 + sub + 16, cols]
            u = e[16:, :]
            s = 1
            while s < w:
                e = e + pltpu.roll(e, s, axis=0)
                s *= 2
            cnt = jnp.minimum(float(w), pos + 1.0)
            pooled = e[16:, :] / cnt - u
            mixed = _dot(pooled.astype(BF16), pw_ref[g]) * ps_ref[:, cols]
            mix_ref[rb, cols] = mixed.astype(BF16)

        for ci in range(r0 // c, (r0 + sub) // c):
            rows = slice(ci * c, (ci + 1) * c)
            c2 = c2_ref[rows, :]
            s2 = s2_ref[rows, :]
            for hd in range(RET_HEADS):
                def col(part, hd=hd):
                    return slice(pd + (part * RET_HEADS + hd) * LANES, pd + (part * RET_HEADS + hd + 1) * LANES)
                q = _rope_full(h_ref[rows, col(0)], c2, s2)
                k = _rope_full(h_ref[rows, col(1)], c2, s2) * k_scale
                vb = h_ref[rows, col(2)].astype(BF16)
                gate = h_ref[rows, col(3)]
                st = s_ref[hd]
                sc = _dot_nt(q.astype(BF16), k.astype(BF16)) * dec_ref[hd]
                o = _dot(sc.astype(BF16), vb)
                o = o + _dot((q * qd_ref[hd]).astype(BF16), st.astype(BF16))
                s_ref[hd] = gc_ref[hd] * st + _dot_tn((k * kd_ref[hd]).astype(BF16), vb)
                ret = _group_norm_gate(o, gate, gn_ref[:, hd * LANES:(hd + 1) * LANES])
                mix_ref[rows, pd + hd * LANES:pd + (hd + 1) * LANES] = ret.astype(BF16)

        y = _dot(mix_ref[rb, :], wo_ref[...])
        o_ref[rb, :] = _layer_norm(ALPHA * x + y, lg_ref[...], lb_ref[...])

    @pl.when(j == n_j - 1)
    def _():
        pst_ref[0] = ext_ref[pl.ds(tm + 1, POOL_HIST), :]
        rst_ref[0] = s_ref[...]


def _even_layer_prompt(x, bsz, seq, w_in, tabs, pool_w, pool_scale, gn_g, w_o, ln_g, ln_b, tm):
    n_j = seq // tm
    d = x.shape[1]
    pd = pool_scale.shape[0]
    ed = pd + RET_HEADS * LANES
    body = functools.partial(_even_layer_prompt_body, tm=tm, n_j=n_j, sub=min(tm, 2 * RET_CHUNK))
    return pl.pallas_call(
        body,
        grid=(bsz, n_j),
        in_specs=[
            pl.BlockSpec((tm, d), lambda b, j: (b * n_j + j, 0)),
            _resident(w_in.shape),
            pl.BlockSpec((tm, LANES), lambda b, j: (j, 0)),
            pl.BlockSpec((tm, LANES), lambda b, j: (j, 0)),
            _resident((RET_HEADS, RET_CHUNK, RET_CHUNK)),
            _resident((RET_HEADS, RET_CHUNK, LANES)),
            _resident((RET_HEADS, RET_CHUNK, LANES)),
            _resident((RET_HEADS, 1, LANES)),
            _resident(pool_w.shape),
            _resident((1, pd)),
            _resident((1, RET_HEADS * LANES)),
            _resident(w_o.shape),
            _resident((1, d)),
            _resident((1, d)),
        ],
        out_specs=[
            pl.BlockSpec((tm, d), lambda b, j: (b * n_j + j, 0)),
            pl.BlockSpec((1, POOL_HIST, pd), lambda b, j: (b, 0, 0)),
            pl.BlockSpec((1, RET_HEADS, RET_DK, LANES), lambda b, j: (b, 0, 0, 0)),
        ],
        out_shape=[
            jax.ShapeDtypeStruct((bsz * seq, d), F32),
            jax.ShapeDtypeStruct((bsz, POOL_HIST, pd), F32),
            jax.ShapeDtypeStruct((bsz, RET_HEADS, RET_DK, LANES), F32),
        ],
        scratch_shapes=[pltpu.VMEM((tm, w_in.shape[1]), F32), pltpu.VMEM((tm, ed), BF16),
                        pltpu.VMEM((tm + 16, pd), F32), pltpu.VMEM((RET_HEADS, RET_DK, LANES), F32)],
        compiler_params=_cparams(("parallel", "arbitrary"), 40),
        name="even_layer_prompt",
    )(x, w_in, tabs["c2"], tabs["s2"], tabs["dec"], tabs["qd"], tabs["kd"], tabs["gc"],
      pool_w, pool_scale.reshape(1, pd), gn_g.reshape(1, -1), w_o, ln_g.reshape(1, d), ln_b.reshape(1, d))


def _even_mix_sample_body(h_ref, hist_ref, s0_ref, c2_ref, s2_ref, dm_ref, qd_ref, kd_ref, gc_ref,
                          pw_ref, ps_ref, gn_ref, mix_ref, hist_o_ref, s_o_ref, oc_ref,
                          *, nb, ls, cnts):
    pd = len(POOL_WINDOWS) * LANES
    ext = [hist_ref[i] for i in range(POOL_HIST)]
    ext += [h_ref[l * nb:(l + 1) * nb, 0:pd] for l in range(ls)]
    for i in range(POOL_HIST):
        hist_o_ref[i] = ext[ls + i]
    for g, w in enumerate(POOL_WINDOWS):
        cols = slice(g * LANES, (g + 1) * LANES)
        outs = []
        for l in range(ls):
            top = POOL_HIST + l
            acc = ext[top][:, cols]
            for jj in range(1, w):
                acc = acc + ext[top - jj][:, cols]
            outs.append(acc / cnts[g][l] - ext[top][:, cols])
        pooled = jnp.concatenate(outs, axis=0)
        mixed = _dot(pooled.astype(BF16), pw_ref[g]) * ps_ref[:, cols]
        mix_ref[:, cols] = mixed.astype(BF16)

    rows_b = lax.broadcasted_iota(jnp.int32, (ls * nb, 1), 0) % nb
    k_scale = RET_DK ** -0.5
    c2 = c2_ref[...]
    s2 = s2_ref[...]
    qs, ks, vs = [], [], []
    for hd in range(RET_HEADS):
        def col(part, hd=hd):
            return slice(pd + (part * RET_HEADS + hd) * LANES, pd + (part * RET_HEADS + hd + 1) * LANES)
        q = _rope_full(h_ref[:, col(0)], c2, s2)
        k = _rope_full(h_ref[:, col(1)], c2, s2) * k_scale
        vb = h_ref[:, col(2)].astype(BF16)
        sc = _dot_nt(q.astype(BF16), k.astype(BF16)) * dm_ref[hd]
        oc_ref[hd] = _dot(sc.astype(BF16), vb)
        qs.append(q * qd_ref[hd])
        ks.append(k * kd_ref[hd])
        vs.append(vb)

    def per_batch(b, carry):
        sel = rows_b == b
        for hd in range(RET_HEADS):
            st = s0_ref[b, hd]
            qm = jnp.where(sel, qs[hd], 0.0).astype(BF16)
            km = jnp.where(sel, ks[hd], 0.0).astype(BF16)
            oc_ref[hd] += _dot(qm, st.astype(BF16))
            s_o_ref[b, hd] = gc_ref[hd] * st + _dot_tn(km, vs[hd])
        return carry

    lax.fori_loop(0, nb, per_batch, 0)

    for hd in range(RET_HEADS):
        gate = h_ref[:, pd + (3 * RET_HEADS + hd) * LANES:pd + (3 * RET_HEADS + hd + 1) * LANES]
        ret = _group_norm_gate(oc_ref[hd], gate, gn_ref[:, hd * LANES:(hd + 1) * LANES])
        mix_ref[:, pd + hd * LANES:pd + (hd + 1) * LANES] = ret.astype(BF16)


def _even_mix_sample(h, hist, s0, tabs, pool_w, pool_scale, gn_g, nb, ls, past_len):
    pd = pool_scale.shape[0]
    ed = pd + RET_HEADS * LANES
    cnts = tuple(tuple(float(min(w, past_len + l + 1)) for l in range(ls)) for w in POOL_WINDOWS)
    body = functools.partial(_even_mix_sample_body, nb=nb, ls=ls, cnts=cnts)
    n = ls * nb
    args = (h, hist, s0, tabs["c2"], tabs["s2"], tabs["dm"], tabs["qd"], tabs["kd"], tabs["gc"],
            pool_w, pool_scale.reshape(1, pd), gn_g.reshape(1, -1))
    return pl.pallas_call(
        body,
        grid=(1,),
        in_specs=[_resident(a.shape) for a in args],
        out_specs=[_whole((n, ed)), _whole(hist.shape), _whole(s0.shape)],
        out_shape=[jax.ShapeDtypeStruct((n, ed), BF16),
                   jax.ShapeDtypeStruct(hist.shape, F32),
                   jax.ShapeDtypeStruct(s0.shape, F32)],
        scratch_shapes=[pltpu.VMEM((RET_HEADS, n, LANES), F32)],
        compiler_params=_cparams(("arbitrary",), 48),
        name="even_mix_sample",
    )(*args)


def _ffn_chunk(xb, wup_ref, wd_ref, cw_ref, cb_ref, c, tf, dff, shift_fn):
    cols = slice(c * tf, (c + 1) * tf)
    a = _dot(xb, wup_ref[:, cols])
    gate_in = _dot(xb, wup_ref[:, dff + c * tf:dff + (c + 1) * tf])
    a1, a2 = shift_fn(a, c)
    conv = cb_ref[:, cols] + cw_ref[0:1, cols] * a2
    conv = conv + cw_ref[1:2, cols] * a1
    conv = conv + cw_ref[2:3, cols] * a
    act = (_silu(conv) * gate_in).astype(BF16)
    return a, _dot(act, wd_ref[cols, :])


def _ffn_prompt_body(x_ref, *rest, tm, n_j, tf, dff, mixer_out):
    if mixer_out:
        a_ref, wo_ref, mg_ref, mb_ref = rest[:4]
        rest = rest[4:]
    wup_ref, wd_ref, cw_ref, cb_ref, g_ref, b_ref, o_ref, st_ref, carry_ref = rest
    j = pl.program_id(1)

    @pl.when(j == 0)
    def _():
        carry_ref[...] = jnp.zeros(carry_ref.shape, F32)

    x = x_ref[...]
    if mixer_out:
        x = _layer_norm(ALPHA * x + _dot(a_ref[...], wo_ref[...]), mg_ref[...], mb_ref[...])
    xb = x.astype(BF16)
    row = lax.broadcasted_iota(jnp.int32, (tm, tf), 0)

    def shift_fn(a, c):
        prev = carry_ref[:, c * tf:(c + 1) * tf]
        a1 = jnp.where(row == 0, prev[7:8, :], pltpu.roll(a, 1, axis=0))
        a2 = jnp.where(row == 0, prev[6:7, :], jnp.where(row == 1, prev[7:8, :], pltpu.roll(a, 2, axis=0)))
        return a1, a2

    acc = None
    for c in range(dff // tf):
        a, y = _ffn_chunk(xb, wup_ref, wd_ref, cw_ref, cb_ref, c, tf, dff, shift_fn)
        acc = y if acc is None else acc + y
        tail = a[tm - SUBLANES:tm, :]
        carry_ref[:, c * tf:(c + 1) * tf] = tail
        st_ref[0, :, c * tf:(c + 1) * tf] = tail
    o_ref[...] = _layer_norm(ALPHA * x + acc, g_ref[...], b_ref[...])


def _ffn_prompt(x, bsz, seq, layer, wup, wd, conv_w, conv_b, g, b, tm, tf, mixer_out=None):
    d = x.shape[1]
    dff = wd.shape[1]
    n_j = seq // tm
    body = functools.partial(_ffn_prompt_body, tm=tm, n_j=n_j, tf=tf, dff=dff, mixer_out=mixer_out is not None)
    row_tile = lambda bi, j: (bi * n_j + j, 0)
    pre_args, pre_specs = [], []
    if mixer_out is not None:
        a, w_o, mg, mb = mixer_out
        pre_args = [a, w_o, mg.reshape(1, d), mb.reshape(1, d)]
        pre_specs = [pl.BlockSpec((tm, a.shape[1]), row_tile), _resident(w_o.shape), _resident((1, d)), _resident((1, d))]
    return pl.pallas_call(
        body,
        grid=(bsz, n_j),
        in_specs=[pl.BlockSpec((tm, d), row_tile)] + pre_specs + [
                  _resident_layer(wup.shape, layer), _resident_layer(wd.shape, layer), _resident(conv_w.shape),
                  _resident((1, dff)), _resident((1, d)), _resident((1, d))],
        out_specs=[pl.BlockSpec((tm, d), lambda bi, j: (bi * n_j + j, 0)),
                   pl.BlockSpec((1, SUBLANES, dff), lambda bi, j: (bi, 0, 0))],
        out_shape=[jax.ShapeDtypeStruct(x.shape, F32),
                   jax.ShapeDtypeStruct((bsz, SUBLANES, dff), F32)],
        scratch_shapes=[pltpu.VMEM((SUBLANES, dff), F32)],
        compiler_params=_cparams(("parallel", "arbitrary"), 56),
        name="ffn_prompt",
    )(x, *pre_args, wup, wd, conv_w, conv_b.reshape(1, dff), g.reshape(1, d), b.reshape(1, d))


def _ffn_sample_body(x_ref, st_ref, wup_ref, wd_ref, cw_ref, cb_ref, g_ref, b_ref, o_ref, st_o_ref,
                     *, nb, ls, tf, dff):
    x = x_ref[...]
    xb = x.astype(BF16)
    nh = CONV_W - 1

    def shift_fn(a, c):
        cols = slice(c * tf, (c + 1) * tf)
        ext = [st_ref[i, :, cols] for i in range(nh)] + [a[l * nb:(l + 1) * nb, :] for l in range(ls)]
        a1 = jnp.concatenate([ext[nh + l - 1] for l in range(ls)], axis=0)
        a2 = jnp.concatenate([ext[nh + l - 2] for l in range(ls)], axis=0)
        for i in range(nh):
            st_o_ref[i, :, cols] = ext[ls + i]
        return a1, a2

    acc = None
    for c in range(dff // tf):
        _, y = _ffn_chunk(xb, wup_ref, wd_ref, cw_ref, cb_ref, c, tf, dff, shift_fn)
        acc = y if acc is None else acc + y
    o_ref[...] = _layer_norm(ALPHA * x + acc, g_ref[...], b_ref[...])


def _ffn_sample(x, st, layer, wup, wd, conv_w, conv_b, g, b, nb, ls, tf):
    d = x.shape[1]
    dff = wd.shape[1]
    body = functools.partial(_ffn_sample_body, nb=nb, ls=ls, tf=tf, dff=dff)
    args = (x, st, wup, wd, conv_w, conv_b.reshape(1, dff), g.reshape(1, d), b.reshape(1, d))
    in_specs = [_resident(a.shape) for a in args]
    in_specs[2] = _resident_layer(wup.shape, layer)
    in_specs[3] = _resident_layer(wd.shape, layer)
    return pl.pallas_call(
        body,
        grid=(1,),
        in_specs=in_specs,
        out_specs=[_whole(x.shape), _whole(st.shape)],
        out_shape=[jax.ShapeDtypeStruct(x.shape, F32), jax.ShapeDtypeStruct(st.shape, F32)],
        compiler_params=_cparams(("arbitrary",), 40),
        name="ffn_sample",
    )(*args)


def _rope_pe(blk, cc, s1, s2):
    return blk * cc + pltpu.roll(blk, 96, axis=1) * s1 + pltpu.roll(blk, 32, axis=1) * s2


def _rms_norm(x, g):
    ms = jnp.mean(x * x, axis=-1, keepdims=True)
    return x * lax.rsqrt(ms + RMS_EPS) * g


def _mla_proj_body(x_ref, cc_ref, s1_ref, s2_ref, wdq_ref, qg_ref, wuq_ref, wdkv_ref, kvg_ref, *rest,
                   decode, sub):
    hw = 2 * LANES
    for r0 in range(0, x_ref.shape[0], sub):
        rb = slice(r0, r0 + sub)
        cc = cc_ref[rb, :]
        s1 = s1_ref[rb, :]
        s2 = s2_ref[rb, :]
        xb = x_ref[rb, :].astype(BF16)
        cq = _rms_norm(_dot(xb, wdq_ref[...]), qg_ref[...])
        q = _dot(cq.astype(BF16), wuq_ref[...])
        kv = _dot(xb, wdkv_ref[...])
        ckv = _rms_norm(kv[:, 0:KV_LORA], kvg_ref[...])
        kpe = _rope_pe(kv[:, KV_LORA:KV_LORA + LANES], cc, s1, s2)
        if decode:
            wukt_ref, ql_ref, qp_ref, ckv_ref, kpe_ref = rest
            for h in range(MLA_HEADS):
                qn = q[:, h * hw:h * hw + LANES].astype(BF16)
                ql_ref[rb, h * KV_LORA:(h + 1) * KV_LORA] = _dot(qn, wukt_ref[h]).astype(BF16)
                qp = _rope_pe(q[:, h * hw + LANES:(h + 1) * hw], cc, s1, s2)
                qp_ref[rb, h * LANES:(h + 1) * LANES] = qp.astype(BF16)
        else:
            wuk_ref, wuv_ref, qo_ref, ko_ref, vo_ref, ckv_ref, kpe_ref = rest
            cb = ckv.astype(BF16)
            kn = _dot(cb, wuk_ref[...])
            vo_ref[rb, :] = _dot(cb, wuv_ref[...]).astype(BF16)
            kpb = kpe.astype(BF16)
            for h in range(MLA_HEADS):
                qo_ref[rb, h * hw:h * hw + LANES] = q[:, h * hw:h * hw + LANES].astype(BF16)
                qp = _rope_pe(q[:, h * hw + LANES:(h + 1) * hw], cc, s1, s2)
                qo_ref[rb, h * hw + LANES:(h + 1) * hw] = qp.astype(BF16)
                ko_ref[rb, h * hw:h * hw + LANES] = kn[:, h * LANES:(h + 1) * LANES].astype(BF16)
                ko_ref[rb, h * hw + LANES:(h + 1) * hw] = kpb
        ckv_ref[rb, :] = ckv
        kpe_ref[rb, :] = kpe[:, 0:QK_ROPE]


def _mla_proj(x, tabs, w, tm, n_pos_blocks, decode):
    t, d = x.shape
    hw = 2 * LANES
    body = functools.partial(_mla_proj_body, decode=decode, sub=tm)
    row = lambda i: (i, 0)
    tab = lambda i: (i % n_pos_blocks, 0)
    ins = [x, tabs["cc"], tabs["s1"], tabs["s2"], w["dq"], w["qg"], w["uq"], w["dkv"], w["kvg"]]
    in_specs = [pl.BlockSpec((tm, d), row)] + [pl.BlockSpec((tm, LANES), tab)] * 3
    in_specs += [_resident(a.shape) for a in ins[4:]]
    if decode:
        ins += [w["ukt"]]
        in_specs += [_resident(w["ukt"].shape)]
        outs = [(MLA_HEADS * KV_LORA, BF16), (MLA_HEADS * LANES, BF16)]
    else:
        ins += [w["uk"], w["uv"]]
        in_specs += [_resident(w["uk"].shape), _resident(w["uv"].shape)]
        outs = [(MLA_HEADS * hw, BF16), (MLA_HEADS * hw, BF16), (MLA_HEADS * V_DIM, BF16)]
    outs += [(KV_LORA, F32), (QK_ROPE, F32)]
    return pl.pallas_call(
        body,
        grid=(t // tm,),
        in_specs=in_specs,
        out_specs=[pl.BlockSpec((tm, n), row) for n, _ in outs],
        out_shape=[jax.ShapeDtypeStruct((t, n), dt) for n, dt in outs],
        compiler_params=_cparams(("parallel",), 40),
        name="mla_proj_decode" if decode else "mla_proj",
    )(*ins)


def _online_softmax_update(s2, m_ref, l_ref, acc_ref, pv_fn, row_chunk):
    rows, width = s2.shape
    n = width // LANES
    aw = acc_ref.shape[-1] // LANES
    p_chunks, alphas = [], []
    for r0 in range(0, rows, row_chunk):
        rs_ = slice(r0, r0 + row_chunk)
        tiles = [s2[rs_, j * LANES:(j + 1) * LANES] for j in range(n)]
        mx = tiles[0]
        for t in tiles[1:]:
            mx = jnp.maximum(mx, t)
        m_prev = m_ref[rs_, :]
        m_new = jnp.maximum(m_prev, jnp.max(mx, axis=-1, keepdims=True))
        alpha = jnp.exp2(m_prev - m_new)
        ps = [jnp.exp2(t - m_new) for t in tiles]
        tot = ps[0]
        for t in ps[1:]:
            tot = tot + t
        l_ref[rs_, :] = alpha * l_ref[rs_, :] + jnp.sum(tot, axis=-1, keepdims=True)
        m_ref[rs_, :] = m_new
        p_chunks.append((jnp.concatenate(ps, axis=1) if n > 1 else ps[0]).astype(BF16))
        alphas.append(alpha)
    p = jnp.concatenate(p_chunks, axis=0) if len(p_chunks) > 1 else p_chunks[0]
    pv = pv_fn(p)
    for i, r0 in enumerate(range(0, rows, row_chunk)):
        rs_ = slice(r0, r0 + row_chunk)
        a_w = alphas[i] if aw == 1 else jnp.concatenate([alphas[i]] * aw, axis=1)
        acc_ref[rs_, :] = a_w * acc_ref[rs_, :] + pv[rs_, :]


def _flash_body(q_ref, k_ref, v_ref, o_ref, m_ref, l_ref, acc_ref, *, tq, tk, row_chunk):
    qi = pl.program_id(1)
    hw = 2 * LANES
    m_ref[...] = jnp.full(m_ref.shape, -jnp.inf, F32)
    l_ref[...] = jnp.zeros(l_ref.shape, F32)
    acc_ref[...] = jnp.zeros(acc_ref.shape, F32)

    def step(h, key0, diag_off):
        start = pl.multiple_of(key0, tk)
        q = q_ref[:, h * hw:(h + 1) * hw]
        k = k_ref[pl.ds(start, tk), h * hw:(h + 1) * hw]
        v = v_ref[pl.ds(start, tk), h * V_DIM:(h + 1) * V_DIM]
        s = _dot_nt(q, k) * SCALE_LOG2E
        if diag_off is not None:
            row = lax.broadcasted_iota(jnp.int32, s.shape, 0)
            colm = lax.broadcasted_iota(jnp.int32, s.shape, 1)
            s = jnp.where(colm + diag_off <= row, s, -jnp.inf)
        _online_softmax_update(s, m_ref.at[h], l_ref.at[h], acc_ref.at[h], lambda p: _dot(p, v), row_chunk)

    def loop_body(kj, carry):
        for h in range(MLA_HEADS):
            step(h, kj * tk, None)
        return carry

    lax.fori_loop(0, qi * (tq // tk), loop_body, 0)
    for h in range(MLA_HEADS):
        for j in range(tq // tk):
            step(h, qi * tq + j * tk, j * tk)
        o_ref[:, h * V_DIM:(h + 1) * V_DIM] = (acc_ref[h] / l_ref[h]).astype(BF16)


def _flash_prompt(qp, kp, vp, bsz, seq, tq, tk):
    nq = seq // tq
    body = functools.partial(_flash_body, tq=tq, tk=tk, row_chunk=min(tq, 64))
    return pl.pallas_call(
        body,
        grid=(bsz, nq),
        in_specs=[pl.BlockSpec((tq, qp.shape[1]), lambda b, i: (b * nq + i, 0)),
                  pl.BlockSpec((seq, kp.shape[1]), lambda b, i: (b, 0)),
                  pl.BlockSpec((seq, vp.shape[1]), lambda b, i: (b, 0))],
        out_specs=pl.BlockSpec((tq, vp.shape[1]), lambda b, i: (b * nq + i, 0)),
        out_shape=jax.ShapeDtypeStruct(vp.shape, BF16),
        scratch_shapes=[pltpu.VMEM((MLA_HEADS, tq, LANES), F32), pltpu.VMEM((MLA_HEADS, tq, LANES), F32),
                        pltpu.VMEM((MLA_HEADS, tq, V_DIM), F32)],
        compiler_params=_cparams(("parallel", "arbitrary"), 56),
        name="flash_prompt",
    )(qp, kp, vp)


def _decode_body(pt_ref, ql_ref, qp_ref, cn_ref, kn_ref, ckv_hbm, kpe_hbm, o_ref,
                 cbuf_ref, rbuf_ref, sem_ref, m_ref, l_ref, acc_ref, *, gp, ls, layer, n_split):
    b = pl.program_id(0)
    g = pl.program_id(1)
    n_b = pl.num_programs(0)
    n_g = pl.num_programs(1)
    slot = lax.rem(b * n_g + g, 2)
    other = 1 - slot

    def page_copies(bb, grp, slot):
        cps = []
        for i in range(gp):
            pg = pt_ref[bb, grp * gp + i]
            keys = pl.ds(i * PAGE_SIZE, PAGE_SIZE)
            cps.append(pltpu.make_async_copy(ckv_hbm.at[layer, pg], cbuf_ref.at[slot, keys, :], sem_ref.at[slot, 0]))
            cps.append(pltpu.make_async_copy(kpe_hbm.at[layer, pg], rbuf_ref.at[slot, :, keys], sem_ref.at[slot, 1]))
        return cps

    def start(bb, grp, slot):
        for cp in page_copies(bb, grp, slot):
            cp.start()

    @pl.when((b == 0) & (g == 0))
    def _():
        start(b, g, slot)

    @pl.when(g < n_g - 1)
    def _():
        start(b, g + 1, other)

    @pl.when((g == n_g - 1) & (b < n_b - 1))
    def _():
        start(b + 1, 0, other)

    @pl.when(g == 0)
    def _():
        m_ref[...] = jnp.full(m_ref.shape, -jnp.inf, F32)
        l_ref[...] = jnp.zeros(l_ref.shape, F32)
        acc_ref[...] = jnp.zeros(acc_ref.shape, F32)

    ql = ql_ref[0]
    qp = qp_ref[0]

    for cp in page_copies(b, g, slot):
        cp.wait()

    kc_len = gp * PAGE_SIZE // n_split
    kbs, scores = [], []
    for c in range(n_split):
        ks = slice(c * kc_len, (c + 1) * kc_len)
        kb = cbuf_ref[slot, ks, :].astype(BF16)
        scores.append((_dot_nt(ql, kb) + _dot(qp, rbuf_ref[slot, :, ks].astype(BF16))) * SCALE_LOG2E)
        kbs.append(kb)
    state = (m_ref.at[0], l_ref.at[0], acc_ref.at[0])
    n_rows = ql.shape[0]
    for c in range(n_split):
        _online_softmax_update(scores[c], *state, lambda p, kb=kbs[c]: _dot(p, kb), n_rows)

    @pl.when(g == n_g - 1)
    def _():
        kc = cn_ref[0].astype(BF16)
        s = (_dot_nt(ql, kc) + _dot(qp, kn_ref[0].astype(BF16))) * SCALE_LOG2E
        r = lax.broadcasted_iota(jnp.int32, s.shape, 0) // MLA_HEADS
        cidx = lax.broadcasted_iota(jnp.int32, s.shape, 1)
        s = jnp.where((cidx <= r) & (cidx < ls), s, -jnp.inf)
        _online_softmax_update(s, *state, lambda p: _dot(p, kc), n_rows)
        l_w = jnp.concatenate([l_ref[0]] * (KV_LORA // LANES), axis=1)
        o_ref[0] = acc_ref[0] / l_w


def _decode_attention(page_table, ql, qp, cn, knt, cache_ckv, cache_kpet, layer, gp):
    nb, rows, _ = ql.shape
    n_pages = page_table.shape[1]
    ls = rows // MLA_HEADS
    n_split = 4 if gp % 4 == 0 else 1
    body = functools.partial(_decode_body, gp=gp, ls=ls, layer=layer, n_split=n_split)
    per_b = lambda b, g, pt: (b, 0, 0)
    in_specs = [pl.BlockSpec((1, rows, KV_LORA), per_b), pl.BlockSpec((1, rows, QK_ROPE), per_b),
                pl.BlockSpec((1, PAGE_SIZE, KV_LORA), per_b), pl.BlockSpec((1, QK_ROPE, PAGE_SIZE), per_b),
                pl.BlockSpec(memory_space=pl.ANY), pl.BlockSpec(memory_space=pl.ANY)]
    n_buf = 2
    grid_spec = pltpu.PrefetchScalarGridSpec(
        num_scalar_prefetch=1,
        grid=(nb, n_pages // gp),
        in_specs=in_specs,
        out_specs=pl.BlockSpec((1, rows, KV_LORA), per_b),
        scratch_shapes=[pltpu.VMEM((n_buf, gp * PAGE_SIZE, KV_LORA), F32),
                        pltpu.VMEM((n_buf, QK_ROPE, gp * PAGE_SIZE), F32),
                        pltpu.SemaphoreType.DMA((n_buf, 2)),
                        pltpu.VMEM((1, rows, LANES), F32), pltpu.VMEM((1, rows, LANES), F32),
                        pltpu.VMEM((1, rows, KV_LORA), F32)],
    )
    return pl.pallas_call(
        body,
        grid_spec=grid_spec,
        out_shape=jax.ShapeDtypeStruct((nb, rows, KV_LORA), F32),
        compiler_params=_cparams(("arbitrary", "arbitrary"), 48),
        name="decode_attention",
    )(page_table, ql, qp, cn, knt, cache_ckv, cache_kpet)


def _decode_out_body(ol_ref, wuv_ref, wo_ref, r_ref, g_ref, b_ref, o_ref):
    y = None
    for h in range(MLA_HEADS):
        oh = _dot(ol_ref[h].astype(BF16), wuv_ref[h]).astype(BF16)
        t = _dot(oh, wo_ref[h * V_DIM:(h + 1) * V_DIM, :])
        y = t if y is None else y + t
    o_ref[...] = _layer_norm(ALPHA * r_ref[...] + y, g_ref[...], b_ref[...])


def _decode_out(ol, wuv3, wo, res, g, b):
    d = res.shape[1]
    args = (ol, wuv3, wo, res, g.reshape(1, d), b.reshape(1, d))
    return pl.pallas_call(
        _decode_out_body,
        grid=(1,),
        in_specs=[_resident(a.shape) for a in args],
        out_specs=_whole(res.shape),
        out_shape=jax.ShapeDtypeStruct(res.shape, F32),
        compiler_params=_cparams(("arbitrary",), 24),
        name="decode_out",
    )(*args)


def _rope_angles(pos, half):
    inv_freq = ROPE_THETA ** (-np.arange(half, dtype=np.float64) / half)
    ang = np.asarray(pos, np.float64)[:, None] * inv_freq[None, :]
    return np.cos(ang), np.sin(ang)


def _f32(tabs):
    return {k: np.ascontiguousarray(v, dtype=np.float32) for k, v in tabs.items()}


def _ret_tables(pos, chunk, rows_per_pos):
    cos, sin = _rope_angles(pos, RET_DK // 2)
    log_gamma = np.log(1.0 - 2.0 ** (-5.0 - np.arange(RET_HEADS, dtype=np.float64)))
    idx = np.repeat(np.arange(chunk, dtype=np.float64), rows_per_pos)
    diff = idx[:, None] - idx[None, :]
    dec = np.where(diff >= 0, np.exp(np.maximum(diff, 0.0)[None] * log_gamma[:, None, None]), 0.0)
    q_dec = np.exp((idx + 1.0)[None, :] * log_gamma[:, None])
    k_dec = np.exp((chunk - 1.0 - idx)[None, :] * log_gamma[:, None])
    gc = np.exp(chunk * log_gamma)
    n = idx.shape[0]
    return _f32({
        "c2": np.concatenate([cos, cos], axis=1),
        "s2": np.concatenate([-sin, sin], axis=1),
        "dec": dec,
        "qd": np.broadcast_to(q_dec[:, :, None], (RET_HEADS, n, LANES)),
        "kd": np.broadcast_to(k_dec[:, :, None], (RET_HEADS, n, LANES)),
        "gc": np.broadcast_to(gc[:, None, None], (RET_HEADS, 1, LANES)),
    })


def _pe_tables(pos):
    cos, sin = _rope_angles(pos, QK_ROPE // 2)
    z = np.zeros_like(cos)
    return _f32({"cc": np.concatenate([cos, cos, z, z], axis=1),
                 "s1": np.concatenate([-sin, z, z, z], axis=1),
                 "s2": np.concatenate([z, sin, z, z], axis=1)})


def _pad_heads(w, nope, rope):
    k = w.shape[0]
    w3 = w.reshape(k, MLA_HEADS, nope + rope)
    pad = jnp.zeros((k, MLA_HEADS, 2 * LANES - nope - rope), w.dtype)
    return jnp.concatenate([w3, pad], axis=2).reshape(k, MLA_HEADS * 2 * LANES)


def _tiles(seq, n_pages, dff):
    def fit(t, n=seq):
        while n % t:
            t //= 2
        return t
    return {"tok": fit(512), "mix": fit(512), "ffn": fit(512), "attn": fit(512), "attn_k": fit(512), "ff_chunk": dff,
            "pages": fit(64, n_pages)}


def kernel(x_prompt, x_sample, state_pool, state_ret, cache_ckv, cache_kpe, state_conv, page_table,
           w_in_even, pool_w, pool_scale, ret_gn_g, w_o_even,
           w_dq, q_norm_g, w_uq, w_dkv, kv_norm_g, w_uk, w_uv, w_o_mla,
           w_up, conv_w, conv_b, w_down, ln_mix_g, ln_mix_b, ln_ffn_g, ln_ffn_b):
    bp, lp, d = x_prompt.shape
    bs, ls, _ = x_sample.shape
    past_len = page_table.shape[1] * PAGE_SIZE
    depth = w_up.shape[0]
    dff = w_down.shape[1]
    tl = _tiles(lp, page_table.shape[1], dff)
    ns = bs * ls
    assert lp % RET_CHUNK == 0 and ls % RET_CHUNK != 0 and tl["mix"] % RET_CHUNK == 0
    assert dff % tl["ff_chunk"] == 0 and page_table.shape[1] % tl["pages"] == 0 and ls <= PAGE_SIZE

    xp = x_prompt.reshape(bp * lp, d)
    xs = jnp.swapaxes(x_sample, 0, 1).reshape(ns, d)
    pos_p = np.arange(lp)
    pos_s_rows = np.repeat(past_len + np.arange(ls), bs)

    wup_all = w_up.astype(BF16)
    wd_all = w_down.astype(BF16)
    outs = {k: [] for k in ("pool_p", "pool_s", "ret_p", "ret_s", "ckv_p", "ckv_s", "kpe_p", "kpe_s",
                            "conv_p", "conv_s")}
    for layer in range(depth):
        mixer_out = None
        if layer % 2 == 0:
            e = layer // 2
            w_in = w_in_even[e].astype(BF16)
            w_o = w_o_even[e].astype(BF16)
            pw = pool_w[e].astype(BF16)
            tp = _ret_tables(pos_p, RET_CHUNK, 1)
            xp, pst, rst = _even_layer_prompt(xp, bp, lp, w_in, tp, pw, pool_scale[e], ret_gn_g[e], w_o,
                                              ln_mix_g[layer], ln_mix_b[layer], tl["mix"])
            outs["pool_p"].append(pst)
            outs["ret_p"].append(rst)
            ts = _ret_tables(pos_s_rows, ls, bs)
            rb = np.arange(ns) % bs
            ts["dm"] = np.where((rb[:, None] == rb[None, :])[None], ts["dec"], np.float32(0.0))
            hs =_matmul(xs, w_in, ns)
            hist = jnp.swapaxes(state_pool[e], 0, 1)
            mix_s, hist_new, s_new = _even_mix_sample(hs, hist, state_ret[e], ts, pw, pool_scale[e],
                                                      ret_gn_g[e], bs, ls, past_len)
            outs["pool_s"].append(jnp.swapaxes(hist_new, 0, 1))
            outs["ret_s"].append(s_new)
            xs = _matmul_res_ln(mix_s, w_o, xs, ln_mix_g[layer], ln_mix_b[layer], ns)
        else:
            o = layer // 2
            hw = 2 * LANES
            w = {
                "dq": w_dq[o].astype(BF16),
                "qg": q_norm_g[o].reshape(1, -1),
                "uq": _pad_heads(w_uq[o], QK_NOPE, QK_ROPE).astype(BF16),
                "dkv": jnp.pad(w_dkv[o], ((0, 0), (0, KV_LORA + LANES - w_dkv.shape[2]))).astype(BF16),
                "kvg": kv_norm_g[o].reshape(1, -1),
                "uk": w_uk[o].reshape(KV_LORA, MLA_HEADS * QK_NOPE).astype(BF16),
                "uv": w_uv[o].reshape(KV_LORA, MLA_HEADS * V_DIM).astype(BF16),
                "ukt": jnp.transpose(w_uk[o], (1, 2, 0)).astype(BF16),
            }
            w_o = w_o_mla[o].astype(BF16)
            qp, kp, vp, ckv_p, kpe_p = _mla_proj(xp, _pe_tables(pos_p), w, tl["tok"], lp // tl["tok"], False)
            att = _flash_prompt(qp, kp, vp, bp, lp, tl["attn"], tl["attn_k"])
            outs["ckv_p"].append(ckv_p.reshape(bp, lp, KV_LORA))
            outs["kpe_p"].append(kpe_p.reshape(bp, lp, QK_ROPE))
            mixer_out = (att, w_o, ln_mix_g[layer], ln_mix_b[layer])
            ql, qpe, ckv_s, kpe_s = _mla_proj(xs, _pe_tables(pos_s_rows), w, ns, 1, True)
            rows = ls * MLA_HEADS

            def per_batch(a, width):
                return jnp.transpose(a.reshape(ls, bs, MLA_HEADS, width), (1, 0, 2, 3)).reshape(bs, rows, width)

            ql_b = per_batch(ql, KV_LORA)
            qp_b = per_batch(qpe, LANES)[:, :, 0:QK_ROPE]
            ckv_sb = jnp.swapaxes(ckv_s.reshape(ls, bs, KV_LORA), 0, 1)
            kpe_sb = jnp.swapaxes(kpe_s.reshape(ls, bs, QK_ROPE), 0, 1)
            cn = jnp.pad(ckv_sb, ((0, 0), (0, PAGE_SIZE - ls), (0, 0)))
            knt = jnp.swapaxes(jnp.pad(kpe_sb, ((0, 0), (0, PAGE_SIZE - ls), (0, 0))), 1, 2)
            o_lat = _decode_attention(page_table, ql_b, qp_b, cn, knt, cache_ckv,
                                      jnp.swapaxes(cache_kpe, 2, 3), o, tl["pages"])
            outs["ckv_s"].append(ckv_sb)
            outs["kpe_s"].append(kpe_sb)
            ol = jnp.transpose(o_lat.reshape(bs, ls, MLA_HEADS, KV_LORA), (2, 1, 0, 3)).reshape(MLA_HEADS, ns, KV_LORA)
            wuv3 = jnp.transpose(w_uv[o], (1, 0, 2)).astype(BF16)
            xs = _decode_out(ol, wuv3, w_o, xs, ln_mix_g[layer], ln_mix_b[layer])
        xp, st_p = _ffn_prompt(xp, bp, lp, layer, wup_all, wd_all, conv_w[layer], conv_b[layer],
                               ln_ffn_g[layer], ln_ffn_b[layer], tl["ffn"], tl["ff_chunk"], mixer_out)
        outs["conv_p"].append(st_p[:, SUBLANES - (CONV_W - 1):, :])
        st_s = jnp.swapaxes(state_conv[layer], 0, 1)
        xs, st_s_new = _ffn_sample(xs, st_s, layer, wup_all, wd_all, conv_w[layer], conv_b[layer],
                                   ln_ffn_g[layer], ln_ffn_b[layer], bs, ls, tl["ff_chunk"])
        outs["conv_s"].append(jnp.swapaxes(st_s_new, 0, 1))

    y_p = xp.reshape(bp, lp, d)
    y_s = jnp.swapaxes(xs.reshape(ls, bs, d), 0, 1)
    return (y_p, y_s,
            jnp.stack(outs["pool_p"]), jnp.stack(outs["pool_s"]),
            jnp.stack(outs["ret_p"]), jnp.stack(outs["ret_s"]),
            jnp.stack(outs["ckv_p"]), jnp.stack(outs["ckv_s"]),
            jnp.stack(outs["kpe_p"]), jnp.stack(outs["kpe_s"]),
            jnp.stack(outs["conv_p"]), jnp.stack(outs["conv_s"]))
```

```python
import functools

import jax
import jax.numpy as jnp
import numpy as np
from jax import lax
from jax.experimental import pallas as pl
from jax.experimental.pallas import tpu as pltpu

F32 = jnp.float32
BF16 = jnp.bfloat16

PAGE_SIZE = 128
POOL_WINDOWS = (2, 4, 8, 16)
POOL_HIST = max(POOL_WINDOWS) - 1
RET_HEADS = 4
RET_DK = 128
RET_CHUNK = 128
MLA_HEADS = 8
QK_NOPE = 128
QK_ROPE = 64
V_DIM = 128
KV_LORA = 256
CONV_W = 3
DEPTH = 2
ALPHA = (2.0 * DEPTH) ** 0.25
ROPE_THETA = 10000.0
LN_EPS = 1e-5
RMS_EPS = 1e-6
GN_EPS = 1e-6
MLA_SCALE = (QK_NOPE + QK_ROPE) ** -0.5
LOG2E = 1.4426950408889634
SCALE_LOG2E = MLA_SCALE * LOG2E

LANES = 128
SUBLANES = 8
MIB = 1024 * 1024


def _cparams(sem, vmem_mib):
    return pltpu.CompilerParams(dimension_semantics=sem, vmem_limit_bytes=int(vmem_mib * MIB))


def _resident(shape):
    nd = len(shape)
    return pl.BlockSpec(shape, lambda *_: (0,) * nd, pipeline_mode=pl.Buffered(1))


def _resident_layer(stacked_shape, layer):
    nd = len(stacked_shape) - 1
    return pl.BlockSpec((None,) + tuple(stacked_shape[1:]), lambda *_: (layer,) + (0,) * nd,
                        pipeline_mode=pl.Buffered(1))


def _whole(shape):
    nd = len(shape)
    return pl.BlockSpec(shape, lambda *_: (0,) * nd)


def _dot(a, b):
    return jnp.dot(a, b, preferred_element_type=F32)


def _dot_nt(a, b):
    return lax.dot_general(a, b, (((1,), (1,)), ((), ())), preferred_element_type=F32)


def _dot_tn(a, b):
    return lax.dot_general(a, b, (((0,), (0,)), ((), ())), preferred_element_type=F32)


def _layer_norm(z, g, b):
    mu = jnp.mean(z, axis=-1, keepdims=True)
    d = z - mu
    var = jnp.mean(d * d, axis=-1, keepdims=True)
    return d * lax.rsqrt(var + LN_EPS) * g + b


def _silu(x):
    return x * jax.nn.sigmoid(x)


def _mm_body(x_ref, w_ref, o_ref):
    o_ref[...] = _dot(x_ref[...].astype(BF16), w_ref[...]).astype(o_ref.dtype)


def _matmul(x, w, tm, out_dtype=F32):
    m, k = x.shape
    n = w.shape[1]
    return pl.pallas_call(
        _mm_body,
        grid=(m // tm,),
        in_specs=[pl.BlockSpec((tm, k), lambda i: (i, 0)), _resident((k, n))],
        out_specs=pl.BlockSpec((tm, n), lambda i: (i, 0)),
        out_shape=jax.ShapeDtypeStruct((m, n), out_dtype),
        compiler_params=_cparams(("parallel",), 40),
        name="matmul",
    )(x, w)


def _mm_ln_body(a_ref, w_ref, r_ref, g_ref, b_ref, o_ref):
    y = _dot(a_ref[...].astype(BF16), w_ref[...])
    o_ref[...] = _layer_norm(ALPHA * r_ref[...] + y, g_ref[...], b_ref[...])


def _matmul_res_ln(a, w, res, g, b, tm):
    m, k = a.shape
    n = w.shape[1]
    return pl.pallas_call(
        _mm_ln_body,
        grid=(m // tm,),
        in_specs=[pl.BlockSpec((tm, k), lambda i: (i, 0)), _resident((k, n)),
                  pl.BlockSpec((tm, n), lambda i: (i, 0)), _resident((1, n)), _resident((1, n))],
        out_specs=pl.BlockSpec((tm, n), lambda i: (i, 0)),
        out_shape=jax.ShapeDtypeStruct((m, n), F32),
        compiler_params=_cparams(("parallel",), 32),
        name="matmul_res_ln",
    )(a, w, res, g.reshape(1, n), b.reshape(1, n))


def _rope_full(x, c2, s2):
    return x * c2 + pltpu.roll(x, 64, axis=1) * s2


def _group_norm_gate(o, gate, gn_row):
    mu = jnp.mean(o, axis=-1, keepdims=True)
    d = o - mu
    var = jnp.mean(d * d, axis=-1, keepdims=True)
    return _silu(gate) * (d * lax.rsqrt(var + GN_EPS) * gn_row)


def _even_layer_prompt_body(x_ref, win_ref, c2_ref, s2_ref, dec_ref, qd_ref, kd_ref, gc_ref, pw_ref, ps_ref,
                            gn_ref, wo_ref, lg_ref, lb_ref, o_ref, pst_ref, rst_ref,
                            mix_ref, ext_ref, s_ref, *h_refs, tm, n_j, sub):
    j = pl.program_id(1)
    pd = len(POOL_WINDOWS) * LANES

    @pl.when(j == 0)
    def _():
        ext_ref[0:16, :] = jnp.zeros((16, pd), F32)
        s_ref[...] = jnp.zeros(s_ref.shape, F32)

    @pl.when(j > 0)
    def _():
        ext_ref[0:16, :] = ext_ref[tm:tm + 16, :]

    c = RET_CHUNK
    k_scale = RET_DK ** -0.5
    nblk = 4 * LANES
    def project(r0):
        h_ref = h_refs[r0 // sub]
        xb = x_ref[r0:r0 + sub, :].astype(BF16)
        for c0 in range(0, h_ref.shape[1], nblk):
            h_ref[:, c0:c0 + nblk] = _dot(xb, win_ref[:, c0:c0 + nblk])

    project(0)
    for r0 in range(0, tm, sub):
        rb = slice(r0, r0 + sub)
        h_ref = h_refs[r0 // sub]
        if r0 + sub < tm:
            project(r0 + sub)
        ext_ref[16 + r0:16 + r0 + sub, :] = h_ref[:, 0:pd]

        pos = (j * tm + r0 + lax.broadcasted_iota(jnp.int32, (sub, 1), 0)).astype(F32)
        for g, w in enumerate(POOL_WINDOWS):
            cols = slice(g * LANES, (g + 1) * LANES)
            e = ext_ref[r0:r0 + sub + 16, cols]
            u = e[16:, :]
            s = 1
            while s < w:
                e = e + pltpu.roll(e, s, axis=0)
                s *= 2
            cnt = jnp.minimum(float(w), pos + 1.0)
            pooled = e[16:, :] / cnt - u
            mixed = _dot(pooled.astype(BF16), pw_ref[g]) * ps_ref[:, cols]
            mix_ref[rb, cols] = mixed.astype(BF16)

        for ci in range(r0 // c, (r0 + sub) // c):
            rows = slice(ci * c, (ci + 1) * c)
            lrows = slice(ci * c - r0, (ci + 1) * c - r0)
            c2 = c2_ref[rows, :]
            s2 = s2_ref[rows, :]
            for hd in range(RET_HEADS):
                def col(part, hd=hd):
                    return slice(pd + (part * RET_HEADS + hd) * LANES, pd + (part * RET_HEADS + hd + 1) * LANES)
                q = _rope_full(h_ref[lrows, col(0)], c2, s2)
                k = _rope_full(h_ref[lrows, col(1)], c2, s2) * k_scale
                vb = h_ref[lrows, col(2)].astype(BF16)
                gate = h_ref[lrows, col(3)]
                st = s_ref[hd]
                sc = _dot_nt(q.astype(BF16), k.astype(BF16)) * dec_ref[hd]
                o = _dot(sc.astype(BF16), vb)
                o = o + _dot((q * qd_ref[hd]).astype(BF16), st.astype(BF16))
                s_ref[hd] = gc_ref[hd] * st + _dot_tn((k * kd_ref[hd]).astype(BF16), vb)
                ret = _group_norm_gate(o, gate, gn_ref[:, hd * LANES:(hd + 1) * LANES])
                mix_ref[rows, pd + hd * LANES:pd + (hd + 1) * LANES] = ret.astype(BF16)

        y = _dot(mix_ref[rb, :], wo_ref[...])
        o_ref[rb, :] = _layer_norm(ALPHA * x_ref[rb, :] + y, lg_ref[...], lb_ref[...])

    @pl.when(j == n_j - 1)
    def _():
        pst_ref[0] = ext_ref[pl.ds(tm + 1, POOL_HIST), :]
        rst_ref[0] = s_ref[...]


def _even_layer_prompt(x, bsz, seq, w_in, tabs, pool_w, pool_scale, gn_g, w_o, ln_g, ln_b, tm):
    n_j = seq // tm
    d = x.shape[1]
    pd = pool_scale.shape[0]
    ed = pd + RET_HEADS * LANES
    sub = min(tm, 2 * RET_CHUNK)
    body = functools.partial(_even_layer_prompt_body, tm=tm, n_j=n_j, sub=sub)
    return pl.pallas_call(
        body,
        grid=(bsz, n_j),
        in_specs=[
            pl.BlockSpec((tm, d), lambda b, j: (b * n_j + j, 0)),
            _resident(w_in.shape),
            pl.BlockSpec((tm, LANES), lambda b, j: (j, 0)),
            pl.BlockSpec((tm, LANES), lambda b, j: (j, 0)),
            _resident((RET_HEADS, RET_CHUNK, RET_CHUNK)),
            _resident((RET_HEADS, RET_CHUNK, LANES)),
            _resident((RET_HEADS, RET_CHUNK, LANES)),
            _resident((RET_HEADS, 1, LANES)),
            _resident(pool_w.shape),
            _resident((1, pd)),
            _resident((1, RET_HEADS * LANES)),
            _resident(w_o.shape),
            _resident((1, d)),
            _resident((1, d)),
        ],
        out_specs=[
            pl.BlockSpec((tm, d), lambda b, j: (b * n_j + j, 0)),
            pl.BlockSpec((1, POOL_HIST, pd), lambda b, j: (b, 0, 0)),
            pl.BlockSpec((1, RET_HEADS, RET_DK, LANES), lambda b, j: (b, 0, 0, 0)),
        ],
        out_shape=[
            jax.ShapeDtypeStruct((bsz * seq, d), F32),
            jax.ShapeDtypeStruct((bsz, POOL_HIST, pd), F32),
            jax.ShapeDtypeStruct((bsz, RET_HEADS, RET_DK, LANES), F32),
        ],
        scratch_shapes=[pltpu.VMEM((tm, ed), BF16), pltpu.VMEM((tm + 16, pd), F32),
                        pltpu.VMEM((RET_HEADS, RET_DK, LANES), F32)]
        + [pltpu.VMEM((sub, w_in.shape[1]), F32)] * (tm // sub),
        compiler_params=_cparams(("parallel", "arbitrary"), 40),
        name="even_layer_prompt",
    )(x, w_in, tabs["c2"], tabs["s2"], tabs["dec"], tabs["qd"], tabs["kd"], tabs["gc"],
      pool_w, pool_scale.reshape(1, pd), gn_g.reshape(1, -1), w_o, ln_g.reshape(1, d), ln_b.reshape(1, d))


def _even_mix_sample_body(h_ref, hist_ref, s0_ref, c2_ref, s2_ref, dm_ref, qd_ref, kd_ref, gc_ref,
                          pw_ref, ps_ref, gn_ref, mix_ref, hist_o_ref, s_o_ref, oc_ref,
                          *, nb, ls, cnts):
    pd = len(POOL_WINDOWS) * LANES
    ext = [hist_ref[i] for i in range(POOL_HIST)]
    ext += [h_ref[l * nb:(l + 1) * nb, 0:pd] for l in range(ls)]
    for i in range(POOL_HIST):
        hist_o_ref[i] = ext[ls + i]
    for g, w in enumerate(POOL_WINDOWS):
        cols = slice(g * LANES, (g + 1) * LANES)
        outs = []
        for l in range(ls):
            top = POOL_HIST + l
            acc = ext[top][:, cols]
            for jj in range(1, w):
                acc = acc + ext[top - jj][:, cols]
            outs.append(acc / cnts[g][l] - ext[top][:, cols])
        pooled = jnp.concatenate(outs, axis=0)
        mixed = _dot(pooled.astype(BF16), pw_ref[g]) * ps_ref[:, cols]
        mix_ref[:, cols] = mixed.astype(BF16)

    rows_b = lax.broadcasted_iota(jnp.int32, (ls * nb, 1), 0) % nb
    k_scale = RET_DK ** -0.5
    c2 = c2_ref[...]
    s2 = s2_ref[...]
    qs, ks, vs = [], [], []
    for hd in range(RET_HEADS):
        def col(part, hd=hd):
            return slice(pd + (part * RET_HEADS + hd) * LANES, pd + (part * RET_HEADS + hd + 1) * LANES)
        q = _rope_full(h_ref[:, col(0)], c2, s2)
        k = _rope_full(h_ref[:, col(1)], c2, s2) * k_scale
        vb = h_ref[:, col(2)].astype(BF16)
        sc = _dot_nt(q.astype(BF16), k.astype(BF16)) * dm_ref[hd]
        oc_ref[hd] = _dot(sc.astype(BF16), vb)
        qs.append(q * qd_ref[hd])
        ks.append(k * kd_ref[hd])
        vs.append(vb)

    def per_batch(b, carry):
        sel = rows_b == b
        for hd in range(RET_HEADS):
            st = s0_ref[b, hd]
            qm = jnp.where(sel, qs[hd], 0.0).astype(BF16)
            km = jnp.where(sel, ks[hd], 0.0).astype(BF16)
            oc_ref[hd] += _dot(qm, st.astype(BF16))
            s_o_ref[b, hd] = gc_ref[hd] * st + _dot_tn(km, vs[hd])
        return carry

    lax.fori_loop(0, nb, per_batch, 0)

    for hd in range(RET_HEADS):
        gate = h_ref[:, pd + (3 * RET_HEADS + hd) * LANES:pd + (3 * RET_HEADS + hd + 1) * LANES]
        ret = _group_norm_gate(oc_ref[hd], gate, gn_ref[:, hd * LANES:(hd + 1) * LANES])
        mix_ref[:, pd + hd * LANES:pd + (hd + 1) * LANES] = ret.astype(BF16)


def _even_mix_sample(h, hist, s0, tabs, pool_w, pool_scale, gn_g, nb, ls, past_len):
    pd = pool_scale.shape[0]
    ed = pd + RET_HEADS * LANES
    cnts = tuple(tuple(float(min(w, past_len + l + 1)) for l in range(ls)) for w in POOL_WINDOWS)
    body = functools.partial(_even_mix_sample_body, nb=nb, ls=ls, cnts=cnts)
    n = ls * nb
    args = (h, hist, s0, tabs["c2"], tabs["s2"], tabs["dm"], tabs["qd"], tabs["kd"], tabs["gc"],
            pool_w, pool_scale.reshape(1, pd), gn_g.reshape(1, -1))
    return pl.pallas_call(
        body,
        grid=(1,),
        in_specs=[_resident(a.shape) for a in args],
        out_specs=[_whole((n, ed)), _whole(hist.shape), _whole(s0.shape)],
        out_shape=[jax.ShapeDtypeStruct((n, ed), BF16),
                   jax.ShapeDtypeStruct(hist.shape, F32),
                   jax.ShapeDtypeStruct(s0.shape, F32)],
        scratch_shapes=[pltpu.VMEM((RET_HEADS, n, LANES), F32)],
        compiler_params=_cparams(("arbitrary",), 48),
        name="even_mix_sample",
    )(*args)


def _ffn_chunk(xb, wup_ref, wd_ref, cw_ref, cb_ref, c, tf, dff, shift_fn):
    cols = slice(c * tf, (c + 1) * tf)
    a = _dot(xb, wup_ref[:, cols])
    gate_in = _dot(xb, wup_ref[:, dff + c * tf:dff + (c + 1) * tf])
    a1, a2 = shift_fn(a, c)
    conv = cb_ref[:, cols] + cw_ref[0:1, cols] * a2
    conv = conv + cw_ref[1:2, cols] * a1
    conv = conv + cw_ref[2:3, cols] * a
    act = (_silu(conv) * gate_in).astype(BF16)
    return a, _dot(act, wd_ref[cols, :])


def _ffn_prompt_body(x_ref, *rest, tm, n_j, tf, dff, mixer_out, sub):
    if mixer_out:
        a_ref, wo_ref, mg_ref, mb_ref = rest[:4]
        rest = rest[4:]
    wup_ref, wd_ref, cw_ref, cb_ref, g_ref, b_ref, o_ref, st_ref, carry_ref = rest
    j = pl.program_id(1)

    @pl.when(j == 0)
    def _():
        carry_ref[...] = jnp.zeros(carry_ref.shape, F32)

    row = lax.broadcasted_iota(jnp.int32, (sub, tf), 0)

    def shift_fn(a, c):
        prev = carry_ref[:, c * tf:(c + 1) * tf]
        a1 = jnp.where(row == 0, prev[7:8, :], pltpu.roll(a, 1, axis=0))
        a2 = jnp.where(row == 0, prev[6:7, :], jnp.where(row == 1, prev[7:8, :], pltpu.roll(a, 2, axis=0)))
        return a1, a2

    for r0 in range(0, tm, sub):
        rb = slice(r0, r0 + sub)
        x = x_ref[rb, :]
        if mixer_out:
            x = _layer_norm(ALPHA * x + _dot(a_ref[rb, :], wo_ref[...]), mg_ref[...], mb_ref[...])
        xb = x.astype(BF16)
        acc = None
        for c in range(dff // tf):
            a, y = _ffn_chunk(xb, wup_ref, wd_ref, cw_ref, cb_ref, c, tf, dff, shift_fn)
            acc = y if acc is None else acc + y
            carry_ref[:, c * tf:(c + 1) * tf] = a[sub - SUBLANES:sub, :]
        o_ref[rb, :] = _layer_norm(ALPHA * x + acc, g_ref[...], b_ref[...])
    st_ref[0] = carry_ref[...]


def _ffn_prompt(x, bsz, seq, layer, wup, wd, conv_w, conv_b, g, b, tm, tf, mixer_out=None):
    d = x.shape[1]
    dff = wd.shape[1]
    n_j = seq // tm
    body = functools.partial(_ffn_prompt_body, tm=tm, n_j=n_j, tf=tf, dff=dff, mixer_out=mixer_out is not None,
                             sub=tm)
    row_tile = lambda bi, j: (bi * n_j + j, 0)
    pre_args, pre_specs = [], []
    if mixer_out is not None:
        a, w_o, mg, mb = mixer_out
        pre_args = [a, w_o, mg.reshape(1, d), mb.reshape(1, d)]
        pre_specs = [pl.BlockSpec((tm, a.shape[1]), row_tile), _resident(w_o.shape), _resident((1, d)), _resident((1, d))]
    return pl.pallas_call(
        body,
        grid=(bsz, n_j),
        in_specs=[pl.BlockSpec((tm, d), row_tile)] + pre_specs + [
                  _resident_layer(wup.shape, layer), _resident_layer(wd.shape, layer), _resident(conv_w.shape),
                  _resident((1, dff)), _resident((1, d)), _resident((1, d))],
        out_specs=[pl.BlockSpec((tm, d), lambda bi, j: (bi * n_j + j, 0)),
                   pl.BlockSpec((1, SUBLANES, dff), lambda bi, j: (bi, 0, 0))],
        out_shape=[jax.ShapeDtypeStruct(x.shape, F32),
                   jax.ShapeDtypeStruct((bsz, SUBLANES, dff), F32)],
        scratch_shapes=[pltpu.VMEM((SUBLANES, dff), F32)],
        compiler_params=_cparams(("parallel", "arbitrary"), 56),
        name="ffn_prompt",
    )(x, *pre_args, wup, wd, conv_w, conv_b.reshape(1, dff), g.reshape(1, d), b.reshape(1, d))


def _ffn_sample_body(x_ref, st_ref, wup_ref, wd_ref, cw_ref, cb_ref, g_ref, b_ref, o_ref, st_o_ref,
                     *, nb, ls, tf, dff):
    x = x_ref[...]
    xb = x.astype(BF16)
    nh = CONV_W - 1

    def shift_fn(a, c):
        cols = slice(c * tf, (c + 1) * tf)
        ext = [st_ref[i, :, cols] for i in range(nh)] + [a[l * nb:(l + 1) * nb, :] for l in range(ls)]
        a1 = jnp.concatenate([ext[nh + l - 1] for l in range(ls)], axis=0)
        a2 = jnp.concatenate([ext[nh + l - 2] for l in range(ls)], axis=0)
        for i in range(nh):
            st_o_ref[i, :, cols] = ext[ls + i]
        return a1, a2

    acc = None
    for c in range(dff // tf):
        _, y = _ffn_chunk(xb, wup_ref, wd_ref, cw_ref, cb_ref, c, tf, dff, shift_fn)
        acc = y if acc is None else acc + y
    o_ref[...] = _layer_norm(ALPHA * x + acc, g_ref[...], b_ref[...])


def _ffn_sample(x, st, layer, wup, wd, conv_w, conv_b, g, b, nb, ls, tf):
    d = x.shape[1]
    dff = wd.shape[1]
    body = functools.partial(_ffn_sample_body, nb=nb, ls=ls, tf=tf, dff=dff)
    args = (x, st, wup, wd, conv_w, conv_b.reshape(1, dff), g.reshape(1, d), b.reshape(1, d))
    in_specs = [_resident(a.shape) for a in args]
    in_specs[2] = _resident_layer(wup.shape, layer)
    in_specs[3] = _resident_layer(wd.shape, layer)
    return pl.pallas_call(
        body,
        grid=(1,),
        in_specs=in_specs,
        out_specs=[_whole(x.shape), _whole(st.shape)],
        out_shape=[jax.ShapeDtypeStruct(x.shape, F32), jax.ShapeDtypeStruct(st.shape, F32)],
        compiler_params=_cparams(("arbitrary",), 40),
        name="ffn_sample",
    )(*args)


def _rope_pe(blk, cc, s1, s2):
    return blk * cc + pltpu.roll(blk, 96, axis=1) * s1 + pltpu.roll(blk, 32, axis=1) * s2


def _rms_norm(x, g):
    ms = jnp.mean(x * x, axis=-1, keepdims=True)
    return x * lax.rsqrt(ms + RMS_EPS) * g


def _mla_proj_body(x_ref, cc_ref, s1_ref, s2_ref, wdq_ref, qg_ref, wuq_ref, wdkv_ref, kvg_ref, *rest,
                   decode, sub):
    hw = 2 * LANES
    for r0 in range(0, x_ref.shape[0], sub):
        rb = slice(r0, r0 + sub)
        cc = cc_ref[rb, :]
        s1 = s1_ref[rb, :]
        s2 = s2_ref[rb, :]
        xb = x_ref[rb, :].astype(BF16)
        cq = _rms_norm(_dot(xb, wdq_ref[...]), qg_ref[...])
        q = _dot(cq.astype(BF16), wuq_ref[...])
        kv = _dot(xb, wdkv_ref[...])
        ckv = _rms_norm(kv[:, 0:KV_LORA], kvg_ref[...])
        kpe = _rope_pe(kv[:, KV_LORA:KV_LORA + LANES], cc, s1, s2)
        if decode:
            wukt_ref, ql_ref, qp_ref, ckv_ref, kpe_ref = rest
            for h in range(MLA_HEADS):
                qn = q[:, h * hw:h * hw + LANES].astype(BF16)
                ql_ref[rb, h * KV_LORA:(h + 1) * KV_LORA] = _dot(qn, wukt_ref[h]).astype(BF16)
                qp = _rope_pe(q[:, h * hw + LANES:(h + 1) * hw], cc, s1, s2)
                qp_ref[rb, h * LANES:(h + 1) * LANES] = qp.astype(BF16)
        else:
            wuk_ref, wuv_ref, qo_ref, ko_ref, vo_ref, ckv_ref, kpe_ref = rest
            cb = ckv.astype(BF16)
            kn = _dot(cb, wuk_ref[...])
            vo_ref[rb, :] = _dot(cb, wuv_ref[...]).astype(BF16)
            kpb = kpe.astype(BF16)
            for h in range(MLA_HEADS):
                qo_ref[rb, h * hw:h * hw + LANES] = q[:, h * hw:h * hw + LANES].astype(BF16)
                qp = _rope_pe(q[:, h * hw + LANES:(h + 1) * hw], cc, s1, s2)
                qo_ref[rb, h * hw + LANES:(h + 1) * hw] = qp.astype(BF16)
                ko_ref[rb, h * hw:h * hw + LANES] = kn[:, h * LANES:(h + 1) * LANES].astype(BF16)
                ko_ref[rb, h * hw + LANES:(h + 1) * hw] = kpb
        ckv_ref[rb, :] = ckv
        kpe_ref[rb, :] = kpe[:, 0:QK_ROPE]


def _mla_proj(x, tabs, w, tm, n_pos_blocks, decode):
    t, d = x.shape
    hw = 2 * LANES
    body = functools.partial(_mla_proj_body, decode=decode, sub=tm)
    row = lambda i: (i, 0)
    tab = lambda i: (i % n_pos_blocks, 0)
    ins = [x, tabs["cc"], tabs["s1"], tabs["s2"], w["dq"], w["qg"], w["uq"], w["dkv"], w["kvg"]]
    in_specs = [pl.BlockSpec((tm, d), row)] + [pl.BlockSpec((tm, LANES), tab)] * 3
    in_specs += [_resident(a.shape) for a in ins[4:]]
    if decode:
        ins += [w["ukt"]]
        in_specs += [_resident(w["ukt"].shape)]
        outs = [(MLA_HEADS * KV_LORA, BF16), (MLA_HEADS * LANES, BF16)]
    else:
        ins += [w["uk"], w["uv"]]
        in_specs += [_resident(w["uk"].shape), _resident(w["uv"].shape)]
        outs = [(MLA_HEADS * hw, BF16), (MLA_HEADS * hw, BF16), (MLA_HEADS * V_DIM, BF16)]
    outs += [(KV_LORA, F32), (QK_ROPE, F32)]
    return pl.pallas_call(
        body,
        grid=(t // tm,),
        in_specs=in_specs,
        out_specs=[pl.BlockSpec((tm, n), row) for n, _ in outs],
        out_shape=[jax.ShapeDtypeStruct((t, n), dt) for n, dt in outs],
        compiler_params=_cparams(("parallel",), 40),
        name="mla_proj_decode" if decode else "mla_proj",
    )(*ins)


def _online_softmax_update(s2, m_ref, l_ref, acc_ref, pv_fn, row_chunk):
    rows, width = s2.shape
    n = width // LANES
    aw = acc_ref.shape[-1] // LANES
    p_chunks, alphas = [], []
    for r0 in range(0, rows, row_chunk):
        rs_ = slice(r0, r0 + row_chunk)
        tiles = [s2[rs_, j * LANES:(j + 1) * LANES] for j in range(n)]
        mx = tiles[0]
        for t in tiles[1:]:
            mx = jnp.maximum(mx, t)
        m_prev = m_ref[rs_, :]
        m_new = jnp.maximum(m_prev, jnp.max(mx, axis=-1, keepdims=True))
        alpha = jnp.exp2(m_prev - m_new)
        ps = [jnp.exp2(t - m_new) for t in tiles]
        if l_ref is not None:
            tot = ps[0]
            for t in ps[1:]:
                tot = tot + t
            l_ref[rs_, :] = alpha * l_ref[rs_, :] + jnp.sum(tot, axis=-1, keepdims=True)
        m_ref[rs_, :] = m_new
        p_chunks.append((jnp.concatenate(ps, axis=1) if n > 1 else ps[0]).astype(BF16))
        alphas.append(alpha)
    p = jnp.concatenate(p_chunks, axis=0) if len(p_chunks) > 1 else p_chunks[0]
    pv = pv_fn(p)
    for i, r0 in enumerate(range(0, rows, row_chunk)):
        rs_ = slice(r0, r0 + row_chunk)
        a_w = alphas[i] if aw == 1 else jnp.concatenate([alphas[i]] * aw, axis=1)
        acc_ref[rs_, :] = a_w * acc_ref[rs_, :] + pv[rs_, :]


def _flash_body(q_ref, k_ref, v_ref, o_ref, m_ref, acc_ref, *, tq, tk, row_chunk, head_group):
    qi = pl.program_id(1)
    hw = 2 * LANES
    m_ref[...] = jnp.full(m_ref.shape, -jnp.inf, F32)
    acc_ref[...] = jnp.zeros(acc_ref.shape, F32)
    ones = jnp.ones((tk, LANES), BF16)

    def step(h, key0, diag_off):
        start = pl.multiple_of(key0, tk)
        q = q_ref[:, h * hw:(h + 1) * hw]
        k = k_ref[pl.ds(start, tk), h * hw:(h + 1) * hw]
        v1 = jnp.concatenate([v_ref[pl.ds(start, tk), h * V_DIM:(h + 1) * V_DIM], ones], axis=1)
        s = _dot_nt(q, k) * SCALE_LOG2E
        if diag_off is not None:
            row = lax.broadcasted_iota(jnp.int32, s.shape, 0)
            colm = lax.broadcasted_iota(jnp.int32, s.shape, 1)
            s = jnp.where(colm + diag_off <= row, s, -jnp.inf)
        _online_softmax_update(s, m_ref.at[h], None, acc_ref.at[h], lambda p: _dot(p, v1), row_chunk)

    for h0 in range(0, MLA_HEADS, head_group):
        def loop_body(kj, carry, h0=h0):
            for h in range(h0, h0 + head_group):
                step(h, kj * tk, None)
            return carry

        lax.fori_loop(0, qi * (tq // tk), loop_body, 0)
    for h in range(MLA_HEADS):
        for j in range(tq // tk):
            step(h, qi * tq + j * tk, j * tk)
        o_ref[:, h * V_DIM:(h + 1) * V_DIM] = (acc_ref[h, :, 0:V_DIM] / acc_ref[h, :, V_DIM:]).astype(BF16)


def _flash_prompt(qp, kp, vp, bsz, seq, tq, tk):
    nq = seq // tq
    body = functools.partial(_flash_body, tq=tq, tk=tk, row_chunk=min(tq, 64), head_group=MLA_HEADS)
    return pl.pallas_call(
        body,
        grid=(bsz, nq),
        in_specs=[pl.BlockSpec((tq, qp.shape[1]), lambda b, i: (b * nq + i, 0)),
                  pl.BlockSpec((seq, kp.shape[1]), lambda b, i: (b, 0)),
                  pl.BlockSpec((seq, vp.shape[1]), lambda b, i: (b, 0))],
        out_specs=pl.BlockSpec((tq, vp.shape[1]), lambda b, i: (b * nq + i, 0)),
        out_shape=jax.ShapeDtypeStruct(vp.shape, BF16),
        scratch_shapes=[pltpu.VMEM((MLA_HEADS, tq, LANES), F32), pltpu.VMEM((MLA_HEADS, tq, V_DIM + LANES), F32)],
        compiler_params=_cparams(("parallel", "arbitrary"), 56),
        name="flash_prompt",
    )(qp, kp, vp)


def _decode_body(pt_ref, ql_ref, qp_ref, cn_ref, kn_ref, ckv_hbm, kpe_hbm, o_ref,
                 cbuf_ref, rbuf_ref, sem_ref, m_ref, l_ref, acc_ref, *, gp, ls, layer, n_split):
    b = pl.program_id(0)
    g = pl.program_id(1)
    n_b = pl.num_programs(0)
    n_g = pl.num_programs(1)
    slot = lax.rem(b * n_g + g, 2)
    other = 1 - slot

    def page_copies(bb, grp, slot):
        cps = []
        for i in range(gp):
            pg = pt_ref[bb, grp * gp + i]
            keys = pl.ds(i * PAGE_SIZE, PAGE_SIZE)
            cps.append(pltpu.make_async_copy(ckv_hbm.at[layer, pg], cbuf_ref.at[slot, keys, :], sem_ref.at[slot, 0]))
            cps.append(pltpu.make_async_copy(kpe_hbm.at[layer, pg], rbuf_ref.at[slot, :, keys], sem_ref.at[slot, 1]))
        return cps

    def start(bb, grp, slot):
        for cp in page_copies(bb, grp, slot):
            cp.start()

    @pl.when((b == 0) & (g == 0))
    def _():
        start(b, g, slot)

    @pl.when(g < n_g - 1)
    def _():
        start(b, g + 1, other)

    @pl.when((g == n_g - 1) & (b < n_b - 1))
    def _():
        start(b + 1, 0, other)

    @pl.when(g == 0)
    def _():
        m_ref[...] = jnp.full(m_ref.shape, -jnp.inf, F32)
        l_ref[...] = jnp.zeros(l_ref.shape, F32)
        acc_ref[...] = jnp.zeros(acc_ref.shape, F32)

    ql = ql_ref[0]
    qp = qp_ref[0]

    for cp in page_copies(b, g, slot):
        cp.wait()

    kc_len = gp * PAGE_SIZE // n_split
    kbs, scores = [], []
    for c in range(n_split):
        ks = slice(c * kc_len, (c + 1) * kc_len)
        kb = cbuf_ref[slot, ks, :].astype(BF16)
        scores.append((_dot_nt(ql, kb) + _dot(qp, rbuf_ref[slot, :, ks].astype(BF16))) * SCALE_LOG2E)
        kbs.append(kb)
    state = (m_ref.at[0], l_ref.at[0], acc_ref.at[0])
    n_rows = ql.shape[0]
    for c in range(n_split):
        _online_softmax_update(scores[c], *state, lambda p, kb=kbs[c]: _dot(p, kb), n_rows)

    @pl.when(g == n_g - 1)
    def _():
        kc = cn_ref[0].astype(BF16)
        s = (_dot_nt(ql, kc) + _dot(qp, kn_ref[0].astype(BF16))) * SCALE_LOG2E
        r = lax.broadcasted_iota(jnp.int32, s.shape, 0) // MLA_HEADS
        cidx = lax.broadcasted_iota(jnp.int32, s.shape, 1)
        s = jnp.where((cidx <= r) & (cidx < ls), s, -jnp.inf)
        _online_softmax_update(s, *state, lambda p: _dot(p, kc), n_rows)
        l_w = jnp.concatenate([l_ref[0]] * (KV_LORA // LANES), axis=1)
        o_ref[0] = acc_ref[0] / l_w


def _decode_attention(page_table, ql, qp, cn, knt, cache_ckv, cache_kpet, layer, gp):
    nb, rows, _ = ql.shape
    n_pages = page_table.shape[1]
    ls = rows // MLA_HEADS
    n_split = 4 if gp % 4 == 0 else 1
    body = functools.partial(_decode_body, gp=gp, ls=ls, layer=layer, n_split=n_split)
    per_b = lambda b, g, pt: (b, 0, 0)
    in_specs = [pl.BlockSpec((1, rows, KV_LORA), per_b), pl.BlockSpec((1, rows, QK_ROPE), per_b),
                pl.BlockSpec((1, PAGE_SIZE, KV_LORA), per_b), pl.BlockSpec((1, QK_ROPE, PAGE_SIZE), per_b),
                pl.BlockSpec(memory_space=pl.ANY), pl.BlockSpec(memory_space=pl.ANY)]
    n_buf = 2
    grid_spec = pltpu.PrefetchScalarGridSpec(
        num_scalar_prefetch=1,
        grid=(nb, n_pages // gp),
        in_specs=in_specs,
        out_specs=pl.BlockSpec((1, rows, KV_LORA), per_b),
        scratch_shapes=[pltpu.VMEM((n_buf, gp * PAGE_SIZE, KV_LORA), F32),
                        pltpu.VMEM((n_buf, QK_ROPE, gp * PAGE_SIZE), F32),
                        pltpu.SemaphoreType.DMA((n_buf, 2)),
                        pltpu.VMEM((1, rows, LANES), F32), pltpu.VMEM((1, rows, LANES), F32),
                        pltpu.VMEM((1, rows, KV_LORA), F32)],
    )
    return pl.pallas_call(
        body,
        grid_spec=grid_spec,
        out_shape=jax.ShapeDtypeStruct((nb, rows, KV_LORA), F32),
        compiler_params=_cparams(("arbitrary", "arbitrary"), 48),
        name="decode_attention",
    )(page_table, ql, qp, cn, knt, cache_ckv, cache_kpet)


def _decode_out_body(ol_ref, wuv_ref, wo_ref, r_ref, g_ref, b_ref, o_ref):
    y = None
    for h in range(MLA_HEADS):
        oh = _dot(ol_ref[h].astype(BF16), wuv_ref[h]).astype(BF16)
        t = _dot(oh, wo_ref[h * V_DIM:(h + 1) * V_DIM, :])
        y = t if y is None else y + t
    o_ref[...] = _layer_norm(ALPHA * r_ref[...] + y, g_ref[...], b_ref[...])


def _decode_out(ol, wuv3, wo, res, g, b):
    d = res.shape[1]
    args = (ol, wuv3, wo, res, g.reshape(1, d), b.reshape(1, d))
    return pl.pallas_call(
        _decode_out_body,
        grid=(1,),
        in_specs=[_resident(a.shape) for a in args],
        out_specs=_whole(res.shape),
        out_shape=jax.ShapeDtypeStruct(res.shape, F32),
        compiler_params=_cparams(("arbitrary",), 24),
        name="decode_out",
    )(*args)


def _rope_angles(pos, half):
    inv_freq = ROPE_THETA ** (-np.arange(half, dtype=np.float64) / half)
    ang = np.asarray(pos, np.float64)[:, None] * inv_freq[None, :]
    return np.cos(ang), np.sin(ang)


def _f32(tabs):
    return {k: np.ascontiguousarray(v, dtype=np.float32) for k, v in tabs.items()}


def _ret_tables(pos, chunk, rows_per_pos):
    cos, sin = _rope_angles(pos, RET_DK // 2)
    log_gamma = np.log(1.0 - 2.0 ** (-5.0 - np.arange(RET_HEADS, dtype=np.float64)))
    idx = np.repeat(np.arange(chunk, dtype=np.float64), rows_per_pos)
    diff = idx[:, None] - idx[None, :]
    dec = np.where(diff >= 0, np.exp(np.maximum(diff, 0.0)[None] * log_gamma[:, None, None]), 0.0)
    q_dec = np.exp((idx + 1.0)[None, :] * log_gamma[:, None])
    k_dec = np.exp((chunk - 1.0 - idx)[None, :] * log_gamma[:, None])
    gc = np.exp(chunk * log_gamma)
    n = idx.shape[0]
    return _f32({
        "c2": np.concatenate([cos, cos], axis=1),
        "s2": np.concatenate([-sin, sin], axis=1),
        "dec": dec,
        "qd": np.broadcast_to(q_dec[:, :, None], (RET_HEADS, n, LANES)),
        "kd": np.broadcast_to(k_dec[:, :, None], (RET_HEADS, n, LANES)),
        "gc": np.broadcast_to(gc[:, None, None], (RET_HEADS, 1, LANES)),
    })


def _pe_tables(pos):
    cos, sin = _rope_angles(pos, QK_ROPE // 2)
    z = np.zeros_like(cos)
    return _f32({"cc": np.concatenate([cos, cos, z, z], axis=1),
                 "s1": np.concatenate([-sin, z, z, z], axis=1),
                 "s2": np.concatenate([z, sin, z, z], axis=1)})


def _pad_heads(w, nope, rope):
    k = w.shape[0]
    w3 = w.reshape(k, MLA_HEADS, nope + rope)
    pad = jnp.zeros((k, MLA_HEADS, 2 * LANES - nope - rope), w.dtype)
    return jnp.concatenate([w3, pad], axis=2).reshape(k, MLA_HEADS * 2 * LANES)


def _tiles(seq, n_pages, dff):
    def fit(t, n=seq):
        while n % t:
            t //= 2
        return t
    return {"tok": fit(512), "mix": fit(512), "ffn": fit(512), "attn": fit(512), "attn_k": fit(512), "ff_chunk": dff,
            "pages": fit(64, n_pages)}


def kernel(x_prompt, x_sample, state_pool, state_ret, cache_ckv, cache_kpe, state_conv, page_table,
           w_in_even, pool_w, pool_scale, ret_gn_g, w_o_even,
           w_dq, q_norm_g, w_uq, w_dkv, kv_norm_g, w_uk, w_uv, w_o_mla,
           w_up, conv_w, conv_b, w_down, ln_mix_g, ln_mix_b, ln_ffn_g, ln_ffn_b):
    bp, lp, d = x_prompt.shape
    bs, ls, _ = x_sample.shape
    past_len = page_table.shape[1] * PAGE_SIZE
    depth = w_up.shape[0]
    dff = w_down.shape[1]
    tl = _tiles(lp, page_table.shape[1], dff)
    ns = bs * ls
    assert lp % RET_CHUNK == 0 and ls % RET_CHUNK != 0 and tl["mix"] % RET_CHUNK == 0
    assert dff % tl["ff_chunk"] == 0 and page_table.shape[1] % tl["pages"] == 0 and ls <= PAGE_SIZE

    xp = x_prompt.reshape(bp * lp, d)
    xs = jnp.swapaxes(x_sample, 0, 1).reshape(ns, d)
    pos_p = np.arange(lp)
    pos_s_rows = np.repeat(past_len + np.arange(ls), bs)

    wup_all = w_up.astype(BF16)
    wd_all = w_down.astype(BF16)
    outs = {k: [] for k in ("pool_p", "pool_s", "ret_p", "ret_s", "ckv_p", "ckv_s", "kpe_p", "kpe_s",
                            "conv_p", "conv_s")}
    for layer in range(depth):
        mixer_out = None
        if layer % 2 == 0:
            e = layer // 2
            w_in = w_in_even[e].astype(BF16)
            w_o = w_o_even[e].astype(BF16)
            pw = pool_w[e].astype(BF16)
            tp = _ret_tables(pos_p, RET_CHUNK, 1)
            xp, pst, rst = _even_layer_prompt(xp, bp, lp, w_in, tp, pw, pool_scale[e], ret_gn_g[e], w_o,
                                              ln_mix_g[layer], ln_mix_b[layer], tl["mix"])
            outs["pool_p"].append(pst)
            outs["ret_p"].append(rst)
            ts = _ret_tables(pos_s_rows, ls, bs)
            rb = np.arange(ns) % bs
            ts["dm"] = np.where((rb[:, None] == rb[None, :])[None], ts["dec"], np.float32(0.0))
            hs =_matmul(xs, w_in, ns)
            hist = jnp.swapaxes(state_pool[e], 0, 1)
            mix_s, hist_new, s_new = _even_mix_sample(hs, hist, state_ret[e], ts, pw, pool_scale[e],
                                                      ret_gn_g[e], bs, ls, past_len)
            outs["pool_s"].append(jnp.swapaxes(hist_new, 0, 1))
            outs["ret_s"].append(s_new)
            xs = _matmul_res_ln(mix_s, w_o, xs, ln_mix_g[layer], ln_mix_b[layer], ns)
        else:
            o = layer // 2
            hw = 2 * LANES
            w = {
                "dq": w_dq[o].astype(BF16),
                "qg": q_norm_g[o].reshape(1, -1),
                "uq": _pad_heads(w_uq[o], QK_NOPE, QK_ROPE).astype(BF16),
                "dkv": jnp.pad(w_dkv[o], ((0, 0), (0, KV_LORA + LANES - w_dkv.shape[2]))).astype(BF16),
                "kvg": kv_norm_g[o].reshape(1, -1),
                "uk": w_uk[o].reshape(KV_LORA, MLA_HEADS * QK_NOPE).astype(BF16),
                "uv": w_uv[o].reshape(KV_LORA, MLA_HEADS * V_DIM).astype(BF16),
                "ukt": jnp.transpose(w_uk[o], (1, 2, 0)).astype(BF16),
            }
            w_o = w_o_mla[o].astype(BF16)
            qp, kp, vp, ckv_p, kpe_p = _mla_proj(xp, _pe_tables(pos_p), w, tl["tok"], lp // tl["tok"], False)
            att = _flash_prompt(qp, kp, vp, bp, lp, tl["attn"], tl["attn_k"])
            outs["ckv_p"].append(ckv_p.reshape(bp, lp, KV_LORA))
            outs["kpe_p"].append(kpe_p.reshape(bp, lp, QK_ROPE))
            mixer_out = (att, w_o, ln_mix_g[layer], ln_mix_b[layer])
            ql, qpe, ckv_s, kpe_s = _mla_proj(xs, _pe_tables(pos_s_rows), w, ns, 1, True)
            rows = ls * MLA_HEADS

            def per_batch(a, width):
                return jnp.transpose(a.reshape(ls, bs, MLA_HEADS, width), (1, 0, 2, 3)).reshape(bs, rows, width)

            ql_b = per_batch(ql, KV_LORA)
            qp_b = per_batch(qpe, LANES)[:, :, 0:QK_ROPE]
            ckv_sb = jnp.swapaxes(ckv_s.reshape(ls, bs, KV_LORA), 0, 1)
            kpe_sb = jnp.swapaxes(kpe_s.reshape(ls, bs, QK_ROPE), 0, 1)
            cn = jnp.pad(ckv_sb, ((0, 0), (0, PAGE_SIZE - ls), (0, 0)))
            knt = jnp.swapaxes(jnp.pad(kpe_sb, ((0, 0), (0, PAGE_SIZE - ls), (0, 0))), 1, 2)
            o_lat = _decode_attention(page_table, ql_b, qp_b, cn, knt, cache_ckv,
                                      jnp.swapaxes(cache_kpe, 2, 3), o, tl["pages"])
            outs["ckv_s"].append(ckv_sb)
            outs["kpe_s"].append(kpe_sb)
            ol = jnp.transpose(o_lat.reshape(bs, ls, MLA_HEADS, KV_LORA), (2, 1, 0, 3)).reshape(MLA_HEADS, ns, KV_LORA)
            wuv3 = jnp.transpose(w_uv[o], (1, 0, 2)).astype(BF16)
            xs = _decode_out(ol, wuv3, w_o, xs, ln_mix_g[layer], ln_mix_b[layer])
        xp, st_p = _ffn_prompt(xp, bp, lp, layer, wup_all, wd_all, conv_w[layer], conv_b[layer],
                               ln_ffn_g[layer], ln_ffn_b[layer], tl["ffn"], tl["ff_chunk"], mixer_out)
        outs["conv_p"].append(st_p[:, SUBLANES - (CONV_W - 1):, :])
        st_s = jnp.swapaxes(state_conv[layer], 0, 1)
        xs, st_s_new = _ffn_sample(xs, st_s, layer, wup_all, wd_all, conv_w[layer], conv_b[layer],
                                   ln_ffn_g[layer], ln_ffn_b[layer], bs, ls, tl["ff_chunk"])
        outs["conv_s"].append(jnp.swapaxes(st_s_new, 0, 1))

    y_p = xp.reshape(bp, lp, d)
    y_s = jnp.swapaxes(xs.reshape(ls, bs, d), 0, 1)
    return (y_p, y_s,
            jnp.stack(outs["pool_p"]), jnp.stack(outs["pool_s"]),
            jnp.stack(outs["ret_p"]), jnp.stack(outs["ret_s"]),
            jnp.stack(outs["ckv_p"]), jnp.stack(outs["ckv_s"]),
            jnp.stack(outs["kpe_p"]), jnp.stack(outs["kpe_s"]),
            jnp.stack(outs["conv_p"]), jnp.stack(outs["conv_s"]))
```

```python
import functools

import jax
import jax.numpy as jnp
import numpy as np
from jax import lax
from jax.experimental import pallas as pl
from jax.experimental.pallas import tpu as pltpu

F32 = jnp.float32
BF16 = jnp.bfloat16

PAGE_SIZE = 128
POOL_WINDOWS = (2, 4, 8, 16)
POOL_HIST = max(POOL_WINDOWS) - 1
RET_HEADS = 4
RET_DK = 128
RET_CHUNK = 128
MLA_HEADS = 8
QK_NOPE = 128
QK_ROPE = 64
V_DIM = 128
KV_LORA = 256
CONV_W = 3
DEPTH = 2
ALPHA = (2.0 * DEPTH) ** 0.25
ROPE_THETA = 10000.0
LN_EPS = 1e-5
RMS_EPS = 1e-6
GN_EPS = 1e-6
MLA_SCALE = (QK_NOPE + QK_ROPE) ** -0.5
LOG2E = 1.4426950408889634
SCALE_LOG2E = MLA_SCALE * LOG2E

LANES = 128
SUBLANES = 8
MIB = 1024 * 1024


def _cparams(sem, vmem_mib):
    return pltpu.CompilerParams(dimension_semantics=sem, vmem_limit_bytes=int(vmem_mib * MIB))


def _resident(shape):
    nd = len(shape)
    return pl.BlockSpec(shape, lambda *_: (0,) * nd, pipeline_mode=pl.Buffered(1))


def _resident_layer(stacked_shape, layer):
    nd = len(stacked_shape) - 1
    return pl.BlockSpec((None,) + tuple(stacked_shape[1:]), lambda *_: (layer,) + (0,) * nd,
                        pipeline_mode=pl.Buffered(1))


def _whole(shape):
    nd = len(shape)
    return pl.BlockSpec(shape, lambda *_: (0,) * nd)


def _dot(a, b):
    return jnp.dot(a, b, preferred_element_type=F32)


def _dot_nt(a, b):
    return lax.dot_general(a, b, (((1,), (1,)), ((), ())), preferred_element_type=F32)


def _dot_tn(a, b):
    return lax.dot_general(a, b, (((0,), (0,)), ((), ())), preferred_element_type=F32)


def _layer_norm(z, g, b):
    mu = jnp.mean(z, axis=-1, keepdims=True)
    d = z - mu
    var = jnp.mean(d * d, axis=-1, keepdims=True)
    return d * lax.rsqrt(var + LN_EPS) * g + b


def _silu(x):
    return x * jax.nn.sigmoid(x)


def _mm_body(x_ref, w_ref, o_ref):
    o_ref[...] = _dot(x_ref[...].astype(BF16), w_ref[...]).astype(o_ref.dtype)


def _matmul(x, w, tm, out_dtype=F32):
    m, k = x.shape
    n = w.shape[1]
    return pl.pallas_call(
        _mm_body,
        grid=(m // tm,),
        in_specs=[pl.BlockSpec((tm, k), lambda i: (i, 0)), _resident((k, n))],
        out_specs=pl.BlockSpec((tm, n), lambda i: (i, 0)),
        out_shape=jax.ShapeDtypeStruct((m, n), out_dtype),
        compiler_params=_cparams(("parallel",), 40),
        name="matmul",
    )(x, w)


def _mm_ln_body(a_ref, w_ref, r_ref, g_ref, b_ref, o_ref):
    y = _dot(a_ref[...].astype(BF16), w_ref[...])
    o_ref[...] = _layer_norm(ALPHA * r_ref[...] + y, g_ref[...], b_ref[...])


def _matmul_res_ln(a, w, res, g, b, tm):
    m, k = a.shape
    n = w.shape[1]
    return pl.pallas_call(
        _mm_ln_body,
        grid=(m // tm,),
        in_specs=[pl.BlockSpec((tm, k), lambda i: (i, 0)), _resident((k, n)),
                  pl.BlockSpec((tm, n), lambda i: (i, 0)), _resident((1, n)), _resident((1, n))],
        out_specs=pl.BlockSpec((tm, n), lambda i: (i, 0)),
        out_shape=jax.ShapeDtypeStruct((m, n), F32),
        compiler_params=_cparams(("parallel",), 32),
        name="matmul_res_ln",
    )(a, w, res, g.reshape(1, n), b.reshape(1, n))


def _rope_full(x, c2, s2):
    return x * c2 + pltpu.roll(x, 64, axis=1) * s2


def _group_norm_gate(o, gate, gn_row):
    mu = jnp.mean(o, axis=-1, keepdims=True)
    d = o - mu
    var = jnp.mean(d * d, axis=-1, keepdims=True)
    return _silu(gate) * (d * lax.rsqrt(var + GN_EPS) * gn_row)


def _even_layer_prompt_body(x_ref, win_ref, c2_ref, s2_ref, dec_ref, qd_ref, kd_ref, gc_ref, pw_ref, ps_ref,
                            gn_ref, wo_ref, lg_ref, lb_ref, o_ref, pst_ref, rst_ref,
                            mix_ref, ext_ref, s_ref, *h_refs, tm, n_j, sub):
    j = pl.program_id(1)
    pd = len(POOL_WINDOWS) * LANES

    @pl.when(j == 0)
    def _():
        ext_ref[0:16, :] = jnp.zeros((16, pd), F32)
        s_ref[...] = jnp.zeros(s_ref.shape, F32)

    @pl.when(j > 0)
    def _():
        ext_ref[0:16, :] = ext_ref[tm:tm + 16, :]

    c = dec_ref.shape[1]
    k_scale = RET_DK ** -0.5
    nblk = 4 * LANES
    def project(r0):
        h_ref = h_refs[r0 // sub]
        xb = x_ref[r0:r0 + sub, :].astype(BF16)
        for c0 in range(0, h_ref.shape[1], nblk):
            h_ref[:, c0:c0 + nblk] = _dot(xb, win_ref[:, c0:c0 + nblk])

    project(0)
    for r0 in range(0, tm, sub):
        rb = slice(r0, r0 + sub)
        h_ref = h_refs[r0 // sub]
        if r0 + sub < tm:
            project(r0 + sub)
        ext_ref[16 + r0:16 + r0 + sub, :] = h_ref[:, 0:pd]

        pos = (j * tm + r0 + lax.broadcasted_iota(jnp.int32, (sub, 1), 0)).astype(F32)
        for g, w in enumerate(POOL_WINDOWS):
            cols = slice(g * LANES, (g + 1) * LANES)
            e = ext_ref[r0:r0 + sub + 16, cols]
            u = e[16:, :]
            s = 1
            while s < w:
                e = e + pltpu.roll(e, s, axis=0)
                s *= 2
            cnt = jnp.minimum(float(w), pos + 1.0)
            pooled = e[16:, :] / cnt - u
            mixed = _dot(pooled.astype(BF16), pw_ref[g]) * ps_ref[:, cols]
            mix_ref[rb, cols] = mixed.astype(BF16)

        for ci in range(r0 // c, (r0 + sub) // c):
            rows = slice(ci * c, (ci + 1) * c)
            lrows = slice(ci * c - r0, (ci + 1) * c - r0)
            c2 = c2_ref[rows, :]
            s2 = s2_ref[rows, :]
            for hd in range(RET_HEADS):
                def col(part, hd=hd):
                    return slice(pd + (part * RET_HEADS + hd) * LANES, pd + (part * RET_HEADS + hd + 1) * LANES)
                q = _rope_full(h_ref[lrows, col(0)], c2, s2)
                k = _rope_full(h_ref[lrows, col(1)], c2, s2) * k_scale
                vb = h_ref[lrows, col(2)].astype(BF16)
                gate = h_ref[lrows, col(3)]
                st = s_ref[hd]
                sc = _dot_nt(q.astype(BF16), k.astype(BF16)) * dec_ref[hd]
                o = _dot(sc.astype(BF16), vb)
                o = o + _dot((q * qd_ref[hd]).astype(BF16), st.astype(BF16))
                s_ref[hd] = gc_ref[hd] * st + _dot_tn((k * kd_ref[hd]).astype(BF16), vb)
                ret = _group_norm_gate(o, gate, gn_ref[:, hd * LANES:(hd + 1) * LANES])
                mix_ref[rows, pd + hd * LANES:pd + (hd + 1) * LANES] = ret.astype(BF16)

        y = _dot(mix_ref[rb, :], wo_ref[...])
        o_ref[rb, :] = _layer_norm(ALPHA * x_ref[rb, :] + y, lg_ref[...], lb_ref[...])

    @pl.when(j == n_j - 1)
    def _():
        pst_ref[0] = ext_ref[pl.ds(tm + 1, POOL_HIST), :]
        rst_ref[0] = s_ref[...]


def _even_layer_prompt(x, bsz, seq, w_in, tabs, pool_w, pool_scale, gn_g, w_o, ln_g, ln_b, tm):
    n_j = seq // tm
    d = x.shape[1]
    pd = pool_scale.shape[0]
    ed = pd + RET_HEADS * LANES
    chunk = tabs["dec"].shape[1]
    sub = min(tm, max(chunk, 2 * LANES))
    body = functools.partial(_even_layer_prompt_body, tm=tm, n_j=n_j, sub=sub)
    return pl.pallas_call(
        body,
        grid=(bsz, n_j),
        in_specs=[
            pl.BlockSpec((tm, d), lambda b, j: (b * n_j + j, 0)),
            _resident(w_in.shape),
            pl.BlockSpec((tm, LANES), lambda b, j: (j, 0)),
            pl.BlockSpec((tm, LANES), lambda b, j: (j, 0)),
            _resident((RET_HEADS, chunk, chunk)),
            _resident((RET_HEADS, chunk, LANES)),
            _resident((RET_HEADS, chunk, LANES)),
            _resident((RET_HEADS, 1, LANES)),
            _resident(pool_w.shape),
            _resident((1, pd)),
            _resident((1, RET_HEADS * LANES)),
            _resident(w_o.shape),
            _resident((1, d)),
            _resident((1, d)),
        ],
        out_specs=[
            pl.BlockSpec((tm, d), lambda b, j: (b * n_j + j, 0)),
            pl.BlockSpec((1, POOL_HIST, pd), lambda b, j: (b, 0, 0)),
            pl.BlockSpec((1, RET_HEADS, RET_DK, LANES), lambda b, j: (b, 0, 0, 0)),
        ],
        out_shape=[
            jax.ShapeDtypeStruct((bsz * seq, d), F32),
            jax.ShapeDtypeStruct((bsz, POOL_HIST, pd), F32),
            jax.ShapeDtypeStruct((bsz, RET_HEADS, RET_DK, LANES), F32),
        ],
        scratch_shapes=[pltpu.VMEM((tm, ed), BF16), pltpu.VMEM((tm + 16, pd), F32),
                        pltpu.VMEM((RET_HEADS, RET_DK, LANES), F32)]
        + [pltpu.VMEM((sub, w_in.shape[1]), F32)] * (tm // sub),
        compiler_params=_cparams(("parallel", "arbitrary"), 40),
        name="even_layer_prompt",
    )(x, w_in, tabs["c2"], tabs["s2"], tabs["dec"], tabs["qd"], tabs["kd"], tabs["gc"],
      pool_w, pool_scale.reshape(1, pd), gn_g.reshape(1, -1), w_o, ln_g.reshape(1, d), ln_b.reshape(1, d))


def _even_mix_sample_body(h_ref, hist_ref, s0_ref, c2_ref, s2_ref, dm_ref, qd_ref, kd_ref, gc_ref,
                          pw_ref, ps_ref, gn_ref, mix_ref, hist_o_ref, s_o_ref, oc_ref,
                          *, nb, ls, cnts):
    pd = len(POOL_WINDOWS) * LANES
    ext = [hist_ref[i] for i in range(POOL_HIST)]
    ext += [h_ref[l * nb:(l + 1) * nb, 0:pd] for l in range(ls)]
    for i in range(POOL_HIST):
        hist_o_ref[i] = ext[ls + i]
    for g, w in enumerate(POOL_WINDOWS):
        cols = slice(g * LANES, (g + 1) * LANES)
        outs = []
        for l in range(ls):
            top = POOL_HIST + l
            acc = ext[top][:, cols]
            for jj in range(1, w):
                acc = acc + ext[top - jj][:, cols]
            outs.append(acc / cnts[g][l] - ext[top][:, cols])
        pooled = jnp.concatenate(outs, axis=0)
        mixed = _dot(pooled.astype(BF16), pw_ref[g]) * ps_ref[:, cols]
        mix_ref[:, cols] = mixed.astype(BF16)

    rows_b = lax.broadcasted_iota(jnp.int32, (ls * nb, 1), 0) % nb
    k_scale = RET_DK ** -0.5
    c2 = c2_ref[...]
    s2 = s2_ref[...]
    qs, ks, vs = [], [], []
    for hd in range(RET_HEADS):
        def col(part, hd=hd):
            return slice(pd + (part * RET_HEADS + hd) * LANES, pd + (part * RET_HEADS + hd + 1) * LANES)
        q = _rope_full(h_ref[:, col(0)], c2, s2)
        k = _rope_full(h_ref[:, col(1)], c2, s2) * k_scale
        vb = h_ref[:, col(2)].astype(BF16)
        sc = _dot_nt(q.astype(BF16), k.astype(BF16)) * dm_ref[hd]
        oc_ref[hd] = _dot(sc.astype(BF16), vb)
        qs.append(q * qd_ref[hd])
        ks.append(k * kd_ref[hd])
        vs.append(vb)

    def per_batch(b, carry):
        sel = rows_b == b
        for hd in range(RET_HEADS):
            st = s0_ref[b, hd]
            qm = jnp.where(sel, qs[hd], 0.0).astype(BF16)
            km = jnp.where(sel, ks[hd], 0.0).astype(BF16)
            oc_ref[hd] += _dot(qm, st.astype(BF16))
            s_o_ref[b, hd] = gc_ref[hd] * st + _dot_tn(km, vs[hd])
        return carry

    lax.fori_loop(0, nb, per_batch, 0)

    for hd in range(RET_HEADS):
        gate = h_ref[:, pd + (3 * RET_HEADS + hd) * LANES:pd + (3 * RET_HEADS + hd + 1) * LANES]
        ret = _group_norm_gate(oc_ref[hd], gate, gn_ref[:, hd * LANES:(hd + 1) * LANES])
        mix_ref[:, pd + hd * LANES:pd + (hd + 1) * LANES] = ret.astype(BF16)


def _even_mix_sample(h, hist, s0, tabs, pool_w, pool_scale, gn_g, nb, ls, past_len):
    pd = pool_scale.shape[0]
    ed = pd + RET_HEADS * LANES
    cnts = tuple(tuple(float(min(w, past_len + l + 1)) for l in range(ls)) for w in POOL_WINDOWS)
    body = functools.partial(_even_mix_sample_body, nb=nb, ls=ls, cnts=cnts)
    n = ls * nb
    args = (h, hist, s0, tabs["c2"], tabs["s2"], tabs["dm"], tabs["qd"], tabs["kd"], tabs["gc"],
            pool_w, pool_scale.reshape(1, pd), gn_g.reshape(1, -1))
    return pl.pallas_call(
        body,
        grid=(1,),
        in_specs=[_resident(a.shape) for a in args],
        out_specs=[_whole((n, ed)), _whole(hist.shape), _whole(s0.shape)],
        out_shape=[jax.ShapeDtypeStruct((n, ed), BF16),
                   jax.ShapeDtypeStruct(hist.shape, F32),
                   jax.ShapeDtypeStruct(s0.shape, F32)],
        scratch_shapes=[pltpu.VMEM((RET_HEADS, n, LANES), F32)],
        compiler_params=_cparams(("arbitrary",), 48),
        name="even_mix_sample",
    )(*args)


def _ffn_chunk(xb, wup_ref, wd_ref, cw_ref, cb_ref, c, tf, dff, shift_fn):
    cols = slice(c * tf, (c + 1) * tf)
    a = _dot(xb, wup_ref[:, cols])
    gate_in = _dot(xb, wup_ref[:, dff + c * tf:dff + (c + 1) * tf])
    a1, a2 = shift_fn(a, c)
    conv = cb_ref[:, cols] + cw_ref[0:1, cols] * a2
    conv = conv + cw_ref[1:2, cols] * a1
    conv = conv + cw_ref[2:3, cols] * a
    act = (_silu(conv) * gate_in).astype(BF16)
    return a, _dot(act, wd_ref[cols, :])


def _ffn_prompt_body(x_ref, *rest, tm, n_j, tf, dff, mixer_out, sub):
    if mixer_out:
        a_ref, wo_ref, mg_ref, mb_ref = rest[:4]
        rest = rest[4:]
    wup_ref, wd_ref, cw_ref, cb_ref, g_ref, b_ref, o_ref, st_ref, carry_ref = rest
    j = pl.program_id(1)

    @pl.when(j == 0)
    def _():
        carry_ref[...] = jnp.zeros(carry_ref.shape, F32)

    row = lax.broadcasted_iota(jnp.int32, (sub, tf), 0)

    def shift_fn(a, c):
        prev = carry_ref[:, c * tf:(c + 1) * tf]
        a1 = jnp.where(row == 0, prev[7:8, :], pltpu.roll(a, 1, axis=0))
        a2 = jnp.where(row == 0, prev[6:7, :], jnp.where(row == 1, prev[7:8, :], pltpu.roll(a, 2, axis=0)))
        return a1, a2

    for r0 in range(0, tm, sub):
        rb = slice(r0, r0 + sub)
        x = x_ref[rb, :]
        if mixer_out:
            x = _layer_norm(ALPHA * x + _dot(a_ref[rb, :], wo_ref[...]), mg_ref[...], mb_ref[...])
        xb = x.astype(BF16)
        acc = None
        for c in range(dff // tf):
            a, y = _ffn_chunk(xb, wup_ref, wd_ref, cw_ref, cb_ref, c, tf, dff, shift_fn)
            acc = y if acc is None else acc + y
            carry_ref[:, c * tf:(c + 1) * tf] = a[sub - SUBLANES:sub, :]
        o_ref[rb, :] = _layer_norm(ALPHA * x + acc, g_ref[...], b_ref[...])
    st_ref[0] = carry_ref[...]


def _ffn_prompt(x, bsz, seq, layer, wup, wd, conv_w, conv_b, g, b, tm, tf, mixer_out=None):
    d = x.shape[1]
    dff = wd.shape[1]
    n_j = seq // tm
    body = functools.partial(_ffn_prompt_body, tm=tm, n_j=n_j, tf=tf, dff=dff, mixer_out=mixer_out is not None,
                             sub=tm)
    row_tile = lambda bi, j: (bi * n_j + j, 0)
    pre_args, pre_specs = [], []
    if mixer_out is not None:
        a, w_o, mg, mb = mixer_out
        pre_args = [a, w_o, mg.reshape(1, d), mb.reshape(1, d)]
        pre_specs = [pl.BlockSpec((tm, a.shape[1]), row_tile), _resident(w_o.shape), _resident((1, d)), _resident((1, d))]
    return pl.pallas_call(
        body,
        grid=(bsz, n_j),
        in_specs=[pl.BlockSpec((tm, d), row_tile)] + pre_specs + [
                  _resident_layer(wup.shape, layer), _resident_layer(wd.shape, layer), _resident(conv_w.shape),
                  _resident((1, dff)), _resident((1, d)), _resident((1, d))],
        out_specs=[pl.BlockSpec((tm, d), lambda bi, j: (bi * n_j + j, 0)),
                   pl.BlockSpec((1, SUBLANES, dff), lambda bi, j: (bi, 0, 0))],
        out_shape=[jax.ShapeDtypeStruct(x.shape, F32),
                   jax.ShapeDtypeStruct((bsz, SUBLANES, dff), F32)],
        scratch_shapes=[pltpu.VMEM((SUBLANES, dff), F32)],
        compiler_params=_cparams(("parallel", "arbitrary"), 56),
        name="ffn_prompt",
    )(x, *pre_args, wup, wd, conv_w, conv_b.reshape(1, dff), g.reshape(1, d), b.reshape(1, d))


def _ffn_sample_body(x_ref, st_ref, wup_ref, wd_ref, cw_ref, cb_ref, g_ref, b_ref, o_ref, st_o_ref,
                     *, nb, ls, tf, dff):
    x = x_ref[...]
    xb = x.astype(BF16)
    nh = CONV_W - 1

    def shift_fn(a, c):
        cols = slice(c * tf, (c + 1) * tf)
        ext = [st_ref[i, :, cols] for i in range(nh)] + [a[l * nb:(l + 1) * nb, :] for l in range(ls)]
        a1 = jnp.concatenate([ext[nh + l - 1] for l in range(ls)], axis=0)
        a2 = jnp.concatenate([ext[nh + l - 2] for l in range(ls)], axis=0)
        for i in range(nh):
            st_o_ref[i, :, cols] = ext[ls + i]
        return a1, a2

    acc = None
    for c in range(dff // tf):
        _, y = _ffn_chunk(xb, wup_ref, wd_ref, cw_ref, cb_ref, c, tf, dff, shift_fn)
        acc = y if acc is None else acc + y
    o_ref[...] = _layer_norm(ALPHA * x + acc, g_ref[...], b_ref[...])


def _ffn_sample(x, st, layer, wup, wd, conv_w, conv_b, g, b, nb, ls, tf):
    d = x.shape[1]
    dff = wd.shape[1]
    body = functools.partial(_ffn_sample_body, nb=nb, ls=ls, tf=tf, dff=dff)
    args = (x, st, wup, wd, conv_w, conv_b.reshape(1, dff), g.reshape(1, d), b.reshape(1, d))
    in_specs = [_resident(a.shape) for a in args]
    in_specs[2] = _resident_layer(wup.shape, layer)
    in_specs[3] = _resident_layer(wd.shape, layer)
    return pl.pallas_call(
        body,
        grid=(1,),
        in_specs=in_specs,
        out_specs=[_whole(x.shape), _whole(st.shape)],
        out_shape=[jax.ShapeDtypeStruct(x.shape, F32), jax.ShapeDtypeStruct(st.shape, F32)],
        compiler_params=_cparams(("arbitrary",), 40),
        name="ffn_sample",
    )(*args)


def _rope_pe(blk, cc, s1, s2):
    return blk * cc + pltpu.roll(blk, 96, axis=1) * s1 + pltpu.roll(blk, 32, axis=1) * s2


def _rms_norm(x, g):
    ms = jnp.mean(x * x, axis=-1, keepdims=True)
    return x * lax.rsqrt(ms + RMS_EPS) * g


def _mla_proj_body(x_ref, cc_ref, s1_ref, s2_ref, wdq_ref, qg_ref, wuq_ref, wdkv_ref, kvg_ref, *rest,
                   decode, sub):
    hw = 2 * LANES
    for r0 in range(0, x_ref.shape[0], sub):
        rb = slice(r0, r0 + sub)
        cc = cc_ref[rb, :]
        s1 = s1_ref[rb, :]
        s2 = s2_ref[rb, :]
        xb = x_ref[rb, :].astype(BF16)
        cq = _rms_norm(_dot(xb, wdq_ref[...]), qg_ref[...])
        q = _dot(cq.astype(BF16), wuq_ref[...])
        kv = _dot(xb, wdkv_ref[...])
        ckv = _rms_norm(kv[:, 0:KV_LORA], kvg_ref[...])
        kpe = _rope_pe(kv[:, KV_LORA:KV_LORA + LANES], cc, s1, s2)
        if decode:
            wukt_ref, ql_ref, qp_ref, ckv_ref, kpe_ref = rest
            for h in range(MLA_HEADS):
                qn = q[:, h * hw:h * hw + LANES].astype(BF16)
                ql_ref[rb, h * KV_LORA:(h + 1) * KV_LORA] = _dot(qn, wukt_ref[h]).astype(BF16)
                qp = _rope_pe(q[:, h * hw + LANES:(h + 1) * hw], cc, s1, s2)
                qp_ref[rb, h * LANES:(h + 1) * LANES] = qp.astype(BF16)
        else:
            wuk_ref, wuv_ref, qo_ref, ko_ref, vo_ref, ckv_ref, kpe_ref = rest
            cb = ckv.astype(BF16)
            kn = _dot(cb, wuk_ref[...])
            vo_ref[rb, :] = _dot(cb, wuv_ref[...]).astype(BF16)
            kpb = kpe.astype(BF16)
            for h in range(MLA_HEADS):
                qo_ref[rb, h * hw:h * hw + LANES] = q[:, h * hw:h * hw + LANES].astype(BF16)
                qp = _rope_pe(q[:, h * hw + LANES:(h + 1) * hw], cc, s1, s2)
                qo_ref[rb, h * hw + LANES:(h + 1) * hw] = qp.astype(BF16)
                ko_ref[rb, h * hw:h * hw + LANES] = kn[:, h * LANES:(h + 1) * LANES].astype(BF16)
                ko_ref[rb, h * hw + LANES:(h + 1) * hw] = kpb
        ckv_ref[rb, :] = ckv
        kpe_ref[rb, :] = kpe[:, 0:QK_ROPE]


def _mla_proj(x, tabs, w, tm, n_pos_blocks, decode):
    t, d = x.shape
    hw = 2 * LANES
    body = functools.partial(_mla_proj_body, decode=decode, sub=tm)
    row = lambda i: (i, 0)
    tab = lambda i: (i % n_pos_blocks, 0)
    ins = [x, tabs["cc"], tabs["s1"], tabs["s2"], w["dq"], w["qg"], w["uq"], w["dkv"], w["kvg"]]
    in_specs = [pl.BlockSpec((tm, d), row)] + [pl.BlockSpec((tm, LANES), tab)] * 3
    in_specs += [_resident(a.shape) for a in ins[4:]]
    if decode:
        ins += [w["ukt"]]
        in_specs += [_resident(w["ukt"].shape)]
        outs = [(MLA_HEADS * KV_LORA, BF16), (MLA_HEADS * LANES, BF16)]
    else:
        ins += [w["uk"], w["uv"]]
        in_specs += [_resident(w["uk"].shape), _resident(w["uv"].shape)]
        outs = [(MLA_HEADS * hw, BF16), (MLA_HEADS * hw, BF16), (MLA_HEADS * V_DIM, BF16)]
    outs += [(KV_LORA, F32), (QK_ROPE, F32)]
    return pl.pallas_call(
        body,
        grid=(t // tm,),
        in_specs=in_specs,
        out_specs=[pl.BlockSpec((tm, n), row) for n, _ in outs],
        out_shape=[jax.ShapeDtypeStruct((t, n), dt) for n, dt in outs],
        compiler_params=_cparams(("parallel",), 40),
        name="mla_proj_decode" if decode else "mla_proj",
    )(*ins)


def _online_softmax_update(s2, m_ref, l_ref, acc_ref, pv_fn, row_chunk):
    rows, width = s2.shape
    n = width // LANES
    aw = acc_ref.shape[-1] // LANES
    p_chunks, alphas = [], []
    for r0 in range(0, rows, row_chunk):
        rs_ = slice(r0, r0 + row_chunk)
        tiles = [s2[rs_, j * LANES:(j + 1) * LANES] for j in range(n)]
        mx = tiles[0]
        for t in tiles[1:]:
            mx = jnp.maximum(mx, t)
        m_prev = m_ref[rs_, :]
        m_new = jnp.maximum(m_prev, jnp.max(mx, axis=-1, keepdims=True))
        alpha = jnp.exp2(m_prev - m_new)
        ps = [jnp.exp2(t - m_new) for t in tiles]
        if l_ref is not None:
            tot = ps[0]
            for t in ps[1:]:
                tot = tot + t
            l_ref[rs_, :] = alpha * l_ref[rs_, :] + jnp.sum(tot, axis=-1, keepdims=True)
        m_ref[rs_, :] = m_new
        p_chunks.append((jnp.concatenate(ps, axis=1) if n > 1 else ps[0]).astype(BF16))
        alphas.append(alpha)
    p = jnp.concatenate(p_chunks, axis=0) if len(p_chunks) > 1 else p_chunks[0]
    pv = pv_fn(p)
    for i, r0 in enumerate(range(0, rows, row_chunk)):
        rs_ = slice(r0, r0 + row_chunk)
        a_w = alphas[i] if aw == 1 else jnp.concatenate([alphas[i]] * aw, axis=1)
        acc_ref[rs_, :] = a_w * acc_ref[rs_, :] + pv[rs_, :]


def _flash_body(q_ref, k_ref, v_ref, o_ref, m_ref, acc_ref, *, tq, tk, row_chunk, head_group):
    qi = pl.program_id(1)
    hw = 2 * LANES
    m_ref[...] = jnp.full(m_ref.shape, -jnp.inf, F32)
    acc_ref[...] = jnp.zeros(acc_ref.shape, F32)
    ones = jnp.ones((tk, LANES), BF16)

    def step(h, key0, diag_off):
        start = pl.multiple_of(key0, tk)
        q = q_ref[:, h * hw:(h + 1) * hw]
        k = k_ref[pl.ds(start, tk), h * hw:(h + 1) * hw]
        v1 = jnp.concatenate([v_ref[pl.ds(start, tk), h * V_DIM:(h + 1) * V_DIM], ones], axis=1)
        s = _dot_nt(q, k) * SCALE_LOG2E
        if diag_off is not None:
            row = lax.broadcasted_iota(jnp.int32, s.shape, 0)
            colm = lax.broadcasted_iota(jnp.int32, s.shape, 1)
            s = jnp.where(colm + diag_off <= row, s, -jnp.inf)
        _online_softmax_update(s, m_ref.at[h], None, acc_ref.at[h], lambda p: _dot(p, v1), row_chunk)

    for h0 in range(0, MLA_HEADS, head_group):
        def loop_body(kj, carry, h0=h0):
            for h in range(h0, h0 + head_group):
                step(h, kj * tk, None)
            return carry

        lax.fori_loop(0, qi * (tq // tk), loop_body, 0)
    for h in range(MLA_HEADS):
        for j in range(tq // tk):
            step(h, qi * tq + j * tk, j * tk)
        o_ref[:, h * V_DIM:(h + 1) * V_DIM] = (acc_ref[h, :, 0:V_DIM] / acc_ref[h, :, V_DIM:]).astype(BF16)


def _flash_prompt(qp, kp, vp, bsz, seq, tq, tk):
    nq = seq // tq
    body = functools.partial(_flash_body, tq=tq, tk=tk, row_chunk=min(tq, 64), head_group=MLA_HEADS)
    return pl.pallas_call(
        body,
        grid=(bsz, nq),
        in_specs=[pl.BlockSpec((tq, qp.shape[1]), lambda b, i: (b * nq + i, 0)),
                  pl.BlockSpec((seq, kp.shape[1]), lambda b, i: (b, 0)),
                  pl.BlockSpec((seq, vp.shape[1]), lambda b, i: (b, 0))],
        out_specs=pl.BlockSpec((tq, vp.shape[1]), lambda b, i: (b * nq + i, 0)),
        out_shape=jax.ShapeDtypeStruct(vp.shape, BF16),
        scratch_shapes=[pltpu.VMEM((MLA_HEADS, tq, LANES), F32), pltpu.VMEM((MLA_HEADS, tq, V_DIM + LANES), F32)],
        compiler_params=_cparams(("parallel", "arbitrary"), 56),
        name="flash_prompt",
    )(qp, kp, vp)


def _decode_body(pt_ref, ql_ref, qp_ref, cn_ref, kn_ref, ckv_hbm, kpe_hbm, o_ref,
                 cbuf_ref, rbuf_ref, sem_ref, m_ref, l_ref, acc_ref, *, gp, ls, layer, n_split):
    b = pl.program_id(0)
    g = pl.program_id(1)
    n_b = pl.num_programs(0)
    n_g = pl.num_programs(1)
    slot = lax.rem(b * n_g + g, 2)
    other = 1 - slot

    def page_copies(bb, grp, slot):
        cps = []
        for i in range(gp):
            pg = pt_ref[bb, grp * gp + i]
            keys = pl.ds(i * PAGE_SIZE, PAGE_SIZE)
            cps.append(pltpu.make_async_copy(ckv_hbm.at[layer, pg], cbuf_ref.at[slot, keys, :], sem_ref.at[slot, 0]))
            cps.append(pltpu.make_async_copy(kpe_hbm.at[layer, pg], rbuf_ref.at[slot, :, keys], sem_ref.at[slot, 1]))
        return cps

    def start(bb, grp, slot):
        for cp in page_copies(bb, grp, slot):
            cp.start()

    @pl.when((b == 0) & (g == 0))
    def _():
        start(b, g, slot)

    @pl.when(g < n_g - 1)
    def _():
        start(b, g + 1, other)

    @pl.when((g == n_g - 1) & (b < n_b - 1))
    def _():
        start(b + 1, 0, other)

    @pl.when(g == 0)
    def _():
        m_ref[...] = jnp.full(m_ref.shape, -jnp.inf, F32)
        l_ref[...] = jnp.zeros(l_ref.shape, F32)
        acc_ref[...] = jnp.zeros(acc_ref.shape, F32)

    ql = ql_ref[0]
    qp = qp_ref[0]

    for cp in page_copies(b, g, slot):
        cp.wait()

    kc_len = gp * PAGE_SIZE // n_split
    kbs, scores = [], []
    for c in range(n_split):
        ks = slice(c * kc_len, (c + 1) * kc_len)
        kb = cbuf_ref[slot, ks, :].astype(BF16)
        scores.append((_dot_nt(ql, kb) + _dot(qp, rbuf_ref[slot, :, ks].astype(BF16))) * SCALE_LOG2E)
        kbs.append(kb)
    state = (m_ref.at[0], l_ref.at[0], acc_ref.at[0])
    n_rows = ql.shape[0]
    for c in range(n_split):
        _online_softmax_update(scores[c], *state, lambda p, kb=kbs[c]: _dot(p, kb), n_rows)

    @pl.when(g == n_g - 1)
    def _():
        kc = cn_ref[0].astype(BF16)
        s = (_dot_nt(ql, kc) + _dot(qp, kn_ref[0].astype(BF16))) * SCALE_LOG2E
        r = lax.broadcasted_iota(jnp.int32, s.shape, 0) // MLA_HEADS
        cidx = lax.broadcasted_iota(jnp.int32, s.shape, 1)
        s = jnp.where((cidx <= r) & (cidx < ls), s, -jnp.inf)
        _online_softmax_update(s, *state, lambda p: _dot(p, kc), n_rows)
        l_w = jnp.concatenate([l_ref[0]] * (KV_LORA // LANES), axis=1)
        o_ref[0] = acc_ref[0] / l_w


def _decode_attention(page_table, ql, qp, cn, knt, cache_ckv, cache_kpet, layer, gp):
    nb, rows, _ = ql.shape
    n_pages = page_table.shape[1]
    ls = rows // MLA_HEADS
    n_split = 4 if gp % 4 == 0 else 1
    body = functools.partial(_decode_body, gp=gp, ls=ls, layer=layer, n_split=n_split)
    per_b = lambda b, g, pt: (b, 0, 0)
    in_specs = [pl.BlockSpec((1, rows, KV_LORA), per_b), pl.BlockSpec((1, rows, QK_ROPE), per_b),
                pl.BlockSpec((1, PAGE_SIZE, KV_LORA), per_b), pl.BlockSpec((1, QK_ROPE, PAGE_SIZE), per_b),
                pl.BlockSpec(memory_space=pl.ANY), pl.BlockSpec(memory_space=pl.ANY)]
    n_buf = 2
    grid_spec = pltpu.PrefetchScalarGridSpec(
        num_scalar_prefetch=1,
        grid=(nb, n_pages // gp),
        in_specs=in_specs,
        out_specs=pl.BlockSpec((1, rows, KV_LORA), per_b),
        scratch_shapes=[pltpu.VMEM((n_buf, gp * PAGE_SIZE, KV_LORA), F32),
                        pltpu.VMEM((n_buf, QK_ROPE, gp * PAGE_SIZE), F32),
                        pltpu.SemaphoreType.DMA((n_buf, 2)),
                        pltpu.VMEM((1, rows, LANES), F32), pltpu.VMEM((1, rows, LANES), F32),
                        pltpu.VMEM((1, rows, KV_LORA), F32)],
    )
    return pl.pallas_call(
        body,
        grid_spec=grid_spec,
        out_shape=jax.ShapeDtypeStruct((nb, rows, KV_LORA), F32),
        compiler_params=_cparams(("arbitrary", "arbitrary"), 48),
        name="decode_attention",
    )(page_table, ql, qp, cn, knt, cache_ckv, cache_kpet)


def _decode_out_body(ol_ref, wuv_ref, wo_ref, r_ref, g_ref, b_ref, o_ref):
    y = None
    for h in range(MLA_HEADS):
        oh = _dot(ol_ref[h].astype(BF16), wuv_ref[h]).astype(BF16)
        t = _dot(oh, wo_ref[h * V_DIM:(h + 1) * V_DIM, :])
        y = t if y is None else y + t
    o_ref[...] = _layer_norm(ALPHA * r_ref[...] + y, g_ref[...], b_ref[...])


def _decode_out(ol, wuv3, wo, res, g, b):
    d = res.shape[1]
    args = (ol, wuv3, wo, res, g.reshape(1, d), b.reshape(1, d))
    return pl.pallas_call(
        _decode_out_body,
        grid=(1,),
        in_specs=[_resident(a.shape) for a in args],
        out_specs=_whole(res.shape),
        out_shape=jax.ShapeDtypeStruct(res.shape, F32),
        compiler_params=_cparams(("arbitrary",), 24),
        name="decode_out",
    )(*args)


def _rope_angles(pos, half):
    inv_freq = ROPE_THETA ** (-np.arange(half, dtype=np.float64) / half)
    ang = np.asarray(pos, np.float64)[:, None] * inv_freq[None, :]
    return np.cos(ang), np.sin(ang)


def _f32(tabs):
    return {k: np.ascontiguousarray(v, dtype=np.float32) for k, v in tabs.items()}


def _ret_tables(pos, chunk, rows_per_pos):
    cos, sin = _rope_angles(pos, RET_DK // 2)
    log_gamma = np.log(1.0 - 2.0 ** (-5.0 - np.arange(RET_HEADS, dtype=np.float64)))
    idx = np.repeat(np.arange(chunk, dtype=np.float64), rows_per_pos)
    diff = idx[:, None] - idx[None, :]
    dec = np.where(diff >= 0, np.exp(np.maximum(diff, 0.0)[None] * log_gamma[:, None, None]), 0.0)
    q_dec = np.exp((idx + 1.0)[None, :] * log_gamma[:, None])
    k_dec = np.exp((chunk - 1.0 - idx)[None, :] * log_gamma[:, None])
    gc = np.exp(chunk * log_gamma)
    n = idx.shape[0]
    return _f32({
        "c2": np.concatenate([cos, cos], axis=1),
        "s2": np.concatenate([-sin, sin], axis=1),
        "dec": dec,
        "qd": np.broadcast_to(q_dec[:, :, None], (RET_HEADS, n, LANES)),
        "kd": np.broadcast_to(k_dec[:, :, None], (RET_HEADS, n, LANES)),
        "gc": np.broadcast_to(gc[:, None, None], (RET_HEADS, 1, LANES)),
    })


def _pe_tables(pos):
    cos, sin = _rope_angles(pos, QK_ROPE // 2)
    z = np.zeros_like(cos)
    return _f32({"cc": np.concatenate([cos, cos, z, z], axis=1),
                 "s1": np.concatenate([-sin, z, z, z], axis=1),
                 "s2": np.concatenate([z, sin, z, z], axis=1)})


def _pad_heads(w, nope, rope):
    k = w.shape[0]
    w3 = w.reshape(k, MLA_HEADS, nope + rope)
    pad = jnp.zeros((k, MLA_HEADS, 2 * LANES - nope - rope), w.dtype)
    return jnp.concatenate([w3, pad], axis=2).reshape(k, MLA_HEADS * 2 * LANES)


def _tiles(seq, n_pages, dff):
    def fit(t, n=seq):
        while n % t:
            t //= 2
        return t
    return {"tok": fit(1024), "mix": fit(512), "ret_chunk": fit(256), "ffn": fit(512), "attn": fit(512), "attn_k": fit(512), "ff_chunk": dff,
            "pages": fit(64, n_pages)}


def kernel(x_prompt, x_sample, state_pool, state_ret, cache_ckv, cache_kpe, state_conv, page_table,
           w_in_even, pool_w, pool_scale, ret_gn_g, w_o_even,
           w_dq, q_norm_g, w_uq, w_dkv, kv_norm_g, w_uk, w_uv, w_o_mla,
           w_up, conv_w, conv_b, w_down, ln_mix_g, ln_mix_b, ln_ffn_g, ln_ffn_b):
    bp, lp, d = x_prompt.shape
    bs, ls, _ = x_sample.shape
    past_len = page_table.shape[1] * PAGE_SIZE
    depth = w_up.shape[0]
    dff = w_down.shape[1]
    tl = _tiles(lp, page_table.shape[1], dff)
    ns = bs * ls
    assert lp % tl["ret_chunk"] == 0 and tl["mix"] % tl["ret_chunk"] == 0
    assert dff % tl["ff_chunk"] == 0 and page_table.shape[1] % tl["pages"] == 0 and ls <= PAGE_SIZE

    xp = x_prompt.reshape(bp * lp, d)
    xs = jnp.swapaxes(x_sample, 0, 1).reshape(ns, d)
    pos_p = np.arange(lp)
    pos_s_rows = np.repeat(past_len + np.arange(ls), bs)

    wup_all = w_up.astype(BF16)
    wd_all = w_down.astype(BF16)
    outs = {k: [] for k in ("pool_p", "pool_s", "ret_p", "ret_s", "ckv_p", "ckv_s", "kpe_p", "kpe_s",
                            "conv_p", "conv_s")}
    for layer in range(depth):
        mixer_out = None
        if layer % 2 == 0:
            e = layer // 2
            w_in = w_in_even[e].astype(BF16)
            w_o = w_o_even[e].astype(BF16)
            pw = pool_w[e].astype(BF16)
            tp = _ret_tables(pos_p, tl["ret_chunk"], 1)
            xp, pst, rst = _even_layer_prompt(xp, bp, lp, w_in, tp, pw, pool_scale[e], ret_gn_g[e], w_o,
                                              ln_mix_g[layer], ln_mix_b[layer], tl["mix"])
            outs["pool_p"].append(pst)
            outs["ret_p"].append(rst)
            ts = _ret_tables(pos_s_rows, ls, bs)
            rb = np.arange(ns) % bs
            ts["dm"] = np.where((rb[:, None] == rb[None, :])[None], ts["dec"], np.float32(0.0))
            hs =_matmul(xs, w_in, ns)
            hist = jnp.swapaxes(state_pool[e], 0, 1)
            mix_s, hist_new, s_new = _even_mix_sample(hs, hist, state_ret[e], ts, pw, pool_scale[e],
                                                      ret_gn_g[e], bs, ls, past_len)
            outs["pool_s"].append(jnp.swapaxes(hist_new, 0, 1))
            outs["ret_s"].append(s_new)
            xs = _matmul_res_ln(mix_s, w_o, xs, ln_mix_g[layer], ln_mix_b[layer], ns)
        else:
            o = layer // 2
            hw = 2 * LANES
            w = {
                "dq": w_dq[o].astype(BF16),
                "qg": q_norm_g[o].reshape(1, -1),
                "uq": _pad_heads(w_uq[o], QK_NOPE, QK_ROPE).astype(BF16),
                "dkv": jnp.pad(w_dkv[o], ((0, 0), (0, KV_LORA + LANES - w_dkv.shape[2]))).astype(BF16),
                "kvg": kv_norm_g[o].reshape(1, -1),
                "uk": w_uk[o].reshape(KV_LORA, MLA_HEADS * QK_NOPE).astype(BF16),
                "uv": w_uv[o].reshape(KV_LORA, MLA_HEADS * V_DIM).astype(BF16),
                "ukt": jnp.transpose(w_uk[o], (1, 2, 0)).astype(BF16),
            }
            w_o = w_o_mla[o].astype(BF16)
            qp, kp, vp, ckv_p, kpe_p = _mla_proj(xp, _pe_tables(pos_p), w, tl["tok"], lp // tl["tok"], False)
            att = _flash_prompt(qp, kp, vp, bp, lp, tl["attn"], tl["attn_k"])
            outs["ckv_p"].append(ckv_p.reshape(bp, lp, KV_LORA))
            outs["kpe_p"].append(kpe_p.reshape(bp, lp, QK_ROPE))
            mixer_out = (att, w_o, ln_mix_g[layer], ln_mix_b[layer])
            ql, qpe, ckv_s, kpe_s = _mla_proj(xs, _pe_tables(pos_s_rows), w, ns, 1, True)
            rows = ls * MLA_HEADS

            def per_batch(a, width):
                return jnp.transpose(a.reshape(ls, bs, MLA_HEADS, width), (1, 0, 2, 3)).reshape(bs, rows, width)

            ql_b = per_batch(ql, KV_LORA)
            qp_b = per_batch(qpe, LANES)[:, :, 0:QK_ROPE]
            ckv_sb = jnp.swapaxes(ckv_s.reshape(ls, bs, KV_LORA), 0, 1)
            kpe_sb = jnp.swapaxes(kpe_s.reshape(ls, bs, QK_ROPE), 0, 1)
            cn = jnp.pad(ckv_sb, ((0, 0), (0, PAGE_SIZE - ls), (0, 0)))
            knt = jnp.swapaxes(jnp.pad(kpe_sb, ((0, 0), (0, PAGE_SIZE - ls), (0, 0))), 1, 2)
            o_lat = _decode_attention(page_table, ql_b, qp_b, cn, knt, cache_ckv,
                                      jnp.swapaxes(cache_kpe, 2, 3), o, tl["pages"])
            outs["ckv_s"].append(ckv_sb)
            outs["kpe_s"].append(kpe_sb)
            ol = jnp.transpose(o_lat.reshape(bs, ls, MLA_HEADS, KV_LORA), (2, 1, 0, 3)).reshape(MLA_HEADS, ns, KV_LORA)
            wuv3 = jnp.transpose(w_uv[o], (1, 0, 2)).astype(BF16)
            xs = _decode_out(ol, wuv3, w_o, xs, ln_mix_g[layer], ln_mix_b[layer])
        xp, st_p = _ffn_prompt(xp, bp, lp, layer, wup_all, wd_all, conv_w[layer], conv_b[layer],
                               ln_ffn_g[layer], ln_ffn_b[layer], tl["ffn"], tl["ff_chunk"], mixer_out)
        outs["conv_p"].append(st_p[:, SUBLANES - (CONV_W - 1):, :])
        st_s = jnp.swapaxes(state_conv[layer], 0, 1)
        xs, st_s_new = _ffn_sample(xs, st_s, layer, wup_all, wd_all, conv_w[layer], conv_b[layer],
                                   ln_ffn_g[layer], ln_ffn_b[layer], bs, ls, tl["ff_chunk"])
        outs["conv_s"].append(jnp.swapaxes(st_s_new, 0, 1))

    y_p = xp.reshape(bp, lp, d)
    y_s = jnp.swapaxes(xs.reshape(ls, bs, d), 0, 1)
    return (y_p, y_s,
            jnp.stack(outs["pool_p"]), jnp.stack(outs["pool_s"]),
            jnp.stack(outs["ret_p"]), jnp.stack(outs["ret_s"]),
            jnp.stack(outs["ckv_p"]), jnp.stack(outs["ckv_s"]),
            jnp.stack(outs["kpe_p"]), jnp.stack(outs["kpe_s"]),
            jnp.stack(outs["conv_p"]), jnp.stack(outs["conv_s"]))
```

```python
import functools

import jax
import jax.numpy as jnp
import numpy as np
from jax import lax
from jax.experimental import pallas as pl
from jax.experimental.pallas import tpu as pltpu

F32 = jnp.float32
BF16 = jnp.bfloat16

PAGE_SIZE = 128
POOL_WINDOWS = (2, 4, 8, 16)
POOL_HIST = max(POOL_WINDOWS) - 1
RET_HEADS = 4
RET_DK = 128
MLA_HEADS = 8
QK_NOPE = 128
QK_ROPE = 64
V_DIM = 128
KV_LORA = 256
CONV_W = 3
DEPTH = 2
ALPHA = (2.0 * DEPTH) ** 0.25
ROPE_THETA = 10000.0
LN_EPS = 1e-5
RMS_EPS = 1e-6
GN_EPS = 1e-6
MLA_SCALE = (QK_NOPE + QK_ROPE) ** -0.5
LOG2E = 1.4426950408889634
SCALE_LOG2E = MLA_SCALE * LOG2E

LANES = 128
SUBLANES = 8
MIB = 1024 * 1024


_VMEM_LIMIT_MIB = {
    "matmul": 40, "matmul_res_ln": 32, "even_layer_prompt": 40, "even_mix_sample": 48, "ffn_prompt": 56,
    "ffn_sample": 40, "mla_proj": 48, "mla_proj_decode": 40, "flash_prompt": 56, "decode_attention": 48,
    "decode_out": 24,
}


def _cparams(sem, call):
    return pltpu.CompilerParams(dimension_semantics=sem, vmem_limit_bytes=_VMEM_LIMIT_MIB[call] * MIB)


def _resident(shape):
    nd = len(shape)
    return pl.BlockSpec(shape, lambda *_: (0,) * nd, pipeline_mode=pl.Buffered(1))


def _resident_layer(stacked_shape, layer):
    nd = len(stacked_shape) - 1
    return pl.BlockSpec((None,) + tuple(stacked_shape[1:]), lambda *_: (layer,) + (0,) * nd,
                        pipeline_mode=pl.Buffered(1))


def _whole(shape):
    nd = len(shape)
    return pl.BlockSpec(shape, lambda *_: (0,) * nd)


def _dot(a, b):
    return jnp.dot(a, b, preferred_element_type=F32)


def _dot_nt(a, b):
    return lax.dot_general(a, b, (((1,), (1,)), ((), ())), preferred_element_type=F32)


def _dot_tn(a, b):
    return lax.dot_general(a, b, (((0,), (0,)), ((), ())), preferred_element_type=F32)


def _layer_norm(z, g, b):
    mu = jnp.mean(z, axis=-1, keepdims=True)
    d = z - mu
    var = jnp.mean(d * d, axis=-1, keepdims=True)
    return d * lax.rsqrt(var + LN_EPS) * g + b


def _silu(x):
    return x * jax.nn.sigmoid(x)


def _mm_body(x_ref, w_ref, o_ref):
    o_ref[...] = _dot(x_ref[...].astype(BF16), w_ref[...]).astype(o_ref.dtype)


def _matmul(x, w, tm, out_dtype=F32):
    m, k = x.shape
    n = w.shape[1]
    return pl.pallas_call(
        _mm_body,
        grid=(m // tm,),
        in_specs=[pl.BlockSpec((tm, k), lambda i: (i, 0)), _resident((k, n))],
        out_specs=pl.BlockSpec((tm, n), lambda i: (i, 0)),
        out_shape=jax.ShapeDtypeStruct((m, n), out_dtype),
        compiler_params=_cparams(("parallel",), "matmul"),
        name="matmul",
    )(x, w)


def _mm_ln_body(a_ref, w_ref, r_ref, g_ref, b_ref, o_ref):
    y = _dot(a_ref[...].astype(BF16), w_ref[...])
    o_ref[...] = _layer_norm(ALPHA * r_ref[...] + y, g_ref[...], b_ref[...])


def _matmul_res_ln(a, w, res, g, b, tm):
    m, k = a.shape
    n = w.shape[1]
    return pl.pallas_call(
        _mm_ln_body,
        grid=(m // tm,),
        in_specs=[pl.BlockSpec((tm, k), lambda i: (i, 0)), _resident((k, n)),
                  pl.BlockSpec((tm, n), lambda i: (i, 0)), _resident((1, n)), _resident((1, n))],
        out_specs=pl.BlockSpec((tm, n), lambda i: (i, 0)),
        out_shape=jax.ShapeDtypeStruct((m, n), F32),
        compiler_params=_cparams(("parallel",), "matmul_res_ln"),
        name="matmul_res_ln",
    )(a, w, res, g.reshape(1, n), b.reshape(1, n))


def _rope_full(x, c2, s2):
    return x * c2 + pltpu.roll(x, 64, axis=1) * s2


def _group_norm_gate(o, gate, gn_row):
    mu = jnp.mean(o, axis=-1, keepdims=True)
    d = o - mu
    var = jnp.mean(d * d, axis=-1, keepdims=True)
    return _silu(gate) * (d * lax.rsqrt(var + GN_EPS) * gn_row)


def _even_layer_prompt_body(x_ref, win_ref, c2_ref, s2_ref, dec_ref, qd_ref, kd_ref, gc_ref, pw_ref, ps_ref,
                            gn_ref, wo_ref, lg_ref, lb_ref, o_ref, pst_ref, rst_ref,
                            mix_ref, ext_ref, s_ref, *h_refs, tm, n_j, sub):
    j = pl.program_id(1)
    pd = len(POOL_WINDOWS) * LANES

    @pl.when(j == 0)
    def _():
        ext_ref[0:16, :] = jnp.zeros((16, pd), F32)
        s_ref[...] = jnp.zeros(s_ref.shape, F32)

    @pl.when(j > 0)
    def _():
        ext_ref[0:16, :] = ext_ref[tm:tm + 16, :]

    c = dec_ref.shape[1]
    k_scale = RET_DK ** -0.5
    nblk = 4 * LANES
    def project(r0):
        h_ref = h_refs[r0 // sub]
        xb = x_ref[r0:r0 + sub, :].astype(BF16)
        for c0 in range(0, h_ref.shape[1], nblk):
            h_ref[:, c0:c0 + nblk] = _dot(xb, win_ref[:, c0:c0 + nblk])

    project(0)
    for r0 in range(0, tm, sub):
        rb = slice(r0, r0 + sub)
        h_ref = h_refs[r0 // sub]
        if r0 + sub < tm:
            project(r0 + sub)
        ext_ref[16 + r0:16 + r0 + sub, :] = h_ref[:, 0:pd]

        pos = (j * tm + r0 + lax.broadcasted_iota(jnp.int32, (sub, 1), 0)).astype(F32)
        for g, w in enumerate(POOL_WINDOWS):
            cols = slice(g * LANES, (g + 1) * LANES)
            e = ext_ref[r0:r0 + sub + 16, cols]
            u = e[16:, :]
            s = 1
            while s < w:
                e = e + pltpu.roll(e, s, axis=0)
                s *= 2
            cnt = jnp.minimum(float(w), pos + 1.0)
            pooled = e[16:, :] / cnt - u
            mixed = _dot(pooled.astype(BF16), pw_ref[g]) * ps_ref[:, cols]
            mix_ref[rb, cols] = mixed.astype(BF16)

        for ci in range(r0 // c, (r0 + sub) // c):
            rows = slice(ci * c, (ci + 1) * c)
            lrows = slice(ci * c - r0, (ci + 1) * c - r0)
            c2 = c2_ref[rows, :]
            s2 = s2_ref[rows, :]
            for hd in range(RET_HEADS):
                def col(part, hd=hd):
                    return slice(pd + (part * RET_HEADS + hd) * LANES, pd + (part * RET_HEADS + hd + 1) * LANES)
                q = _rope_full(h_ref[lrows, col(0)], c2, s2)
                k = _rope_full(h_ref[lrows, col(1)], c2, s2) * k_scale
                vb = h_ref[lrows, col(2)].astype(BF16)
                gate = h_ref[lrows, col(3)]
                st = s_ref[hd]
                sc = _dot_nt(q.astype(BF16), k.astype(BF16)) * dec_ref[hd]
                o = _dot(sc.astype(BF16), vb)
                o = o + _dot((q * qd_ref[hd]).astype(BF16), st.astype(BF16))
                s_ref[hd] = gc_ref[hd] * st + _dot_tn((k * kd_ref[hd]).astype(BF16), vb)
                ret = _group_norm_gate(o, gate, gn_ref[:, hd * LANES:(hd + 1) * LANES])
                mix_ref[rows, pd + hd * LANES:pd + (hd + 1) * LANES] = ret.astype(BF16)

        y = _dot(mix_ref[rb, :], wo_ref[...])
        o_ref[rb, :] = _layer_norm(ALPHA * x_ref[rb, :] + y, lg_ref[...], lb_ref[...])

    @pl.when(j == n_j - 1)
    def _():
        pst_ref[0] = ext_ref[pl.ds(tm + 1, POOL_HIST), :]
        rst_ref[0] = s_ref[...]


def _even_layer_prompt(x, bsz, seq, w_in, tabs, pool_w, pool_scale, gn_g, w_o, ln_g, ln_b, tm):
    n_j = seq // tm
    d = x.shape[1]
    pd = pool_scale.shape[0]
    ed = pd + RET_HEADS * LANES
    chunk = tabs["dec"].shape[1]
    sub = min(tm, max(chunk, 2 * LANES))
    body = functools.partial(_even_layer_prompt_body, tm=tm, n_j=n_j, sub=sub)
    return pl.pallas_call(
        body,
        grid=(bsz, n_j),
        in_specs=[
            pl.BlockSpec((tm, d), lambda b, j: (b * n_j + j, 0)),
            _resident(w_in.shape),
            pl.BlockSpec((tm, LANES), lambda b, j: (j, 0)),
            pl.BlockSpec((tm, LANES), lambda b, j: (j, 0)),
            _resident((RET_HEADS, chunk, chunk)),
            _resident((RET_HEADS, chunk, LANES)),
            _resident((RET_HEADS, chunk, LANES)),
            _resident((RET_HEADS, 1, LANES)),
            _resident(pool_w.shape),
            _resident((1, pd)),
            _resident((1, RET_HEADS * LANES)),
            _resident(w_o.shape),
            _resident((1, d)),
            _resident((1, d)),
        ],
        out_specs=[
            pl.BlockSpec((tm, d), lambda b, j: (b * n_j + j, 0)),
            pl.BlockSpec((1, POOL_HIST, pd), lambda b, j: (b, 0, 0)),
            pl.BlockSpec((1, RET_HEADS, RET_DK, LANES), lambda b, j: (b, 0, 0, 0)),
        ],
        out_shape=[
            jax.ShapeDtypeStruct((bsz * seq, d), F32),
            jax.ShapeDtypeStruct((bsz, POOL_HIST, pd), F32),
            jax.ShapeDtypeStruct((bsz, RET_HEADS, RET_DK, LANES), F32),
        ],
        scratch_shapes=[pltpu.VMEM((tm, ed), BF16), pltpu.VMEM((tm + 16, pd), F32),
                        pltpu.VMEM((RET_HEADS, RET_DK, LANES), F32)]
        + [pltpu.VMEM((sub, w_in.shape[1]), F32)] * (tm // sub),
        compiler_params=_cparams(("parallel", "arbitrary"), "even_layer_prompt"),
        name="even_layer_prompt",
    )(x, w_in, tabs["c2"], tabs["s2"], tabs["dec"], tabs["qd"], tabs["kd"], tabs["gc"],
      pool_w, pool_scale.reshape(1, pd), gn_g.reshape(1, -1), w_o, ln_g.reshape(1, d), ln_b.reshape(1, d))


def _even_mix_sample_body(h_ref, hist_ref, s0_ref, c2_ref, s2_ref, dm_ref, qd_ref, kd_ref, gc_ref,
                          pw_ref, ps_ref, gn_ref, mix_ref, hist_o_ref, s_o_ref, oc_ref,
                          *, nb, ls, cnts):
    pd = len(POOL_WINDOWS) * LANES
    ext = [hist_ref[i] for i in range(POOL_HIST)]
    ext += [h_ref[l * nb:(l + 1) * nb, 0:pd] for l in range(ls)]
    for i in range(POOL_HIST):
        hist_o_ref[i] = ext[ls + i]
    for g, w in enumerate(POOL_WINDOWS):
        cols = slice(g * LANES, (g + 1) * LANES)
        outs = []
        for l in range(ls):
            top = POOL_HIST + l
            acc = ext[top][:, cols]
            for jj in range(1, w):
                acc = acc + ext[top - jj][:, cols]
            outs.append(acc / cnts[g][l] - ext[top][:, cols])
        pooled = jnp.concatenate(outs, axis=0)
        mixed = _dot(pooled.astype(BF16), pw_ref[g]) * ps_ref[:, cols]
        mix_ref[:, cols] = mixed.astype(BF16)

    rows_b = lax.broadcasted_iota(jnp.int32, (ls * nb, 1), 0) % nb
    k_scale = RET_DK ** -0.5
    c2 = c2_ref[...]
    s2 = s2_ref[...]
    qs, ks, vs = [], [], []
    for hd in range(RET_HEADS):
        def col(part, hd=hd):
            return slice(pd + (part * RET_HEADS + hd) * LANES, pd + (part * RET_HEADS + hd + 1) * LANES)
        q = _rope_full(h_ref[:, col(0)], c2, s2)
        k = _rope_full(h_ref[:, col(1)], c2, s2) * k_scale
        vb = h_ref[:, col(2)].astype(BF16)
        sc = _dot_nt(q.astype(BF16), k.astype(BF16)) * dm_ref[hd]
        oc_ref[hd] = _dot(sc.astype(BF16), vb)
        qs.append(q * qd_ref[hd])
        ks.append(k * kd_ref[hd])
        vs.append(vb)

    def per_batch(b, carry):
        sel = rows_b == b
        for hd in range(RET_HEADS):
            st = s0_ref[b, hd]
            qm = jnp.where(sel, qs[hd], 0.0).astype(BF16)
            km = jnp.where(sel, ks[hd], 0.0).astype(BF16)
            oc_ref[hd] += _dot(qm, st.astype(BF16))
            s_o_ref[b, hd] = gc_ref[hd] * st + _dot_tn(km, vs[hd])
        return carry

    lax.fori_loop(0, nb, per_batch, 0)

    for hd in range(RET_HEADS):
        gate = h_ref[:, pd + (3 * RET_HEADS + hd) * LANES:pd + (3 * RET_HEADS + hd + 1) * LANES]
        ret = _group_norm_gate(oc_ref[hd], gate, gn_ref[:, hd * LANES:(hd + 1) * LANES])
        mix_ref[:, pd + hd * LANES:pd + (hd + 1) * LANES] = ret.astype(BF16)


def _even_mix_sample(h, hist, s0, tabs, pool_w, pool_scale, gn_g, nb, ls, past_len):
    pd = pool_scale.shape[0]
    ed = pd + RET_HEADS * LANES
    cnts = tuple(tuple(float(min(w, past_len + l + 1)) for l in range(ls)) for w in POOL_WINDOWS)
    body = functools.partial(_even_mix_sample_body, nb=nb, ls=ls, cnts=cnts)
    n = ls * nb
    args = (h, hist, s0, tabs["c2"], tabs["s2"], tabs["dm"], tabs["qd"], tabs["kd"], tabs["gc"],
            pool_w, pool_scale.reshape(1, pd), gn_g.reshape(1, -1))
    return pl.pallas_call(
        body,
        grid=(1,),
        in_specs=[_resident(a.shape) for a in args],
        out_specs=[_whole((n, ed)), _whole(hist.shape), _whole(s0.shape)],
        out_shape=[jax.ShapeDtypeStruct((n, ed), BF16),
                   jax.ShapeDtypeStruct(hist.shape, F32),
                   jax.ShapeDtypeStruct(s0.shape, F32)],
        scratch_shapes=[pltpu.VMEM((RET_HEADS, n, LANES), F32)],
        compiler_params=_cparams(("arbitrary",), "even_mix_sample"),
        name="even_mix_sample",
    )(*args)


def _ffn_chunk(xb, wup_ref, wd_ref, cw_ref, cb_ref, c, tf, dff, shift_fn):
    cols = slice(c * tf, (c + 1) * tf)
    a = _dot(xb, wup_ref[:, cols])
    gate_in = _dot(xb, wup_ref[:, dff + c * tf:dff + (c + 1) * tf])
    a1, a2 = shift_fn(a, c)
    conv = cb_ref[:, cols] + cw_ref[0:1, cols] * a2
    conv = conv + cw_ref[1:2, cols] * a1
    conv = conv + cw_ref[2:3, cols] * a
    act = (_silu(conv) * gate_in).astype(BF16)
    return a, _dot(act, wd_ref[cols, :])


def _ffn_prompt_body(x_ref, *rest, tm, n_j, tf, dff, mixer_out):
    if mixer_out:
        a_ref, wo_ref, mg_ref, mb_ref = rest[:4]
        rest = rest[4:]
    wup_ref, wd_ref, cw_ref, cb_ref, g_ref, b_ref, o_ref, st_ref, carry_ref = rest
    j = pl.program_id(1)

    @pl.when(j == 0)
    def _():
        carry_ref[...] = jnp.zeros(carry_ref.shape, F32)

    x = x_ref[...]
    if mixer_out:
        x = _layer_norm(ALPHA * x + _dot(a_ref[...], wo_ref[...]), mg_ref[...], mb_ref[...])
    row = lax.broadcasted_iota(jnp.int32, (tm, tf), 0)

    def shift_fn(a, c):
        prev = carry_ref[:, c * tf:(c + 1) * tf]
        a1 = jnp.where(row == 0, prev[7:8, :], pltpu.roll(a, 1, axis=0))
        a2 = jnp.where(row == 0, prev[6:7, :], jnp.where(row == 1, prev[7:8, :], pltpu.roll(a, 2, axis=0)))
        return a1, a2

    xb = x.astype(BF16)
    acc = None
    for c in range(dff // tf):
        a, y = _ffn_chunk(xb, wup_ref, wd_ref, cw_ref, cb_ref, c, tf, dff, shift_fn)
        acc = y if acc is None else acc + y
        tail = a[tm - SUBLANES:tm, :]
        carry_ref[:, c * tf:(c + 1) * tf] = tail
        st_ref[0, :, c * tf:(c + 1) * tf] = tail
    o_ref[...] = _layer_norm(ALPHA * x + acc, g_ref[...], b_ref[...])


def _ffn_prompt(x, bsz, seq, layer, wup, wd, conv_w, conv_b, g, b, tm, tf, mixer_out=None):
    d = x.shape[1]
    dff = wd.shape[1]
    n_j = seq // tm
    body = functools.partial(_ffn_prompt_body, tm=tm, n_j=n_j, tf=tf, dff=dff, mixer_out=mixer_out is not None)
    row_tile = lambda bi, j: (bi * n_j + j, 0)
    pre_args, pre_specs = [], []
    if mixer_out is not None:
        a, w_o, mg, mb = mixer_out
        pre_args = [a, w_o, mg.reshape(1, d), mb.reshape(1, d)]
        pre_specs = [pl.BlockSpec((tm, a.shape[1]), row_tile), _resident(w_o.shape), _resident((1, d)), _resident((1, d))]
    return pl.pallas_call(
        body,
        grid=(bsz, n_j),
        in_specs=[pl.BlockSpec((tm, d), row_tile)] + pre_specs + [
                  _resident_layer(wup.shape, layer), _resident_layer(wd.shape, layer), _resident(conv_w.shape),
                  _resident((1, dff)), _resident((1, d)), _resident((1, d))],
        out_specs=[pl.BlockSpec((tm, d), row_tile),
                   pl.BlockSpec((1, SUBLANES, dff), lambda bi, j: (bi, 0, 0))],
        out_shape=[jax.ShapeDtypeStruct(x.shape, F32),
                   jax.ShapeDtypeStruct((bsz, SUBLANES, dff), F32)],
        scratch_shapes=[pltpu.VMEM((SUBLANES, dff), F32)],
        compiler_params=_cparams(("parallel", "arbitrary"), "ffn_prompt"),
        name="ffn_prompt",
    )(x, *pre_args, wup, wd, conv_w, conv_b.reshape(1, dff), g.reshape(1, d), b.reshape(1, d))


def _ffn_sample_body(x_ref, st_ref, wup_ref, wd_ref, cw_ref, cb_ref, g_ref, b_ref, o_ref, st_o_ref,
                     *, nb, ls, tf, dff):
    x = x_ref[...]
    xb = x.astype(BF16)
    nh = CONV_W - 1

    def shift_fn(a, c):
        cols = slice(c * tf, (c + 1) * tf)
        ext = [st_ref[i, :, cols] for i in range(nh)] + [a[l * nb:(l + 1) * nb, :] for l in range(ls)]
        a1 = jnp.concatenate([ext[nh + l - 1] for l in range(ls)], axis=0)
        a2 = jnp.concatenate([ext[nh + l - 2] for l in range(ls)], axis=0)
        for i in range(nh):
            st_o_ref[i, :, cols] = ext[ls + i]
        return a1, a2

    acc = None
    for c in range(dff // tf):
        _, y = _ffn_chunk(xb, wup_ref, wd_ref, cw_ref, cb_ref, c, tf, dff, shift_fn)
        acc = y if acc is None else acc + y
    o_ref[...] = _layer_norm(ALPHA * x + acc, g_ref[...], b_ref[...])


def _ffn_sample(x, st, layer, wup, wd, conv_w, conv_b, g, b, nb, ls, tf):
    d = x.shape[1]
    dff = wd.shape[1]
    body = functools.partial(_ffn_sample_body, nb=nb, ls=ls, tf=tf, dff=dff)
    args = (x, st, wup, wd, conv_w, conv_b.reshape(1, dff), g.reshape(1, d), b.reshape(1, d))
    in_specs = [_resident(a.shape) for a in args]
    in_specs[2] = _resident_layer(wup.shape, layer)
    in_specs[3] = _resident_layer(wd.shape, layer)
    return pl.pallas_call(
        body,
        grid=(1,),
        in_specs=in_specs,
        out_specs=[_whole(x.shape), _whole(st.shape)],
        out_shape=[jax.ShapeDtypeStruct(x.shape, F32), jax.ShapeDtypeStruct(st.shape, F32)],
        compiler_params=_cparams(("arbitrary",), "ffn_sample"),
        name="ffn_sample",
    )(*args)


def _rope_pe(blk, cc, s1, s2):
    return blk * cc + pltpu.roll(blk, 96, axis=1) * s1 + pltpu.roll(blk, 32, axis=1) * s2


def _rms_norm(x, g):
    ms = jnp.mean(x * x, axis=-1, keepdims=True)
    return x * lax.rsqrt(ms + RMS_EPS) * g


def _mla_proj_body(x_ref, cc_ref, s1_ref, s2_ref, wdq_ref, qg_ref, wuq_ref, wdkv_ref, kvg_ref, *rest,
                   decode, sub):
    hw = 2 * LANES
    for r0 in range(0, x_ref.shape[0], sub):
        rb = slice(r0, r0 + sub)
        cc = cc_ref[rb, :]
        s1 = s1_ref[rb, :]
        s2 = s2_ref[rb, :]
        xb = x_ref[rb, :].astype(BF16)
        cq = _rms_norm(_dot(xb, wdq_ref[...]), qg_ref[...])
        q = _dot(cq.astype(BF16), wuq_ref[...])
        kv = _dot(xb, wdkv_ref[...])
        ckv = _rms_norm(kv[:, 0:KV_LORA], kvg_ref[...])
        kpe = _rope_pe(kv[:, KV_LORA:KV_LORA + LANES], cc, s1, s2)
        if decode:
            wukt_ref, ql_ref, qp_ref, ckv_ref, kpe_ref = rest
            for h in range(MLA_HEADS):
                qn = q[:, h * hw:h * hw + LANES].astype(BF16)
                ql_ref[rb, h * KV_LORA:(h + 1) * KV_LORA] = _dot(qn, wukt_ref[h]).astype(BF16)
                qp = _rope_pe(q[:, h * hw + LANES:(h + 1) * hw], cc, s1, s2)
                qp_ref[rb, h * LANES:(h + 1) * LANES] = qp.astype(BF16)
        else:
            wuk_ref, wuv_ref, qo_ref, ko_ref, vo_ref, ckv_ref, kpe_ref = rest
            cb = ckv.astype(BF16)
            kn = _dot(cb, wuk_ref[...])
            vo_ref[rb, :] = _dot(cb, wuv_ref[...]).astype(BF16)
            kpb = kpe.astype(BF16)
            for h in range(MLA_HEADS):
                qo_ref[rb, h * hw:h * hw + LANES] = q[:, h * hw:h * hw + LANES].astype(BF16)
                qp = _rope_pe(q[:, h * hw + LANES:(h + 1) * hw], cc, s1, s2)
                qo_ref[rb, h * hw + LANES:(h + 1) * hw] = qp.astype(BF16)
                ko_ref[rb, h * hw:h * hw + LANES] = kn[:, h * LANES:(h + 1) * LANES].astype(BF16)
                ko_ref[rb, h * hw + LANES:(h + 1) * hw] = kpb
        ckv_ref[rb, :] = ckv
        kpe_ref[rb, :] = kpe[:, 0:QK_ROPE]


def _mla_proj(x, tabs, w, tm, n_pos_blocks, decode):
    t, d = x.shape
    hw = 2 * LANES
    body = functools.partial(_mla_proj_body, decode=decode, sub=tm)
    row = lambda i: (i, 0)
    tab = lambda i: (i % n_pos_blocks, 0)
    ins = [x, tabs["cc"], tabs["s1"], tabs["s2"], w["dq"], w["qg"], w["uq"], w["dkv"], w["kvg"]]
    in_specs = [pl.BlockSpec((tm, d), row)] + [pl.BlockSpec((tm, LANES), tab)] * 3
    in_specs += [_resident(a.shape) for a in ins[4:]]
    if decode:
        ins += [w["ukt"]]
        in_specs += [_resident(w["ukt"].shape)]
        outs = [(MLA_HEADS * KV_LORA, BF16), (MLA_HEADS * LANES, BF16)]
    else:
        ins += [w["uk"], w["uv"]]
        in_specs += [_resident(w["uk"].shape), _resident(w["uv"].shape)]
        outs = [(MLA_HEADS * hw, BF16), (MLA_HEADS * hw, BF16), (MLA_HEADS * V_DIM, BF16)]
    outs += [(KV_LORA, F32), (QK_ROPE, F32)]
    return pl.pallas_call(
        body,
        grid=(t // tm,),
        in_specs=in_specs,
        out_specs=[pl.BlockSpec((tm, n), row) for n, _ in outs],
        out_shape=[jax.ShapeDtypeStruct((t, n), dt) for n, dt in outs],
        compiler_params=_cparams(("parallel",), "mla_proj_decode" if decode else "mla_proj"),
        name="mla_proj_decode" if decode else "mla_proj",
    )(*ins)


def _online_softmax_update(s2, m_ref, l_ref, acc_ref, pv_fn, row_chunk):
    rows, width = s2.shape
    n = width // LANES
    aw = acc_ref.shape[-1] // LANES
    p_chunks, alphas = [], []
    for r0 in range(0, rows, row_chunk):
        rs_ = slice(r0, r0 + row_chunk)
        tiles = [s2[rs_, j * LANES:(j + 1) * LANES] for j in range(n)]
        mx = tiles[0]
        for t in tiles[1:]:
            mx = jnp.maximum(mx, t)
        m_prev = m_ref[rs_, :]
        m_new = jnp.maximum(m_prev, jnp.max(mx, axis=-1, keepdims=True))
        alpha = jnp.exp2(m_prev - m_new)
        ps = [jnp.exp2(t - m_new) for t in tiles]
        if l_ref is not None:
            tot = ps[0]
            for t in ps[1:]:
                tot = tot + t
            l_ref[rs_, :] = alpha * l_ref[rs_, :] + jnp.sum(tot, axis=-1, keepdims=True)
        m_ref[rs_, :] = m_new
        p_chunks.append((jnp.concatenate(ps, axis=1) if n > 1 else ps[0]).astype(BF16))
        alphas.append(alpha)
    p = jnp.concatenate(p_chunks, axis=0) if len(p_chunks) > 1 else p_chunks[0]
    pv = pv_fn(p)
    for i, r0 in enumerate(range(0, rows, row_chunk)):
        rs_ = slice(r0, r0 + row_chunk)
        a_w = alphas[i] if aw == 1 else jnp.concatenate([alphas[i]] * aw, axis=1)
        acc_ref[rs_, :] = a_w * acc_ref[rs_, :] + pv[rs_, :]


def _flash_body(q_ref, k_ref, v_ref, o_ref, m_ref, acc_ref, *, tq, tk, row_chunk):
    qi = pl.program_id(1)
    hw = 2 * LANES
    m_ref[...] = jnp.full(m_ref.shape, -jnp.inf, F32)
    acc_ref[...] = jnp.zeros(acc_ref.shape, F32)
    ones = jnp.ones((tk, LANES), BF16)

    def step(h, key0, diag_off):
        start = pl.multiple_of(key0, tk)
        q = q_ref[:, h * hw:(h + 1) * hw]
        k = k_ref[pl.ds(start, tk), h * hw:(h + 1) * hw]
        v1 = jnp.concatenate([v_ref[pl.ds(start, tk), h * V_DIM:(h + 1) * V_DIM], ones], axis=1)
        s = _dot_nt(q, k) * SCALE_LOG2E
        if diag_off is not None:
            row = lax.broadcasted_iota(jnp.int32, s.shape, 0)
            colm = lax.broadcasted_iota(jnp.int32, s.shape, 1)
            s = jnp.where(colm + diag_off <= row, s, -jnp.inf)
        _online_softmax_update(s, m_ref.at[h], None, acc_ref.at[h], lambda p: _dot(p, v1), row_chunk)

    def loop_body(kj, carry):
        for h in range(MLA_HEADS):
            step(h, kj * tk, None)
        return carry

    lax.fori_loop(0, qi * (tq // tk), loop_body, 0)
    for h in range(MLA_HEADS):
        for j in range(tq // tk):
            step(h, qi * tq + j * tk, j * tk)
        o_ref[:, h * V_DIM:(h + 1) * V_DIM] = (acc_ref[h, :, 0:V_DIM] / acc_ref[h, :, V_DIM:]).astype(BF16)


def _flash_prompt(qp, kp, vp, bsz, seq, tq, tk):
    nq = seq // tq
    body = functools.partial(_flash_body, tq=tq, tk=tk, row_chunk=min(tq, 64))
    return pl.pallas_call(
        body,
        grid=(bsz, nq),
        in_specs=[pl.BlockSpec((tq, qp.shape[1]), lambda b, i: (b * nq + i, 0)),
                  pl.BlockSpec((seq, kp.shape[1]), lambda b, i: (b, 0)),
                  pl.BlockSpec((seq, vp.shape[1]), lambda b, i: (b, 0))],
        out_specs=pl.BlockSpec((tq, vp.shape[1]), lambda b, i: (b * nq + i, 0)),
        out_shape=jax.ShapeDtypeStruct(vp.shape, BF16),
        scratch_shapes=[pltpu.VMEM((MLA_HEADS, tq, LANES), F32), pltpu.VMEM((MLA_HEADS, tq, V_DIM + LANES), F32)],
        compiler_params=_cparams(("parallel", "arbitrary"), "flash_prompt"),
        name="flash_prompt",
    )(qp, kp, vp)


def _decode_body(pt_ref, ql_ref, qp_ref, cn_ref, kn_ref, ckv_hbm, kpe_hbm, o_ref,
                 cbuf_ref, rbuf_ref, sem_ref, m_ref, l_ref, acc_ref, *, gp, ls, layer, n_split):
    b = pl.program_id(0)
    g = pl.program_id(1)
    n_b = pl.num_programs(0)
    n_g = pl.num_programs(1)
    slot = lax.rem(b * n_g + g, 2)
    other = 1 - slot

    def page_copies(bb, grp, slot):
        cps = []
        for i in range(gp):
            pg = pt_ref[bb, grp * gp + i]
            keys = pl.ds(i * PAGE_SIZE, PAGE_SIZE)
            cps.append(pltpu.make_async_copy(ckv_hbm.at[layer, pg], cbuf_ref.at[slot, keys, :], sem_ref.at[slot, 0]))
            cps.append(pltpu.make_async_copy(kpe_hbm.at[layer, pg], rbuf_ref.at[slot, i], sem_ref.at[slot, 1]))
        return cps

    def start(bb, grp, slot):
        for cp in page_copies(bb, grp, slot):
            cp.start()

    @pl.when((b == 0) & (g == 0))
    def _():
        start(b, g, slot)

    @pl.when(g < n_g - 1)
    def _():
        start(b, g + 1, other)

    @pl.when((g == n_g - 1) & (b < n_b - 1))
    def _():
        start(b + 1, 0, other)

    @pl.when(g == 0)
    def _():
        m_ref[...] = jnp.full(m_ref.shape, -jnp.inf, F32)
        l_ref[...] = jnp.zeros(l_ref.shape, F32)
        acc_ref[...] = jnp.zeros(acc_ref.shape, F32)

    ql = ql_ref[0]
    qp = qp_ref[0]

    for cp in page_copies(b, g, slot):
        cp.wait()

    kc_len = gp * PAGE_SIZE // n_split
    kbs, scores = [], []
    for c in range(n_split):
        ks = slice(c * kc_len, (c + 1) * kc_len)
        kb = cbuf_ref[slot, ks, :].astype(BF16)
        pages = range(c * kc_len // PAGE_SIZE, (c + 1) * kc_len // PAGE_SIZE)
        s_pe = jnp.concatenate([_dot(qp, rbuf_ref[slot, p].astype(BF16)) for p in pages], axis=1)
        scores.append((_dot_nt(ql, kb) + s_pe) * SCALE_LOG2E)
        kbs.append(kb)
    state = (m_ref.at[0], l_ref.at[0], acc_ref.at[0])
    n_rows = ql.shape[0]
    for c in range(n_split):
        _online_softmax_update(scores[c], *state, lambda p, kb=kbs[c]: _dot(p, kb), n_rows)

    @pl.when(g == n_g - 1)
    def _():
        kc = cn_ref[0].astype(BF16)
        s = (_dot_nt(ql, kc) + _dot(qp, kn_ref[0].astype(BF16))) * SCALE_LOG2E
        r = lax.broadcasted_iota(jnp.int32, s.shape, 0) // MLA_HEADS
        cidx = lax.broadcasted_iota(jnp.int32, s.shape, 1)
        s = jnp.where((cidx <= r) & (cidx < ls), s, -jnp.inf)
        _online_softmax_update(s, *state, lambda p: _dot(p, kc), n_rows)
        l_w = jnp.concatenate([l_ref[0]] * (KV_LORA // LANES), axis=1)
        o_ref[0] = acc_ref[0] / l_w


def _decode_attention(page_table, ql, qp, cn, knt, cache_ckv, cache_kpet, layer, gp):
    nb, rows, _ = ql.shape
    n_pages = page_table.shape[1]
    ls = rows // MLA_HEADS
    n_split = 4 if gp % 4 == 0 else 1
    body = functools.partial(_decode_body, gp=gp, ls=ls, layer=layer, n_split=n_split)
    per_b = lambda b, g, pt: (b, 0, 0)
    in_specs = [pl.BlockSpec((1, rows, KV_LORA), per_b), pl.BlockSpec((1, rows, QK_ROPE), per_b),
                pl.BlockSpec((1, PAGE_SIZE, KV_LORA), per_b), pl.BlockSpec((1, QK_ROPE, PAGE_SIZE), per_b),
                pl.BlockSpec(memory_space=pl.ANY), pl.BlockSpec(memory_space=pl.ANY)]
    n_buf = 2
    grid_spec = pltpu.PrefetchScalarGridSpec(
        num_scalar_prefetch=1,
        grid=(nb, n_pages // gp),
        in_specs=in_specs,
        out_specs=pl.BlockSpec((1, rows, KV_LORA), per_b),
        scratch_shapes=[pltpu.VMEM((n_buf, gp * PAGE_SIZE, KV_LORA), F32),
                        pltpu.VMEM((n_buf, gp, QK_ROPE, PAGE_SIZE), F32),
                        pltpu.SemaphoreType.DMA((n_buf, 2)),
                        pltpu.VMEM((1, rows, LANES), F32), pltpu.VMEM((1, rows, LANES), F32),
                        pltpu.VMEM((1, rows, KV_LORA), F32)],
    )
    return pl.pallas_call(
        body,
        grid_spec=grid_spec,
        out_shape=jax.ShapeDtypeStruct((nb, rows, KV_LORA), F32),
        compiler_params=_cparams(("arbitrary", "arbitrary"), "decode_attention"),
        name="decode_attention",
    )(page_table, ql, qp, cn, knt, cache_ckv, cache_kpet)


def _decode_out_body(ol_ref, wuv_ref, wo_ref, r_ref, g_ref, b_ref, o_ref):
    y = None
    for h in range(MLA_HEADS):
        oh = _dot(ol_ref[h].astype(BF16), wuv_ref[h]).astype(BF16)
        t = _dot(oh, wo_ref[h * V_DIM:(h + 1) * V_DIM, :])
        y = t if y is None else y + t
    o_ref[...] = _layer_norm(ALPHA * r_ref[...] + y, g_ref[...], b_ref[...])


def _decode_out(ol, wuv3, wo, res, g, b):
    d = res.shape[1]
    args = (ol, wuv3, wo, res, g.reshape(1, d), b.reshape(1, d))
    return pl.pallas_call(
        _decode_out_body,
        grid=(1,),
        in_specs=[_resident(a.shape) for a in args],
        out_specs=_whole(res.shape),
        out_shape=jax.ShapeDtypeStruct(res.shape, F32),
        compiler_params=_cparams(("arbitrary",), "decode_out"),
        name="decode_out",
    )(*args)


def _rope_angles(pos, half):
    inv_freq = ROPE_THETA ** (-np.arange(half, dtype=np.float64) / half)
    ang = np.asarray(pos, np.float64)[:, None] * inv_freq[None, :]
    return np.cos(ang), np.sin(ang)


def _f32(tabs):
    return {k: np.ascontiguousarray(v, dtype=np.float32) for k, v in tabs.items()}


def _ret_tables(pos, chunk, rows_per_pos):
    cos, sin = _rope_angles(pos, RET_DK // 2)
    log_gamma = np.log(1.0 - 2.0 ** (-5.0 - np.arange(RET_HEADS, dtype=np.float64)))
    idx = np.repeat(np.arange(chunk, dtype=np.float64), rows_per_pos)
    diff = idx[:, None] - idx[None, :]
    dec = np.where(diff >= 0, np.exp(np.maximum(diff, 0.0)[None] * log_gamma[:, None, None]), 0.0)
    q_dec = np.exp((idx + 1.0)[None, :] * log_gamma[:, None])
    k_dec = np.exp((chunk - 1.0 - idx)[None, :] * log_gamma[:, None])
    gc = np.exp(chunk * log_gamma)
    n = idx.shape[0]
    return _f32({
        "c2": np.concatenate([cos, cos], axis=1),
        "s2": np.concatenate([-sin, sin], axis=1),
        "dec": dec,
        "qd": np.broadcast_to(q_dec[:, :, None], (RET_HEADS, n, LANES)),
        "kd": np.broadcast_to(k_dec[:, :, None], (RET_HEADS, n, LANES)),
        "gc": np.broadcast_to(gc[:, None, None], (RET_HEADS, 1, LANES)),
    })


def _pe_tables(pos):
    cos, sin = _rope_angles(pos, QK_ROPE // 2)
    z = np.zeros_like(cos)
    return _f32({"cc": np.concatenate([cos, cos, z, z], axis=1),
                 "s1": np.concatenate([-sin, z, z, z], axis=1),
                 "s2": np.concatenate([z, sin, z, z], axis=1)})


def _pad_heads(w, nope, rope):
    k = w.shape[0]
    w3 = w.reshape(k, MLA_HEADS, nope + rope)
    pad = jnp.zeros((k, MLA_HEADS, 2 * LANES - nope - rope), w.dtype)
    return jnp.concatenate([w3, pad], axis=2).reshape(k, MLA_HEADS * 2 * LANES)


def _tiles(seq, n_pages, dff):
    def fit(t, n=seq):
        while n % t:
            t //= 2
        return t
    return {"tok": fit(1024), "mix": fit(512), "ret_chunk": fit(256), "ffn": fit(512), "attn": fit(512), "attn_k": fit(512), "ff_chunk": dff,
            "pages": fit(64, n_pages)}


def kernel(x_prompt, x_sample, state_pool, state_ret, cache_ckv, cache_kpe, state_conv, page_table,
           w_in_even, pool_w, pool_scale, ret_gn_g, w_o_even,
           w_dq, q_norm_g, w_uq, w_dkv, kv_norm_g, w_uk, w_uv, w_o_mla,
           w_up, conv_w, conv_b, w_down, ln_mix_g, ln_mix_b, ln_ffn_g, ln_ffn_b):
    bp, lp, d = x_prompt.shape
    bs, ls, _ = x_sample.shape
    past_len = page_table.shape[1] * PAGE_SIZE
    depth = w_up.shape[0]
    dff = w_down.shape[1]
    tl = _tiles(lp, page_table.shape[1], dff)
    ns = bs * ls
    assert lp % tl["ret_chunk"] == 0 and tl["mix"] % tl["ret_chunk"] == 0
    assert dff % tl["ff_chunk"] == 0 and page_table.shape[1] % tl["pages"] == 0 and ls <= PAGE_SIZE

    xp = x_prompt.reshape(bp * lp, d)
    xs = jnp.swapaxes(x_sample, 0, 1).reshape(ns, d)
    pos_p = np.arange(lp)
    pos_s_rows = np.repeat(past_len + np.arange(ls), bs)

    wup_all = w_up.astype(BF16)
    wd_all = w_down.astype(BF16)
    outs = {k: [] for k in ("pool_p", "pool_s", "ret_p", "ret_s", "ckv_p", "ckv_s", "kpe_p", "kpe_s",
                            "conv_p", "conv_s")}
    for layer in range(depth):
        mixer_out = None
        if layer % 2 == 0:
            e = layer // 2
            w_in = w_in_even[e].astype(BF16)
            w_o = w_o_even[e].astype(BF16)
            pw = pool_w[e].astype(BF16)
            tp = _ret_tables(pos_p, tl["ret_chunk"], 1)
            xp, pst, rst = _even_layer_prompt(xp, bp, lp, w_in, tp, pw, pool_scale[e], ret_gn_g[e], w_o,
                                              ln_mix_g[layer], ln_mix_b[layer], tl["mix"])
            outs["pool_p"].append(pst)
            outs["ret_p"].append(rst)
            ts = _ret_tables(pos_s_rows, ls, bs)
            rb = np.arange(ns) % bs
            ts["dm"] = np.where((rb[:, None] == rb[None, :])[None], ts["dec"], np.float32(0.0))
            hs =_matmul(xs, w_in, ns)
            hist = jnp.swapaxes(state_pool[e], 0, 1)
            mix_s, hist_new, s_new = _even_mix_sample(hs, hist, state_ret[e], ts, pw, pool_scale[e],
                                                      ret_gn_g[e], bs, ls, past_len)
            outs["pool_s"].append(jnp.swapaxes(hist_new, 0, 1))
            outs["ret_s"].append(s_new)
            xs = _matmul_res_ln(mix_s, w_o, xs, ln_mix_g[layer], ln_mix_b[layer], ns)
        else:
            o = layer // 2
            hw = 2 * LANES
            w = {
                "dq": w_dq[o].astype(BF16),
                "qg": q_norm_g[o].reshape(1, -1),
                "uq": _pad_heads(w_uq[o], QK_NOPE, QK_ROPE).astype(BF16),
                "dkv": jnp.pad(w_dkv[o], ((0, 0), (0, KV_LORA + LANES - w_dkv.shape[2]))).astype(BF16),
                "kvg": kv_norm_g[o].reshape(1, -1),
                "uk": w_uk[o].reshape(KV_LORA, MLA_HEADS * QK_NOPE).astype(BF16),
                "uv": w_uv[o].reshape(KV_LORA, MLA_HEADS * V_DIM).astype(BF16),
                "ukt": jnp.transpose(w_uk[o], (1, 2, 0)).astype(BF16),
            }
            w_o = w_o_mla[o].astype(BF16)
            qp, kp, vp, ckv_p, kpe_p = _mla_proj(xp, _pe_tables(pos_p), w, tl["tok"], lp // tl["tok"], False)
            att = _flash_prompt(qp, kp, vp, bp, lp, tl["attn"], tl["attn_k"])
            outs["ckv_p"].append(ckv_p.reshape(bp, lp, KV_LORA))
            outs["kpe_p"].append(kpe_p.reshape(bp, lp, QK_ROPE))
            mixer_out = (att, w_o, ln_mix_g[layer], ln_mix_b[layer])
            ql, qpe, ckv_s, kpe_s = _mla_proj(xs, _pe_tables(pos_s_rows), w, ns, 1, True)
            rows = ls * MLA_HEADS

            def per_batch(a, width):
                return jnp.transpose(a.reshape(ls, bs, MLA_HEADS, width), (1, 0, 2, 3)).reshape(bs, rows, width)

            ql_b = per_batch(ql, KV_LORA)
            qp_b = per_batch(qpe, LANES)[:, :, 0:QK_ROPE]
            ckv_sb = jnp.swapaxes(ckv_s.reshape(ls, bs, KV_LORA), 0, 1)
            kpe_sb = jnp.swapaxes(kpe_s.reshape(ls, bs, QK_ROPE), 0, 1)
            cn = jnp.pad(ckv_sb, ((0, 0), (0, PAGE_SIZE - ls), (0, 0)))
            knt = jnp.swapaxes(jnp.pad(kpe_sb, ((0, 0), (0, PAGE_SIZE - ls), (0, 0))), 1, 2)
            o_lat = _decode_attention(page_table, ql_b, qp_b, cn, knt, cache_ckv,
                                      jnp.swapaxes(cache_kpe, 2, 3), o, tl["pages"])
            outs["ckv_s"].append(ckv_sb)
            outs["kpe_s"].append(kpe_sb)
            ol = jnp.transpose(o_lat.reshape(bs, ls, MLA_HEADS, KV_LORA), (2, 1, 0, 3)).reshape(MLA_HEADS, ns, KV_LORA)
            wuv3 = jnp.transpose(w_uv[o], (1, 0, 2)).astype(BF16)
            xs = _decode_out(ol, wuv3, w_o, xs, ln_mix_g[layer], ln_mix_b[layer])
        xp, st_p = _ffn_prompt(xp, bp, lp, layer, wup_all, wd_all, conv_w[layer], conv_b[layer],
                               ln_ffn_g[layer], ln_ffn_b[layer], tl["ffn"], tl["ff_chunk"], mixer_out)
        outs["conv_p"].append(st_p[:, SUBLANES - (CONV_W - 1):, :])
        st_s = jnp.swapaxes(state_conv[layer], 0, 1)
        xs, st_s_new = _ffn_sample(xs, st_s, layer, wup_all, wd_all, conv_w[layer], conv_b[layer],
                                   ln_ffn_g[layer], ln_ffn_b[layer], bs, ls, tl["ff_chunk"])
        outs["conv_s"].append(jnp.swapaxes(st_s_new, 0, 1))

    y_p = xp.reshape(bp, lp, d)
    y_s = jnp.swapaxes(xs.reshape(ls, bs, d), 0, 1)
    return (y_p, y_s,
            jnp.stack(outs["pool_p"]), jnp.stack(outs["pool_s"]),
            jnp.stack(outs["ret_p"]), jnp.stack(outs["ret_s"]),
            jnp.stack(outs["ckv_p"]), jnp.stack(outs["ckv_s"]),
            jnp.stack(outs["kpe_p"]), jnp.stack(outs["kpe_s"]),
            jnp.stack(outs["conv_p"]), jnp.stack(outs["conv_s"]))
```

```python
import functools

import jax
import jax.numpy as jnp
import numpy as np
from jax import lax
from jax.experimental import pallas as pl
from jax.experimental.pallas import tpu as pltpu

F32 = jnp.float32
BF16 = jnp.bfloat16

PAGE_SIZE = 128
POOL_WINDOWS = (2, 4, 8, 16)
POOL_HIST = max(POOL_WINDOWS) - 1
RET_HEADS = 4
RET_DK = 128
MLA_HEADS = 8
QK_NOPE = 128
QK_ROPE = 64
V_DIM = 128
KV_LORA = 256
CONV_W = 3
DEPTH = 2
ALPHA = (2.0 * DEPTH) ** 0.25
ROPE_THETA = 10000.0
LN_EPS = 1e-5
RMS_EPS = 1e-6
GN_EPS = 1e-6
MLA_SCALE = (QK_NOPE + QK_ROPE) ** -0.5
LOG2E = 1.4426950408889634
SCALE_LOG2E = MLA_SCALE * LOG2E

LANES = 128
SUBLANES = 8
MIB = 1024 * 1024


_VMEM_LIMIT_MIB = {
    "matmul": 40, "matmul_res_ln": 32, "even_layer_prompt": 40, "even_mix_sample": 48, "ffn_prompt": 56,
    "ffn_sample": 40, "mla_proj": 48, "mla_proj_decode": 40, "flash_prompt": 56, "decode_attention": 48,
    "decode_out": 24,
}


def _cparams(sem, call):
    return pltpu.CompilerParams(dimension_semantics=sem, vmem_limit_bytes=_VMEM_LIMIT_MIB[call] * MIB)


def _resident(shape):
    nd = len(shape)
    return pl.BlockSpec(shape, lambda *_: (0,) * nd, pipeline_mode=pl.Buffered(1))


def _resident_layer(stacked_shape, layer):
    nd = len(stacked_shape) - 1
    return pl.BlockSpec((None,) + tuple(stacked_shape[1:]), lambda *_: (layer,) + (0,) * nd,
                        pipeline_mode=pl.Buffered(1))


def _whole(shape):
    nd = len(shape)
    return pl.BlockSpec(shape, lambda *_: (0,) * nd)


def _dot(a, b):
    return jnp.dot(a, b, preferred_element_type=F32)


def _dot_nt(a, b):
    return lax.dot_general(a, b, (((1,), (1,)), ((), ())), preferred_element_type=F32)


def _dot_tn(a, b):
    return lax.dot_general(a, b, (((0,), (0,)), ((), ())), preferred_element_type=F32)


def _layer_norm(z, g, b):
    mu = jnp.mean(z, axis=-1, keepdims=True)
    d = z - mu
    var = jnp.mean(d * d, axis=-1, keepdims=True)
    return d * lax.rsqrt(var + LN_EPS) * g + b


def _silu(x):
    return x * jax.nn.sigmoid(x)


def _mm_body(x_ref, w_ref, o_ref):
    o_ref[...] = _dot(x_ref[...].astype(BF16), w_ref[...]).astype(o_ref.dtype)


def _matmul(x, w, tm, out_dtype=F32):
    m, k = x.shape
    n = w.shape[1]
    return pl.pallas_call(
        _mm_body,
        grid=(m // tm,),
        in_specs=[pl.BlockSpec((tm, k), lambda i: (i, 0)), _resident((k, n))],
        out_specs=pl.BlockSpec((tm, n), lambda i: (i, 0)),
        out_shape=jax.ShapeDtypeStruct((m, n), out_dtype),
        compiler_params=_cparams(("parallel",), "matmul"),
        name="matmul",
    )(x, w)


def _mm_ln_body(a_ref, w_ref, r_ref, g_ref, b_ref, o_ref):
    y = _dot(a_ref[...].astype(BF16), w_ref[...])
    o_ref[...] = _layer_norm(ALPHA * r_ref[...] + y, g_ref[...], b_ref[...])


def _matmul_res_ln(a, w, res, g, b, tm):
    m, k = a.shape
    n = w.shape[1]
    return pl.pallas_call(
        _mm_ln_body,
        grid=(m // tm,),
        in_specs=[pl.BlockSpec((tm, k), lambda i: (i, 0)), _resident((k, n)),
                  pl.BlockSpec((tm, n), lambda i: (i, 0)), _resident((1, n)), _resident((1, n))],
        out_specs=pl.BlockSpec((tm, n), lambda i: (i, 0)),
        out_shape=jax.ShapeDtypeStruct((m, n), F32),
        compiler_params=_cparams(("parallel",), "matmul_res_ln"),
        name="matmul_res_ln",
    )(a, w, res, g.reshape(1, n), b.reshape(1, n))


def _rope_full(x, c2, s2):
    return x * c2 + pltpu.roll(x, 64, axis=1) * s2


def _group_norm_gate(o, gate, gn_row):
    mu = jnp.mean(o, axis=-1, keepdims=True)
    d = o - mu
    var = jnp.mean(d * d, axis=-1, keepdims=True)
    return _silu(gate) * (d * lax.rsqrt(var + GN_EPS) * gn_row)


def _even_layer_prompt_body(x_ref, win_ref, c2_ref, s2_ref, dec_ref, qd_ref, kd_ref, gc_ref, pw_ref, ps_ref,
                            gn_ref, wo_ref, lg_ref, lb_ref, o_ref, pst_ref, rst_ref,
                            mix_ref, ext_ref, s_ref, *h_refs, tm, n_j, sub):
    j = pl.program_id(1)
    pd = len(POOL_WINDOWS) * LANES

    @pl.when(j == 0)
    def _():
        ext_ref[0:16, :] = jnp.zeros((16, pd), F32)
        s_ref[...] = jnp.zeros(s_ref.shape, F32)

    @pl.when(j > 0)
    def _():
        ext_ref[0:16, :] = ext_ref[tm:tm + 16, :]

    c = dec_ref.shape[1]
    k_scale = RET_DK ** -0.5
    nblk = 4 * LANES
    def project(r0):
        h_ref = h_refs[r0 // sub]
        xb = x_ref[r0:r0 + sub, :].astype(BF16)
        for c0 in range(0, h_ref.shape[1], nblk):
            h_ref[:, c0:c0 + nblk] = _dot(xb, win_ref[:, c0:c0 + nblk])

    project(0)
    for r0 in range(0, tm, sub):
        rb = slice(r0, r0 + sub)
        h_ref = h_refs[r0 // sub]
        if r0 + sub < tm:
            project(r0 + sub)
        ext_ref[16 + r0:16 + r0 + sub, :] = h_ref[:, 0:pd]

        pos = (j * tm + r0 + lax.broadcasted_iota(jnp.int32, (sub, 1), 0)).astype(F32)
        for g, w in enumerate(POOL_WINDOWS):
            cols = slice(g * LANES, (g + 1) * LANES)
            e = ext_ref[r0:r0 + sub + 16, cols]
            u = e[16:, :]
            s = 1
            while s < w:
                e = e + pltpu.roll(e, s, axis=0)
                s *= 2
            cnt = jnp.minimum(float(w), pos + 1.0)
            pooled = e[16:, :] / cnt - u
            mixed = _dot(pooled.astype(BF16), pw_ref[g]) * ps_ref[:, cols]
            mix_ref[rb, cols] = mixed.astype(BF16)

        for ci in range(r0 // c, (r0 + sub) // c):
            rows = slice(ci * c, (ci + 1) * c)
            lrows = slice(ci * c - r0, (ci + 1) * c - r0)
            c2 = c2_ref[rows, :]
            s2 = s2_ref[rows, :]
            for hd in range(RET_HEADS):
                def col(part, hd=hd):
                    return slice(pd + (part * RET_HEADS + hd) * LANES, pd + (part * RET_HEADS + hd + 1) * LANES)
                q = _rope_full(h_ref[lrows, col(0)], c2, s2)
                k = _rope_full(h_ref[lrows, col(1)], c2, s2) * k_scale
                vb = h_ref[lrows, col(2)].astype(BF16)
                gate = h_ref[lrows, col(3)]
                st = s_ref[hd]
                sc = _dot_nt(q.astype(BF16), k.astype(BF16)) * dec_ref[hd]
                o = _dot(sc.astype(BF16), vb)
                o = o + _dot((q * qd_ref[hd]).astype(BF16), st.astype(BF16))
                s_ref[hd] = gc_ref[hd] * st + _dot_tn((k * kd_ref[hd]).astype(BF16), vb)
                ret = _group_norm_gate(o, gate, gn_ref[:, hd * LANES:(hd + 1) * LANES])
                mix_ref[rows, pd + hd * LANES:pd + (hd + 1) * LANES] = ret.astype(BF16)

        y = _dot(mix_ref[rb, :], wo_ref[...])
        o_ref[rb, :] = _layer_norm(ALPHA * x_ref[rb, :] + y, lg_ref[...], lb_ref[...])

    @pl.when(j == n_j - 1)
    def _():
        pst_ref[0] = ext_ref[pl.ds(tm + 1, POOL_HIST), :]
        rst_ref[0] = s_ref[...]


def _even_layer_prompt(x, bsz, seq, w_in, tabs, pool_w, pool_scale, gn_g, w_o, ln_g, ln_b, tm):
    n_j = seq // tm
    d = x.shape[1]
    pd = pool_scale.shape[0]
    ed = pd + RET_HEADS * LANES
    chunk = tabs["dec"].shape[1]
    sub = min(tm, max(chunk, 2 * LANES))
    body = functools.partial(_even_layer_prompt_body, tm=tm, n_j=n_j, sub=sub)
    return pl.pallas_call(
        body,
        grid=(bsz, n_j),
        in_specs=[
            pl.BlockSpec((tm, d), lambda b, j: (b * n_j + j, 0)),
            _resident(w_in.shape),
            pl.BlockSpec((tm, LANES), lambda b, j: (j, 0)),
            pl.BlockSpec((tm, LANES), lambda b, j: (j, 0)),
            _resident((RET_HEADS, chunk, chunk)),
            _resident((RET_HEADS, chunk, LANES)),
            _resident((RET_HEADS, chunk, LANES)),
            _resident((RET_HEADS, 1, LANES)),
            _resident(pool_w.shape),
            _resident((1, pd)),
            _resident((1, RET_HEADS * LANES)),
            _resident(w_o.shape),
            _resident((1, d)),
            _resident((1, d)),
        ],
        out_specs=[
            pl.BlockSpec((tm, d), lambda b, j: (b * n_j + j, 0)),
            pl.BlockSpec((1, POOL_HIST, pd), lambda b, j: (b, 0, 0)),
            pl.BlockSpec((1, RET_HEADS, RET_DK, LANES), lambda b, j: (b, 0, 0, 0)),
        ],
        out_shape=[
            jax.ShapeDtypeStruct((bsz * seq, d), F32),
            jax.ShapeDtypeStruct((bsz, POOL_HIST, pd), F32),
            jax.ShapeDtypeStruct((bsz, RET_HEADS, RET_DK, LANES), F32),
        ],
        scratch_shapes=[pltpu.VMEM((tm, ed), BF16), pltpu.VMEM((tm + 16, pd), F32),
                        pltpu.VMEM((RET_HEADS, RET_DK, LANES), F32)]
        + [pltpu.VMEM((sub, w_in.shape[1]), F32)] * (tm // sub),
        compiler_params=_cparams(("parallel", "arbitrary"), "even_layer_prompt"),
        name="even_layer_prompt",
    )(x, w_in, tabs["c2"], tabs["s2"], tabs["dec"], tabs["qd"], tabs["kd"], tabs["gc"],
      pool_w, pool_scale.reshape(1, pd), gn_g.reshape(1, -1), w_o, ln_g.reshape(1, d), ln_b.reshape(1, d))


def _even_mix_sample_body(h_ref, hist_ref, s0_ref, c2_ref, s2_ref, dm_ref, qd_ref, kd_ref, gc_ref,
                          pw_ref, ps_ref, gn_ref, mix_ref, hist_o_ref, s_o_ref, oc_ref,
                          *, nb, ls, cnts):
    pd = len(POOL_WINDOWS) * LANES
    ext = [hist_ref[i] for i in range(POOL_HIST)]
    ext += [h_ref[l * nb:(l + 1) * nb, 0:pd] for l in range(ls)]
    for i in range(POOL_HIST):
        hist_o_ref[i] = ext[ls + i]
    for g, w in enumerate(POOL_WINDOWS):
        cols = slice(g * LANES, (g + 1) * LANES)
        outs = []
        for l in range(ls):
            top = POOL_HIST + l
            acc = ext[top][:, cols]
            for jj in range(1, w):
                acc = acc + ext[top - jj][:, cols]
            outs.append(acc / cnts[g][l] - ext[top][:, cols])
        pooled = jnp.concatenate(outs, axis=0)
        mixed = _dot(pooled.astype(BF16), pw_ref[g]) * ps_ref[:, cols]
        mix_ref[:, cols] = mixed.astype(BF16)

    rows_b = lax.broadcasted_iota(jnp.int32, (ls * nb, 1), 0) % nb
    k_scale = RET_DK ** -0.5
    c2 = c2_ref[...]
    s2 = s2_ref[...]
    qs, ks, vs = [], [], []
    for hd in range(RET_HEADS):
        def col(part, hd=hd):
            return slice(pd + (part * RET_HEADS + hd) * LANES, pd + (part * RET_HEADS + hd + 1) * LANES)
        q = _rope_full(h_ref[:, col(0)], c2, s2)
        k = _rope_full(h_ref[:, col(1)], c2, s2) * k_scale
        vb = h_ref[:, col(2)].astype(BF16)
        sc = _dot_nt(q.astype(BF16), k.astype(BF16)) * dm_ref[hd]
        oc_ref[hd] = _dot(sc.astype(BF16), vb)
        qs.append(q * qd_ref[hd])
        ks.append((k * kd_ref[hd]).T)
        vs.append(vb)

    cols_b = lax.broadcasted_iota(jnp.int32, (1, ls * nb), 1) % nb

    def per_batch(b, carry):
        sel = rows_b == b
        sel_t = cols_b == b
        for hd in range(RET_HEADS):
            st = s0_ref[b, hd]
            qm = jnp.where(sel, qs[hd], 0.0).astype(BF16)
            km_t = jnp.where(sel_t, ks[hd], 0.0).astype(BF16)
            oc_ref[hd] += _dot(qm, st.astype(BF16))
            s_o_ref[b, hd] = gc_ref[hd] * st + _dot(km_t, vs[hd])
        return carry

    lax.fori_loop(0, nb, per_batch, 0)

    for hd in range(RET_HEADS):
        gate = h_ref[:, pd + (3 * RET_HEADS + hd) * LANES:pd + (3 * RET_HEADS + hd + 1) * LANES]
        ret = _group_norm_gate(oc_ref[hd], gate, gn_ref[:, hd * LANES:(hd + 1) * LANES])
        mix_ref[:, pd + hd * LANES:pd + (hd + 1) * LANES] = ret.astype(BF16)


def _even_mix_sample(h, hist, s0, tabs, pool_w, pool_scale, gn_g, nb, ls, past_len):
    pd = pool_scale.shape[0]
    ed = pd + RET_HEADS * LANES
    cnts = tuple(tuple(float(min(w, past_len + l + 1)) for l in range(ls)) for w in POOL_WINDOWS)
    body = functools.partial(_even_mix_sample_body, nb=nb, ls=ls, cnts=cnts)
    n = ls * nb
    args = (h, hist, s0, tabs["c2"], tabs["s2"], tabs["dm"], tabs["qd"], tabs["kd"], tabs["gc"],
            pool_w, pool_scale.reshape(1, pd), gn_g.reshape(1, -1))
    return pl.pallas_call(
        body,
        grid=(1,),
        in_specs=[_resident(a.shape) for a in args],
        out_specs=[_whole((n, ed)), _whole(hist.shape), _whole(s0.shape)],
        out_shape=[jax.ShapeDtypeStruct((n, ed), BF16),
                   jax.ShapeDtypeStruct(hist.shape, F32),
                   jax.ShapeDtypeStruct(s0.shape, F32)],
        scratch_shapes=[pltpu.VMEM((RET_HEADS, n, LANES), F32)],
        compiler_params=_cparams(("arbitrary",), "even_mix_sample"),
        name="even_mix_sample",
    )(*args)


def _ffn_chunk(xb, wup_ref, wd_ref, cw_ref, cb_ref, c, tf, dff, shift_fn):
    cols = slice(c * tf, (c + 1) * tf)
    a = _dot(xb, wup_ref[:, cols])
    gate_in = _dot(xb, wup_ref[:, dff + c * tf:dff + (c + 1) * tf])
    a1, a2 = shift_fn(a, c)
    conv = cb_ref[:, cols] + cw_ref[0:1, cols] * a2
    conv = conv + cw_ref[1:2, cols] * a1
    conv = conv + cw_ref[2:3, cols] * a
    act = (_silu(conv) * gate_in).astype(BF16)
    return a, _dot(act, wd_ref[cols, :])


def _ffn_prompt_body(x_ref, *rest, tm, n_j, tf, dff, mixer_out):
    if mixer_out:
        a_ref, wo_ref, mg_ref, mb_ref = rest[:4]
        rest = rest[4:]
    wup_ref, wd_ref, cw_ref, cb_ref, g_ref, b_ref, o_ref, st_ref, carry_ref = rest
    j = pl.program_id(1)

    @pl.when(j == 0)
    def _():
        carry_ref[...] = jnp.zeros(carry_ref.shape, F32)

    x = x_ref[...]
    if mixer_out:
        x = _layer_norm(ALPHA * x + _dot(a_ref[...], wo_ref[...]), mg_ref[...], mb_ref[...])
    row = lax.broadcasted_iota(jnp.int32, (tm, tf), 0)

    def shift_fn(a, c):
        prev = carry_ref[:, c * tf:(c + 1) * tf]
        a1 = jnp.where(row == 0, prev[7:8, :], pltpu.roll(a, 1, axis=0))
        a2 = jnp.where(row == 0, prev[6:7, :], jnp.where(row == 1, prev[7:8, :], pltpu.roll(a, 2, axis=0)))
        return a1, a2

    xb = x.astype(BF16)
    acc = None
    for c in range(dff // tf):
        a, y = _ffn_chunk(xb, wup_ref, wd_ref, cw_ref, cb_ref, c, tf, dff, shift_fn)
        acc = y if acc is None else acc + y
        tail = a[tm - SUBLANES:tm, :]
        carry_ref[:, c * tf:(c + 1) * tf] = tail
        st_ref[0, :, c * tf:(c + 1) * tf] = tail
    o_ref[...] = _layer_norm(ALPHA * x + acc, g_ref[...], b_ref[...])


def _ffn_prompt(x, bsz, seq, layer, wup, wd, conv_w, conv_b, g, b, tm, tf, mixer_out=None):
    d = x.shape[1]
    dff = wd.shape[1]
    n_j = seq // tm
    body = functools.partial(_ffn_prompt_body, tm=tm, n_j=n_j, tf=tf, dff=dff, mixer_out=mixer_out is not None)
    row_tile = lambda bi, j: (bi * n_j + j, 0)
    pre_args, pre_specs = [], []
    if mixer_out is not None:
        a, w_o, mg, mb = mixer_out
        pre_args = [a, w_o, mg.reshape(1, d), mb.reshape(1, d)]
        pre_specs = [pl.BlockSpec((tm, a.shape[1]), row_tile), _resident(w_o.shape), _resident((1, d)), _resident((1, d))]
    return pl.pallas_call(
        body,
        grid=(bsz, n_j),
        in_specs=[pl.BlockSpec((tm, d), row_tile)] + pre_specs + [
                  _resident_layer(wup.shape, layer), _resident_layer(wd.shape, layer), _resident(conv_w.shape),
                  _resident((1, dff)), _resident((1, d)), _resident((1, d))],
        out_specs=[pl.BlockSpec((tm, d), row_tile),
                   pl.BlockSpec((1, SUBLANES, dff), lambda bi, j: (bi, 0, 0))],
        out_shape=[jax.ShapeDtypeStruct(x.shape, F32),
                   jax.ShapeDtypeStruct((bsz, SUBLANES, dff), F32)],
        scratch_shapes=[pltpu.VMEM((SUBLANES, dff), F32)],
        compiler_params=_cparams(("parallel", "arbitrary"), "ffn_prompt"),
        name="ffn_prompt",
    )(x, *pre_args, wup, wd, conv_w, conv_b.reshape(1, dff), g.reshape(1, d), b.reshape(1, d))


def _ffn_sample_body(x_ref, st_ref, wup_ref, wd_ref, cw_ref, cb_ref, g_ref, b_ref, o_ref, st_o_ref,
                     *, nb, ls, tf, dff):
    x = x_ref[...]
    xb = x.astype(BF16)
    nh = CONV_W - 1

    def shift_fn(a, c):
        cols = slice(c * tf, (c + 1) * tf)
        ext = [st_ref[i, :, cols] for i in range(nh)] + [a[l * nb:(l + 1) * nb, :] for l in range(ls)]
        a1 = jnp.concatenate([ext[nh + l - 1] for l in range(ls)], axis=0)
        a2 = jnp.concatenate([ext[nh + l - 2] for l in range(ls)], axis=0)
        for i in range(nh):
            st_o_ref[i, :, cols] = ext[ls + i]
        return a1, a2

    acc = None
    for c in range(dff // tf):
        _, y = _ffn_chunk(xb, wup_ref, wd_ref, cw_ref, cb_ref, c, tf, dff, shift_fn)
        acc = y if acc is None else acc + y
    o_ref[...] = _layer_norm(ALPHA * x + acc, g_ref[...], b_ref[...])


def _ffn_sample(x, st, layer, wup, wd, conv_w, conv_b, g, b, nb, ls, tf):
    d = x.shape[1]
    dff = wd.shape[1]
    body = functools.partial(_ffn_sample_body, nb=nb, ls=ls, tf=tf, dff=dff)
    args = (x, st, wup, wd, conv_w, conv_b.reshape(1, dff), g.reshape(1, d), b.reshape(1, d))
    in_specs = [_resident(a.shape) for a in args]
    in_specs[2] = _resident_layer(wup.shape, layer)
    in_specs[3] = _resident_layer(wd.shape, layer)
    return pl.pallas_call(
        body,
        grid=(1,),
        in_specs=in_specs,
        out_specs=[_whole(x.shape), _whole(st.shape)],
        out_shape=[jax.ShapeDtypeStruct(x.shape, F32), jax.ShapeDtypeStruct(st.shape, F32)],
        compiler_params=_cparams(("arbitrary",), "ffn_sample"),
        name="ffn_sample",
    )(*args)


def _rope_pe(blk, cc, s1, s2):
    return blk * cc + pltpu.roll(blk, 96, axis=1) * s1 + pltpu.roll(blk, 32, axis=1) * s2


def _rms_norm(x, g):
    ms = jnp.mean(x * x, axis=-1, keepdims=True)
    return x * lax.rsqrt(ms + RMS_EPS) * g


def _mla_proj_body(x_ref, cc_ref, s1_ref, s2_ref, wdq_ref, qg_ref, wuq_ref, wdkv_ref, kvg_ref, *rest,
                   decode, sub):
    hw = 2 * LANES
    for r0 in range(0, x_ref.shape[0], sub):
        rb = slice(r0, r0 + sub)
        cc = cc_ref[rb, :]
        s1 = s1_ref[rb, :]
        s2 = s2_ref[rb, :]
        xb = x_ref[rb, :].astype(BF16)
        cq = _rms_norm(_dot(xb, wdq_ref[...]), qg_ref[...])
        q = _dot(cq.astype(BF16), wuq_ref[...])
        kv = _dot(xb, wdkv_ref[...])
        ckv = _rms_norm(kv[:, 0:KV_LORA], kvg_ref[...])
        kpe = _rope_pe(kv[:, KV_LORA:KV_LORA + LANES], cc, s1, s2)
        if decode:
            wukt_ref, ql_ref, qp_ref, ckv_ref, kpe_ref = rest
            for h in range(MLA_HEADS):
                qn = q[:, h * hw:h * hw + LANES].astype(BF16)
                ql_ref[rb, h * KV_LORA:(h + 1) * KV_LORA] = _dot(qn, wukt_ref[h]).astype(BF16)
                qp = _rope_pe(q[:, h * hw + LANES:(h + 1) * hw], cc, s1, s2)
                qp_ref[rb, h * LANES:(h + 1) * LANES] = qp.astype(BF16)
        else:
            wuk_ref, wuv_ref, qo_ref, ko_ref, vo_ref, ckv_ref, kpe_ref = rest
            cb = ckv.astype(BF16)
            kn = _dot(cb, wuk_ref[...])
            vo_ref[rb, :] = _dot(cb, wuv_ref[...]).astype(BF16)
            kpb = kpe.astype(BF16)
            for h in range(MLA_HEADS):
                qo_ref[rb, h * hw:h * hw + LANES] = q[:, h * hw:h * hw + LANES].astype(BF16)
                qp = _rope_pe(q[:, h * hw + LANES:(h + 1) * hw], cc, s1, s2)
                qo_ref[rb, h * hw + LANES:(h + 1) * hw] = qp.astype(BF16)
                ko_ref[rb, h * hw:h * hw + LANES] = kn[:, h * LANES:(h + 1) * LANES].astype(BF16)
                ko_ref[rb, h * hw + LANES:(h + 1) * hw] = kpb
        ckv_ref[rb, :] = ckv
        kpe_ref[rb, :] = kpe[:, 0:QK_ROPE]


def _mla_proj(x, tabs, w, tm, n_pos_blocks, decode):
    t, d = x.shape
    hw = 2 * LANES
    body = functools.partial(_mla_proj_body, decode=decode, sub=tm)
    row = lambda i: (i, 0)
    tab = lambda i: (i % n_pos_blocks, 0)
    ins = [x, tabs["cc"], tabs["s1"], tabs["s2"], w["dq"], w["qg"], w["uq"], w["dkv"], w["kvg"]]
    in_specs = [pl.BlockSpec((tm, d), row)] + [pl.BlockSpec((tm, LANES), tab)] * 3
    in_specs += [_resident(a.shape) for a in ins[4:]]
    if decode:
        ins += [w["ukt"]]
        in_specs += [_resident(w["ukt"].shape)]
        outs = [(MLA_HEADS * KV_LORA, BF16), (MLA_HEADS * LANES, BF16)]
    else:
        ins += [w["uk"], w["uv"]]
        in_specs += [_resident(w["uk"].shape), _resident(w["uv"].shape)]
        outs = [(MLA_HEADS * hw, BF16), (MLA_HEADS * hw, BF16), (MLA_HEADS * V_DIM, BF16)]
    outs += [(KV_LORA, F32), (QK_ROPE, F32)]
    return pl.pallas_call(
        body,
        grid=(t // tm,),
        in_specs=in_specs,
        out_specs=[pl.BlockSpec((tm, n), row) for n, _ in outs],
        out_shape=[jax.ShapeDtypeStruct((t, n), dt) for n, dt in outs],
        compiler_params=_cparams(("parallel",), "mla_proj_decode" if decode else "mla_proj"),
        name="mla_proj_decode" if decode else "mla_proj",
    )(*ins)


def _online_softmax_update(s2, m_ref, l_ref, acc_ref, pv_fn, row_chunk):
    rows, width = s2.shape
    n = width // LANES
    aw = acc_ref.shape[-1] // LANES
    p_chunks, alphas = [], []
    for r0 in range(0, rows, row_chunk):
        rs_ = slice(r0, r0 + row_chunk)
        tiles = [s2[rs_, j * LANES:(j + 1) * LANES] for j in range(n)]
        mx = tiles[0]
        for t in tiles[1:]:
            mx = jnp.maximum(mx, t)
        m_prev = m_ref[rs_, :]
        m_new = jnp.maximum(m_prev, jnp.max(mx, axis=-1, keepdims=True))
        alpha = jnp.exp2(m_prev - m_new)
        ps = [jnp.exp2(t - m_new) for t in tiles]
        if l_ref is not None:
            tot = ps[0]
            for t in ps[1:]:
                tot = tot + t
            l_ref[rs_, :] = alpha * l_ref[rs_, :] + jnp.sum(tot, axis=-1, keepdims=True)
        m_ref[rs_, :] = m_new
        p_chunks.append((jnp.concatenate(ps, axis=1) if n > 1 else ps[0]).astype(BF16))
        alphas.append(alpha)
    p = jnp.concatenate(p_chunks, axis=0) if len(p_chunks) > 1 else p_chunks[0]
    pv = pv_fn(p)
    for i, r0 in enumerate(range(0, rows, row_chunk)):
        rs_ = slice(r0, r0 + row_chunk)
        a_w = alphas[i] if aw == 1 else jnp.concatenate([alphas[i]] * aw, axis=1)
        acc_ref[rs_, :] = a_w * acc_ref[rs_, :] + pv[rs_, :]


def _flash_body(q_ref, k_ref, v_ref, o_ref, m_ref, acc_ref, *, tq, tk, row_chunk):
    qi = pl.program_id(1)
    hw = 2 * LANES
    m_ref[...] = jnp.full(m_ref.shape, -jnp.inf, F32)
    acc_ref[...] = jnp.zeros(acc_ref.shape, F32)
    ones = jnp.ones((tk, LANES), BF16)

    def step(h, key0, diag_off):
        start = pl.multiple_of(key0, tk)
        q = q_ref[:, h * hw:(h + 1) * hw]
        k = k_ref[pl.ds(start, tk), h * hw:(h + 1) * hw]
        v1 = jnp.concatenate([v_ref[pl.ds(start, tk), h * V_DIM:(h + 1) * V_DIM], ones], axis=1)
        s = _dot_nt(q, k) * SCALE_LOG2E
        if diag_off is not None:
            row = lax.broadcasted_iota(jnp.int32, s.shape, 0)
            colm = lax.broadcasted_iota(jnp.int32, s.shape, 1)
            s = jnp.where(colm + diag_off <= row, s, -jnp.inf)
        _online_softmax_update(s, m_ref.at[h], None, acc_ref.at[h], lambda p: _dot(p, v1), row_chunk)

    def loop_body(kj, carry):
        for h in range(MLA_HEADS):
            step(h, kj * tk, None)
        return carry

    lax.fori_loop(0, qi * (tq // tk), loop_body, 0)
    for h in range(MLA_HEADS):
        for j in range(tq // tk):
            step(h, qi * tq + j * tk, j * tk)
        o_ref[:, h * V_DIM:(h + 1) * V_DIM] = (acc_ref[h, :, 0:V_DIM] / acc_ref[h, :, V_DIM:]).astype(BF16)


def _flash_prompt(qp, kp, vp, bsz, seq, tq, tk):
    nq = seq // tq
    body = functools.partial(_flash_body, tq=tq, tk=tk, row_chunk=min(tq, 64))
    return pl.pallas_call(
        body,
        grid=(bsz, nq),
        in_specs=[pl.BlockSpec((tq, qp.shape[1]), lambda b, i: (b * nq + i, 0)),
                  pl.BlockSpec((seq, kp.shape[1]), lambda b, i: (b, 0)),
                  pl.BlockSpec((seq, vp.shape[1]), lambda b, i: (b, 0))],
        out_specs=pl.BlockSpec((tq, vp.shape[1]), lambda b, i: (b * nq + i, 0)),
        out_shape=jax.ShapeDtypeStruct(vp.shape, BF16),
        scratch_shapes=[pltpu.VMEM((MLA_HEADS, tq, LANES), F32), pltpu.VMEM((MLA_HEADS, tq, V_DIM + LANES), F32)],
        compiler_params=_cparams(("parallel", "arbitrary"), "flash_prompt"),
        name="flash_prompt",
    )(qp, kp, vp)


def _decode_body(pt_ref, ql_ref, qp_ref, cn_ref, kn_ref, ckv_hbm, kpe_hbm, o_ref,
                 cbuf_ref, rbuf_ref, sem_ref, m_ref, l_ref, acc_ref, *, gp, ls, layer, n_split):
    b = pl.program_id(0)
    g = pl.program_id(1)
    n_b = pl.num_programs(0)
    n_g = pl.num_programs(1)
    slot = lax.rem(b * n_g + g, 2)
    other = 1 - slot

    def page_copies(bb, grp, slot):
        cps = []
        for i in range(gp):
            pg = pt_ref[bb, grp * gp + i]
            keys = pl.ds(i * PAGE_SIZE, PAGE_SIZE)
            cps.append(pltpu.make_async_copy(ckv_hbm.at[layer, pg], cbuf_ref.at[slot, keys, :], sem_ref.at[slot, 0]))
            cps.append(pltpu.make_async_copy(kpe_hbm.at[layer, pg], rbuf_ref.at[slot, :, keys], sem_ref.at[slot, 1]))
        return cps

    def start(bb, grp, slot):
        for n, cp in enumerate(page_copies(bb, grp, slot)):
            cp.start(priority=(n // 2) % 2)

    @pl.when((b == 0) & (g == 0))
    def _():
        start(b, g, slot)

    @pl.when(g < n_g - 1)
    def _():
        start(b, g + 1, other)

    @pl.when((g == n_g - 1) & (b < n_b - 1))
    def _():
        start(b + 1, 0, other)

    @pl.when(g == 0)
    def _():
        m_ref[...] = jnp.full(m_ref.shape, -jnp.inf, F32)
        l_ref[...] = jnp.zeros(l_ref.shape, F32)
        acc_ref[...] = jnp.zeros(acc_ref.shape, F32)

    ql = ql_ref[0]
    qp = qp_ref[0]

    for cp in page_copies(b, g, slot):
        cp.wait()

    kc_len = gp * PAGE_SIZE // n_split
    kbs, scores = [], []
    for c in range(n_split):
        ks = slice(c * kc_len, (c + 1) * kc_len)
        kb = cbuf_ref[slot, ks, :].astype(BF16)
        scores.append((_dot_nt(ql, kb) + _dot(qp, rbuf_ref[slot, :, ks].astype(BF16))) * SCALE_LOG2E)
        kbs.append(kb)
    state = (m_ref.at[0], l_ref.at[0], acc_ref.at[0])
    n_rows = ql.shape[0]
    for c in range(n_split):
        _online_softmax_update(scores[c], *state, lambda p, kb=kbs[c]: _dot(p, kb), n_rows)

    @pl.when(g == n_g - 1)
    def _():
        kc = cn_ref[0].astype(BF16)
        s = (_dot_nt(ql, kc) + _dot(qp, kn_ref[0].astype(BF16))) * SCALE_LOG2E
        r = lax.broadcasted_iota(jnp.int32, s.shape, 0) // MLA_HEADS
        cidx = lax.broadcasted_iota(jnp.int32, s.shape, 1)
        s = jnp.where((cidx <= r) & (cidx < ls), s, -jnp.inf)
        _online_softmax_update(s, *state, lambda p: _dot(p, kc), n_rows)
        l_w = jnp.concatenate([l_ref[0]] * (KV_LORA // LANES), axis=1)
        o_ref[0] = acc_ref[0] / l_w


def _decode_attention(page_table, ql, qp, cn, knt, cache_ckv, cache_kpet, layer, gp):
    nb, rows, _ = ql.shape
    n_pages = page_table.shape[1]
    ls = rows // MLA_HEADS
    n_split = 4 if gp % 4 == 0 else 1
    body = functools.partial(_decode_body, gp=gp, ls=ls, layer=layer, n_split=n_split)
    per_b = lambda b, g, pt: (b, 0, 0)
    in_specs = [pl.BlockSpec((1, rows, KV_LORA), per_b), pl.BlockSpec((1, rows, QK_ROPE), per_b),
                pl.BlockSpec((1, PAGE_SIZE, KV_LORA), per_b), pl.BlockSpec((1, QK_ROPE, PAGE_SIZE), per_b),
                pl.BlockSpec(memory_space=pl.ANY), pl.BlockSpec(memory_space=pl.ANY)]
    n_buf = 2
    grid_spec = pltpu.PrefetchScalarGridSpec(
        num_scalar_prefetch=1,
        grid=(nb, n_pages // gp),
        in_specs=in_specs,
        out_specs=pl.BlockSpec((1, rows, KV_LORA), per_b),
        scratch_shapes=[pltpu.VMEM((n_buf, gp * PAGE_SIZE, KV_LORA), F32),
                        pltpu.VMEM((n_buf, QK_ROPE, gp * PAGE_SIZE), F32),
                        pltpu.SemaphoreType.DMA((n_buf, 2)),
                        pltpu.VMEM((1, rows, LANES), F32), pltpu.VMEM((1, rows, LANES), F32),
                        pltpu.VMEM((1, rows, KV_LORA), F32)],
    )
    return pl.pallas_call(
        body,
        grid_spec=grid_spec,
        out_shape=jax.ShapeDtypeStruct((nb, rows, KV_LORA), F32),
        compiler_params=_cparams(("arbitrary", "arbitrary"), "decode_attention"),
        name="decode_attention",
    )(page_table, ql, qp, cn, knt, cache_ckv, cache_kpet)


def _decode_out_body(ol_ref, wuv_ref, wo_ref, r_ref, g_ref, b_ref, o_ref):
    y = None
    for h in range(MLA_HEADS):
        oh = _dot(ol_ref[h].astype(BF16), wuv_ref[h]).astype(BF16)
        t = _dot(oh, wo_ref[h * V_DIM:(h + 1) * V_DIM, :])
        y = t if y is None else y + t
    o_ref[...] = _layer_norm(ALPHA * r_ref[...] + y, g_ref[...], b_ref[...])


def _decode_out(ol, wuv3, wo, res, g, b):
    d = res.shape[1]
    args = (ol, wuv3, wo, res, g.reshape(1, d), b.reshape(1, d))
    return pl.pallas_call(
        _decode_out_body,
        grid=(1,),
        in_specs=[_resident(a.shape) for a in args],
        out_specs=_whole(res.shape),
        out_shape=jax.ShapeDtypeStruct(res.shape, F32),
        compiler_params=_cparams(("arbitrary",), "decode_out"),
        name="decode_out",
    )(*args)


def _rope_angles(pos, half):
    inv_freq = ROPE_THETA ** (-np.arange(half, dtype=np.float64) / half)
    ang = np.asarray(pos, np.float64)[:, None] * inv_freq[None, :]
    return np.cos(ang), np.sin(ang)


def _f32(tabs):
    return {k: np.ascontiguousarray(v, dtype=np.float32) for k, v in tabs.items()}


def _ret_tables(pos, chunk, rows_per_pos):
    cos, sin = _rope_angles(pos, RET_DK // 2)
    log_gamma = np.log(1.0 - 2.0 ** (-5.0 - np.arange(RET_HEADS, dtype=np.float64)))
    idx = np.repeat(np.arange(chunk, dtype=np.float64), rows_per_pos)
    diff = idx[:, None] - idx[None, :]
    dec = np.where(diff >= 0, np.exp(np.maximum(diff, 0.0)[None] * log_gamma[:, None, None]), 0.0)
    q_dec = np.exp((idx + 1.0)[None, :] * log_gamma[:, None])
    k_dec = np.exp((chunk - 1.0 - idx)[None, :] * log_gamma[:, None])
    gc = np.exp(chunk * log_gamma)
    n = idx.shape[0]
    return _f32({
        "c2": np.concatenate([cos, cos], axis=1),
        "s2": np.concatenate([-sin, sin], axis=1),
        "dec": dec,
        "qd": np.broadcast_to(q_dec[:, :, None], (RET_HEADS, n, LANES)),
        "kd": np.broadcast_to(k_dec[:, :, None], (RET_HEADS, n, LANES)),
        "gc": np.broadcast_to(gc[:, None, None], (RET_HEADS, 1, LANES)),
    })


def _pe_tables(pos):
    cos, sin = _rope_angles(pos, QK_ROPE // 2)
    z = np.zeros_like(cos)
    return _f32({"cc": np.concatenate([cos, cos, z, z], axis=1),
                 "s1": np.concatenate([-sin, z, z, z], axis=1),
                 "s2": np.concatenate([z, sin, z, z], axis=1)})


def _pad_heads(w, nope, rope):
    k = w.shape[0]
    w3 = w.reshape(k, MLA_HEADS, nope + rope)
    pad = jnp.zeros((k, MLA_HEADS, 2 * LANES - nope - rope), w.dtype)
    return jnp.concatenate([w3, pad], axis=2).reshape(k, MLA_HEADS * 2 * LANES)


def _tiles(seq, n_pages, dff):
    def fit(t, n=seq):
        while n % t:
            t //= 2
        return t
    return {"tok": fit(1024), "mix": fit(512), "ret_chunk": fit(256), "ffn": fit(512), "attn": fit(512), "attn_k": fit(512), "ff_chunk": dff,
            "pages": fit(64, n_pages)}


def kernel(x_prompt, x_sample, state_pool, state_ret, cache_ckv, cache_kpe, state_conv, page_table,
           w_in_even, pool_w, pool_scale, ret_gn_g, w_o_even,
           w_dq, q_norm_g, w_uq, w_dkv, kv_norm_g, w_uk, w_uv, w_o_mla,
           w_up, conv_w, conv_b, w_down, ln_mix_g, ln_mix_b, ln_ffn_g, ln_ffn_b):
    bp, lp, d = x_prompt.shape
    bs, ls, _ = x_sample.shape
    past_len = page_table.shape[1] * PAGE_SIZE
    depth = w_up.shape[0]
    dff = w_down.shape[1]
    tl = _tiles(lp, page_table.shape[1], dff)
    ns = bs * ls
    assert lp % tl["ret_chunk"] == 0 and tl["mix"] % tl["ret_chunk"] == 0
    assert dff % tl["ff_chunk"] == 0 and page_table.shape[1] % tl["pages"] == 0 and ls <= PAGE_SIZE

    xp = x_prompt.reshape(bp * lp, d)
    xs = jnp.swapaxes(x_sample, 0, 1).reshape(ns, d)
    pos_p = np.arange(lp)
    pos_s_rows = np.repeat(past_len + np.arange(ls), bs)

    wup_all = w_up.astype(BF16)
    wd_all = w_down.astype(BF16)
    outs = {k: [] for k in ("pool_p", "pool_s", "ret_p", "ret_s", "ckv_p", "ckv_s", "kpe_p", "kpe_s",
                            "conv_p", "conv_s")}
    for layer in range(depth):
        mixer_out = None
        if layer % 2 == 0:
            e = layer // 2
            w_in = w_in_even[e].astype(BF16)
            w_o = w_o_even[e].astype(BF16)
            pw = pool_w[e].astype(BF16)
            tp = _ret_tables(pos_p, tl["ret_chunk"], 1)
            xp, pst, rst = _even_layer_prompt(xp, bp, lp, w_in, tp, pw, pool_scale[e], ret_gn_g[e], w_o,
                                              ln_mix_g[layer], ln_mix_b[layer], tl["mix"])
            outs["pool_p"].append(pst)
            outs["ret_p"].append(rst)
            ts = _ret_tables(pos_s_rows, ls, bs)
            rb = np.arange(ns) % bs
            ts["dm"] = np.where((rb[:, None] == rb[None, :])[None], ts["dec"], np.float32(0.0))
            hs =_matmul(xs, w_in, ns)
            hist = jnp.swapaxes(state_pool[e], 0, 1)
            mix_s, hist_new, s_new = _even_mix_sample(hs, hist, state_ret[e], ts, pw, pool_scale[e],
                                                      ret_gn_g[e], bs, ls, past_len)
            outs["pool_s"].append(jnp.swapaxes(hist_new, 0, 1))
            outs["ret_s"].append(s_new)
            xs = _matmul_res_ln(mix_s, w_o, xs, ln_mix_g[layer], ln_mix_b[layer], ns)
        else:
            o = layer // 2
            hw = 2 * LANES
            w = {
                "dq": w_dq[o].astype(BF16),
                "qg": q_norm_g[o].reshape(1, -1),
                "uq": _pad_heads(w_uq[o], QK_NOPE, QK_ROPE).astype(BF16),
                "dkv": jnp.pad(w_dkv[o], ((0, 0), (0, KV_LORA + LANES - w_dkv.shape[2]))).astype(BF16),
                "kvg": kv_norm_g[o].reshape(1, -1),
                "uk": w_uk[o].reshape(KV_LORA, MLA_HEADS * QK_NOPE).astype(BF16),
                "uv": w_uv[o].reshape(KV_LORA, MLA_HEADS * V_DIM).astype(BF16),
                "ukt": jnp.transpose(w_uk[o], (1, 2, 0)).astype(BF16),
            }
            w_o = w_o_mla[o].astype(BF16)
            qp, kp, vp, ckv_p, kpe_p = _mla_proj(xp, _pe_tables(pos_p), w, tl["tok"], lp // tl["tok"], False)
            att = _flash_prompt(qp, kp, vp, bp, lp, tl["attn"], tl["attn_k"])
            outs["ckv_p"].append(ckv_p.reshape(bp, lp, KV_LORA))
            outs["kpe_p"].append(kpe_p.reshape(bp, lp, QK_ROPE))
            mixer_out = (att, w_o, ln_mix_g[layer], ln_mix_b[layer])
            ql, qpe, ckv_s, kpe_s = _mla_proj(xs, _pe_tables(pos_s_rows), w, ns, 1, True)
            rows = ls * MLA_HEADS

            def per_batch(a, width):
                return jnp.transpose(a.reshape(ls, bs, MLA_HEADS, width), (1, 0, 2, 3)).reshape(bs, rows, width)

            ql_b = per_batch(ql, KV_LORA)
            qp_b = per_batch(qpe, LANES)[:, :, 0:QK_ROPE]
            ckv_sb = jnp.swapaxes(ckv_s.reshape(ls, bs, KV_LORA), 0, 1)
            kpe_sb = jnp.swapaxes(kpe_s.reshape(ls, bs, QK_ROPE), 0, 1)
            cn = jnp.pad(ckv_sb, ((0, 0), (0, PAGE_SIZE - ls), (0, 0)))
            knt = jnp.swapaxes(jnp.pad(kpe_sb, ((0, 0), (0, PAGE_SIZE - ls), (0, 0))), 1, 2)
            o_lat = _decode_attention(page_table, ql_b, qp_b, cn, knt, cache_ckv,
                                      jnp.swapaxes(cache_kpe, 2, 3), o, tl["pages"])
            outs["ckv_s"].append(ckv_sb)
            outs["kpe_s"].append(kpe_sb)
            ol = jnp.transpose(o_lat.reshape(bs, ls, MLA_HEADS, KV_LORA), (2, 1, 0, 3)).reshape(MLA_HEADS, ns, KV_LORA)
            wuv3 = jnp.transpose(w_uv[o], (1, 0, 2)).astype(BF16)
            xs = _decode_out(ol, wuv3, w_o, xs, ln_mix_g[layer], ln_mix_b[layer])
        xp, st_p = _ffn_prompt(xp, bp, lp, layer, wup_all, wd_all, conv_w[layer], conv_b[layer],
                               ln_ffn_g[layer], ln_ffn_b[layer], tl["ffn"], tl["ff_chunk"], mixer_out)
        outs["conv_p"].append(st_p[:, SUBLANES - (CONV_W - 1):, :])
        st_s = jnp.swapaxes(state_conv[layer], 0, 1)
        xs, st_s_new = _ffn_sample(xs, st_s, layer, wup_all, wd_all, conv_w[layer], conv_b[layer],
                                   ln_ffn_g[layer], ln_ffn_b[layer], bs, ls, tl["ff_chunk"])
        outs["conv_s"].append(jnp.swapaxes(st_s_new, 0, 1))

    y_p = xp.reshape(bp, lp, d)
    y_s = jnp.swapaxes(xs.reshape(ls, bs, d), 0, 1)
    return (y_p, y_s,
            jnp.stack(outs["pool_p"]), jnp.stack(outs["pool_s"]),
            jnp.stack(outs["ret_p"]), jnp.stack(outs["ret_s"]),
            jnp.stack(outs["ckv_p"]), jnp.stack(outs["ckv_s"]),
            jnp.stack(outs["kpe_p"]), jnp.stack(outs["kpe_s"]),
            jnp.stack(outs["conv_p"]), jnp.stack(outs["conv_s"]))
```

```python
import functools

import jax
import jax.numpy as jnp
import numpy as np
from jax import lax
from jax.experimental import pallas as pl
from jax.experimental.pallas import tpu as pltpu

F32 = jnp.float32
BF16 = jnp.bfloat16

PAGE_SIZE = 128
POOL_WINDOWS = (2, 4, 8, 16)
POOL_HIST = max(POOL_WINDOWS) - 1
RET_HEADS = 4
RET_DK = 128
MLA_HEADS = 8
QK_NOPE = 128
QK_ROPE = 64
V_DIM = 128
KV_LORA = 256
CONV_W = 3
DEPTH = 2
ALPHA = (2.0 * DEPTH) ** 0.25
ROPE_THETA = 10000.0
LN_EPS = 1e-5
RMS_EPS = 1e-6
GN_EPS = 1e-6
MLA_SCALE = (QK_NOPE + QK_ROPE) ** -0.5
LOG2E = 1.4426950408889634
SCALE_LOG2E = MLA_SCALE * LOG2E

LANES = 128
SUBLANES = 8
MIB = 1024 * 1024


_VMEM_LIMIT_MIB = {
    "matmul": 40, "matmul_res_ln": 32, "even_layer_prompt": 40, "even_mix_sample": 48, "ffn_prompt": 56,
    "ffn_sample": 40, "mla_proj": 48, "mla_proj_decode": 40, "flash_prompt": 56, "decode_attention": 56,
    "decode_out": 24,
}


def _cparams(sem, call):
    return pltpu.CompilerParams(dimension_semantics=sem, vmem_limit_bytes=_VMEM_LIMIT_MIB[call] * MIB)


def _resident(shape):
    nd = len(shape)
    return pl.BlockSpec(shape, lambda *_: (0,) * nd, pipeline_mode=pl.Buffered(1))


def _resident_layer(stacked_shape, layer):
    nd = len(stacked_shape) - 1
    return pl.BlockSpec((None,) + tuple(stacked_shape[1:]), lambda *_: (layer,) + (0,) * nd,
                        pipeline_mode=pl.Buffered(1))


def _whole(shape):
    nd = len(shape)
    return pl.BlockSpec(shape, lambda *_: (0,) * nd)


def _dot(a, b):
    return jnp.dot(a, b, preferred_element_type=F32)


def _dot_nt(a, b):
    return lax.dot_general(a, b, (((1,), (1,)), ((), ())), preferred_element_type=F32)


def _dot_tn(a, b):
    return lax.dot_general(a, b, (((0,), (0,)), ((), ())), preferred_element_type=F32)


def _layer_norm(z, g, b):
    mu = jnp.mean(z, axis=-1, keepdims=True)
    d = z - mu
    var = jnp.mean(d * d, axis=-1, keepdims=True)
    return d * lax.rsqrt(var + LN_EPS) * g + b


def _silu(x):
    return x * jax.nn.sigmoid(x)


def _mm_body(x_ref, w_ref, o_ref):
    o_ref[...] = _dot(x_ref[...].astype(BF16), w_ref[...]).astype(o_ref.dtype)


def _matmul(x, w, tm, out_dtype=F32):
    m, k = x.shape
    n = w.shape[1]
    return pl.pallas_call(
        _mm_body,
        grid=(m // tm,),
        in_specs=[pl.BlockSpec((tm, k), lambda i: (i, 0)), _resident((k, n))],
        out_specs=pl.BlockSpec((tm, n), lambda i: (i, 0)),
        out_shape=jax.ShapeDtypeStruct((m, n), out_dtype),
        compiler_params=_cparams(("parallel",), "matmul"),
        name="matmul",
    )(x, w)


def _mm_ln_body(a_ref, w_ref, r_ref, g_ref, b_ref, o_ref):
    y = _dot(a_ref[...].astype(BF16), w_ref[...])
    o_ref[...] = _layer_norm(ALPHA * r_ref[...] + y, g_ref[...], b_ref[...])


def _matmul_res_ln(a, w, res, g, b, tm):
    m, k = a.shape
    n = w.shape[1]
    return pl.pallas_call(
        _mm_ln_body,
        grid=(m // tm,),
        in_specs=[pl.BlockSpec((tm, k), lambda i: (i, 0)), _resident((k, n)),
                  pl.BlockSpec((tm, n), lambda i: (i, 0)), _resident((1, n)), _resident((1, n))],
        out_specs=pl.BlockSpec((tm, n), lambda i: (i, 0)),
        out_shape=jax.ShapeDtypeStruct((m, n), F32),
        compiler_params=_cparams(("parallel",), "matmul_res_ln"),
        name="matmul_res_ln",
    )(a, w, res, g.reshape(1, n), b.reshape(1, n))


def _rope_full(x, c2, s2):
    return x * c2 + pltpu.roll(x, 64, axis=1) * s2


def _group_norm_gate(o, gate, gn_row):
    mu = jnp.mean(o, axis=-1, keepdims=True)
    d = o - mu
    var = jnp.mean(d * d, axis=-1, keepdims=True)
    return _silu(gate) * (d * lax.rsqrt(var + GN_EPS) * gn_row)


def _even_layer_prompt_body(x_ref, win_ref, c2_ref, s2_ref, dec_ref, qd_ref, kd_ref, gc_ref, pw_ref, ps_ref,
                            gn_ref, wo_ref, lg_ref, lb_ref, o_ref, pst_ref, rst_ref,
                            mix_ref, ext_ref, s_ref, *h_refs, tm, n_j, sub):
    j = pl.program_id(1)
    pd = len(POOL_WINDOWS) * LANES

    @pl.when(j == 0)
    def _():
        ext_ref[0:16, :] = jnp.zeros((16, pd), F32)
        s_ref[...] = jnp.zeros(s_ref.shape, F32)

    @pl.when(j > 0)
    def _():
        ext_ref[0:16, :] = ext_ref[tm:tm + 16, :]

    c = dec_ref.shape[1]
    k_scale = RET_DK ** -0.5
    nblk = 4 * LANES
    def project(r0):
        h_ref = h_refs[r0 // sub]
        xb = x_ref[r0:r0 + sub, :].astype(BF16)
        for c0 in range(0, h_ref.shape[1], nblk):
            h_ref[:, c0:c0 + nblk] = _dot(xb, win_ref[:, c0:c0 + nblk])

    project(0)
    for r0 in range(0, tm, sub):
        rb = slice(r0, r0 + sub)
        h_ref = h_refs[r0 // sub]
        if r0 + sub < tm:
            project(r0 + sub)
        ext_ref[16 + r0:16 + r0 + sub, :] = h_ref[:, 0:pd]

        pos = (j * tm + r0 + lax.broadcasted_iota(jnp.int32, (sub, 1), 0)).astype(F32)
        for g, w in enumerate(POOL_WINDOWS):
            cols = slice(g * LANES, (g + 1) * LANES)
            e = ext_ref[r0:r0 + sub + 16, cols]
            u = e[16:, :]
            s = 1
            while s < w:
                e = e + pltpu.roll(e, s, axis=0)
                s *= 2
            cnt = jnp.minimum(float(w), pos + 1.0)
            pooled = e[16:, :] / cnt - u
            mixed = _dot(pooled.astype(BF16), pw_ref[g]) * ps_ref[:, cols]
            mix_ref[rb, cols] = mixed.astype(BF16)

        for ci in range(r0 // c, (r0 + sub) // c):
            rows = slice(ci * c, (ci + 1) * c)
            lrows = slice(ci * c - r0, (ci + 1) * c - r0)
            c2 = c2_ref[rows, :]
            s2 = s2_ref[rows, :]
            for hd in range(RET_HEADS):
                def col(part, hd=hd):
                    return slice(pd + (part * RET_HEADS + hd) * LANES, pd + (part * RET_HEADS + hd + 1) * LANES)
                q = _rope_full(h_ref[lrows, col(0)], c2, s2)
                k = _rope_full(h_ref[lrows, col(1)], c2, s2) * k_scale
                vb = h_ref[lrows, col(2)].astype(BF16)
                gate = h_ref[lrows, col(3)]
                st = s_ref[hd]
                sc = _dot_nt(q.astype(BF16), k.astype(BF16)) * dec_ref[hd]
                o = _dot(sc.astype(BF16), vb)
                o = o + _dot((q * qd_ref[hd]).astype(BF16), st.astype(BF16))
                s_ref[hd] = gc_ref[hd] * st + _dot_tn((k * kd_ref[hd]).astype(BF16), vb)
                ret = _group_norm_gate(o, gate, gn_ref[:, hd * LANES:(hd + 1) * LANES])
                mix_ref[rows, pd + hd * LANES:pd + (hd + 1) * LANES] = ret.astype(BF16)

        y = _dot(mix_ref[rb, :], wo_ref[...])
        o_ref[rb, :] = _layer_norm(ALPHA * x_ref[rb, :] + y, lg_ref[...], lb_ref[...])

    @pl.when(j == n_j - 1)
    def _():
        pst_ref[0] = ext_ref[pl.ds(tm + 1, POOL_HIST), :]
        rst_ref[0] = s_ref[...]


def _even_layer_prompt(x, bsz, seq, w_in, tabs, pool_w, pool_scale, gn_g, w_o, ln_g, ln_b, tm):
    n_j = seq // tm
    d = x.shape[1]
    pd = pool_scale.shape[0]
    ed = pd + RET_HEADS * LANES
    chunk = tabs["dec"].shape[1]
    sub = min(tm, max(chunk, 2 * LANES))
    body = functools.partial(_even_layer_prompt_body, tm=tm, n_j=n_j, sub=sub)
    return pl.pallas_call(
        body,
        grid=(bsz, n_j),
        in_specs=[
            pl.BlockSpec((tm, d), lambda b, j: (b * n_j + j, 0)),
            _resident(w_in.shape),
            pl.BlockSpec((tm, LANES), lambda b, j: (j, 0)),
            pl.BlockSpec((tm, LANES), lambda b, j: (j, 0)),
            _resident((RET_HEADS, chunk, chunk)),
            _resident((RET_HEADS, chunk, LANES)),
            _resident((RET_HEADS, chunk, LANES)),
            _resident((RET_HEADS, 1, LANES)),
            _resident(pool_w.shape),
            _resident((1, pd)),
            _resident((1, RET_HEADS * LANES)),
            _resident(w_o.shape),
            _resident((1, d)),
            _resident((1, d)),
        ],
        out_specs=[
            pl.BlockSpec((tm, d), lambda b, j: (b * n_j + j, 0)),
            pl.BlockSpec((1, POOL_HIST, pd), lambda b, j: (b, 0, 0)),
            pl.BlockSpec((1, RET_HEADS, RET_DK, LANES), lambda b, j: (b, 0, 0, 0)),
        ],
        out_shape=[
            jax.ShapeDtypeStruct((bsz * seq, d), F32),
            jax.ShapeDtypeStruct((bsz, POOL_HIST, pd), F32),
            jax.ShapeDtypeStruct((bsz, RET_HEADS, RET_DK, LANES), F32),
        ],
        scratch_shapes=[pltpu.VMEM((tm, ed), BF16), pltpu.VMEM((tm + 16, pd), F32),
                        pltpu.VMEM((RET_HEADS, RET_DK, LANES), F32)]
        + [pltpu.VMEM((sub, w_in.shape[1]), F32)] * (tm // sub),
        compiler_params=_cparams(("parallel", "arbitrary"), "even_layer_prompt"),
        name="even_layer_prompt",
    )(x, w_in, tabs["c2"], tabs["s2"], tabs["dec"], tabs["qd"], tabs["kd"], tabs["gc"],
      pool_w, pool_scale.reshape(1, pd), gn_g.reshape(1, -1), w_o, ln_g.reshape(1, d), ln_b.reshape(1, d))


def _even_mix_sample_body(h_ref, hist_ref, s0_ref, c2_ref, s2_ref, dm_ref, qd_ref, kd_ref, gc_ref,
                          pw_ref, ps_ref, gn_ref, mix_ref, hist_o_ref, s_o_ref, oc_ref,
                          *, nb, ls, cnts):
    pd = len(POOL_WINDOWS) * LANES
    ext = [hist_ref[i] for i in range(POOL_HIST)]
    ext += [h_ref[l * nb:(l + 1) * nb, 0:pd] for l in range(ls)]
    for i in range(POOL_HIST):
        hist_o_ref[i] = ext[ls + i]
    for g, w in enumerate(POOL_WINDOWS):
        cols = slice(g * LANES, (g + 1) * LANES)
        outs = []
        for l in range(ls):
            top = POOL_HIST + l
            acc = ext[top][:, cols]
            for jj in range(1, w):
                acc = acc + ext[top - jj][:, cols]
            outs.append(acc / cnts[g][l] - ext[top][:, cols])
        pooled = jnp.concatenate(outs, axis=0)
        mixed = _dot(pooled.astype(BF16), pw_ref[g]) * ps_ref[:, cols]
        mix_ref[:, cols] = mixed.astype(BF16)

    rows_b = lax.broadcasted_iota(jnp.int32, (ls * nb, 1), 0) % nb
    k_scale = RET_DK ** -0.5
    c2 = c2_ref[...]
    s2 = s2_ref[...]
    qs, ks, vs = [], [], []
    for hd in range(RET_HEADS):
        def col(part, hd=hd):
            return slice(pd + (part * RET_HEADS + hd) * LANES, pd + (part * RET_HEADS + hd + 1) * LANES)
        q = _rope_full(h_ref[:, col(0)], c2, s2)
        k = _rope_full(h_ref[:, col(1)], c2, s2) * k_scale
        vb = h_ref[:, col(2)].astype(BF16)
        sc = _dot_nt(q.astype(BF16), k.astype(BF16)) * dm_ref[hd]
        oc_ref[hd] = _dot(sc.astype(BF16), vb)
        qs.append(q * qd_ref[hd])
        ks.append((k * kd_ref[hd]).T)
        vs.append(vb)

    cols_b = lax.broadcasted_iota(jnp.int32, (1, ls * nb), 1) % nb

    def per_batch(b, carry):
        sel = rows_b == b
        sel_t = cols_b == b
        for hd in range(RET_HEADS):
            st = s0_ref[b, hd]
            qm = jnp.where(sel, qs[hd], 0.0).astype(BF16)
            km_t = jnp.where(sel_t, ks[hd], 0.0).astype(BF16)
            oc_ref[hd] += _dot(qm, st.astype(BF16))
            s_o_ref[b, hd] = gc_ref[hd] * st + _dot(km_t, vs[hd])
        return carry

    lax.fori_loop(0, nb, per_batch, 0)

    for hd in range(RET_HEADS):
        gate = h_ref[:, pd + (3 * RET_HEADS + hd) * LANES:pd + (3 * RET_HEADS + hd + 1) * LANES]
        ret = _group_norm_gate(oc_ref[hd], gate, gn_ref[:, hd * LANES:(hd + 1) * LANES])
        mix_ref[:, pd + hd * LANES:pd + (hd + 1) * LANES] = ret.astype(BF16)


def _even_mix_sample(h, hist, s0, tabs, pool_w, pool_scale, gn_g, nb, ls, past_len):
    pd = pool_scale.shape[0]
    ed = pd + RET_HEADS * LANES
    cnts = tuple(tuple(float(min(w, past_len + l + 1)) for l in range(ls)) for w in POOL_WINDOWS)
    body = functools.partial(_even_mix_sample_body, nb=nb, ls=ls, cnts=cnts)
    n = ls * nb
    args = (h, hist, s0, tabs["c2"], tabs["s2"], tabs["dm"], tabs["qd"], tabs["kd"], tabs["gc"],
            pool_w, pool_scale.reshape(1, pd), gn_g.reshape(1, -1))
    return pl.pallas_call(
        body,
        grid=(1,),
        in_specs=[_resident(a.shape) for a in args],
        out_specs=[_whole((n, ed)), _whole(hist.shape), _whole(s0.shape)],
        out_shape=[jax.ShapeDtypeStruct((n, ed), BF16),
                   jax.ShapeDtypeStruct(hist.shape, F32),
                   jax.ShapeDtypeStruct(s0.shape, F32)],
        scratch_shapes=[pltpu.VMEM((RET_HEADS, n, LANES), F32)],
        compiler_params=_cparams(("arbitrary",), "even_mix_sample"),
        name="even_mix_sample",
    )(*args)


def _ffn_chunk(xb, wup_ref, wd_ref, cw_ref, cb_ref, c, tf, dff, shift_fn):
    cols = slice(c * tf, (c + 1) * tf)
    a = _dot(xb, wup_ref[:, cols])
    gate_in = _dot(xb, wup_ref[:, dff + c * tf:dff + (c + 1) * tf])
    a1, a2 = shift_fn(a, c)
    conv = cb_ref[:, cols] + cw_ref[0:1, cols] * a2
    conv = conv + cw_ref[1:2, cols] * a1
    conv = conv + cw_ref[2:3, cols] * a
    act = (_silu(conv) * gate_in).astype(BF16)
    return a, _dot(act, wd_ref[cols, :])


def _ffn_prompt_body(x_ref, *rest, tm, n_j, tf, dff, mixer_out):
    if mixer_out:
        a_ref, wo_ref, mg_ref, mb_ref = rest[:4]
        rest = rest[4:]
    wup_ref, wd_ref, cw_ref, cb_ref, g_ref, b_ref, o_ref, st_ref, carry_ref = rest
    j = pl.program_id(1)

    @pl.when(j == 0)
    def _():
        carry_ref[...] = jnp.zeros(carry_ref.shape, F32)

    x = x_ref[...]
    if mixer_out:
        x = _layer_norm(ALPHA * x + _dot(a_ref[...], wo_ref[...]), mg_ref[...], mb_ref[...])
    row = lax.broadcasted_iota(jnp.int32, (tm, tf), 0)

    def shift_fn(a, c):
        prev = carry_ref[:, c * tf:(c + 1) * tf]
        a1 = jnp.where(row == 0, prev[7:8, :], pltpu.roll(a, 1, axis=0))
        a2 = jnp.where(row == 0, prev[6:7, :], jnp.where(row == 1, prev[7:8, :], pltpu.roll(a, 2, axis=0)))
        return a1, a2

    xb = x.astype(BF16)
    acc = None
    for c in range(dff // tf):
        a, y = _ffn_chunk(xb, wup_ref, wd_ref, cw_ref, cb_ref, c, tf, dff, shift_fn)
        acc = y if acc is None else acc + y
        tail = a[tm - SUBLANES:tm, :]
        carry_ref[:, c * tf:(c + 1) * tf] = tail
        st_ref[0, :, c * tf:(c + 1) * tf] = tail
    o_ref[...] = _layer_norm(ALPHA * x + acc, g_ref[...], b_ref[...])


def _ffn_prompt(x, bsz, seq, layer, wup, wd, conv_w, conv_b, g, b, tm, tf, mixer_out=None):
    d = x.shape[1]
    dff = wd.shape[1]
    n_j = seq // tm
    body = functools.partial(_ffn_prompt_body, tm=tm, n_j=n_j, tf=tf, dff=dff, mixer_out=mixer_out is not None)
    row_tile = lambda bi, j: (bi * n_j + j, 0)
    pre_args, pre_specs = [], []
    if mixer_out is not None:
        a, w_o, mg, mb = mixer_out
        pre_args = [a, w_o, mg.reshape(1, d), mb.reshape(1, d)]
        pre_specs = [pl.BlockSpec((tm, a.shape[1]), row_tile), _resident(w_o.shape), _resident((1, d)), _resident((1, d))]
    return pl.pallas_call(
        body,
        grid=(bsz, n_j),
        in_specs=[pl.BlockSpec((tm, d), row_tile)] + pre_specs + [
                  _resident_layer(wup.shape, layer), _resident_layer(wd.shape, layer), _resident(conv_w.shape),
                  _resident((1, dff)), _resident((1, d)), _resident((1, d))],
        out_specs=[pl.BlockSpec((tm, d), row_tile),
                   pl.BlockSpec((1, SUBLANES, dff), lambda bi, j: (bi, 0, 0))],
        out_shape=[jax.ShapeDtypeStruct(x.shape, F32),
                   jax.ShapeDtypeStruct((bsz, SUBLANES, dff), F32)],
        scratch_shapes=[pltpu.VMEM((SUBLANES, dff), F32)],
        compiler_params=_cparams(("parallel", "arbitrary"), "ffn_prompt"),
        name="ffn_prompt",
    )(x, *pre_args, wup, wd, conv_w, conv_b.reshape(1, dff), g.reshape(1, d), b.reshape(1, d))


def _ffn_sample_body(x_ref, st_ref, wup_ref, wd_ref, cw_ref, cb_ref, g_ref, b_ref, o_ref, st_o_ref,
                     *, nb, ls, tf, dff):
    x = x_ref[...]
    xb = x.astype(BF16)
    nh = CONV_W - 1

    def shift_fn(a, c):
        cols = slice(c * tf, (c + 1) * tf)
        ext = [st_ref[i, :, cols] for i in range(nh)] + [a[l * nb:(l + 1) * nb, :] for l in range(ls)]
        a1 = jnp.concatenate([ext[nh + l - 1] for l in range(ls)], axis=0)
        a2 = jnp.concatenate([ext[nh + l - 2] for l in range(ls)], axis=0)
        for i in range(nh):
            st_o_ref[i, :, cols] = ext[ls + i]
        return a1, a2

    acc = None
    for c in range(dff // tf):
        _, y = _ffn_chunk(xb, wup_ref, wd_ref, cw_ref, cb_ref, c, tf, dff, shift_fn)
        acc = y if acc is None else acc + y
    o_ref[...] = _layer_norm(ALPHA * x + acc, g_ref[...], b_ref[...])


def _ffn_sample(x, st, layer, wup, wd, conv_w, conv_b, g, b, nb, ls, tf):
    d = x.shape[1]
    dff = wd.shape[1]
    body = functools.partial(_ffn_sample_body, nb=nb, ls=ls, tf=tf, dff=dff)
    args = (x, st, wup, wd, conv_w, conv_b.reshape(1, dff), g.reshape(1, d), b.reshape(1, d))
    in_specs = [_resident(a.shape) for a in args]
    in_specs[2] = _resident_layer(wup.shape, layer)
    in_specs[3] = _resident_layer(wd.shape, layer)
    return pl.pallas_call(
        body,
        grid=(1,),
        in_specs=in_specs,
        out_specs=[_whole(x.shape), _whole(st.shape)],
        out_shape=[jax.ShapeDtypeStruct(x.shape, F32), jax.ShapeDtypeStruct(st.shape, F32)],
        compiler_params=_cparams(("arbitrary",), "ffn_sample"),
        name="ffn_sample",
    )(*args)


def _rope_pe(blk, cc, s1, s2):
    return blk * cc + pltpu.roll(blk, 96, axis=1) * s1 + pltpu.roll(blk, 32, axis=1) * s2


def _rms_norm(x, g):
    ms = jnp.mean(x * x, axis=-1, keepdims=True)
    return x * lax.rsqrt(ms + RMS_EPS) * g


def _mla_proj_body(x_ref, cc_ref, s1_ref, s2_ref, wdq_ref, qg_ref, wuq_ref, wdkv_ref, kvg_ref, *rest,
                   decode, sub):
    hw = 2 * LANES
    for r0 in range(0, x_ref.shape[0], sub):
        rb = slice(r0, r0 + sub)
        cc = cc_ref[rb, :]
        s1 = s1_ref[rb, :]
        s2 = s2_ref[rb, :]
        xb = x_ref[rb, :].astype(BF16)
        cq = _rms_norm(_dot(xb, wdq_ref[...]), qg_ref[...])
        q = _dot(cq.astype(BF16), wuq_ref[...])
        kv = _dot(xb, wdkv_ref[...])
        ckv = _rms_norm(kv[:, 0:KV_LORA], kvg_ref[...])
        kpe = _rope_pe(kv[:, KV_LORA:KV_LORA + LANES], cc, s1, s2)
        if decode:
            wukt_ref, ql_ref, qp_ref, ckv_ref, kpe_ref = rest
            for h in range(MLA_HEADS):
                qn = q[:, h * hw:h * hw + LANES].astype(BF16)
                ql_ref[rb, h * KV_LORA:(h + 1) * KV_LORA] = _dot(qn, wukt_ref[h]).astype(BF16)
                qp = _rope_pe(q[:, h * hw + LANES:(h + 1) * hw], cc, s1, s2)
                qp_ref[rb, h * LANES:(h + 1) * LANES] = qp.astype(BF16)
        else:
            wuk_ref, wuv_ref, qo_ref, ko_ref, vo_ref, ckv_ref, kpe_ref = rest
            cb = ckv.astype(BF16)
            kn = _dot(cb, wuk_ref[...])
            vo_ref[rb, :] = _dot(cb, wuv_ref[...]).astype(BF16)
            kpb = kpe.astype(BF16)
            for h in range(MLA_HEADS):
                qo_ref[rb, h * hw:h * hw + LANES] = q[:, h * hw:h * hw + LANES].astype(BF16)
                qp = _rope_pe(q[:, h * hw + LANES:(h + 1) * hw], cc, s1, s2)
                qo_ref[rb, h * hw + LANES:(h + 1) * hw] = qp.astype(BF16)
                ko_ref[rb, h * hw:h * hw + LANES] = kn[:, h * LANES:(h + 1) * LANES].astype(BF16)
                ko_ref[rb, h * hw + LANES:(h + 1) * hw] = kpb
        ckv_ref[rb, :] = ckv
        kpe_ref[rb, :] = kpe[:, 0:QK_ROPE]


def _mla_proj(x, tabs, w, tm, n_pos_blocks, decode):
    t, d = x.shape
    hw = 2 * LANES
    body = functools.partial(_mla_proj_body, decode=decode, sub=tm)
    row = lambda i: (i, 0)
    tab = lambda i: (i % n_pos_blocks, 0)
    ins = [x, tabs["cc"], tabs["s1"], tabs["s2"], w["dq"], w["qg"], w["uq"], w["dkv"], w["kvg"]]
    in_specs = [pl.BlockSpec((tm, d), row)] + [pl.BlockSpec((tm, LANES), tab)] * 3
    in_specs += [_resident(a.shape) for a in ins[4:]]
    if decode:
        ins += [w["ukt"]]
        in_specs += [_resident(w["ukt"].shape)]
        outs = [(MLA_HEADS * KV_LORA, BF16), (MLA_HEADS * LANES, BF16)]
    else:
        ins += [w["uk"], w["uv"]]
        in_specs += [_resident(w["uk"].shape), _resident(w["uv"].shape)]
        outs = [(MLA_HEADS * hw, BF16), (MLA_HEADS * hw, BF16), (MLA_HEADS * V_DIM, BF16)]
    outs += [(KV_LORA, F32), (QK_ROPE, F32)]
    return pl.pallas_call(
        body,
        grid=(t // tm,),
        in_specs=in_specs,
        out_specs=[pl.BlockSpec((tm, n), row) for n, _ in outs],
        out_shape=[jax.ShapeDtypeStruct((t, n), dt) for n, dt in outs],
        compiler_params=_cparams(("parallel",), "mla_proj_decode" if decode else "mla_proj"),
        name="mla_proj_decode" if decode else "mla_proj",
    )(*ins)


def _online_softmax_update(s2, m_ref, l_ref, acc_ref, pv_fn, row_chunk):
    rows, width = s2.shape
    n = width // LANES
    aw = acc_ref.shape[-1] // LANES
    p_chunks, alphas = [], []
    for r0 in range(0, rows, row_chunk):
        rs_ = slice(r0, r0 + row_chunk)
        tiles = [s2[rs_, j * LANES:(j + 1) * LANES] for j in range(n)]
        mx = tiles[0]
        for t in tiles[1:]:
            mx = jnp.maximum(mx, t)
        m_prev = m_ref[rs_, :]
        m_new = jnp.maximum(m_prev, jnp.max(mx, axis=-1, keepdims=True))
        alpha = jnp.exp2(m_prev - m_new)
        ps = [jnp.exp2(t - m_new) for t in tiles]
        if l_ref is not None:
            tot = ps[0]
            for t in ps[1:]:
                tot = tot + t
            l_ref[rs_, :] = alpha * l_ref[rs_, :] + jnp.sum(tot, axis=-1, keepdims=True)
        m_ref[rs_, :] = m_new
        p_chunks.append((jnp.concatenate(ps, axis=1) if n > 1 else ps[0]).astype(BF16))
        alphas.append(alpha)
    p = jnp.concatenate(p_chunks, axis=0) if len(p_chunks) > 1 else p_chunks[0]
    pv = pv_fn(p)
    for i, r0 in enumerate(range(0, rows, row_chunk)):
        rs_ = slice(r0, r0 + row_chunk)
        a_w = alphas[i] if aw == 1 else jnp.concatenate([alphas[i]] * aw, axis=1)
        acc_ref[rs_, :] = a_w * acc_ref[rs_, :] + pv[rs_, :]


def _flash_body(q_ref, k_ref, v_ref, o_ref, m_ref, acc_ref, *, tq, tk, row_chunk):
    qi = pl.program_id(1)
    hw = 2 * LANES
    m_ref[...] = jnp.full(m_ref.shape, -jnp.inf, F32)
    acc_ref[...] = jnp.zeros(acc_ref.shape, F32)
    ones = jnp.ones((tk, LANES), BF16)

    def step(h, key0, diag_off):
        start = pl.multiple_of(key0, tk)
        q = q_ref[:, h * hw:(h + 1) * hw]
        k = k_ref[pl.ds(start, tk), h * hw:(h + 1) * hw]
        v1 = jnp.concatenate([v_ref[pl.ds(start, tk), h * V_DIM:(h + 1) * V_DIM], ones], axis=1)
        s = _dot_nt(q, k) * SCALE_LOG2E
        if diag_off is not None:
            row = lax.broadcasted_iota(jnp.int32, s.shape, 0)
            colm = lax.broadcasted_iota(jnp.int32, s.shape, 1)
            s = jnp.where(colm + diag_off <= row, s, -jnp.inf)
        _online_softmax_update(s, m_ref.at[h], None, acc_ref.at[h], lambda p: _dot(p, v1), row_chunk)

    def loop_body(kj, carry):
        for h in range(MLA_HEADS):
            step(h, kj * tk, None)
        return carry

    lax.fori_loop(0, qi * (tq // tk), loop_body, 0)
    for h in range(MLA_HEADS):
        for j in range(tq // tk):
            step(h, qi * tq + j * tk, j * tk)
        o_ref[:, h * V_DIM:(h + 1) * V_DIM] = (acc_ref[h, :, 0:V_DIM] / acc_ref[h, :, V_DIM:]).astype(BF16)


def _flash_prompt(qp, kp, vp, bsz, seq, tq, tk):
    nq = seq // tq
    body = functools.partial(_flash_body, tq=tq, tk=tk, row_chunk=min(tq, 64))
    return pl.pallas_call(
        body,
        grid=(bsz, nq),
        in_specs=[pl.BlockSpec((tq, qp.shape[1]), lambda b, i: (b * nq + i, 0)),
                  pl.BlockSpec((seq, kp.shape[1]), lambda b, i: (b, 0)),
                  pl.BlockSpec((seq, vp.shape[1]), lambda b, i: (b, 0))],
        out_specs=pl.BlockSpec((tq, vp.shape[1]), lambda b, i: (b * nq + i, 0)),
        out_shape=jax.ShapeDtypeStruct(vp.shape, BF16),
        scratch_shapes=[pltpu.VMEM((MLA_HEADS, tq, LANES), F32), pltpu.VMEM((MLA_HEADS, tq, V_DIM + LANES), F32)],
        compiler_params=_cparams(("parallel", "arbitrary"), "flash_prompt"),
        name="flash_prompt",
    )(qp, kp, vp)


def _decode_body(pt_ref, ql_ref, qp_ref, cn_ref, kn_ref, ckv_hbm, kpe_hbm, o_ref,
                 cbuf_ref, rbuf_ref, sem_ref, m_ref, l_ref, acc_ref, *, gp, ls, layer, n_split, n_b, n_g, n_buf):
    b = pl.program_id(0)
    g = pl.program_id(1)
    t = b * n_g + g
    slot = lax.rem(t, n_buf)
    ahead = n_buf - 1

    def page_copies(bb, grp, slot):
        cps = []
        for i in range(gp):
            pg = pt_ref[bb, grp * gp + i]
            keys = pl.ds(i * PAGE_SIZE, PAGE_SIZE)
            cps.append(pltpu.make_async_copy(ckv_hbm.at[layer, pg], cbuf_ref.at[slot, keys, :], sem_ref.at[slot, 0]))
            cps.append(pltpu.make_async_copy(kpe_hbm.at[layer, pg], rbuf_ref.at[slot, :, keys], sem_ref.at[slot, 1]))
        return cps

    def start(step):
        for cp in page_copies(lax.div(step, n_g), lax.rem(step, n_g), lax.rem(step, n_buf)):
            cp.start()

    @pl.when(t == 0)
    def _():
        for s in range(min(ahead, n_b * n_g)):
            start(jnp.int32(s))

    @pl.when(t + ahead < n_b * n_g)
    def _():
        start(t + ahead)

    @pl.when(g == 0)
    def _():
        m_ref[...] = jnp.full(m_ref.shape, -jnp.inf, F32)
        l_ref[...] = jnp.zeros(l_ref.shape, F32)
        acc_ref[...] = jnp.zeros(acc_ref.shape, F32)

    ql = ql_ref[0]
    qp = qp_ref[0]

    for cp in page_copies(b, g, slot):
        cp.wait()

    kc_len = gp * PAGE_SIZE // n_split
    kbs, scores = [], []
    for c in range(n_split):
        ks = slice(c * kc_len, (c + 1) * kc_len)
        kb = cbuf_ref[slot, ks, :].astype(BF16)
        scores.append((_dot_nt(ql, kb) + _dot(qp, rbuf_ref[slot, :, ks].astype(BF16))) * SCALE_LOG2E)
        kbs.append(kb)
    state = (m_ref.at[0], l_ref.at[0], acc_ref.at[0])
    n_rows = ql.shape[0]
    for c in range(n_split):
        _online_softmax_update(scores[c], *state, lambda p, kb=kbs[c]: _dot(p, kb), n_rows)

    @pl.when(g == n_g - 1)
    def _():
        kc = cn_ref[0].astype(BF16)
        s = (_dot_nt(ql, kc) + _dot(qp, kn_ref[0].astype(BF16))) * SCALE_LOG2E
        r = lax.broadcasted_iota(jnp.int32, s.shape, 0) // MLA_HEADS
        cidx = lax.broadcasted_iota(jnp.int32, s.shape, 1)
        s = jnp.where((cidx <= r) & (cidx < ls), s, -jnp.inf)
        _online_softmax_update(s, *state, lambda p: _dot(p, kc), n_rows)
        l_w = jnp.concatenate([l_ref[0]] * (KV_LORA // LANES), axis=1)
        o_ref[0] = acc_ref[0] / l_w


def _decode_attention(page_table, ql, qp, cn, knt, cache_ckv, cache_kpet, layer, gp):
    nb, rows, _ = ql.shape
    n_pages = page_table.shape[1]
    ls = rows // MLA_HEADS
    n_split = 4 if gp % 4 == 0 else 1
    n_g = n_pages // gp
    n_buf = 3
    body = functools.partial(_decode_body, gp=gp, ls=ls, layer=layer, n_split=n_split, n_b=nb, n_g=n_g, n_buf=n_buf)
    per_b = lambda b, g, pt: (b, 0, 0)
    in_specs = [pl.BlockSpec((1, rows, KV_LORA), per_b), pl.BlockSpec((1, rows, QK_ROPE), per_b),
                pl.BlockSpec((1, PAGE_SIZE, KV_LORA), per_b), pl.BlockSpec((1, QK_ROPE, PAGE_SIZE), per_b),
                pl.BlockSpec(memory_space=pl.ANY), pl.BlockSpec(memory_space=pl.ANY)]
    grid_spec = pltpu.PrefetchScalarGridSpec(
        num_scalar_prefetch=1,
        grid=(nb, n_g),
        in_specs=in_specs,
        out_specs=pl.BlockSpec((1, rows, KV_LORA), per_b),
        scratch_shapes=[pltpu.VMEM((n_buf, gp * PAGE_SIZE, KV_LORA), F32),
                        pltpu.VMEM((n_buf, QK_ROPE, gp * PAGE_SIZE), F32),
                        pltpu.SemaphoreType.DMA((n_buf, 2)),
                        pltpu.VMEM((1, rows, LANES), F32), pltpu.VMEM((1, rows, LANES), F32),
                        pltpu.VMEM((1, rows, KV_LORA), F32)],
    )
    return pl.pallas_call(
        body,
        grid_spec=grid_spec,
        out_shape=jax.ShapeDtypeStruct((nb, rows, KV_LORA), F32),
        compiler_params=_cparams(("arbitrary", "arbitrary"), "decode_attention"),
        name="decode_attention",
    )(page_table, ql, qp, cn, knt, cache_ckv, cache_kpet)


def _decode_out_body(ol_ref, wuv_ref, wo_ref, r_ref, g_ref, b_ref, o_ref):
    y = None
    for h in range(MLA_HEADS):
        oh = _dot(ol_ref[h].astype(BF16), wuv_ref[h]).astype(BF16)
        t = _dot(oh, wo_ref[h * V_DIM:(h + 1) * V_DIM, :])
        y = t if y is None else y + t
    o_ref[...] = _layer_norm(ALPHA * r_ref[...] + y, g_ref[...], b_ref[...])


def _decode_out(ol, wuv3, wo, res, g, b):
    d = res.shape[1]
    args = (ol, wuv3, wo, res, g.reshape(1, d), b.reshape(1, d))
    return pl.pallas_call(
        _decode_out_body,
        grid=(1,),
        in_specs=[_resident(a.shape) for a in args],
        out_specs=_whole(res.shape),
        out_shape=jax.ShapeDtypeStruct(res.shape, F32),
        compiler_params=_cparams(("arbitrary",), "decode_out"),
        name="decode_out",
    )(*args)


def _rope_angles(pos, half):
    inv_freq = ROPE_THETA ** (-np.arange(half, dtype=np.float64) / half)
    ang = np.asarray(pos, np.float64)[:, None] * inv_freq[None, :]
    return np.cos(ang), np.sin(ang)


def _f32(tabs):
    return {k: np.ascontiguousarray(v, dtype=np.float32) for k, v in tabs.items()}


def _ret_tables(pos, chunk, rows_per_pos):
    cos, sin = _rope_angles(pos, RET_DK // 2)
    log_gamma = np.log(1.0 - 2.0 ** (-5.0 - np.arange(RET_HEADS, dtype=np.float64)))
    idx = np.repeat(np.arange(chunk, dtype=np.float64), rows_per_pos)
    diff = idx[:, None] - idx[None, :]
    dec = np.where(diff >= 0, np.exp(np.maximum(diff, 0.0)[None] * log_gamma[:, None, None]), 0.0)
    q_dec = np.exp((idx + 1.0)[None, :] * log_gamma[:, None])
    k_dec = np.exp((chunk - 1.0 - idx)[None, :] * log_gamma[:, None])
    gc = np.exp(chunk * log_gamma)
    n = idx.shape[0]
    return _f32({
        "c2": np.concatenate([cos, cos], axis=1),
        "s2": np.concatenate([-sin, sin], axis=1),
        "dec": dec,
        "qd": np.broadcast_to(q_dec[:, :, None], (RET_HEADS, n, LANES)),
        "kd": np.broadcast_to(k_dec[:, :, None], (RET_HEADS, n, LANES)),
        "gc": np.broadcast_to(gc[:, None, None], (RET_HEADS, 1, LANES)),
    })


def _pe_tables(pos):
    cos, sin = _rope_angles(pos, QK_ROPE // 2)
    z = np.zeros_like(cos)
    return _f32({"cc": np.concatenate([cos, cos, z, z], axis=1),
                 "s1": np.concatenate([-sin, z, z, z], axis=1),
                 "s2": np.concatenate([z, sin, z, z], axis=1)})


def _pad_heads(w, nope, rope):
    k = w.shape[0]
    w3 = w.reshape(k, MLA_HEADS, nope + rope)
    pad = jnp.zeros((k, MLA_HEADS, 2 * LANES - nope - rope), w.dtype)
    return jnp.concatenate([w3, pad], axis=2).reshape(k, MLA_HEADS * 2 * LANES)


def _tiles(seq, n_pages, dff):
    def fit(t, n=seq):
        while n % t:
            t //= 2
        return t
    return {"tok": fit(1024), "mix": fit(512), "ret_chunk": fit(256), "ffn": fit(512), "attn": fit(512), "attn_k": fit(512), "ff_chunk": dff,
            "pages": fit(64, n_pages)}


def kernel(x_prompt, x_sample, state_pool, state_ret, cache_ckv, cache_kpe, state_conv, page_table,
           w_in_even, pool_w, pool_scale, ret_gn_g, w_o_even,
           w_dq, q_norm_g, w_uq, w_dkv, kv_norm_g, w_uk, w_uv, w_o_mla,
           w_up, conv_w, conv_b, w_down, ln_mix_g, ln_mix_b, ln_ffn_g, ln_ffn_b):
    bp, lp, d = x_prompt.shape
    bs, ls, _ = x_sample.shape
    past_len = page_table.shape[1] * PAGE_SIZE
    depth = w_up.shape[0]
    dff = w_down.shape[1]
    tl = _tiles(lp, page_table.shape[1], dff)
    ns = bs * ls
    assert lp % tl["ret_chunk"] == 0 and tl["mix"] % tl["ret_chunk"] == 0
    assert dff % tl["ff_chunk"] == 0 and page_table.shape[1] % tl["pages"] == 0 and ls <= PAGE_SIZE

    xp = x_prompt.reshape(bp * lp, d)
    xs = jnp.swapaxes(x_sample, 0, 1).reshape(ns, d)
    pos_p = np.arange(lp)
    pos_s_rows = np.repeat(past_len + np.arange(ls), bs)

    wup_all = w_up.astype(BF16)
    wd_all = w_down.astype(BF16)
    outs = {k: [] for k in ("pool_p", "pool_s", "ret_p", "ret_s", "ckv_p", "ckv_s", "kpe_p", "kpe_s",
                            "conv_p", "conv_s")}
    for layer in range(depth):
        mixer_out = None
        if layer % 2 == 0:
            e = layer // 2
            w_in = w_in_even[e].astype(BF16)
            w_o = w_o_even[e].astype(BF16)
            pw = pool_w[e].astype(BF16)
            tp = _ret_tables(pos_p, tl["ret_chunk"], 1)
            xp, pst, rst = _even_layer_prompt(xp, bp, lp, w_in, tp, pw, pool_scale[e], ret_gn_g[e], w_o,
                                              ln_mix_g[layer], ln_mix_b[layer], tl["mix"])
            outs["pool_p"].append(pst)
            outs["ret_p"].append(rst)
            ts = _ret_tables(pos_s_rows, ls, bs)
            rb = np.arange(ns) % bs
            ts["dm"] = np.where((rb[:, None] == rb[None, :])[None], ts["dec"], np.float32(0.0))
            hs =_matmul(xs, w_in, ns)
            hist = jnp.swapaxes(state_pool[e], 0, 1)
            mix_s, hist_new, s_new = _even_mix_sample(hs, hist, state_ret[e], ts, pw, pool_scale[e],
                                                      ret_gn_g[e], bs, ls, past_len)
            outs["pool_s"].append(jnp.swapaxes(hist_new, 0, 1))
            outs["ret_s"].append(s_new)
            xs = _matmul_res_ln(mix_s, w_o, xs, ln_mix_g[layer], ln_mix_b[layer], ns)
        else:
            o = layer // 2
            hw = 2 * LANES
            w = {
                "dq": w_dq[o].astype(BF16),
                "qg": q_norm_g[o].reshape(1, -1),
                "uq": _pad_heads(w_uq[o], QK_NOPE, QK_ROPE).astype(BF16),
                "dkv": jnp.pad(w_dkv[o], ((0, 0), (0, KV_LORA + LANES - w_dkv.shape[2]))).astype(BF16),
                "kvg": kv_norm_g[o].reshape(1, -1),
                "uk": w_uk[o].reshape(KV_LORA, MLA_HEADS * QK_NOPE).astype(BF16),
                "uv": w_uv[o].reshape(KV_LORA, MLA_HEADS * V_DIM).astype(BF16),
                "ukt": jnp.transpose(w_uk[o], (1, 2, 0)).astype(BF16),
            }
            w_o = w_o_mla[o].astype(BF16)
            qp, kp, vp, ckv_p, kpe_p = _mla_proj(xp, _pe_tables(pos_p), w, tl["tok"], lp // tl["tok"], False)
            att = _flash_prompt(qp, kp, vp, bp, lp, tl["attn"], tl["attn_k"])
            outs["ckv_p"].append(ckv_p.reshape(bp, lp, KV_LORA))
            outs["kpe_p"].append(kpe_p.reshape(bp, lp, QK_ROPE))
            mixer_out = (att, w_o, ln_mix_g[layer], ln_mix_b[layer])
            ql, qpe, ckv_s, kpe_s = _mla_proj(xs, _pe_tables(pos_s_rows), w, ns, 1, True)
            rows = ls * MLA_HEADS

            def per_batch(a, width):
                return jnp.transpose(a.reshape(ls, bs, MLA_HEADS, width), (1, 0, 2, 3)).reshape(bs, rows, width)

            ql_b = per_batch(ql, KV_LORA)
            qp_b = per_batch(qpe, LANES)[:, :, 0:QK_ROPE]
            ckv_sb = jnp.swapaxes(ckv_s.reshape(ls, bs, KV_LORA), 0, 1)
            kpe_sb = jnp.swapaxes(kpe_s.reshape(ls, bs, QK_ROPE), 0, 1)
            cn = jnp.pad(ckv_sb, ((0, 0), (0, PAGE_SIZE - ls), (0, 0)))
            knt = jnp.swapaxes(jnp.pad(kpe_sb, ((0, 0), (0, PAGE_SIZE - ls), (0, 0))), 1, 2)
            o_lat = _decode_attention(page_table, ql_b, qp_b, cn, knt, cache_ckv,
                                      jnp.swapaxes(cache_kpe, 2, 3), o, tl["pages"])
            outs["ckv_s"].append(ckv_sb)
            outs["kpe_s"].append(kpe_sb)
            ol = jnp.transpose(o_lat.reshape(bs, ls, MLA_HEADS, KV_LORA), (2, 1, 0, 3)).reshape(MLA_HEADS, ns, KV_LORA)
            wuv3 = jnp.transpose(w_uv[o], (1, 0, 2)).astype(BF16)
            xs = _decode_out(ol, wuv3, w_o, xs, ln_mix_g[layer], ln_mix_b[layer])
        xp, st_p = _ffn_prompt(xp, bp, lp, layer, wup_all, wd_all, conv_w[layer], conv_b[layer],
                               ln_ffn_g[layer], ln_ffn_b[layer], tl["ffn"], tl["ff_chunk"], mixer_out)
        outs["conv_p"].append(st_p[:, SUBLANES - (CONV_W - 1):, :])
        st_s = jnp.swapaxes(state_conv[layer], 0, 1)
        xs, st_s_new = _ffn_sample(xs, st_s, layer, wup_all, wd_all, conv_w[layer], conv_b[layer],
                                   ln_ffn_g[layer], ln_ffn_b[layer], bs, ls, tl["ff_chunk"])
        outs["conv_s"].append(jnp.swapaxes(st_s_new, 0, 1))

    y_p = xp.reshape(bp, lp, d)
    y_s = jnp.swapaxes(xs.reshape(ls, bs, d), 0, 1)
    return (y_p, y_s,
            jnp.stack(outs["pool_p"]), jnp.stack(outs["pool_s"]),
            jnp.stack(outs["ret_p"]), jnp.stack(outs["ret_s"]),
            jnp.stack(outs["ckv_p"]), jnp.stack(outs["ckv_s"]),
            jnp.stack(outs["kpe_p"]), jnp.stack(outs["kpe_s"]),
            jnp.stack(outs["conv_p"]), jnp.stack(outs["conv_s"]))
```

```python
import functools

import jax
import jax.numpy as jnp
import numpy as np
from jax import lax
from jax.experimental import pallas as pl
from jax.experimental.pallas import tpu as pltpu

F32 = jnp.float32
BF16 = jnp.bfloat16

PAGE_SIZE = 128
POOL_WINDOWS = (2, 4, 8, 16)
POOL_HIST = max(POOL_WINDOWS) - 1
RET_HEADS = 4
RET_DK = 128
MLA_HEADS = 8
QK_NOPE = 128
QK_ROPE = 64
V_DIM = 128
KV_LORA = 256
CONV_W = 3
DEPTH = 2
ALPHA = (2.0 * DEPTH) ** 0.25
ROPE_THETA = 10000.0
LN_EPS = 1e-5
RMS_EPS = 1e-6
GN_EPS = 1e-6
MLA_SCALE = (QK_NOPE + QK_ROPE) ** -0.5
LOG2E = 1.4426950408889634
SCALE_LOG2E = MLA_SCALE * LOG2E

LANES = 128
SUBLANES = 8
MIB = 1024 * 1024


_VMEM_LIMIT_MIB = {
    "even_layer_prompt": 40, "even_layer_sample": 48, "ffn_prompt": 56,
    "ffn_sample": 40, "mla_proj": 48, "mla_proj_decode": 40, "flash_prompt": 56, "decode_attention": 56,
    "decode_out": 24,
}


def _cparams(sem, call):
    return pltpu.CompilerParams(dimension_semantics=sem, vmem_limit_bytes=_VMEM_LIMIT_MIB[call] * MIB)


def _resident(shape):
    nd = len(shape)
    return pl.BlockSpec(shape, lambda *_: (0,) * nd, pipeline_mode=pl.Buffered(1))


def _resident_layer(stacked_shape, layer):
    nd = len(stacked_shape) - 1
    return pl.BlockSpec((None,) + tuple(stacked_shape[1:]), lambda *_: (layer,) + (0,) * nd,
                        pipeline_mode=pl.Buffered(1))


def _whole(shape):
    nd = len(shape)
    return pl.BlockSpec(shape, lambda *_: (0,) * nd)


def _dot(a, b):
    return jnp.dot(a, b, preferred_element_type=F32)


def _dot_nt(a, b):
    return lax.dot_general(a, b, (((1,), (1,)), ((), ())), preferred_element_type=F32)


def _dot_tn(a, b):
    return lax.dot_general(a, b, (((0,), (0,)), ((), ())), preferred_element_type=F32)


def _layer_norm(z, g, b):
    mu = jnp.mean(z, axis=-1, keepdims=True)
    d = z - mu
    var = jnp.mean(d * d, axis=-1, keepdims=True)
    return d * lax.rsqrt(var + LN_EPS) * g + b


def _silu(x):
    return x * jax.nn.sigmoid(x)


def _rope_full(x, c2, s2):
    return x * c2 + pltpu.roll(x, 64, axis=1) * s2


def _group_norm_gate(o, gate, gn_row):
    mu = jnp.mean(o, axis=-1, keepdims=True)
    d = o - mu
    var = jnp.mean(d * d, axis=-1, keepdims=True)
    return _silu(gate) * (d * lax.rsqrt(var + GN_EPS) * gn_row)


def _even_layer_prompt_body(x_ref, win_ref, c2_ref, s2_ref, dec_ref, qd_ref, kd_ref, gc_ref, pw_ref, ps_ref,
                            gn_ref, wo_ref, lg_ref, lb_ref, o_ref, pst_ref, rst_ref,
                            mix_ref, ext_ref, s_ref, *h_refs, tm, n_j, sub):
    j = pl.program_id(1)
    pd = len(POOL_WINDOWS) * LANES

    @pl.when(j == 0)
    def _():
        ext_ref[0:16, :] = jnp.zeros((16, pd), F32)
        s_ref[...] = jnp.zeros(s_ref.shape, F32)

    @pl.when(j > 0)
    def _():
        ext_ref[0:16, :] = ext_ref[tm:tm + 16, :]

    c = dec_ref.shape[1]
    k_scale = RET_DK ** -0.5
    nblk = 4 * LANES
    def project(r0):
        h_ref = h_refs[r0 // sub]
        xb = x_ref[r0:r0 + sub, :].astype(BF16)
        for c0 in range(0, h_ref.shape[1], nblk):
            h_ref[:, c0:c0 + nblk] = _dot(xb, win_ref[:, c0:c0 + nblk])

    project(0)
    for r0 in range(0, tm, sub):
        rb = slice(r0, r0 + sub)
        h_ref = h_refs[r0 // sub]
        if r0 + sub < tm:
            project(r0 + sub)
        ext_ref[16 + r0:16 + r0 + sub, :] = h_ref[:, 0:pd]

        pos = (j * tm + r0 + lax.broadcasted_iota(jnp.int32, (sub, 1), 0)).astype(F32)
        for g, w in enumerate(POOL_WINDOWS):
            cols = slice(g * LANES, (g + 1) * LANES)
            e = ext_ref[r0:r0 + sub + 16, cols]
            u = e[16:, :]
            s = 1
            while s < w:
                e = e + pltpu.roll(e, s, axis=0)
                s *= 2
            cnt = jnp.minimum(float(w), pos + 1.0)
            pooled = e[16:, :] / cnt - u
            mixed = _dot(pooled.astype(BF16), pw_ref[g]) * ps_ref[:, cols]
            mix_ref[rb, cols] = mixed.astype(BF16)

        for ci in range(r0 // c, (r0 + sub) // c):
            rows = slice(ci * c, (ci + 1) * c)
            lrows = slice(ci * c - r0, (ci + 1) * c - r0)
            c2 = c2_ref[rows, :]
            s2 = s2_ref[rows, :]
            for hd in range(RET_HEADS):
                def col(part, hd=hd):
                    return slice(pd + (part * RET_HEADS + hd) * LANES, pd + (part * RET_HEADS + hd + 1) * LANES)
                q = _rope_full(h_ref[lrows, col(0)], c2, s2)
                k = _rope_full(h_ref[lrows, col(1)], c2, s2) * k_scale
                vb = h_ref[lrows, col(2)].astype(BF16)
                gate = h_ref[lrows, col(3)]
                st = s_ref[hd]
                sc = _dot_nt(q.astype(BF16), k.astype(BF16)) * dec_ref[hd]
                o = _dot(sc.astype(BF16), vb)
                o = o + _dot((q * qd_ref[hd]).astype(BF16), st.astype(BF16))
                s_ref[hd] = gc_ref[hd] * st + _dot_tn((k * kd_ref[hd]).astype(BF16), vb)
                ret = _group_norm_gate(o, gate, gn_ref[:, hd * LANES:(hd + 1) * LANES])
                mix_ref[rows, pd + hd * LANES:pd + (hd + 1) * LANES] = ret.astype(BF16)

        y = _dot(mix_ref[rb, :], wo_ref[...])
        o_ref[rb, :] = _layer_norm(ALPHA * x_ref[rb, :] + y, lg_ref[...], lb_ref[...])

    @pl.when(j == n_j - 1)
    def _():
        pst_ref[0] = ext_ref[pl.ds(tm + 1, POOL_HIST), :]
        rst_ref[0] = s_ref[...]


def _even_layer_prompt(x, bsz, seq, w_in, tabs, pool_w, pool_scale, gn_g, w_o, ln_g, ln_b, tm):
    n_j = seq // tm
    d = x.shape[1]
    pd = pool_scale.shape[0]
    ed = pd + RET_HEADS * LANES
    chunk = tabs["dec"].shape[1]
    sub = min(tm, max(chunk, 2 * LANES))
    body = functools.partial(_even_layer_prompt_body, tm=tm, n_j=n_j, sub=sub)
    return pl.pallas_call(
        body,
        grid=(bsz, n_j),
        in_specs=[
            pl.BlockSpec((tm, d), lambda b, j: (b * n_j + j, 0)),
            _resident(w_in.shape),
            pl.BlockSpec((tm, LANES), lambda b, j: (j, 0)),
            pl.BlockSpec((tm, LANES), lambda b, j: (j, 0)),
            _resident((RET_HEADS, chunk, chunk)),
            _resident((RET_HEADS, chunk, LANES)),
            _resident((RET_HEADS, chunk, LANES)),
            _resident((RET_HEADS, 1, LANES)),
            _resident(pool_w.shape),
            _resident((1, pd)),
            _resident((1, RET_HEADS * LANES)),
            _resident(w_o.shape),
            _resident((1, d)),
            _resident((1, d)),
        ],
        out_specs=[
            pl.BlockSpec((tm, d), lambda b, j: (b * n_j + j, 0)),
            pl.BlockSpec((1, POOL_HIST, pd), lambda b, j: (b, 0, 0)),
            pl.BlockSpec((1, RET_HEADS, RET_DK, LANES), lambda b, j: (b, 0, 0, 0)),
        ],
        out_shape=[
            jax.ShapeDtypeStruct((bsz * seq, d), F32),
            jax.ShapeDtypeStruct((bsz, POOL_HIST, pd), F32),
            jax.ShapeDtypeStruct((bsz, RET_HEADS, RET_DK, LANES), F32),
        ],
        scratch_shapes=[pltpu.VMEM((tm, ed), BF16), pltpu.VMEM((tm + 16, pd), F32),
                        pltpu.VMEM((RET_HEADS, RET_DK, LANES), F32)]
        + [pltpu.VMEM((sub, w_in.shape[1]), F32)] * (tm // sub),
        compiler_params=_cparams(("parallel", "arbitrary"), "even_layer_prompt"),
        name="even_layer_prompt",
    )(x, w_in, tabs["c2"], tabs["s2"], tabs["dec"], tabs["qd"], tabs["kd"], tabs["gc"],
      pool_w, pool_scale.reshape(1, pd), gn_g.reshape(1, -1), w_o, ln_g.reshape(1, d), ln_b.reshape(1, d))


def _even_layer_sample_body(x_ref, win_ref, hist_ref, s0_ref, c2_ref, s2_ref, dm_ref, qd_ref, kd_ref, gc_ref,
                            pw_ref, ps_ref, gn_ref, wo_ref, lg_ref, lb_ref, o_ref, hist_o_ref, s_o_ref,
                            h_ref, mix_ref, oc_ref, *, nb, ls, cnts):
    pd = len(POOL_WINDOWS) * LANES
    x = x_ref[...]
    h_ref[...] = _dot(x.astype(BF16), win_ref[...])
    ext = [hist_ref[i] for i in range(POOL_HIST)]
    ext += [h_ref[l * nb:(l + 1) * nb, 0:pd] for l in range(ls)]
    for i in range(POOL_HIST):
        hist_o_ref[i] = ext[ls + i]
    for g, w in enumerate(POOL_WINDOWS):
        cols = slice(g * LANES, (g + 1) * LANES)
        outs = []
        for l in range(ls):
            top = POOL_HIST + l
            acc = ext[top][:, cols]
            for jj in range(1, w):
                acc = acc + ext[top - jj][:, cols]
            outs.append(acc / cnts[g][l] - ext[top][:, cols])
        pooled = jnp.concatenate(outs, axis=0)
        mixed = _dot(pooled.astype(BF16), pw_ref[g]) * ps_ref[:, cols]
        mix_ref[:, cols] = mixed.astype(BF16)

    rows_b = lax.broadcasted_iota(jnp.int32, (ls * nb, 1), 0) % nb
    k_scale = RET_DK ** -0.5
    c2 = c2_ref[...]
    s2 = s2_ref[...]
    qs, ks, vs = [], [], []
    for hd in range(RET_HEADS):
        def col(part, hd=hd):
            return slice(pd + (part * RET_HEADS + hd) * LANES, pd + (part * RET_HEADS + hd + 1) * LANES)
        q = _rope_full(h_ref[:, col(0)], c2, s2)
        k = _rope_full(h_ref[:, col(1)], c2, s2) * k_scale
        vb = h_ref[:, col(2)].astype(BF16)
        sc = _dot_nt(q.astype(BF16), k.astype(BF16)) * dm_ref[hd]
        oc_ref[hd] = _dot(sc.astype(BF16), vb)
        qs.append(q * qd_ref[hd])
        ks.append((k * kd_ref[hd]).T)
        vs.append(vb)

    cols_b = lax.broadcasted_iota(jnp.int32, (1, ls * nb), 1) % nb

    def per_batch(b, carry):
        sel = rows_b == b
        sel_t = cols_b == b
        for hd in range(RET_HEADS):
            st = s0_ref[b, hd]
            qm = jnp.where(sel, qs[hd], 0.0).astype(BF16)
            km_t = jnp.where(sel_t, ks[hd], 0.0).astype(BF16)
            oc_ref[hd] += _dot(qm, st.astype(BF16))
            s_o_ref[b, hd] = gc_ref[hd] * st + _dot(km_t, vs[hd])
        return carry

    lax.fori_loop(0, nb, per_batch, 0)

    for hd in range(RET_HEADS):
        gate = h_ref[:, pd + (3 * RET_HEADS + hd) * LANES:pd + (3 * RET_HEADS + hd + 1) * LANES]
        ret = _group_norm_gate(oc_ref[hd], gate, gn_ref[:, hd * LANES:(hd + 1) * LANES])
        mix_ref[:, pd + hd * LANES:pd + (hd + 1) * LANES] = ret.astype(BF16)

    o_ref[...] = _layer_norm(ALPHA * x + _dot(mix_ref[...], wo_ref[...]), lg_ref[...], lb_ref[...])


def _even_layer_sample(x, w_in, hist, s0, tabs, pool_w, pool_scale, gn_g, w_o, ln_g, ln_b, nb, ls, past_len):
    d = x.shape[1]
    pd = pool_scale.shape[0]
    ed = pd + RET_HEADS * LANES
    cnts = tuple(tuple(float(min(w, past_len + l + 1)) for l in range(ls)) for w in POOL_WINDOWS)
    body = functools.partial(_even_layer_sample_body, nb=nb, ls=ls, cnts=cnts)
    n = ls * nb
    args = (x, w_in, hist, s0, tabs["c2"], tabs["s2"], tabs["dm"], tabs["qd"], tabs["kd"], tabs["gc"],
            pool_w, pool_scale.reshape(1, pd), gn_g.reshape(1, -1), w_o, ln_g.reshape(1, d), ln_b.reshape(1, d))
    return pl.pallas_call(
        body,
        grid=(1,),
        in_specs=[_resident(a.shape) for a in args],
        out_specs=[_whole((n, d)), _whole(hist.shape), _whole(s0.shape)],
        out_shape=[jax.ShapeDtypeStruct((n, d), F32),
                   jax.ShapeDtypeStruct(hist.shape, F32),
                   jax.ShapeDtypeStruct(s0.shape, F32)],
        scratch_shapes=[pltpu.VMEM((n, w_in.shape[1]), F32), pltpu.VMEM((n, ed), BF16),
                        pltpu.VMEM((RET_HEADS, n, LANES), F32)],
        compiler_params=_cparams(("arbitrary",), "even_layer_sample"),
        name="even_layer_sample",
    )(*args)


def _ffn_chunk(xb, wup_ref, wd_ref, cw_ref, cb_ref, c, tf, dff, shift_fn):
    cols = slice(c * tf, (c + 1) * tf)
    a = _dot(xb, wup_ref[:, cols])
    gate_in = _dot(xb, wup_ref[:, dff + c * tf:dff + (c + 1) * tf])
    a1, a2 = shift_fn(a, c)
    conv = cb_ref[:, cols] + cw_ref[0:1, cols] * a2
    conv = conv + cw_ref[1:2, cols] * a1
    conv = conv + cw_ref[2:3, cols] * a
    act = (_silu(conv) * gate_in).astype(BF16)
    return a, _dot(act, wd_ref[cols, :])


def _ffn_prompt_body(x_ref, *rest, tm, n_j, tf, dff, mixer_out):
    if mixer_out:
        a_ref, wo_ref, mg_ref, mb_ref = rest[:4]
        rest = rest[4:]
    wup_ref, wd_ref, cw_ref, cb_ref, g_ref, b_ref, o_ref, st_ref, carry_ref = rest
    j = pl.program_id(1)

    @pl.when(j == 0)
    def _():
        carry_ref[...] = jnp.zeros(carry_ref.shape, F32)

    x = x_ref[...]
    if mixer_out:
        x = _layer_norm(ALPHA * x + _dot(a_ref[...], wo_ref[...]), mg_ref[...], mb_ref[...])
    row = lax.broadcasted_iota(jnp.int32, (tm, tf), 0)

    def shift_fn(a, c):
        prev = carry_ref[:, c * tf:(c + 1) * tf]
        a1 = jnp.where(row == 0, prev[7:8, :], pltpu.roll(a, 1, axis=0))
        a2 = jnp.where(row == 0, prev[6:7, :], jnp.where(row == 1, prev[7:8, :], pltpu.roll(a, 2, axis=0)))
        return a1, a2

    xb = x.astype(BF16)
    acc = None
    for c in range(dff // tf):
        a, y = _ffn_chunk(xb, wup_ref, wd_ref, cw_ref, cb_ref, c, tf, dff, shift_fn)
        acc = y if acc is None else acc + y
        tail = a[tm - SUBLANES:tm, :]
        carry_ref[:, c * tf:(c + 1) * tf] = tail
        st_ref[0, :, c * tf:(c + 1) * tf] = tail
    o_ref[...] = _layer_norm(ALPHA * x + acc, g_ref[...], b_ref[...])


def _ffn_prompt(x, bsz, seq, layer, wup, wd, conv_w, conv_b, g, b, tm, tf, mixer_out=None):
    d = x.shape[1]
    dff = wd.shape[1]
    n_j = seq // tm
    body = functools.partial(_ffn_prompt_body, tm=tm, n_j=n_j, tf=tf, dff=dff, mixer_out=mixer_out is not None)
    row_tile = lambda bi, j: (bi * n_j + j, 0)
    pre_args, pre_specs = [], []
    if mixer_out is not None:
        a, w_o, mg, mb = mixer_out
        pre_args = [a, w_o, mg.reshape(1, d), mb.reshape(1, d)]
        pre_specs = [pl.BlockSpec((tm, a.shape[1]), row_tile), _resident(w_o.shape), _resident((1, d)), _resident((1, d))]
    return pl.pallas_call(
        body,
        grid=(bsz, n_j),
        in_specs=[pl.BlockSpec((tm, d), row_tile)] + pre_specs + [
                  _resident_layer(wup.shape, layer), _resident_layer(wd.shape, layer), _resident(conv_w.shape),
                  _resident((1, dff)), _resident((1, d)), _resident((1, d))],
        out_specs=[pl.BlockSpec((tm, d), row_tile),
                   pl.BlockSpec((1, SUBLANES, dff), lambda bi, j: (bi, 0, 0))],
        out_shape=[jax.ShapeDtypeStruct(x.shape, F32),
                   jax.ShapeDtypeStruct((bsz, SUBLANES, dff), F32)],
        scratch_shapes=[pltpu.VMEM((SUBLANES, dff), F32)],
        compiler_params=_cparams(("parallel", "arbitrary"), "ffn_prompt"),
        name="ffn_prompt",
    )(x, *pre_args, wup, wd, conv_w, conv_b.reshape(1, dff), g.reshape(1, d), b.reshape(1, d))


def _ffn_sample_body(x_ref, st_ref, wup_ref, wd_ref, cw_ref, cb_ref, g_ref, b_ref, o_ref, st_o_ref,
                     *, nb, ls, tf, dff):
    x = x_ref[...]
    xb = x.astype(BF16)
    nh = CONV_W - 1

    def shift_fn(a, c):
        cols = slice(c * tf, (c + 1) * tf)
        ext = [st_ref[i, :, cols] for i in range(nh)] + [a[l * nb:(l + 1) * nb, :] for l in range(ls)]
        a1 = jnp.concatenate([ext[nh + l - 1] for l in range(ls)], axis=0)
        a2 = jnp.concatenate([ext[nh + l - 2] for l in range(ls)], axis=0)
        for i in range(nh):
            st_o_ref[i, :, cols] = ext[ls + i]
        return a1, a2

    acc = None
    for c in range(dff // tf):
        _, y = _ffn_chunk(xb, wup_ref, wd_ref, cw_ref, cb_ref, c, tf, dff, shift_fn)
        acc = y if acc is None else acc + y
    o_ref[...] = _layer_norm(ALPHA * x + acc, g_ref[...], b_ref[...])


def _ffn_sample(x, st, layer, wup, wd, conv_w, conv_b, g, b, nb, ls, tf):
    d = x.shape[1]
    dff = wd.shape[1]
    body = functools.partial(_ffn_sample_body, nb=nb, ls=ls, tf=tf, dff=dff)
    args = (x, st, wup, wd, conv_w, conv_b.reshape(1, dff), g.reshape(1, d), b.reshape(1, d))
    in_specs = [_resident(a.shape) for a in args]
    in_specs[2] = _resident_layer(wup.shape, layer)
    in_specs[3] = _resident_layer(wd.shape, layer)
    return pl.pallas_call(
        body,
        grid=(1,),
        in_specs=in_specs,
        out_specs=[_whole(x.shape), _whole(st.shape)],
        out_shape=[jax.ShapeDtypeStruct(x.shape, F32), jax.ShapeDtypeStruct(st.shape, F32)],
        compiler_params=_cparams(("arbitrary",), "ffn_sample"),
        name="ffn_sample",
    )(*args)


def _rope_pe(blk, cc, s1, s2):
    return blk * cc + pltpu.roll(blk, 96, axis=1) * s1 + pltpu.roll(blk, 32, axis=1) * s2


def _rms_norm(x, g):
    ms = jnp.mean(x * x, axis=-1, keepdims=True)
    return x * lax.rsqrt(ms + RMS_EPS) * g


def _mla_proj_body(x_ref, cc_ref, s1_ref, s2_ref, wdq_ref, qg_ref, wuq_ref, wdkv_ref, kvg_ref, *rest,
                   decode, sub):
    hw = 2 * LANES
    for r0 in range(0, x_ref.shape[0], sub):
        rb = slice(r0, r0 + sub)
        cc = cc_ref[rb, :]
        s1 = s1_ref[rb, :]
        s2 = s2_ref[rb, :]
        xb = x_ref[rb, :].astype(BF16)
        cq = _rms_norm(_dot(xb, wdq_ref[...]), qg_ref[...])
        q = _dot(cq.astype(BF16), wuq_ref[...])
        kv = _dot(xb, wdkv_ref[...])
        ckv = _rms_norm(kv[:, 0:KV_LORA], kvg_ref[...])
        kpe = _rope_pe(kv[:, KV_LORA:KV_LORA + LANES], cc, s1, s2)
        if decode:
            wukt_ref, ql_ref, qp_ref, ckv_ref, kpe_ref = rest
            for h in range(MLA_HEADS):
                qn = q[:, h * hw:h * hw + LANES].astype(BF16)
                ql_ref[rb, h * KV_LORA:(h + 1) * KV_LORA] = _dot(qn, wukt_ref[h]).astype(BF16)
                qp = _rope_pe(q[:, h * hw + LANES:(h + 1) * hw], cc, s1, s2)
                qp_ref[rb, h * LANES:(h + 1) * LANES] = qp.astype(BF16)
        else:
            wuk_ref, wuv_ref, qo_ref, ko_ref, vo_ref, ckv_ref, kpe_ref = rest
            cb = ckv.astype(BF16)
            kn = _dot(cb, wuk_ref[...])
            vo_ref[rb, :] = _dot(cb, wuv_ref[...]).astype(BF16)
            kpb = kpe.astype(BF16)
            for h in range(MLA_HEADS):
                qo_ref[rb, h * hw:h * hw + LANES] = q[:, h * hw:h * hw + LANES].astype(BF16)
                qp = _rope_pe(q[:, h * hw + LANES:(h + 1) * hw], cc, s1, s2)
                qo_ref[rb, h * hw + LANES:(h + 1) * hw] = qp.astype(BF16)
                ko_ref[rb, h * hw:h * hw + LANES] = kn[:, h * LANES:(h + 1) * LANES].astype(BF16)
                ko_ref[rb, h * hw + LANES:(h + 1) * hw] = kpb
        ckv_ref[rb, :] = ckv
        kpe_ref[rb, :] = kpe[:, 0:QK_ROPE]


def _mla_proj(x, tabs, w, tm, n_pos_blocks, decode):
    t, d = x.shape
    hw = 2 * LANES
    body = functools.partial(_mla_proj_body, decode=decode, sub=tm)
    row = lambda i: (i, 0)
    tab = lambda i: (i % n_pos_blocks, 0)
    ins = [x, tabs["cc"], tabs["s1"], tabs["s2"], w["dq"], w["qg"], w["uq"], w["dkv"], w["kvg"]]
    in_specs = [pl.BlockSpec((tm, d), row)] + [pl.BlockSpec((tm, LANES), tab)] * 3
    in_specs += [_resident(a.shape) for a in ins[4:]]
    if decode:
        ins += [w["ukt"]]
        in_specs += [_resident(w["ukt"].shape)]
        outs = [(MLA_HEADS * KV_LORA, BF16), (MLA_HEADS * LANES, BF16)]
    else:
        ins += [w["uk"], w["uv"]]
        in_specs += [_resident(w["uk"].shape), _resident(w["uv"].shape)]
        outs = [(MLA_HEADS * hw, BF16), (MLA_HEADS * hw, BF16), (MLA_HEADS * V_DIM, BF16)]
    outs += [(KV_LORA, F32), (QK_ROPE, F32)]
    return pl.pallas_call(
        body,
        grid=(t // tm,),
        in_specs=in_specs,
        out_specs=[pl.BlockSpec((tm, n), row) for n, _ in outs],
        out_shape=[jax.ShapeDtypeStruct((t, n), dt) for n, dt in outs],
        compiler_params=_cparams(("parallel",), "mla_proj_decode" if decode else "mla_proj"),
        name="mla_proj_decode" if decode else "mla_proj",
    )(*ins)


def _online_softmax_update(s2, m_ref, l_ref, acc_ref, pv_fn, row_chunk):
    rows, width = s2.shape
    n = width // LANES
    aw = acc_ref.shape[-1] // LANES
    p_chunks, alphas = [], []
    for r0 in range(0, rows, row_chunk):
        rs_ = slice(r0, r0 + row_chunk)
        tiles = [s2[rs_, j * LANES:(j + 1) * LANES] for j in range(n)]
        mx = tiles[0]
        for t in tiles[1:]:
            mx = jnp.maximum(mx, t)
        m_prev = m_ref[rs_, :]
        m_new = jnp.maximum(m_prev, jnp.max(mx, axis=-1, keepdims=True))
        alpha = jnp.exp2(m_prev - m_new)
        ps = [jnp.exp2(t - m_new) for t in tiles]
        if l_ref is not None:
            tot = ps[0]
            for t in ps[1:]:
                tot = tot + t
            l_ref[rs_, :] = alpha * l_ref[rs_, :] + jnp.sum(tot, axis=-1, keepdims=True)
        m_ref[rs_, :] = m_new
        p_chunks.append((jnp.concatenate(ps, axis=1) if n > 1 else ps[0]).astype(BF16))
        alphas.append(alpha)
    p = jnp.concatenate(p_chunks, axis=0) if len(p_chunks) > 1 else p_chunks[0]
    pv = pv_fn(p)
    for i, r0 in enumerate(range(0, rows, row_chunk)):
        rs_ = slice(r0, r0 + row_chunk)
        a_w = alphas[i] if aw == 1 else jnp.concatenate([alphas[i]] * aw, axis=1)
        acc_ref[rs_, :] = a_w * acc_ref[rs_, :] + pv[rs_, :]


def _flash_body(q_ref, k_ref, v_ref, o_ref, m_ref, acc_ref, *, tq, tk, row_chunk):
    qi = pl.program_id(1)
    hw = 2 * LANES
    m_ref[...] = jnp.full(m_ref.shape, -jnp.inf, F32)
    acc_ref[...] = jnp.zeros(acc_ref.shape, F32)
    ones = jnp.ones((tk, LANES), BF16)

    def step(h, key0, diag_off):
        start = pl.multiple_of(key0, tk)
        q = q_ref[:, h * hw:(h + 1) * hw]
        k = k_ref[pl.ds(start, tk), h * hw:(h + 1) * hw]
        v1 = jnp.concatenate([v_ref[pl.ds(start, tk), h * V_DIM:(h + 1) * V_DIM], ones], axis=1)
        s = _dot_nt(q, k) * SCALE_LOG2E
        if diag_off is not None:
            row = lax.broadcasted_iota(jnp.int32, s.shape, 0)
            colm = lax.broadcasted_iota(jnp.int32, s.shape, 1)
            s = jnp.where(colm + diag_off <= row, s, -jnp.inf)
        _online_softmax_update(s, m_ref.at[h], None, acc_ref.at[h], lambda p: _dot(p, v1), row_chunk)

    def loop_body(kj, carry):
        for h in range(MLA_HEADS):
            step(h, kj * tk, None)
        return carry

    lax.fori_loop(0, qi * (tq // tk), loop_body, 0)
    for h in range(MLA_HEADS):
        for j in range(tq // tk):
            step(h, qi * tq + j * tk, j * tk)
        o_ref[:, h * V_DIM:(h + 1) * V_DIM] = (acc_ref[h, :, 0:V_DIM] / acc_ref[h, :, V_DIM:]).astype(BF16)


def _flash_prompt(qp, kp, vp, bsz, seq, tq, tk):
    nq = seq // tq
    body = functools.partial(_flash_body, tq=tq, tk=tk, row_chunk=min(tq, 64))
    return pl.pallas_call(
        body,
        grid=(bsz, nq),
        in_specs=[pl.BlockSpec((tq, qp.shape[1]), lambda b, i: (b * nq + i, 0)),
                  pl.BlockSpec((seq, kp.shape[1]), lambda b, i: (b, 0)),
                  pl.BlockSpec((seq, vp.shape[1]), lambda b, i: (b, 0))],
        out_specs=pl.BlockSpec((tq, vp.shape[1]), lambda b, i: (b * nq + i, 0)),
        out_shape=jax.ShapeDtypeStruct(vp.shape, BF16),
        scratch_shapes=[pltpu.VMEM((MLA_HEADS, tq, LANES), F32), pltpu.VMEM((MLA_HEADS, tq, V_DIM + LANES), F32)],
        compiler_params=_cparams(("parallel", "arbitrary"), "flash_prompt"),
        name="flash_prompt",
    )(qp, kp, vp)


def _decode_body(pt_ref, ql_ref, qp_ref, cn_ref, kn_ref, ckv_hbm, kpe_hbm, o_ref,
                 cbuf_ref, rbuf_ref, sem_ref, m_ref, l_ref, acc_ref, *, gp, ls, layer, n_split, n_b, n_g, n_buf):
    b = pl.program_id(0)
    g = pl.program_id(1)
    t = b * n_g + g
    slot = lax.rem(t, n_buf)
    ahead = n_buf - 1

    def page_copies(bb, grp, slot):
        cps = []
        for i in range(gp):
            pg = pt_ref[bb, grp * gp + i]
            keys = pl.ds(i * PAGE_SIZE, PAGE_SIZE)
            cps.append(pltpu.make_async_copy(ckv_hbm.at[layer, pg], cbuf_ref.at[slot, keys, :], sem_ref.at[slot, 0]))
            cps.append(pltpu.make_async_copy(kpe_hbm.at[layer, pg], rbuf_ref.at[slot, :, keys], sem_ref.at[slot, 1]))
        return cps

    def start(step):
        for cp in page_copies(lax.div(step, n_g), lax.rem(step, n_g), lax.rem(step, n_buf)):
            cp.start()

    @pl.when(t == 0)
    def _():
        for s in range(min(ahead, n_b * n_g)):
            start(jnp.int32(s))

    @pl.when(t + ahead < n_b * n_g)
    def _():
        start(t + ahead)

    @pl.when(g == 0)
    def _():
        m_ref[...] = jnp.full(m_ref.shape, -jnp.inf, F32)
        l_ref[...] = jnp.zeros(l_ref.shape, F32)
        acc_ref[...] = jnp.zeros(acc_ref.shape, F32)

    ql = ql_ref[0]
    qp = qp_ref[0]

    for cp in page_copies(b, g, slot):
        cp.wait()

    kc_len = gp * PAGE_SIZE // n_split
    kbs, scores = [], []
    for c in range(n_split):
        ks = slice(c * kc_len, (c + 1) * kc_len)
        kb = cbuf_ref[slot, ks, :].astype(BF16)
        scores.append((_dot_nt(ql, kb) + _dot(qp, rbuf_ref[slot, :, ks].astype(BF16))) * SCALE_LOG2E)
        kbs.append(kb)
    state = (m_ref.at[0], l_ref.at[0], acc_ref.at[0])
    n_rows = ql.shape[0]
    for c in range(n_split):
        _online_softmax_update(scores[c], *state, lambda p, kb=kbs[c]: _dot(p, kb), n_rows)

    @pl.when(g == n_g - 1)
    def _():
        kc = cn_ref[0].astype(BF16)
        s = (_dot_nt(ql, kc) + _dot(qp, kn_ref[0].astype(BF16))) * SCALE_LOG2E
        r = lax.broadcasted_iota(jnp.int32, s.shape, 0) // MLA_HEADS
        cidx = lax.broadcasted_iota(jnp.int32, s.shape, 1)
        s = jnp.where((cidx <= r) & (cidx < ls), s, -jnp.inf)
        _online_softmax_update(s, *state, lambda p: _dot(p, kc), n_rows)
        l_w = jnp.concatenate([l_ref[0]] * (KV_LORA // LANES), axis=1)
        o_ref[0] = acc_ref[0] / l_w


def _decode_attention(page_table, ql, qp, cn, knt, cache_ckv, cache_kpet, layer, gp):
    nb, rows, _ = ql.shape
    n_pages = page_table.shape[1]
    ls = rows // MLA_HEADS
    n_split = 4 if gp % 4 == 0 else 1
    n_g = n_pages // gp
    n_buf = 4
    body = functools.partial(_decode_body, gp=gp, ls=ls, layer=layer, n_split=n_split, n_b=nb, n_g=n_g, n_buf=n_buf)
    per_b = lambda b, g, pt: (b, 0, 0)
    in_specs = [pl.BlockSpec((1, rows, KV_LORA), per_b), pl.BlockSpec((1, rows, QK_ROPE), per_b),
                pl.BlockSpec((1, PAGE_SIZE, KV_LORA), per_b), pl.BlockSpec((1, QK_ROPE, PAGE_SIZE), per_b),
                pl.BlockSpec(memory_space=pl.ANY), pl.BlockSpec(memory_space=pl.ANY)]
    grid_spec = pltpu.PrefetchScalarGridSpec(
        num_scalar_prefetch=1,
        grid=(nb, n_g),
        in_specs=in_specs,
        out_specs=pl.BlockSpec((1, rows, KV_LORA), per_b),
        scratch_shapes=[pltpu.VMEM((n_buf, gp * PAGE_SIZE, KV_LORA), F32),
                        pltpu.VMEM((n_buf, QK_ROPE, gp * PAGE_SIZE), F32),
                        pltpu.SemaphoreType.DMA((n_buf, 2)),
                        pltpu.VMEM((1, rows, LANES), F32), pltpu.VMEM((1, rows, LANES), F32),
                        pltpu.VMEM((1, rows, KV_LORA), F32)],
    )
    return pl.pallas_call(
        body,
        grid_spec=grid_spec,
        out_shape=jax.ShapeDtypeStruct((nb, rows, KV_LORA), F32),
        compiler_params=_cparams(("arbitrary", "arbitrary"), "decode_attention"),
        name="decode_attention",
    )(page_table, ql, qp, cn, knt, cache_ckv, cache_kpet)


def _decode_out_body(ol_ref, wuv_ref, wo_ref, r_ref, g_ref, b_ref, o_ref):
    y = None
    for h in range(MLA_HEADS):
        oh = _dot(ol_ref[h].astype(BF16), wuv_ref[h]).astype(BF16)
        t = _dot(oh, wo_ref[h * V_DIM:(h + 1) * V_DIM, :])
        y = t if y is None else y + t
    o_ref[...] = _layer_norm(ALPHA * r_ref[...] + y, g_ref[...], b_ref[...])


def _decode_out(ol, wuv3, wo, res, g, b):
    d = res.shape[1]
    args = (ol, wuv3, wo, res, g.reshape(1, d), b.reshape(1, d))
    return pl.pallas_call(
        _decode_out_body,
        grid=(1,),
        in_specs=[_resident(a.shape) for a in args],
        out_specs=_whole(res.shape),
        out_shape=jax.ShapeDtypeStruct(res.shape, F32),
        compiler_params=_cparams(("arbitrary",), "decode_out"),
        name="decode_out",
    )(*args)


def _rope_angles(pos, half):
    inv_freq = ROPE_THETA ** (-np.arange(half, dtype=np.float64) / half)
    ang = np.asarray(pos, np.float64)[:, None] * inv_freq[None, :]
    return np.cos(ang), np.sin(ang)


def _f32(tabs):
    return {k: np.ascontiguousarray(v, dtype=np.float32) for k, v in tabs.items()}


def _ret_tables(pos, chunk, rows_per_pos):
    cos, sin = _rope_angles(pos, RET_DK // 2)
    log_gamma = np.log(1.0 - 2.0 ** (-5.0 - np.arange(RET_HEADS, dtype=np.float64)))
    idx = np.repeat(np.arange(chunk, dtype=np.float64), rows_per_pos)
    diff = idx[:, None] - idx[None, :]
    dec = np.where(diff >= 0, np.exp(np.maximum(diff, 0.0)[None] * log_gamma[:, None, None]), 0.0)
    q_dec = np.exp((idx + 1.0)[None, :] * log_gamma[:, None])
    k_dec = np.exp((chunk - 1.0 - idx)[None, :] * log_gamma[:, None])
    gc = np.exp(chunk * log_gamma)
    n = idx.shape[0]
    return _f32({
        "c2": np.concatenate([cos, cos], axis=1),
        "s2": np.concatenate([-sin, sin], axis=1),
        "dec": dec,
        "qd": np.broadcast_to(q_dec[:, :, None], (RET_HEADS, n, LANES)),
        "kd": np.broadcast_to(k_dec[:, :, None], (RET_HEADS, n, LANES)),
        "gc": np.broadcast_to(gc[:, None, None], (RET_HEADS, 1, LANES)),
    })


def _pe_tables(pos):
    cos, sin = _rope_angles(pos, QK_ROPE // 2)
    z = np.zeros_like(cos)
    return _f32({"cc": np.concatenate([cos, cos, z, z], axis=1),
                 "s1": np.concatenate([-sin, z, z, z], axis=1),
                 "s2": np.concatenate([z, sin, z, z], axis=1)})


def _pad_heads(w, nope, rope):
    k = w.shape[0]
    w3 = w.reshape(k, MLA_HEADS, nope + rope)
    pad = jnp.zeros((k, MLA_HEADS, 2 * LANES - nope - rope), w.dtype)
    return jnp.concatenate([w3, pad], axis=2).reshape(k, MLA_HEADS * 2 * LANES)


def _tiles(seq, n_pages, dff):
    def fit(t, n=seq):
        while n % t:
            t //= 2
        return t
    return {"tok": fit(1024), "mix": fit(512), "ret_chunk": fit(256), "ffn": fit(512), "attn": fit(512), "attn_k": fit(512), "ff_chunk": dff,
            "pages": fit(64, n_pages)}


def kernel(x_prompt, x_sample, state_pool, state_ret, cache_ckv, cache_kpe, state_conv, page_table,
           w_in_even, pool_w, pool_scale, ret_gn_g, w_o_even,
           w_dq, q_norm_g, w_uq, w_dkv, kv_norm_g, w_uk, w_uv, w_o_mla,
           w_up, conv_w, conv_b, w_down, ln_mix_g, ln_mix_b, ln_ffn_g, ln_ffn_b):
    bp, lp, d = x_prompt.shape
    bs, ls, _ = x_sample.shape
    past_len = page_table.shape[1] * PAGE_SIZE
    depth = w_up.shape[0]
    dff = w_down.shape[1]
    tl = _tiles(lp, page_table.shape[1], dff)
    ns = bs * ls
    assert lp % tl["ret_chunk"] == 0 and tl["mix"] % tl["ret_chunk"] == 0
    assert dff % tl["ff_chunk"] == 0 and page_table.shape[1] % tl["pages"] == 0 and ls <= PAGE_SIZE

    xp = x_prompt.reshape(bp * lp, d)
    xs = jnp.swapaxes(x_sample, 0, 1).reshape(ns, d)
    pos_p = np.arange(lp)
    pos_s_rows = np.repeat(past_len + np.arange(ls), bs)

    wup_all = w_up.astype(BF16)
    wd_all = w_down.astype(BF16)
    outs = {k: [] for k in ("pool_p", "pool_s", "ret_p", "ret_s", "ckv_p", "ckv_s", "kpe_p", "kpe_s",
                            "conv_p", "conv_s")}
    for layer in range(depth):
        mixer_out = None
        if layer % 2 == 0:
            e = layer // 2
            w_in = w_in_even[e].astype(BF16)
            w_o = w_o_even[e].astype(BF16)
            pw = pool_w[e].astype(BF16)
            tp = _ret_tables(pos_p, tl["ret_chunk"], 1)
            xp, pst, rst = _even_layer_prompt(xp, bp, lp, w_in, tp, pw, pool_scale[e], ret_gn_g[e], w_o,
                                              ln_mix_g[layer], ln_mix_b[layer], tl["mix"])
            outs["pool_p"].append(pst)
            outs["ret_p"].append(rst)
            ts = _ret_tables(pos_s_rows, ls, bs)
            rb = np.arange(ns) % bs
            ts["dm"] = np.where((rb[:, None] == rb[None, :])[None], ts["dec"], np.float32(0.0))
            hist = jnp.swapaxes(state_pool[e], 0, 1)
            xs, hist_new, s_new = _even_layer_sample(xs, w_in, hist, state_ret[e], ts, pw, pool_scale[e],
                                                     ret_gn_g[e], w_o, ln_mix_g[layer], ln_mix_b[layer],
                                                     bs, ls, past_len)
            outs["pool_s"].append(jnp.swapaxes(hist_new, 0, 1))
            outs["ret_s"].append(s_new)
        else:
            o = layer // 2
            hw = 2 * LANES
            w = {
                "dq": w_dq[o].astype(BF16),
                "qg": q_norm_g[o].reshape(1, -1),
                "uq": _pad_heads(w_uq[o], QK_NOPE, QK_ROPE).astype(BF16),
                "dkv": jnp.pad(w_dkv[o], ((0, 0), (0, KV_LORA + LANES - w_dkv.shape[2]))).astype(BF16),
                "kvg": kv_norm_g[o].reshape(1, -1),
                "uk": w_uk[o].reshape(KV_LORA, MLA_HEADS * QK_NOPE).astype(BF16),
                "uv": w_uv[o].reshape(KV_LORA, MLA_HEADS * V_DIM).astype(BF16),
                "ukt": jnp.transpose(w_uk[o], (1, 2, 0)).astype(BF16),
            }
            w_o = w_o_mla[o].astype(BF16)
            qp, kp, vp, ckv_p, kpe_p = _mla_proj(xp, _pe_tables(pos_p), w, tl["tok"], lp // tl["tok"], False)
            att = _flash_prompt(qp, kp, vp, bp, lp, tl["attn"], tl["attn_k"])
            outs["ckv_p"].append(ckv_p.reshape(bp, lp, KV_LORA))
            outs["kpe_p"].append(kpe_p.reshape(bp, lp, QK_ROPE))
            mixer_out = (att, w_o, ln_mix_g[layer], ln_mix_b[layer])
            ql, qpe, ckv_s, kpe_s = _mla_proj(xs, _pe_tables(pos_s_rows), w, ns, 1, True)
            rows = ls * MLA_HEADS

            def per_batch(a, width):
                return jnp.transpose(a.reshape(ls, bs, MLA_HEADS, width), (1, 0, 2, 3)).reshape(bs, rows, width)

            ql_b = per_batch(ql, KV_LORA)
            qp_b = per_batch(qpe, LANES)[:, :, 0:QK_ROPE]
            ckv_sb = jnp.swapaxes(ckv_s.reshape(ls, bs, KV_LORA), 0, 1)
            kpe_sb = jnp.swapaxes(kpe_s.reshape(ls, bs, QK_ROPE), 0, 1)
            cn = jnp.pad(ckv_sb, ((0, 0), (0, PAGE_SIZE - ls), (0, 0)))
            knt = jnp.swapaxes(jnp.pad(kpe_sb, ((0, 0), (0, PAGE_SIZE - ls), (0, 0))), 1, 2)
            o_lat = _decode_attention(page_table, ql_b, qp_b, cn, knt, cache_ckv,
                                      jnp.swapaxes(cache_kpe, 2, 3), o, tl["pages"])
            outs["ckv_s"].append(ckv_sb)
            outs["kpe_s"].append(kpe_sb)
            ol = jnp.transpose(o_lat.reshape(bs, ls, MLA_HEADS, KV_LORA), (2, 1, 0, 3)).reshape(MLA_HEADS, ns, KV_LORA)
            wuv3 = jnp.transpose(w_uv[o], (1, 0, 2)).astype(BF16)
            xs = _decode_out(ol, wuv3, w_o, xs, ln_mix_g[layer], ln_mix_b[layer])
        xp, st_p = _ffn_prompt(xp, bp, lp, layer, wup_all, wd_all, conv_w[layer], conv_b[layer],
                               ln_ffn_g[layer], ln_ffn_b[layer], tl["ffn"], tl["ff_chunk"], mixer_out)
        outs["conv_p"].append(st_p[:, SUBLANES - (CONV_W - 1):, :])
        st_s = jnp.swapaxes(state_conv[layer], 0, 1)
        xs, st_s_new = _ffn_sample(xs, st_s, layer, wup_all, wd_all, conv_w[layer], conv_b[layer],
                                   ln_ffn_g[layer], ln_ffn_b[layer], bs, ls, tl["ff_chunk"])
        outs["conv_s"].append(jnp.swapaxes(st_s_new, 0, 1))

    y_p = xp.reshape(bp, lp, d)
    y_s = jnp.swapaxes(xs.reshape(ls, bs, d), 0, 1)
    return (y_p, y_s,
            jnp.stack(outs["pool_p"]), jnp.stack(outs["pool_s"]),
            jnp.stack(outs["ret_p"]), jnp.stack(outs["ret_s"]),
            jnp.stack(outs["ckv_p"]), jnp.stack(outs["ckv_s"]),
            jnp.stack(outs["kpe_p"]), jnp.stack(outs["kpe_s"]),
            jnp.stack(outs["conv_p"]), jnp.stack(outs["conv_s"]))
```

```python
import functools

import jax
import jax.numpy as jnp
import numpy as np
from jax import lax
from jax.experimental import pallas as pl
from jax.experimental.pallas import tpu as pltpu

F32 = jnp.float32
BF16 = jnp.bfloat16

PAGE_SIZE = 128
POOL_WINDOWS = (2, 4, 8, 16)
POOL_HIST = max(POOL_WINDOWS) - 1
RET_HEADS = 4
RET_DK = 128
MLA_HEADS = 8
QK_NOPE = 128
QK_ROPE = 64
V_DIM = 128
KV_LORA = 256
CONV_W = 3
DEPTH = 2
ALPHA = (2.0 * DEPTH) ** 0.25
ROPE_THETA = 10000.0
LN_EPS = 1e-5
RMS_EPS = 1e-6
GN_EPS = 1e-6
MLA_SCALE = (QK_NOPE + QK_ROPE) ** -0.5
LOG2E = 1.4426950408889634
SCALE_LOG2E = MLA_SCALE * LOG2E

LANES = 128
SUBLANES = 8
MIB = 1024 * 1024


_VMEM_LIMIT_MIB = {
    "even_layer_prompt": 40, "even_layer_sample": 48, "ffn_prompt": 56,
    "ffn_sample": 40, "mla_proj": 48, "mla_proj_decode": 40, "flash_prompt": 56, "decode_attention": 56,
    "decode_out": 24,
}


def _cparams(sem, call):
    return pltpu.CompilerParams(dimension_semantics=sem, vmem_limit_bytes=_VMEM_LIMIT_MIB[call] * MIB)


def _resident(shape):
    nd = len(shape)
    return pl.BlockSpec(shape, lambda *_: (0,) * nd, pipeline_mode=pl.Buffered(1))


def _resident_layer(stacked_shape, layer):
    nd = len(stacked_shape) - 1
    return pl.BlockSpec((None,) + tuple(stacked_shape[1:]), lambda *_: (layer,) + (0,) * nd,
                        pipeline_mode=pl.Buffered(1))


def _whole(shape):
    nd = len(shape)
    return pl.BlockSpec(shape, lambda *_: (0,) * nd)


def _dot(a, b):
    return jnp.dot(a, b, preferred_element_type=F32)


def _dot_nt(a, b):
    return lax.dot_general(a, b, (((1,), (1,)), ((), ())), preferred_element_type=F32)


def _dot_tn(a, b):
    return lax.dot_general(a, b, (((0,), (0,)), ((), ())), preferred_element_type=F32)


def _layer_norm(z, g, b):
    mu = jnp.mean(z, axis=-1, keepdims=True)
    d = z - mu
    var = jnp.mean(d * d, axis=-1, keepdims=True)
    return d * lax.rsqrt(var + LN_EPS) * g + b


def _silu(x):
    return x * jax.nn.sigmoid(x)


def _rope_full(x, c2, s2):
    return x * c2 + pltpu.roll(x, 64, axis=1) * s2


def _group_norm_gate(o, gate, gn_row):
    mu = jnp.mean(o, axis=-1, keepdims=True)
    d = o - mu
    var = jnp.mean(d * d, axis=-1, keepdims=True)
    return _silu(gate) * (d * lax.rsqrt(var + GN_EPS) * gn_row)


def _even_layer_prompt_body(x_ref, win_ref, c2_ref, s2_ref, dec_ref, qd_ref, kd_ref, gc_ref, pw_ref, ps_ref,
                            gn_ref, wo_ref, lg_ref, lb_ref, o_ref, pst_ref, rst_ref,
                            mix_ref, ext_ref, s_ref, *h_refs, tm, n_j, sub):
    j = pl.program_id(1)
    pd = len(POOL_WINDOWS) * LANES

    @pl.when(j == 0)
    def _():
        ext_ref[0:16, :] = jnp.zeros((16, pd), F32)
        s_ref[...] = jnp.zeros(s_ref.shape, F32)

    @pl.when(j > 0)
    def _():
        ext_ref[0:16, :] = ext_ref[tm:tm + 16, :]

    c = dec_ref.shape[1]
    k_scale = RET_DK ** -0.5
    nblk = 4 * LANES
    def project(r0):
        h_ref = h_refs[r0 // sub]
        xb = x_ref[r0:r0 + sub, :].astype(BF16)
        for c0 in range(0, h_ref.shape[1], nblk):
            h_ref[:, c0:c0 + nblk] = _dot(xb, win_ref[:, c0:c0 + nblk])

    project(0)
    for r0 in range(0, tm, sub):
        rb = slice(r0, r0 + sub)
        h_ref = h_refs[r0 // sub]
        if r0 + sub < tm:
            project(r0 + sub)
        ext_ref[16 + r0:16 + r0 + sub, :] = h_ref[:, 0:pd]

        pos = (j * tm + r0 + lax.broadcasted_iota(jnp.int32, (sub, 1), 0)).astype(F32)
        for g, w in enumerate(POOL_WINDOWS):
            cols = slice(g * LANES, (g + 1) * LANES)
            e = ext_ref[r0:r0 + sub + 16, cols]
            u = e[16:, :]
            s = 1
            while s < w:
                e = e + pltpu.roll(e, s, axis=0)
                s *= 2
            cnt = jnp.minimum(float(w), pos + 1.0)
            pooled = e[16:, :] / cnt - u
            mixed = _dot(pooled.astype(BF16), pw_ref[g]) * ps_ref[:, cols]
            mix_ref[rb, cols] = mixed.astype(BF16)

        for ci in range(r0 // c, (r0 + sub) // c):
            rows = slice(ci * c, (ci + 1) * c)
            lrows = slice(ci * c - r0, (ci + 1) * c - r0)
            c2 = c2_ref[rows, :]
            s2 = s2_ref[rows, :]
            for hd in range(RET_HEADS):
                def col(part, hd=hd):
                    return slice(pd + (part * RET_HEADS + hd) * LANES, pd + (part * RET_HEADS + hd + 1) * LANES)
                q = _rope_full(h_ref[lrows, col(0)], c2, s2)
                k = _rope_full(h_ref[lrows, col(1)], c2, s2) * k_scale
                vb = h_ref[lrows, col(2)].astype(BF16)
                gate = h_ref[lrows, col(3)]
                st = s_ref[hd]
                sc = _dot_nt(q.astype(BF16), k.astype(BF16)) * dec_ref[hd]
                o = _dot(sc.astype(BF16), vb)
                o = o + _dot((q * qd_ref[hd]).astype(BF16), st.astype(BF16))
                s_ref[hd] = gc_ref[hd] * st + _dot_tn((k * kd_ref[hd]).astype(BF16), vb)
                ret = _group_norm_gate(o, gate, gn_ref[:, hd * LANES:(hd + 1) * LANES])
                mix_ref[rows, pd + hd * LANES:pd + (hd + 1) * LANES] = ret.astype(BF16)

        y = _dot(mix_ref[rb, :], wo_ref[...])
        o_ref[rb, :] = _layer_norm(ALPHA * x_ref[rb, :] + y, lg_ref[...], lb_ref[...])

    @pl.when(j == n_j - 1)
    def _():
        pst_ref[0] = ext_ref[pl.ds(tm + 1, POOL_HIST), :]
        rst_ref[0] = s_ref[...]


def _even_layer_prompt(x, bsz, seq, w_in, tabs, pool_w, pool_scale, gn_g, w_o, ln_g, ln_b, tm):
    n_j = seq // tm
    d = x.shape[1]
    pd = pool_scale.shape[0]
    ed = pd + RET_HEADS * LANES
    chunk = tabs["dec"].shape[1]
    sub = min(tm, max(chunk, 2 * LANES))
    body = functools.partial(_even_layer_prompt_body, tm=tm, n_j=n_j, sub=sub)
    return pl.pallas_call(
        body,
        grid=(bsz, n_j),
        in_specs=[
            pl.BlockSpec((tm, d), lambda b, j: (b * n_j + j, 0)),
            _resident(w_in.shape),
            pl.BlockSpec((tm, LANES), lambda b, j: (j, 0)),
            pl.BlockSpec((tm, LANES), lambda b, j: (j, 0)),
            _resident((RET_HEADS, chunk, chunk)),
            _resident((RET_HEADS, chunk, LANES)),
            _resident((RET_HEADS, chunk, LANES)),
            _resident((RET_HEADS, 1, LANES)),
            _resident(pool_w.shape),
            _resident((1, pd)),
            _resident((1, RET_HEADS * LANES)),
            _resident(w_o.shape),
            _resident((1, d)),
            _resident((1, d)),
        ],
        out_specs=[
            pl.BlockSpec((tm, d), lambda b, j: (b * n_j + j, 0)),
            pl.BlockSpec((1, POOL_HIST, pd), lambda b, j: (b, 0, 0)),
            pl.BlockSpec((1, RET_HEADS, RET_DK, LANES), lambda b, j: (b, 0, 0, 0)),
        ],
        out_shape=[
            jax.ShapeDtypeStruct((bsz * seq, d), F32),
            jax.ShapeDtypeStruct((bsz, POOL_HIST, pd), F32),
            jax.ShapeDtypeStruct((bsz, RET_HEADS, RET_DK, LANES), F32),
        ],
        scratch_shapes=[pltpu.VMEM((tm, ed), BF16), pltpu.VMEM((tm + 16, pd), F32),
                        pltpu.VMEM((RET_HEADS, RET_DK, LANES), F32)]
        + [pltpu.VMEM((sub, w_in.shape[1]), F32)] * (tm // sub),
        compiler_params=_cparams(("parallel", "arbitrary"), "even_layer_prompt"),
        name="even_layer_prompt",
    )(x, w_in, tabs["c2"], tabs["s2"], tabs["dec"], tabs["qd"], tabs["kd"], tabs["gc"],
      pool_w, pool_scale.reshape(1, pd), gn_g.reshape(1, -1), w_o, ln_g.reshape(1, d), ln_b.reshape(1, d))


def _even_layer_sample_body(x_ref, win_ref, hist_ref, s0_ref, c2_ref, s2_ref, dm_ref, qd_ref, kd_ref, gc_ref,
                            pw_ref, ps_ref, gn_ref, wo_ref, lg_ref, lb_ref, o_ref, hist_o_ref, s_o_ref,
                            h_ref, mix_ref, oc_ref, *, nb, ls, cnts):
    pd = len(POOL_WINDOWS) * LANES
    x = x_ref[...]
    h_ref[...] = _dot(x.astype(BF16), win_ref[...])
    ext = [hist_ref[i] for i in range(POOL_HIST)]
    ext += [h_ref[l * nb:(l + 1) * nb, 0:pd] for l in range(ls)]
    for i in range(POOL_HIST):
        hist_o_ref[i] = ext[ls + i]
    for g, w in enumerate(POOL_WINDOWS):
        cols = slice(g * LANES, (g + 1) * LANES)
        outs = []
        for l in range(ls):
            top = POOL_HIST + l
            acc = ext[top][:, cols]
            for jj in range(1, w):
                acc = acc + ext[top - jj][:, cols]
            outs.append(acc / cnts[g][l] - ext[top][:, cols])
        pooled = jnp.concatenate(outs, axis=0)
        mixed = _dot(pooled.astype(BF16), pw_ref[g]) * ps_ref[:, cols]
        mix_ref[:, cols] = mixed.astype(BF16)

    rows_b = lax.broadcasted_iota(jnp.int32, (ls * nb, 1), 0) % nb
    k_scale = RET_DK ** -0.5
    c2 = c2_ref[...]
    s2 = s2_ref[...]
    qs, ks, vs = [], [], []
    for hd in range(RET_HEADS):
        def col(part, hd=hd):
            return slice(pd + (part * RET_HEADS + hd) * LANES, pd + (part * RET_HEADS + hd + 1) * LANES)
        q = _rope_full(h_ref[:, col(0)], c2, s2)
        k = _rope_full(h_ref[:, col(1)], c2, s2) * k_scale
        vb = h_ref[:, col(2)].astype(BF16)
        sc = _dot_nt(q.astype(BF16), k.astype(BF16)) * dm_ref[hd]
        oc_ref[hd] = _dot(sc.astype(BF16), vb)
        qs.append(q * qd_ref[hd])
        ks.append((k * kd_ref[hd]).T)
        vs.append(vb)

    cols_b = lax.broadcasted_iota(jnp.int32, (1, ls * nb), 1) % nb

    def per_batch(b, carry):
        sel = rows_b == b
        sel_t = cols_b == b
        for hd in range(RET_HEADS):
            st = s0_ref[b, hd]
            qm = jnp.where(sel, qs[hd], 0.0).astype(BF16)
            km_t = jnp.where(sel_t, ks[hd], 0.0).astype(BF16)
            oc_ref[hd] += _dot(qm, st.astype(BF16))
            s_o_ref[b, hd] = gc_ref[hd] * st + _dot(km_t, vs[hd])
        return carry

    lax.fori_loop(0, nb, per_batch, 0)

    for hd in range(RET_HEADS):
        gate = h_ref[:, pd + (3 * RET_HEADS + hd) * LANES:pd + (3 * RET_HEADS + hd + 1) * LANES]
        ret = _group_norm_gate(oc_ref[hd], gate, gn_ref[:, hd * LANES:(hd + 1) * LANES])
        mix_ref[:, pd + hd * LANES:pd + (hd + 1) * LANES] = ret.astype(BF16)

    o_ref[...] = _layer_norm(ALPHA * x + _dot(mix_ref[...], wo_ref[...]), lg_ref[...], lb_ref[...])


def _even_layer_sample(x, w_in, hist, s0, tabs, pool_w, pool_scale, gn_g, w_o, ln_g, ln_b, nb, ls, past_len):
    d = x.shape[1]
    pd = pool_scale.shape[0]
    ed = pd + RET_HEADS * LANES
    cnts = tuple(tuple(float(min(w, past_len + l + 1)) for l in range(ls)) for w in POOL_WINDOWS)
    body = functools.partial(_even_layer_sample_body, nb=nb, ls=ls, cnts=cnts)
    n = ls * nb
    args = (x, w_in, hist, s0, tabs["c2"], tabs["s2"], tabs["dm"], tabs["qd"], tabs["kd"], tabs["gc"],
            pool_w, pool_scale.reshape(1, pd), gn_g.reshape(1, -1), w_o, ln_g.reshape(1, d), ln_b.reshape(1, d))
    return pl.pallas_call(
        body,
        grid=(1,),
        in_specs=[_resident(a.shape) for a in args],
        out_specs=[_whole((n, d)), _whole(hist.shape), _whole(s0.shape)],
        out_shape=[jax.ShapeDtypeStruct((n, d), F32),
                   jax.ShapeDtypeStruct(hist.shape, F32),
                   jax.ShapeDtypeStruct(s0.shape, F32)],
        scratch_shapes=[pltpu.VMEM((n, w_in.shape[1]), F32), pltpu.VMEM((n, ed), BF16),
                        pltpu.VMEM((RET_HEADS, n, LANES), F32)],
        compiler_params=_cparams(("arbitrary",), "even_layer_sample"),
        name="even_layer_sample",
    )(*args)


def _ffn_chunk(xb, wup_ref, wd_ref, cw_ref, cb_ref, c, tf, dff, shift_fn):
    cols = slice(c * tf, (c + 1) * tf)
    a = _dot(xb, wup_ref[:, cols])
    gate_in = _dot(xb, wup_ref[:, dff + c * tf:dff + (c + 1) * tf])
    a1, a2 = shift_fn(a, c)
    conv = cb_ref[:, cols] + cw_ref[0:1, cols] * a2
    conv = conv + cw_ref[1:2, cols] * a1
    conv = conv + cw_ref[2:3, cols] * a
    act = (_silu(conv) * gate_in).astype(BF16)
    return a, _dot(act, wd_ref[cols, :])


def _ffn_prompt_body(x_ref, *rest, tm, n_j, tf, dff, mixer_out):
    if mixer_out:
        a_ref, wo_ref, mg_ref, mb_ref = rest[:4]
        rest = rest[4:]
    wup_ref, wd_ref, cw_ref, cb_ref, g_ref, b_ref, o_ref, st_ref, carry_ref = rest
    j = pl.program_id(1)

    @pl.when(j == 0)
    def _():
        carry_ref[...] = jnp.zeros(carry_ref.shape, F32)

    x = x_ref[...]
    if mixer_out:
        x = _layer_norm(ALPHA * x + _dot(a_ref[...], wo_ref[...]), mg_ref[...], mb_ref[...])
    row = lax.broadcasted_iota(jnp.int32, (tm, tf), 0)

    def shift_fn(a, c):
        prev = carry_ref[:, c * tf:(c + 1) * tf]
        a1 = jnp.where(row == 0, prev[7:8, :], pltpu.roll(a, 1, axis=0))
        a2 = jnp.where(row == 0, prev[6:7, :], jnp.where(row == 1, prev[7:8, :], pltpu.roll(a, 2, axis=0)))
        return a1, a2

    xb = x.astype(BF16)
    acc = None
    for c in range(dff // tf):
        a, y = _ffn_chunk(xb, wup_ref, wd_ref, cw_ref, cb_ref, c, tf, dff, shift_fn)
        acc = y if acc is None else acc + y
        tail = a[tm - SUBLANES:tm, :]
        carry_ref[:, c * tf:(c + 1) * tf] = tail
        st_ref[0, :, c * tf:(c + 1) * tf] = tail
    o_ref[...] = _layer_norm(ALPHA * x + acc, g_ref[...], b_ref[...])


def _ffn_prompt(x, bsz, seq, layer, wup, wd, conv_w, conv_b, g, b, tm, tf, mixer_out=None):
    d = x.shape[1]
    dff = wd.shape[1]
    n_j = seq // tm
    body = functools.partial(_ffn_prompt_body, tm=tm, n_j=n_j, tf=tf, dff=dff, mixer_out=mixer_out is not None)
    row_tile = lambda bi, j: (bi * n_j + j, 0)
    pre_args, pre_specs = [], []
    if mixer_out is not None:
        a, w_o, mg, mb = mixer_out
        pre_args = [a, w_o, mg.reshape(1, d), mb.reshape(1, d)]
        pre_specs = [pl.BlockSpec((tm, a.shape[1]), row_tile), _resident(w_o.shape), _resident((1, d)), _resident((1, d))]
    return pl.pallas_call(
        body,
        grid=(bsz, n_j),
        in_specs=[pl.BlockSpec((tm, d), row_tile)] + pre_specs + [
                  _resident_layer(wup.shape, layer), _resident_layer(wd.shape, layer), _resident(conv_w.shape),
                  _resident((1, dff)), _resident((1, d)), _resident((1, d))],
        out_specs=[pl.BlockSpec((tm, d), row_tile),
                   pl.BlockSpec((1, SUBLANES, dff), lambda bi, j: (bi, 0, 0))],
        out_shape=[jax.ShapeDtypeStruct(x.shape, F32),
                   jax.ShapeDtypeStruct((bsz, SUBLANES, dff), F32)],
        scratch_shapes=[pltpu.VMEM((SUBLANES, dff), F32)],
        compiler_params=_cparams(("parallel", "arbitrary"), "ffn_prompt"),
        name="ffn_prompt",
    )(x, *pre_args, wup, wd, conv_w, conv_b.reshape(1, dff), g.reshape(1, d), b.reshape(1, d))


def _ffn_sample_body(x_ref, st_ref, wup_ref, wd_ref, cw_ref, cb_ref, g_ref, b_ref, o_ref, st_o_ref,
                     *, nb, ls, tf, dff):
    x = x_ref[...]
    xb = x.astype(BF16)
    nh = CONV_W - 1

    def shift_fn(a, c):
        cols = slice(c * tf, (c + 1) * tf)
        ext = [st_ref[i, :, cols] for i in range(nh)] + [a[l * nb:(l + 1) * nb, :] for l in range(ls)]
        a1 = jnp.concatenate([ext[nh + l - 1] for l in range(ls)], axis=0)
        a2 = jnp.concatenate([ext[nh + l - 2] for l in range(ls)], axis=0)
        for i in range(nh):
            st_o_ref[i, :, cols] = ext[ls + i]
        return a1, a2

    acc = None
    for c in range(dff // tf):
        _, y = _ffn_chunk(xb, wup_ref, wd_ref, cw_ref, cb_ref, c, tf, dff, shift_fn)
        acc = y if acc is None else acc + y
    o_ref[...] = _layer_norm(ALPHA * x + acc, g_ref[...], b_ref[...])


def _ffn_sample(x, st, layer, wup, wd, conv_w, conv_b, g, b, nb, ls, tf):
    d = x.shape[1]
    dff = wd.shape[1]
    body = functools.partial(_ffn_sample_body, nb=nb, ls=ls, tf=tf, dff=dff)
    args = (x, st, wup, wd, conv_w, conv_b.reshape(1, dff), g.reshape(1, d), b.reshape(1, d))
    in_specs = [_resident(a.shape) for a in args]
    in_specs[2] = _resident_layer(wup.shape, layer)
    in_specs[3] = _resident_layer(wd.shape, layer)
    return pl.pallas_call(
        body,
        grid=(1,),
        in_specs=in_specs,
        out_specs=[_whole(x.shape), _whole(st.shape)],
        out_shape=[jax.ShapeDtypeStruct(x.shape, F32), jax.ShapeDtypeStruct(st.shape, F32)],
        compiler_params=_cparams(("arbitrary",), "ffn_sample"),
        name="ffn_sample",
    )(*args)


def _rope_pe(blk, cc, s1, s2):
    return blk * cc + pltpu.roll(blk, 96, axis=1) * s1 + pltpu.roll(blk, 32, axis=1) * s2


def _rms_norm(x, g):
    ms = jnp.mean(x * x, axis=-1, keepdims=True)
    return x * lax.rsqrt(ms + RMS_EPS) * g


def _mla_proj_body(x_ref, cc_ref, s1_ref, s2_ref, wdq_ref, qg_ref, wuq_ref, wdkv_ref, kvg_ref, *rest,
                   decode, sub):
    hw = 2 * LANES
    for r0 in range(0, x_ref.shape[0], sub):
        rb = slice(r0, r0 + sub)
        cc = cc_ref[rb, :]
        s1 = s1_ref[rb, :]
        s2 = s2_ref[rb, :]
        xb = x_ref[rb, :].astype(BF16)
        cq = _rms_norm(_dot(xb, wdq_ref[...]), qg_ref[...])
        q = _dot(cq.astype(BF16), wuq_ref[...])
        kv = _dot(xb, wdkv_ref[...])
        ckv = _rms_norm(kv[:, 0:KV_LORA], kvg_ref[...])
        kpe = _rope_pe(kv[:, KV_LORA:KV_LORA + LANES], cc, s1, s2)
        if decode:
            wukt_ref, ql_ref, qp_ref, ckv_ref, kpe_ref = rest
            for h in range(MLA_HEADS):
                qn = q[:, h * hw:h * hw + LANES].astype(BF16)
                ql_ref[rb, h * KV_LORA:(h + 1) * KV_LORA] = _dot(qn, wukt_ref[h]).astype(BF16)
                qp = _rope_pe(q[:, h * hw + LANES:(h + 1) * hw], cc, s1, s2)
                qp_ref[rb, h * LANES:(h + 1) * LANES] = qp.astype(BF16)
        else:
            wuk_ref, wuv_ref, qo_ref, ko_ref, vo_ref, ckv_ref, kpe_ref = rest
            cb = ckv.astype(BF16)
            kn = _dot(cb, wuk_ref[...])
            vo_ref[rb, :] = _dot(cb, wuv_ref[...]).astype(BF16)
            kpb = kpe.astype(BF16)
            for h in range(MLA_HEADS):
                qo_ref[rb, h * hw:h * hw + LANES] = q[:, h * hw:h * hw + LANES].astype(BF16)
                qp = _rope_pe(q[:, h * hw + LANES:(h + 1) * hw], cc, s1, s2)
                qo_ref[rb, h * hw + LANES:(h + 1) * hw] = qp.astype(BF16)
                ko_ref[rb, h * hw:h * hw + LANES] = kn[:, h * LANES:(h + 1) * LANES].astype(BF16)
                ko_ref[rb, h * hw + LANES:(h + 1) * hw] = kpb
        ckv_ref[rb, :] = ckv
        kpe_ref[rb, :] = kpe[:, 0:QK_ROPE]


def _mla_proj(x, tabs, w, tm, n_pos_blocks, decode):
    t, d = x.shape
    hw = 2 * LANES
    body = functools.partial(_mla_proj_body, decode=decode, sub=tm)
    row = lambda i: (i, 0)
    tab = lambda i: (i % n_pos_blocks, 0)
    ins = [x, tabs["cc"], tabs["s1"], tabs["s2"], w["dq"], w["qg"], w["uq"], w["dkv"], w["kvg"]]
    in_specs = [pl.BlockSpec((tm, d), row)] + [pl.BlockSpec((tm, LANES), tab)] * 3
    in_specs += [_resident(a.shape) for a in ins[4:]]
    if decode:
        ins += [w["ukt"]]
        in_specs += [_resident(w["ukt"].shape)]
        outs = [(MLA_HEADS * KV_LORA, BF16), (MLA_HEADS * LANES, BF16)]
    else:
        ins += [w["uk"], w["uv"]]
        in_specs += [_resident(w["uk"].shape), _resident(w["uv"].shape)]
        outs = [(MLA_HEADS * hw, BF16), (MLA_HEADS * hw, BF16), (MLA_HEADS * V_DIM, BF16)]
    outs += [(KV_LORA, F32), (QK_ROPE, F32)]
    return pl.pallas_call(
        body,
        grid=(t // tm,),
        in_specs=in_specs,
        out_specs=[pl.BlockSpec((tm, n), row) for n, _ in outs],
        out_shape=[jax.ShapeDtypeStruct((t, n), dt) for n, dt in outs],
        compiler_params=_cparams(("parallel",), "mla_proj_decode" if decode else "mla_proj"),
        name="mla_proj_decode" if decode else "mla_proj",
    )(*ins)


def _online_softmax_update(s2, m_ref, l_ref, acc_ref, pv_fn, row_chunk):
    rows, width = s2.shape
    n = width // LANES
    aw = acc_ref.shape[-1] // LANES
    p_chunks, alphas = [], []
    for r0 in range(0, rows, row_chunk):
        rs_ = slice(r0, r0 + row_chunk)
        tiles = [s2[rs_, j * LANES:(j + 1) * LANES] for j in range(n)]
        mx = tiles[0]
        for t in tiles[1:]:
            mx = jnp.maximum(mx, t)
        m_prev = m_ref[rs_, :]
        m_new = jnp.maximum(m_prev, jnp.max(mx, axis=-1, keepdims=True))
        alpha = jnp.exp2(m_prev - m_new)
        ps = [jnp.exp2(t - m_new) for t in tiles]
        if l_ref is not None:
            tot = ps[0]
            for t in ps[1:]:
                tot = tot + t
            l_ref[rs_, :] = alpha * l_ref[rs_, :] + jnp.sum(tot, axis=-1, keepdims=True)
        m_ref[rs_, :] = m_new
        p_chunks.append((jnp.concatenate(ps, axis=1) if n > 1 else ps[0]).astype(BF16))
        alphas.append(alpha)
    p = jnp.concatenate(p_chunks, axis=0) if len(p_chunks) > 1 else p_chunks[0]
    pv = pv_fn(p)
    for i, r0 in enumerate(range(0, rows, row_chunk)):
        rs_ = slice(r0, r0 + row_chunk)
        a_w = alphas[i] if aw == 1 else jnp.concatenate([alphas[i]] * aw, axis=1)
        acc_ref[rs_, :] = a_w * acc_ref[rs_, :] + pv[rs_, :]


def _flash_body(q_ref, k_ref, v_ref, o_ref, m_ref, acc_ref, *, tq, tk, row_chunk):
    qi = pl.program_id(1)
    hw = 2 * LANES
    m_ref[...] = jnp.full(m_ref.shape, -jnp.inf, F32)
    acc_ref[...] = jnp.zeros(acc_ref.shape, F32)
    ones = jnp.ones((tk, LANES), BF16)

    def step(h, key0, diag_off):
        start = pl.multiple_of(key0, tk)
        q = q_ref[:, h * hw:(h + 1) * hw]
        k = k_ref[pl.ds(start, tk), h * hw:(h + 1) * hw]
        v1 = jnp.concatenate([v_ref[pl.ds(start, tk), h * V_DIM:(h + 1) * V_DIM], ones], axis=1)
        s = _dot_nt(q, k) * SCALE_LOG2E
        if diag_off is not None:
            row = lax.broadcasted_iota(jnp.int32, s.shape, 0)
            colm = lax.broadcasted_iota(jnp.int32, s.shape, 1)
            s = jnp.where(colm + diag_off <= row, s, -jnp.inf)
        _online_softmax_update(s, m_ref.at[h], None, acc_ref.at[h], lambda p: _dot(p, v1), row_chunk)

    def loop_body(kj, carry):
        for h in range(MLA_HEADS):
            step(h, kj * tk, None)
        return carry

    lax.fori_loop(0, qi * (tq // tk), loop_body, 0)
    for h in range(MLA_HEADS):
        for j in range(tq // tk):
            step(h, qi * tq + j * tk, j * tk)
        o_ref[:, h * V_DIM:(h + 1) * V_DIM] = (acc_ref[h, :, 0:V_DIM] / acc_ref[h, :, V_DIM:]).astype(BF16)


def _flash_prompt(qp, kp, vp, bsz, seq, tq, tk):
    nq = seq // tq
    body = functools.partial(_flash_body, tq=tq, tk=tk, row_chunk=min(tq, 64))
    return pl.pallas_call(
        body,
        grid=(bsz, nq),
        in_specs=[pl.BlockSpec((tq, qp.shape[1]), lambda b, i: (b * nq + i, 0)),
                  pl.BlockSpec((seq, kp.shape[1]), lambda b, i: (b, 0)),
                  pl.BlockSpec((seq, vp.shape[1]), lambda b, i: (b, 0))],
        out_specs=pl.BlockSpec((tq, vp.shape[1]), lambda b, i: (b * nq + i, 0)),
        out_shape=jax.ShapeDtypeStruct(vp.shape, BF16),
        scratch_shapes=[pltpu.VMEM((MLA_HEADS, tq, LANES), F32), pltpu.VMEM((MLA_HEADS, tq, V_DIM + LANES), F32)],
        compiler_params=_cparams(("parallel", "arbitrary"), "flash_prompt"),
        name="flash_prompt",
    )(qp, kp, vp)


def _decode_body(pt_ref, ql_ref, qp_ref, cn_ref, kn_ref, ckv_hbm, kpe_hbm, o_ref,
                 cbuf_ref, rbuf_ref, sem_ref, m_ref, l_ref, acc_ref, *, gp, ls, layer, n_split, n_b, n_g, n_buf):
    b = pl.program_id(0)
    g = pl.program_id(1)
    t = b * n_g + g
    slot = lax.rem(t, n_buf)
    ahead = n_buf - 1

    def page_copies(bb, grp, slot):
        cps = []
        for i in range(gp):
            pg = pt_ref[bb, grp * gp + i]
            keys = pl.ds(i * PAGE_SIZE, PAGE_SIZE)
            cps.append(pltpu.make_async_copy(ckv_hbm.at[layer, pg], cbuf_ref.at[slot, keys, :], sem_ref.at[slot, 0]))
            cps.append(pltpu.make_async_copy(kpe_hbm.at[layer, pg], rbuf_ref.at[slot, :, keys], sem_ref.at[slot, 1]))
        return cps

    def start(step):
        for cp in page_copies(lax.div(step, n_g), lax.rem(step, n_g), lax.rem(step, n_buf)):
            cp.start()

    @pl.when(t == 0)
    def _():
        for s in range(min(ahead, n_b * n_g)):
            start(jnp.int32(s))

    @pl.when(t + ahead < n_b * n_g)
    def _():
        start(t + ahead)

    @pl.when(g == 0)
    def _():
        m_ref[...] = jnp.full(m_ref.shape, -jnp.inf, F32)
        l_ref[...] = jnp.zeros(l_ref.shape, F32)
        acc_ref[...] = jnp.zeros(acc_ref.shape, F32)

    ql = ql_ref[0]
    qp = qp_ref[0]

    for cp in page_copies(b, g, slot):
        cp.wait()

    kc_len = gp * PAGE_SIZE // n_split
    kbs, scores = [], []
    for c in range(n_split):
        ks = slice(c * kc_len, (c + 1) * kc_len)
        kb = cbuf_ref[slot, ks, :].astype(BF16)
        scores.append((_dot_nt(ql, kb) + _dot(qp, rbuf_ref[slot, :, ks].astype(BF16))) * SCALE_LOG2E)
        kbs.append(kb)
    state = (m_ref.at[0], l_ref.at[0], acc_ref.at[0])
    n_rows = ql.shape[0]
    for c in range(n_split):
        _online_softmax_update(scores[c], *state, lambda p, kb=kbs[c]: _dot(p, kb), n_rows)

    @pl.when(g == n_g - 1)
    def _():
        kc = cn_ref[0].astype(BF16)
        s = (_dot_nt(ql, kc) + _dot(qp, kn_ref[0].astype(BF16))) * SCALE_LOG2E
        r = lax.broadcasted_iota(jnp.int32, s.shape, 0) // MLA_HEADS
        cidx = lax.broadcasted_iota(jnp.int32, s.shape, 1)
        s = jnp.where((cidx <= r) & (cidx < ls), s, -jnp.inf)
        _online_softmax_update(s, *state, lambda p: _dot(p, kc), n_rows)
        l_w = jnp.concatenate([l_ref[0]] * (KV_LORA // LANES), axis=1)
        o_ref[0] = acc_ref[0] / l_w


def _decode_attention(page_table, ql, qp, cn, knt, cache_ckv, cache_kpet, layer, gp):
    nb, rows, _ = ql.shape
    n_pages = page_table.shape[1]
    ls = rows // MLA_HEADS
    n_split = 4 if gp % 4 == 0 else 1
    n_g = n_pages // gp
    n_buf = 3
    body = functools.partial(_decode_body, gp=gp, ls=ls, layer=layer, n_split=n_split, n_b=nb, n_g=n_g, n_buf=n_buf)
    per_b = lambda b, g, pt: (b, 0, 0)
    in_specs = [pl.BlockSpec((1, rows, KV_LORA), per_b), pl.BlockSpec((1, rows, QK_ROPE), per_b),
                pl.BlockSpec((1, PAGE_SIZE, KV_LORA), per_b), pl.BlockSpec((1, QK_ROPE, PAGE_SIZE), per_b),
                pl.BlockSpec(memory_space=pl.ANY), pl.BlockSpec(memory_space=pl.ANY)]
    grid_spec = pltpu.PrefetchScalarGridSpec(
        num_scalar_prefetch=1,
        grid=(nb, n_g),
        in_specs=in_specs,
        out_specs=pl.BlockSpec((1, rows, KV_LORA), per_b),
        scratch_shapes=[pltpu.VMEM((n_buf, gp * PAGE_SIZE, KV_LORA), F32),
                        pltpu.VMEM((n_buf, QK_ROPE, gp * PAGE_SIZE), F32),
                        pltpu.SemaphoreType.DMA((n_buf, 2)),
                        pltpu.VMEM((1, rows, LANES), F32), pltpu.VMEM((1, rows, LANES), F32),
                        pltpu.VMEM((1, rows, KV_LORA), F32)],
    )
    return pl.pallas_call(
        body,
        grid_spec=grid_spec,
        out_shape=jax.ShapeDtypeStruct((nb, rows, KV_LORA), F32),
        compiler_params=_cparams(("arbitrary", "arbitrary"), "decode_attention"),
        name="decode_attention",
    )(page_table, ql, qp, cn, knt, cache_ckv, cache_kpet)


def _decode_out_body(ol_ref, wuv_ref, wo_ref, r_ref, g_ref, b_ref, o_ref):
    y = None
    for h in range(MLA_HEADS):
        oh = _dot(ol_ref[h].astype(BF16), wuv_ref[h]).astype(BF16)
        t = _dot(oh, wo_ref[h * V_DIM:(h + 1) * V_DIM, :])
        y = t if y is None else y + t
    o_ref[...] = _layer_norm(ALPHA * r_ref[...] + y, g_ref[...], b_ref[...])


def _decode_out(ol, wuv3, wo, res, g, b):
    d = res.shape[1]
    args = (ol, wuv3, wo, res, g.reshape(1, d), b.reshape(1, d))
    return pl.pallas_call(
        _decode_out_body,
        grid=(1,),
        in_specs=[_resident(a.shape) for a in args],
        out_specs=_whole(res.shape),
        out_shape=jax.ShapeDtypeStruct(res.shape, F32),
        compiler_params=_cparams(("arbitrary",), "decode_out"),
        name="decode_out",
    )(*args)


def _rope_angles(pos, half):
    inv_freq = ROPE_THETA ** (-np.arange(half, dtype=np.float64) / half)
    ang = np.asarray(pos, np.float64)[:, None] * inv_freq[None, :]
    return np.cos(ang), np.sin(ang)


def _f32(tabs):
    return {k: np.ascontiguousarray(v, dtype=np.float32) for k, v in tabs.items()}


def _ret_tables(pos, chunk, rows_per_pos):
    cos, sin = _rope_angles(pos, RET_DK // 2)
    log_gamma = np.log(1.0 - 2.0 ** (-5.0 - np.arange(RET_HEADS, dtype=np.float64)))
    idx = np.repeat(np.arange(chunk, dtype=np.float64), rows_per_pos)
    diff = idx[:, None] - idx[None, :]
    dec = np.where(diff >= 0, np.exp(np.maximum(diff, 0.0)[None] * log_gamma[:, None, None]), 0.0)
    q_dec = np.exp((idx + 1.0)[None, :] * log_gamma[:, None])
    k_dec = np.exp((chunk - 1.0 - idx)[None, :] * log_gamma[:, None])
    gc = np.exp(chunk * log_gamma)
    n = idx.shape[0]
    return _f32({
        "c2": np.concatenate([cos, cos], axis=1),
        "s2": np.concatenate([-sin, sin], axis=1),
        "dec": dec,
        "qd": np.broadcast_to(q_dec[:, :, None], (RET_HEADS, n, LANES)),
        "kd": np.broadcast_to(k_dec[:, :, None], (RET_HEADS, n, LANES)),
        "gc": np.broadcast_to(gc[:, None, None], (RET_HEADS, 1, LANES)),
    })


def _pe_tables(pos):
    cos, sin = _rope_angles(pos, QK_ROPE // 2)
    z = np.zeros_like(cos)
    return _f32({"cc": np.concatenate([cos, cos, z, z], axis=1),
                 "s1": np.concatenate([-sin, z, z, z], axis=1),
                 "s2": np.concatenate([z, sin, z, z], axis=1)})


def _pad_heads(w, nope, rope):
    k = w.shape[0]
    w3 = w.reshape(k, MLA_HEADS, nope + rope)
    pad = jnp.zeros((k, MLA_HEADS, 2 * LANES - nope - rope), w.dtype)
    return jnp.concatenate([w3, pad], axis=2).reshape(k, MLA_HEADS * 2 * LANES)


def _tiles(seq, n_pages, dff):
    def fit(t, n=seq):
        while n % t:
            t //= 2
        return t
    return {"tok": fit(1024), "mix": fit(512), "ret_chunk": fit(256), "ffn": fit(512), "attn": fit(512), "attn_k": fit(512), "ff_chunk": dff,
            "pages": fit(64, n_pages)}


def kernel(x_prompt, x_sample, state_pool, state_ret, cache_ckv, cache_kpe, state_conv, page_table,
           w_in_even, pool_w, pool_scale, ret_gn_g, w_o_even,
           w_dq, q_norm_g, w_uq, w_dkv, kv_norm_g, w_uk, w_uv, w_o_mla,
           w_up, conv_w, conv_b, w_down, ln_mix_g, ln_mix_b, ln_ffn_g, ln_ffn_b):
    bp, lp, d = x_prompt.shape
    bs, ls, _ = x_sample.shape
    past_len = page_table.shape[1] * PAGE_SIZE
    depth = w_up.shape[0]
    dff = w_down.shape[1]
    tl = _tiles(lp, page_table.shape[1], dff)
    ns = bs * ls
    assert lp % tl["ret_chunk"] == 0 and tl["mix"] % tl["ret_chunk"] == 0
    assert dff % tl["ff_chunk"] == 0 and page_table.shape[1] % tl["pages"] == 0 and ls <= PAGE_SIZE

    xp = x_prompt.reshape(bp * lp, d)
    xs = jnp.swapaxes(x_sample, 0, 1).reshape(ns, d)
    pos_p = np.arange(lp)
    pos_s_rows = np.repeat(past_len + np.arange(ls), bs)

    wup_all = w_up.astype(BF16)
    wd_all = w_down.astype(BF16)
    outs = {k: [] for k in ("pool_p", "pool_s", "ret_p", "ret_s", "ckv_p", "ckv_s", "kpe_p", "kpe_s",
                            "conv_p", "conv_s")}
    for layer in range(depth):
        mixer_out = None
        if layer % 2 == 0:
            e = layer // 2
            w_in = w_in_even[e].astype(BF16)
            w_o = w_o_even[e].astype(BF16)
            pw = pool_w[e].astype(BF16)
            tp = _ret_tables(pos_p, tl["ret_chunk"], 1)
            xp, pst, rst = _even_layer_prompt(xp, bp, lp, w_in, tp, pw, pool_scale[e], ret_gn_g[e], w_o,
                                              ln_mix_g[layer], ln_mix_b[layer], tl["mix"])
            outs["pool_p"].append(pst)
            outs["ret_p"].append(rst)
            ts = _ret_tables(pos_s_rows, ls, bs)
            rb = np.arange(ns) % bs
            ts["dm"] = np.where((rb[:, None] == rb[None, :])[None], ts["dec"], np.float32(0.0))
            hist = jnp.swapaxes(state_pool[e], 0, 1)
            xs, hist_new, s_new = _even_layer_sample(xs, w_in, hist, state_ret[e], ts, pw, pool_scale[e],
                                                     ret_gn_g[e], w_o, ln_mix_g[layer], ln_mix_b[layer],
                                                     bs, ls, past_len)
            outs["pool_s"].append(jnp.swapaxes(hist_new, 0, 1))
            outs["ret_s"].append(s_new)
        else:
            o = layer // 2
            hw = 2 * LANES
            w = {
                "dq": w_dq[o].astype(BF16),
                "qg": q_norm_g[o].reshape(1, -1),
                "uq": _pad_heads(w_uq[o], QK_NOPE, QK_ROPE).astype(BF16),
                "dkv": jnp.pad(w_dkv[o], ((0, 0), (0, KV_LORA + LANES - w_dkv.shape[2]))).astype(BF16),
                "kvg": kv_norm_g[o].reshape(1, -1),
                "uk": w_uk[o].reshape(KV_LORA, MLA_HEADS * QK_NOPE).astype(BF16),
                "uv": w_uv[o].reshape(KV_LORA, MLA_HEADS * V_DIM).astype(BF16),
                "ukt": jnp.transpose(w_uk[o], (1, 2, 0)).astype(BF16),
            }
            w_o = w_o_mla[o].astype(BF16)
            qp, kp, vp, ckv_p, kpe_p = _mla_proj(xp, _pe_tables(pos_p), w, tl["tok"], lp // tl["tok"], False)
            att = _flash_prompt(qp, kp, vp, bp, lp, tl["attn"], tl["attn_k"])
            outs["ckv_p"].append(ckv_p.reshape(bp, lp, KV_LORA))
            outs["kpe_p"].append(kpe_p.reshape(bp, lp, QK_ROPE))
            mixer_out = (att, w_o, ln_mix_g[layer], ln_mix_b[layer])
            ql, qpe, ckv_s, kpe_s = _mla_proj(xs, _pe_tables(pos_s_rows), w, ns, 1, True)
            rows = ls * MLA_HEADS

            def per_batch(a, width):
                return jnp.transpose(a.reshape(ls, bs, MLA_HEADS, width), (1, 0, 2, 3)).reshape(bs, rows, width)

            ql_b = per_batch(ql, KV_LORA)
            qp_b = per_batch(qpe, LANES)[:, :, 0:QK_ROPE]
            ckv_sb = jnp.swapaxes(ckv_s.reshape(ls, bs, KV_LORA), 0, 1)
            kpe_sb = jnp.swapaxes(kpe_s.reshape(ls, bs, QK_ROPE), 0, 1)
            cn = jnp.pad(ckv_sb, ((0, 0), (0, PAGE_SIZE - ls), (0, 0)))
            knt = jnp.swapaxes(jnp.pad(kpe_sb, ((0, 0), (0, PAGE_SIZE - ls), (0, 0))), 1, 2)
            o_lat = _decode_attention(page_table, ql_b, qp_b, cn, knt, cache_ckv,
                                      jnp.swapaxes(cache_kpe, 2, 3), o, tl["pages"])
            outs["ckv_s"].append(ckv_sb)
            outs["kpe_s"].append(kpe_sb)
            ol = jnp.transpose(o_lat.reshape(bs, ls, MLA_HEADS, KV_LORA), (2, 1, 0, 3)).reshape(MLA_HEADS, ns, KV_LORA)
            wuv3 = jnp.transpose(w_uv[o], (1, 0, 2)).astype(BF16)
            xs = _decode_out(ol, wuv3, w_o, xs, ln_mix_g[layer], ln_mix_b[layer])
        xp, st_p = _ffn_prompt(xp, bp, lp, layer, wup_all, wd_all, conv_w[layer], conv_b[layer],
                               ln_ffn_g[layer], ln_ffn_b[layer], tl["ffn"], tl["ff_chunk"], mixer_out)
        outs["conv_p"].append(st_p[:, SUBLANES - (CONV_W - 1):, :])
        st_s = jnp.swapaxes(state_conv[layer], 0, 1)
        xs, st_s_new = _ffn_sample(xs, st_s, layer, wup_all, wd_all, conv_w[layer], conv_b[layer],
                                   ln_ffn_g[layer], ln_ffn_b[layer], bs, ls, tl["ff_chunk"])
        outs["conv_s"].append(jnp.swapaxes(st_s_new, 0, 1))

    y_p = xp.reshape(bp, lp, d)
    y_s = jnp.swapaxes(xs.reshape(ls, bs, d), 0, 1)
    return (y_p, y_s,
            jnp.stack(outs["pool_p"]), jnp.stack(outs["pool_s"]),
            jnp.stack(outs["ret_p"]), jnp.stack(outs["ret_s"]),
            jnp.stack(outs["ckv_p"]), jnp.stack(outs["ckv_s"]),
            jnp.stack(outs["kpe_p"]), jnp.stack(outs["kpe_s"]),
            jnp.stack(outs["conv_p"]), jnp.stack(outs["conv_s"]))
```

```python
import functools

import jax
import jax.numpy as jnp
import numpy as np
from jax import lax
from jax.experimental import pallas as pl
from jax.experimental.pallas import tpu as pltpu

F32 = jnp.float32
BF16 = jnp.bfloat16

PAGE_SIZE = 128
POOL_WINDOWS = (2, 4, 8, 16)
POOL_HIST = max(POOL_WINDOWS) - 1
RET_HEADS = 4
RET_DK = 128
MLA_HEADS = 8
QK_NOPE = 128
QK_ROPE = 64
V_DIM = 128
KV_LORA = 256
CONV_W = 3
DEPTH = 2
ALPHA = (2.0 * DEPTH) ** 0.25
ROPE_THETA = 10000.0
LN_EPS = 1e-5
RMS_EPS = 1e-6
GN_EPS = 1e-6
MLA_SCALE = (QK_NOPE + QK_ROPE) ** -0.5
LOG2E = 1.4426950408889634
SCALE_LOG2E = MLA_SCALE * LOG2E

LANES = 128
SUBLANES = 8
MIB = 1024 * 1024


_VMEM_LIMIT_MIB = {
    "matmul": 8, "matmul_res_ln": 8, "even_layer_prompt": 20, "even_mix_sample": 22, "ffn_prompt": 44,
    "ffn_sample": 16, "mla_proj": 43, "mla_proj_decode": 8, "flash_prompt": 45, "decode_attention": 41,
    "decode_out": 8,
}


def _cparams(sem, call):
    return pltpu.CompilerParams(dimension_semantics=sem, vmem_limit_bytes=_VMEM_LIMIT_MIB[call] * MIB)


def _resident(shape):
    nd = len(shape)
    return pl.BlockSpec(shape, lambda *_: (0,) * nd, pipeline_mode=pl.Buffered(1))


def _resident_layer(stacked_shape, layer):
    nd = len(stacked_shape) - 1
    return pl.BlockSpec((None,) + tuple(stacked_shape[1:]), lambda *_: (layer,) + (0,) * nd,
                        pipeline_mode=pl.Buffered(1))


def _whole(shape):
    nd = len(shape)
    return pl.BlockSpec(shape, lambda *_: (0,) * nd)


def _dot(a, b):
    return jnp.dot(a, b, preferred_element_type=F32)


def _dot_nt(a, b):
    return lax.dot_general(a, b, (((1,), (1,)), ((), ())), preferred_element_type=F32)


def _dot_tn(a, b):
    return lax.dot_general(a, b, (((0,), (0,)), ((), ())), preferred_element_type=F32)


def _layer_norm(z, g, b):
    mu = jnp.mean(z, axis=-1, keepdims=True)
    d = z - mu
    var = jnp.mean(d * d, axis=-1, keepdims=True)
    return d * lax.rsqrt(var + LN_EPS) * g + b


def _silu(x):
    return x * jax.nn.sigmoid(x)


def _mm_body(x_ref, w_ref, o_ref):
    o_ref[...] = _dot(x_ref[...].astype(BF16), w_ref[...]).astype(o_ref.dtype)


def _matmul(x, w, tm, out_dtype=F32):
    m, k = x.shape
    n = w.shape[1]
    return pl.pallas_call(
        _mm_body,
        grid=(m // tm,),
        in_specs=[pl.BlockSpec((tm, k), lambda i: (i, 0)), _resident((k, n))],
        out_specs=pl.BlockSpec((tm, n), lambda i: (i, 0)),
        out_shape=jax.ShapeDtypeStruct((m, n), out_dtype),
        compiler_params=_cparams(("parallel",), "matmul"),
        name="matmul",
    )(x, w)


def _mm_ln_body(a_ref, w_ref, r_ref, g_ref, b_ref, o_ref):
    y = _dot(a_ref[...].astype(BF16), w_ref[...])
    o_ref[...] = _layer_norm(ALPHA * r_ref[...] + y, g_ref[...], b_ref[...])


def _matmul_res_ln(a, w, res, g, b, tm):
    m, k = a.shape
    n = w.shape[1]
    return pl.pallas_call(
        _mm_ln_body,
        grid=(m // tm,),
        in_specs=[pl.BlockSpec((tm, k), lambda i: (i, 0)), _resident((k, n)),
                  pl.BlockSpec((tm, n), lambda i: (i, 0)), _resident((1, n)), _resident((1, n))],
        out_specs=pl.BlockSpec((tm, n), lambda i: (i, 0)),
        out_shape=jax.ShapeDtypeStruct((m, n), F32),
        compiler_params=_cparams(("parallel",), "matmul_res_ln"),
        name="matmul_res_ln",
    )(a, w, res, g.reshape(1, n), b.reshape(1, n))


def _rope_full(x, c2, s2):
    return x * c2 + pltpu.roll(x, 64, axis=1) * s2


def _group_norm_gate(o, gate, gn_row):
    mu = jnp.mean(o, axis=-1, keepdims=True)
    d = o - mu
    var = jnp.mean(d * d, axis=-1, keepdims=True)
    return _silu(gate) * (d * lax.rsqrt(var + GN_EPS) * gn_row)


def _even_layer_prompt_body(x_ref, win_ref, c2_ref, s2_ref, dec_ref, qd_ref, kd_ref, gc_ref, pw_ref, ps_ref,
                            gn_ref, wo_ref, lg_ref, lb_ref, o_ref, pst_ref, rst_ref,
                            mix_ref, ext_ref, s_ref, *h_refs, tm, n_j, sub):
    j = pl.program_id(1)
    pd = len(POOL_WINDOWS) * LANES

    @pl.when(j == 0)
    def _():
        ext_ref[0:16, :] = jnp.zeros((16, pd), F32)
        s_ref[...] = jnp.zeros(s_ref.shape, F32)

    @pl.when(j > 0)
    def _():
        ext_ref[0:16, :] = ext_ref[tm:tm + 16, :]

    c = dec_ref.shape[1]
    k_scale = RET_DK ** -0.5
    nblk = 4 * LANES
    def project(r0):
        h_ref = h_refs[r0 // sub]
        xb = x_ref[r0:r0 + sub, :].astype(BF16)
        for c0 in range(0, h_ref.shape[1], nblk):
            h_ref[:, c0:c0 + nblk] = _dot(xb, win_ref[:, c0:c0 + nblk])

    project(0)
    for r0 in range(0, tm, sub):
        rb = slice(r0, r0 + sub)
        h_ref = h_refs[r0 // sub]
        if r0 + sub < tm:
            project(r0 + sub)
        ext_ref[16 + r0:16 + r0 + sub, :] = h_ref[:, 0:pd]

        pos = (j * tm + r0 + lax.broadcasted_iota(jnp.int32, (sub, 1), 0)).astype(F32)
        for g, w in enumerate(POOL_WINDOWS):
            cols = slice(g * LANES, (g + 1) * LANES)
            e = ext_ref[r0:r0 + sub + 16, cols]
            u = e[16:, :]
            s = 1
            while s < w:
                e = e + pltpu.roll(e, s, axis=0)
                s *= 2
            cnt = jnp.minimum(float(w), pos + 1.0)
            pooled = e[16:, :] / cnt - u
            mixed = _dot(pooled.astype(BF16), pw_ref[g]) * ps_ref[:, cols]
            mix_ref[rb, cols] = mixed.astype(BF16)

        for ci in range(r0 // c, (r0 + sub) // c):
            rows = slice(ci * c, (ci + 1) * c)
            lrows = slice(ci * c - r0, (ci + 1) * c - r0)
            c2 = c2_ref[rows, :]
            s2 = s2_ref[rows, :]
            for hd in range(RET_HEADS):
                def col(part, hd=hd):
                    return slice(pd + (part * RET_HEADS + hd) * LANES, pd + (part * RET_HEADS + hd + 1) * LANES)
                q = _rope_full(h_ref[lrows, col(0)], c2, s2)
                k = _rope_full(h_ref[lrows, col(1)], c2, s2) * k_scale
                vb = h_ref[lrows, col(2)].astype(BF16)
                gate = h_ref[lrows, col(3)]
                st = s_ref[hd]
                sc = _dot_nt(q.astype(BF16), k.astype(BF16)) * dec_ref[hd]
                o = _dot(sc.astype(BF16), vb)
                o = o + _dot((q * qd_ref[hd]).astype(BF16), st.astype(BF16))
                s_ref[hd] = gc_ref[hd] * st + _dot_tn((k * kd_ref[hd]).astype(BF16), vb)
                ret = _group_norm_gate(o, gate, gn_ref[:, hd * LANES:(hd + 1) * LANES])
                mix_ref[rows, pd + hd * LANES:pd + (hd + 1) * LANES] = ret.astype(BF16)

        y = _dot(mix_ref[rb, :], wo_ref[...])
        o_ref[rb, :] = _layer_norm(ALPHA * x_ref[rb, :] + y, lg_ref[...], lb_ref[...])

    @pl.when(j == n_j - 1)
    def _():
        pst_ref[0] = ext_ref[pl.ds(tm + 1, POOL_HIST), :]
        rst_ref[0] = s_ref[...]


def _even_layer_prompt(x, bsz, seq, w_in, tabs, pool_w, pool_scale, gn_g, w_o, ln_g, ln_b, tm):
    n_j = seq // tm
    d = x.shape[1]
    pd = pool_scale.shape[0]
    ed = pd + RET_HEADS * LANES
    chunk = tabs["dec"].shape[1]
    sub = min(tm, max(chunk, 2 * LANES))
    body = functools.partial(_even_layer_prompt_body, tm=tm, n_j=n_j, sub=sub)
    return pl.pallas_call(
        body,
        grid=(bsz, n_j),
        in_specs=[
            pl.BlockSpec((tm, d), lambda b, j: (b * n_j + j, 0)),
            _resident(w_in.shape),
            pl.BlockSpec((tm, LANES), lambda b, j: (j, 0)),
            pl.BlockSpec((tm, LANES), lambda b, j: (j, 0)),
            _resident((RET_HEADS, chunk, chunk)),
            _resident((RET_HEADS, chunk, LANES)),
            _resident((RET_HEADS, chunk, LANES)),
            _resident((RET_HEADS, 1, LANES)),
            _resident(pool_w.shape),
            _resident((1, pd)),
            _resident((1, RET_HEADS * LANES)),
            _resident(w_o.shape),
            _resident((1, d)),
            _resident((1, d)),
        ],
        out_specs=[
            pl.BlockSpec((tm, d), lambda b, j: (b * n_j + j, 0)),
            pl.BlockSpec((1, POOL_HIST, pd), lambda b, j: (b, 0, 0)),
            pl.BlockSpec((1, RET_HEADS, RET_DK, LANES), lambda b, j: (b, 0, 0, 0)),
        ],
        out_shape=[
            jax.ShapeDtypeStruct((bsz * seq, d), F32),
            jax.ShapeDtypeStruct((bsz, POOL_HIST, pd), F32),
            jax.ShapeDtypeStruct((bsz, RET_HEADS, RET_DK, LANES), F32),
        ],
        scratch_shapes=[pltpu.VMEM((tm, ed), BF16), pltpu.VMEM((tm + 16, pd), F32),
                        pltpu.VMEM((RET_HEADS, RET_DK, LANES), F32)]
        + [pltpu.VMEM((sub, w_in.shape[1]), F32)] * (tm // sub),
        compiler_params=_cparams(("parallel", "arbitrary"), "even_layer_prompt"),
        name="even_layer_prompt",
    )(x, w_in, tabs["c2"], tabs["s2"], tabs["dec"], tabs["qd"], tabs["kd"], tabs["gc"],
      pool_w, pool_scale.reshape(1, pd), gn_g.reshape(1, -1), w_o, ln_g.reshape(1, d), ln_b.reshape(1, d))


def _even_mix_sample_body(h_ref, hist_ref, s0_ref, c2_ref, s2_ref, dm_ref, qd_ref, kd_ref, gc_ref,
                          pw_ref, ps_ref, gn_ref, mix_ref, hist_o_ref, s_o_ref, oc_ref,
                          *, nb, ls, cnts):
    pd = len(POOL_WINDOWS) * LANES
    ext = [hist_ref[i] for i in range(POOL_HIST)]
    ext += [h_ref[l * nb:(l + 1) * nb, 0:pd] for l in range(ls)]
    for i in range(POOL_HIST):
        hist_o_ref[i] = ext[ls + i]
    for g, w in enumerate(POOL_WINDOWS):
        cols = slice(g * LANES, (g + 1) * LANES)
        outs = []
        for l in range(ls):
            top = POOL_HIST + l
            acc = ext[top][:, cols]
            for jj in range(1, w):
                acc = acc + ext[top - jj][:, cols]
            outs.append(acc / cnts[g][l] - ext[top][:, cols])
        pooled = jnp.concatenate(outs, axis=0)
        mixed = _dot(pooled.astype(BF16), pw_ref[g]) * ps_ref[:, cols]
        mix_ref[:, cols] = mixed.astype(BF16)

    rows_b = lax.broadcasted_iota(jnp.int32, (ls * nb, 1), 0) % nb
    k_scale = RET_DK ** -0.5
    c2 = c2_ref[...]
    s2 = s2_ref[...]
    qs, ks, vs = [], [], []
    for hd in range(RET_HEADS):
        def col(part, hd=hd):
            return slice(pd + (part * RET_HEADS + hd) * LANES, pd + (part * RET_HEADS + hd + 1) * LANES)
        q = _rope_full(h_ref[:, col(0)], c2, s2)
        k = _rope_full(h_ref[:, col(1)], c2, s2) * k_scale
        vb = h_ref[:, col(2)].astype(BF16)
        sc = _dot_nt(q.astype(BF16), k.astype(BF16)) * dm_ref[hd]
        oc_ref[hd] = _dot(sc.astype(BF16), vb)
        qs.append(q * qd_ref[hd])
        ks.append((k * kd_ref[hd]).T)
        vs.append(vb)

    cols_b = lax.broadcasted_iota(jnp.int32, (1, ls * nb), 1) % nb

    def per_batch(b, carry):
        sel = rows_b == b
        sel_t = cols_b == b
        for hd in range(RET_HEADS):
            st = s0_ref[b, hd]
            qm = jnp.where(sel, qs[hd], 0.0).astype(BF16)
            km_t = jnp.where(sel_t, ks[hd], 0.0).astype(BF16)
            oc_ref[hd] += _dot(qm, st.astype(BF16))
            s_o_ref[b, hd] = gc_ref[hd] * st + _dot(km_t, vs[hd])
        return carry

    lax.fori_loop(0, nb, per_batch, 0)

    for hd in range(RET_HEADS):
        gate = h_ref[:, pd + (3 * RET_HEADS + hd) * LANES:pd + (3 * RET_HEADS + hd + 1) * LANES]
        ret = _group_norm_gate(oc_ref[hd], gate, gn_ref[:, hd * LANES:(hd + 1) * LANES])
        mix_ref[:, pd + hd * LANES:pd + (hd + 1) * LANES] = ret.astype(BF16)


def _even_mix_sample(h, hist, s0, tabs, pool_w, pool_scale, gn_g, nb, ls, past_len):
    pd = pool_scale.shape[0]
    ed = pd + RET_HEADS * LANES
    cnts = tuple(tuple(float(min(w, past_len + l + 1)) for l in range(ls)) for w in POOL_WINDOWS)
    body = functools.partial(_even_mix_sample_body, nb=nb, ls=ls, cnts=cnts)
    n = ls * nb
    args = (h, hist, s0, tabs["c2"], tabs["s2"], tabs["dm"], tabs["qd"], tabs["kd"], tabs["gc"],
            pool_w, pool_scale.reshape(1, pd), gn_g.reshape(1, -1))
    return pl.pallas_call(
        body,
        grid=(1,),
        in_specs=[_resident(a.shape) for a in args],
        out_specs=[_whole((n, ed)), _whole(hist.shape), _whole(s0.shape)],
        out_shape=[jax.ShapeDtypeStruct((n, ed), BF16),
                   jax.ShapeDtypeStruct(hist.shape, F32),
                   jax.ShapeDtypeStruct(s0.shape, F32)],
        scratch_shapes=[pltpu.VMEM((RET_HEADS, n, LANES), F32)],
        compiler_params=_cparams(("arbitrary",), "even_mix_sample"),
        name="even_mix_sample",
    )(*args)


def _ffn_chunk(xb, wup_ref, wd_ref, cw_ref, cb_ref, c, tf, dff, shift_fn):
    cols = slice(c * tf, (c + 1) * tf)
    a = _dot(xb, wup_ref[:, cols])
    gate_in = _dot(xb, wup_ref[:, dff + c * tf:dff + (c + 1) * tf])
    a1, a2 = shift_fn(a, c)
    conv = cb_ref[:, cols] + cw_ref[0:1, cols] * a2
    conv = conv + cw_ref[1:2, cols] * a1
    conv = conv + cw_ref[2:3, cols] * a
    act = (_silu(conv) * gate_in).astype(BF16)
    return a, _dot(act, wd_ref[cols, :])


def _ffn_prompt_body(x_ref, *rest, tm, n_j, tf, dff, mixer_out):
    if mixer_out:
        a_ref, wo_ref, mg_ref, mb_ref = rest[:4]
        rest = rest[4:]
    wup_ref, wd_ref, cw_ref, cb_ref, g_ref, b_ref, o_ref, st_ref, carry_ref = rest
    j = pl.program_id(1)

    @pl.when(j == 0)
    def _():
        carry_ref[...] = jnp.zeros(carry_ref.shape, F32)

    x = x_ref[...]
    if mixer_out:
        x = _layer_norm(ALPHA * x + _dot(a_ref[...], wo_ref[...]), mg_ref[...], mb_ref[...])
    row = lax.broadcasted_iota(jnp.int32, (tm, tf), 0)

    def shift_fn(a, c):
        prev = carry_ref[:, c * tf:(c + 1) * tf]
        a1 = jnp.where(row == 0, prev[7:8, :], pltpu.roll(a, 1, axis=0))
        a2 = jnp.where(row == 0, prev[6:7, :], jnp.where(row == 1, prev[7:8, :], pltpu.roll(a, 2, axis=0)))
        return a1, a2

    xb = x.astype(BF16)
    acc = None
    for c in range(dff // tf):
        a, y = _ffn_chunk(xb, wup_ref, wd_ref, cw_ref, cb_ref, c, tf, dff, shift_fn)
        acc = y if acc is None else acc + y
        tail = a[tm - SUBLANES:tm, :]
        carry_ref[:, c * tf:(c + 1) * tf] = tail
        st_ref[0, :, c * tf:(c + 1) * tf] = tail
    o_ref[...] = _layer_norm(ALPHA * x + acc, g_ref[...], b_ref[...])


def _ffn_prompt(x, bsz, seq, layer, wup, wd, conv_w, conv_b, g, b, tm, tf, mixer_out=None):
    d = x.shape[1]
    dff = wd.shape[1]
    n_j = seq // tm
    body = functools.partial(_ffn_prompt_body, tm=tm, n_j=n_j, tf=tf, dff=dff, mixer_out=mixer_out is not None)
    row_tile = lambda bi, j: (bi * n_j + j, 0)
    pre_args, pre_specs = [], []
    if mixer_out is not None:
        a, w_o, mg, mb = mixer_out
        pre_args = [a, w_o, mg.reshape(1, d), mb.reshape(1, d)]
        pre_specs = [pl.BlockSpec((tm, a.shape[1]), row_tile), _resident(w_o.shape), _resident((1, d)), _resident((1, d))]
    return pl.pallas_call(
        body,
        grid=(bsz, n_j),
        in_specs=[pl.BlockSpec((tm, d), row_tile)] + pre_specs + [
                  _resident_layer(wup.shape, layer), _resident_layer(wd.shape, layer), _resident(conv_w.shape),
                  _resident((1, dff)), _resident((1, d)), _resident((1, d))],
        out_specs=[pl.BlockSpec((tm, d), row_tile),
                   pl.BlockSpec((1, SUBLANES, dff), lambda bi, j: (bi, 0, 0))],
        out_shape=[jax.ShapeDtypeStruct(x.shape, F32),
                   jax.ShapeDtypeStruct((bsz, SUBLANES, dff), F32)],
        scratch_shapes=[pltpu.VMEM((SUBLANES, dff), F32)],
        compiler_params=_cparams(("parallel", "arbitrary"), "ffn_prompt"),
        name="ffn_prompt",
    )(x, *pre_args, wup, wd, conv_w, conv_b.reshape(1, dff), g.reshape(1, d), b.reshape(1, d))


def _ffn_sample_body(x_ref, st_ref, wup_ref, wd_ref, cw_ref, cb_ref, g_ref, b_ref, o_ref, st_o_ref,
                     *, nb, ls, tf, dff):
    x = x_ref[...]
    xb = x.astype(BF16)
    nh = CONV_W - 1

    def shift_fn(a, c):
        cols = slice(c * tf, (c + 1) * tf)
        ext = [st_ref[i, :, cols] for i in range(nh)] + [a[l * nb:(l + 1) * nb, :] for l in range(ls)]
        a1 = jnp.concatenate([ext[nh + l - 1] for l in range(ls)], axis=0)
        a2 = jnp.concatenate([ext[nh + l - 2] for l in range(ls)], axis=0)
        for i in range(nh):
            st_o_ref[i, :, cols] = ext[ls + i]
        return a1, a2

    acc = None
    for c in range(dff // tf):
        _, y = _ffn_chunk(xb, wup_ref, wd_ref, cw_ref, cb_ref, c, tf, dff, shift_fn)
        acc = y if acc is None else acc + y
    o_ref[...] = _layer_norm(ALPHA * x + acc, g_ref[...], b_ref[...])


def _ffn_sample(x, st, layer, wup, wd, conv_w, conv_b, g, b, nb, ls, tf):
    d = x.shape[1]
    dff = wd.shape[1]
    body = functools.partial(_ffn_sample_body, nb=nb, ls=ls, tf=tf, dff=dff)
    args = (x, st, wup, wd, conv_w, conv_b.reshape(1, dff), g.reshape(1, d), b.reshape(1, d))
    in_specs = [_resident(a.shape) for a in args]
    in_specs[2] = _resident_layer(wup.shape, layer)
    in_specs[3] = _resident_layer(wd.shape, layer)
    return pl.pallas_call(
        body,
        grid=(1,),
        in_specs=in_specs,
        out_specs=[_whole(x.shape), _whole(st.shape)],
        out_shape=[jax.ShapeDtypeStruct(x.shape, F32), jax.ShapeDtypeStruct(st.shape, F32)],
        compiler_params=_cparams(("arbitrary",), "ffn_sample"),
        name="ffn_sample",
    )(*args)


def _rope_pe(blk, cc, s1, s2):
    return blk * cc + pltpu.roll(blk, 96, axis=1) * s1 + pltpu.roll(blk, 32, axis=1) * s2


def _rms_norm(x, g):
    ms = jnp.mean(x * x, axis=-1, keepdims=True)
    return x * lax.rsqrt(ms + RMS_EPS) * g


def _mla_proj_body(x_ref, cc_ref, s1_ref, s2_ref, wdq_ref, qg_ref, wuq_ref, wdkv_ref, kvg_ref, *rest,
                   decode, sub):
    hw = 2 * LANES
    for r0 in range(0, x_ref.shape[0], sub):
        rb = slice(r0, r0 + sub)
        cc = cc_ref[rb, :]
        s1 = s1_ref[rb, :]
        s2 = s2_ref[rb, :]
        xb = x_ref[rb, :].astype(BF16)
        cq = _rms_norm(_dot(xb, wdq_ref[...]), qg_ref[...])
        q = _dot(cq.astype(BF16), wuq_ref[...])
        kv = _dot(xb, wdkv_ref[...])
        ckv = _rms_norm(kv[:, 0:KV_LORA], kvg_ref[...])
        kpe = _rope_pe(kv[:, KV_LORA:KV_LORA + LANES], cc, s1, s2)
        if decode:
            wukt_ref, ql_ref, qp_ref, ckv_ref, kpe_ref = rest
            for h in range(MLA_HEADS):
                qn = q[:, h * hw:h * hw + LANES].astype(BF16)
                ql_ref[rb, h * KV_LORA:(h + 1) * KV_LORA] = _dot(qn, wukt_ref[h]).astype(BF16)
                qp = _rope_pe(q[:, h * hw + LANES:(h + 1) * hw], cc, s1, s2)
                qp_ref[rb, h * LANES:(h + 1) * LANES] = qp.astype(BF16)
        else:
            wuk_ref, wuv_ref, qo_ref, ko_ref, vo_ref, ckv_ref, kpe_ref = rest
            cb = ckv.astype(BF16)
            kn = _dot(cb, wuk_ref[...])
            vo_ref[rb, :] = _dot(cb, wuv_ref[...]).astype(BF16)
            kpb = kpe.astype(BF16)
            for h in range(MLA_HEADS):
                qo_ref[rb, h * hw:h * hw + LANES] = q[:, h * hw:h * hw + LANES].astype(BF16)
                qp = _rope_pe(q[:, h * hw + LANES:(h + 1) * hw], cc, s1, s2)
                qo_ref[rb, h * hw + LANES:(h + 1) * hw] = qp.astype(BF16)
                ko_ref[rb, h * hw:h * hw + LANES] = kn[:, h * LANES:(h + 1) * LANES].astype(BF16)
                ko_ref[rb, h * hw + LANES:(h + 1) * hw] = kpb
        ckv_ref[rb, :] = ckv
        kpe_ref[rb, :] = kpe[:, 0:QK_ROPE]


def _mla_proj(x, tabs, w, tm, n_pos_blocks, decode):
    t, d = x.shape
    hw = 2 * LANES
    body = functools.partial(_mla_proj_body, decode=decode, sub=tm)
    row = lambda i: (i, 0)
    tab = lambda i: (i % n_pos_blocks, 0)
    ins = [x, tabs["cc"], tabs["s1"], tabs["s2"], w["dq"], w["qg"], w["uq"], w["dkv"], w["kvg"]]
    in_specs = [pl.BlockSpec((tm, d), row)] + [pl.BlockSpec((tm, LANES), tab)] * 3
    in_specs += [_resident(a.shape) for a in ins[4:]]
    if decode:
        ins += [w["ukt"]]
        in_specs += [_resident(w["ukt"].shape)]
        outs = [(MLA_HEADS * KV_LORA, BF16), (MLA_HEADS * LANES, BF16)]
    else:
        ins += [w["uk"], w["uv"]]
        in_specs += [_resident(w["uk"].shape), _resident(w["uv"].shape)]
        outs = [(MLA_HEADS * hw, BF16), (MLA_HEADS * hw, BF16), (MLA_HEADS * V_DIM, BF16)]
    outs += [(KV_LORA, F32), (QK_ROPE, F32)]
    return pl.pallas_call(
        body,
        grid=(t // tm,),
        in_specs=in_specs,
        out_specs=[pl.BlockSpec((tm, n), row) for n, _ in outs],
        out_shape=[jax.ShapeDtypeStruct((t, n), dt) for n, dt in outs],
        compiler_params=_cparams(("parallel",), "mla_proj_decode" if decode else "mla_proj"),
        name="mla_proj_decode" if decode else "mla_proj",
    )(*ins)


def _online_softmax_update(s2, m_ref, l_ref, acc_ref, pv_fn, row_chunk):
    rows, width = s2.shape
    n = width // LANES
    aw = acc_ref.shape[-1] // LANES
    p_chunks, alphas = [], []
    for r0 in range(0, rows, row_chunk):
        rs_ = slice(r0, r0 + row_chunk)
        tiles = [s2[rs_, j * LANES:(j + 1) * LANES] for j in range(n)]
        mx = tiles[0]
        for t in tiles[1:]:
            mx = jnp.maximum(mx, t)
        m_prev = m_ref[rs_, :]
        m_new = jnp.maximum(m_prev, jnp.max(mx, axis=-1, keepdims=True))
        alpha = jnp.exp2(m_prev - m_new)
        ps = [jnp.exp2(t - m_new) for t in tiles]
        if l_ref is not None:
            tot = ps[0]
            for t in ps[1:]:
                tot = tot + t
            l_ref[rs_, :] = alpha * l_ref[rs_, :] + jnp.sum(tot, axis=-1, keepdims=True)
        m_ref[rs_, :] = m_new
        p_chunks.append((jnp.concatenate(ps, axis=1) if n > 1 else ps[0]).astype(BF16))
        alphas.append(alpha)
    p = jnp.concatenate(p_chunks, axis=0) if len(p_chunks) > 1 else p_chunks[0]
    pv = pv_fn(p)
    for i, r0 in enumerate(range(0, rows, row_chunk)):
        rs_ = slice(r0, r0 + row_chunk)
        a_w = alphas[i] if aw == 1 else jnp.concatenate([alphas[i]] * aw, axis=1)
        acc_ref[rs_, :] = a_w * acc_ref[rs_, :] + pv[rs_, :]


def _flash_body(q_ref, k_ref, v_ref, o_ref, m_ref, acc_ref, *, tq, tk, row_chunk):
    qi = pl.program_id(1)
    hw = 2 * LANES
    m_ref[...] = jnp.full(m_ref.shape, -jnp.inf, F32)
    acc_ref[...] = jnp.zeros(acc_ref.shape, F32)
    ones = jnp.ones((tk, LANES), BF16)

    def step(h, key0, diag_off):
        start = pl.multiple_of(key0, tk)
        q = q_ref[:, h * hw:(h + 1) * hw]
        k = k_ref[pl.ds(start, tk), h * hw:(h + 1) * hw]
        v1 = jnp.concatenate([v_ref[pl.ds(start, tk), h * V_DIM:(h + 1) * V_DIM], ones], axis=1)
        s = _dot_nt(q, k) * SCALE_LOG2E
        if diag_off is not None:
            row = lax.broadcasted_iota(jnp.int32, s.shape, 0)
            colm = lax.broadcasted_iota(jnp.int32, s.shape, 1)
            s = jnp.where(colm + diag_off <= row, s, -jnp.inf)
        _online_softmax_update(s, m_ref.at[h], None, acc_ref.at[h], lambda p: _dot(p, v1), row_chunk)

    def loop_body(kj, carry):
        for h in range(MLA_HEADS):
            step(h, kj * tk, None)
        return carry

    lax.fori_loop(0, qi * (tq // tk), loop_body, 0)
    for h in range(MLA_HEADS):
        for j in range(tq // tk):
            step(h, qi * tq + j * tk, j * tk)
        o_ref[:, h * V_DIM:(h + 1) * V_DIM] = (acc_ref[h, :, 0:V_DIM] / acc_ref[h, :, V_DIM:]).astype(BF16)


def _flash_prompt(qp, kp, vp, bsz, seq, tq, tk):
    nq = seq // tq
    body = functools.partial(_flash_body, tq=tq, tk=tk, row_chunk=min(tq, 64))
    return pl.pallas_call(
        body,
        grid=(bsz, nq),
        in_specs=[pl.BlockSpec((tq, qp.shape[1]), lambda b, i: (b * nq + i, 0)),
                  pl.BlockSpec((seq, kp.shape[1]), lambda b, i: (b, 0)),
                  pl.BlockSpec((seq, vp.shape[1]), lambda b, i: (b, 0))],
        out_specs=pl.BlockSpec((tq, vp.shape[1]), lambda b, i: (b * nq + i, 0)),
        out_shape=jax.ShapeDtypeStruct(vp.shape, BF16),
        scratch_shapes=[pltpu.VMEM((MLA_HEADS, tq, LANES), F32), pltpu.VMEM((MLA_HEADS, tq, V_DIM + LANES), F32)],
        compiler_params=_cparams(("parallel", "arbitrary"), "flash_prompt"),
        name="flash_prompt",
    )(qp, kp, vp)


def _decode_body(pt_ref, ql_ref, qp_ref, cn_ref, kn_ref, ckv_hbm, kpe_hbm, o_ref,
                 cbuf_ref, rbuf_ref, sem_ref, m_ref, l_ref, acc_ref, *, gp, ls, layer, n_split, n_b, n_g, n_buf):
    b = pl.program_id(0)
    g = pl.program_id(1)
    t = b * n_g + g
    slot = lax.rem(t, n_buf)
    ahead = n_buf - 1

    def page_copies(bb, grp, slot):
        cps = []
        for i in range(gp):
            pg = pt_ref[bb, grp * gp + i]
            keys = pl.ds(i * PAGE_SIZE, PAGE_SIZE)
            cps.append(pltpu.make_async_copy(ckv_hbm.at[layer, pg], cbuf_ref.at[slot, keys, :], sem_ref.at[slot, 0]))
            cps.append(pltpu.make_async_copy(kpe_hbm.at[layer, pg], rbuf_ref.at[slot, :, keys], sem_ref.at[slot, 1]))
        return cps

    def start(step):
        for cp in page_copies(lax.div(step, n_g), lax.rem(step, n_g), lax.rem(step, n_buf)):
            cp.start()

    @pl.when(t == 0)
    def _():
        for s in range(min(ahead, n_b * n_g)):
            start(jnp.int32(s))

    @pl.when(t + ahead < n_b * n_g)
    def _():
        start(t + ahead)

    @pl.when(g == 0)
    def _():
        m_ref[...] = jnp.full(m_ref.shape, -jnp.inf, F32)
        l_ref[...] = jnp.zeros(l_ref.shape, F32)
        acc_ref[...] = jnp.zeros(acc_ref.shape, F32)

    ql = ql_ref[0]
    qp = qp_ref[0]

    for cp in page_copies(b, g, slot):
        cp.wait()

    kc_len = gp * PAGE_SIZE // n_split
    kbs, scores = [], []
    for c in range(n_split):
        ks = slice(c * kc_len, (c + 1) * kc_len)
        kb = cbuf_ref[slot, ks, :].astype(BF16)
        scores.append((_dot_nt(ql, kb) + _dot(qp, rbuf_ref[slot, :, ks].astype(BF16))) * SCALE_LOG2E)
        kbs.append(kb)
    state = (m_ref.at[0], l_ref.at[0], acc_ref.at[0])
    n_rows = ql.shape[0]
    for c in range(n_split):
        _online_softmax_update(scores[c], *state, lambda p, kb=kbs[c]: _dot(p, kb), n_rows)

    @pl.when(g == n_g - 1)
    def _():
        kc = cn_ref[0].astype(BF16)
        s = (_dot_nt(ql, kc) + _dot(qp, kn_ref[0].astype(BF16))) * SCALE_LOG2E
        r = lax.broadcasted_iota(jnp.int32, s.shape, 0) // MLA_HEADS
        cidx = lax.broadcasted_iota(jnp.int32, s.shape, 1)
        s = jnp.where((cidx <= r) & (cidx < ls), s, -jnp.inf)
        _online_softmax_update(s, *state, lambda p: _dot(p, kc), n_rows)
        l_w = jnp.concatenate([l_ref[0]] * (KV_LORA // LANES), axis=1)
        o_ref[0] = acc_ref[0] / l_w


def _decode_attention(page_table, ql, qp, cn, knt, cache_ckv, cache_kpet, layer, gp):
    nb, rows, _ = ql.shape
    n_pages = page_table.shape[1]
    ls = rows // MLA_HEADS
    n_split = 4 if gp % 4 == 0 else 1
    n_g = n_pages // gp
    n_buf = 3
    body = functools.partial(_decode_body, gp=gp, ls=ls, layer=layer, n_split=n_split, n_b=nb, n_g=n_g, n_buf=n_buf)
    per_b = lambda b, g, pt: (b, 0, 0)
    in_specs = [pl.BlockSpec((1, rows, KV_LORA), per_b), pl.BlockSpec((1, rows, QK_ROPE), per_b),
                pl.BlockSpec((1, PAGE_SIZE, KV_LORA), per_b), pl.BlockSpec((1, QK_ROPE, PAGE_SIZE), per_b),
                pl.BlockSpec(memory_space=pl.ANY), pl.BlockSpec(memory_space=pl.ANY)]
    grid_spec = pltpu.PrefetchScalarGridSpec(
        num_scalar_prefetch=1,
        grid=(nb, n_g),
        in_specs=in_specs,
        out_specs=pl.BlockSpec((1, rows, KV_LORA), per_b),
        scratch_shapes=[pltpu.VMEM((n_buf, gp * PAGE_SIZE, KV_LORA), F32),
                        pltpu.VMEM((n_buf, QK_ROPE, gp * PAGE_SIZE), F32),
                        pltpu.SemaphoreType.DMA((n_buf, 2)),
                        pltpu.VMEM((1, rows, LANES), F32), pltpu.VMEM((1, rows, LANES), F32),
                        pltpu.VMEM((1, rows, KV_LORA), F32)],
    )
    return pl.pallas_call(
        body,
        grid_spec=grid_spec,
        out_shape=jax.ShapeDtypeStruct((nb, rows, KV_LORA), F32),
        compiler_params=_cparams(("arbitrary", "arbitrary"), "decode_attention"),
        name="decode_attention",
    )(page_table, ql, qp, cn, knt, cache_ckv, cache_kpet)


def _decode_out_body(ol_ref, wuv_ref, wo_ref, r_ref, g_ref, b_ref, o_ref):
    y = None
    for h in range(MLA_HEADS):
        oh = _dot(ol_ref[h].astype(BF16), wuv_ref[h]).astype(BF16)
        t = _dot(oh, wo_ref[h * V_DIM:(h + 1) * V_DIM, :])
        y = t if y is None else y + t
    o_ref[...] = _layer_norm(ALPHA * r_ref[...] + y, g_ref[...], b_ref[...])


def _decode_out(ol, wuv3, wo, res, g, b):
    d = res.shape[1]
    args = (ol, wuv3, wo, res, g.reshape(1, d), b.reshape(1, d))
    return pl.pallas_call(
        _decode_out_body,
        grid=(1,),
        in_specs=[_resident(a.shape) for a in args],
        out_specs=_whole(res.shape),
        out_shape=jax.ShapeDtypeStruct(res.shape, F32),
        compiler_params=_cparams(("arbitrary",), "decode_out"),
        name="decode_out",
    )(*args)


def _rope_angles(pos, half):
    inv_freq = ROPE_THETA ** (-np.arange(half, dtype=np.float64) / half)
    ang = np.asarray(pos, np.float64)[:, None] * inv_freq[None, :]
    return np.cos(ang), np.sin(ang)


def _f32(tabs):
    return {k: np.ascontiguousarray(v, dtype=np.float32) for k, v in tabs.items()}


def _ret_tables(pos, chunk, rows_per_pos):
    cos, sin = _rope_angles(pos, RET_DK // 2)
    log_gamma = np.log(1.0 - 2.0 ** (-5.0 - np.arange(RET_HEADS, dtype=np.float64)))
    idx = np.repeat(np.arange(chunk, dtype=np.float64), rows_per_pos)
    diff = idx[:, None] - idx[None, :]
    dec = np.where(diff >= 0, np.exp(np.maximum(diff, 0.0)[None] * log_gamma[:, None, None]), 0.0)
    q_dec = np.exp((idx + 1.0)[None, :] * log_gamma[:, None])
    k_dec = np.exp((chunk - 1.0 - idx)[None, :] * log_gamma[:, None])
    gc = np.exp(chunk * log_gamma)
    n = idx.shape[0]
    return _f32({
        "c2": np.concatenate([cos, cos], axis=1),
        "s2": np.concatenate([-sin, sin], axis=1),
        "dec": dec,
        "qd": np.broadcast_to(q_dec[:, :, None], (RET_HEADS, n, LANES)),
        "kd": np.broadcast_to(k_dec[:, :, None], (RET_HEADS, n, LANES)),
        "gc": np.broadcast_to(gc[:, None, None], (RET_HEADS, 1, LANES)),
    })


def _pe_tables(pos):
    cos, sin = _rope_angles(pos, QK_ROPE // 2)
    z = np.zeros_like(cos)
    return _f32({"cc": np.concatenate([cos, cos, z, z], axis=1),
                 "s1": np.concatenate([-sin, z, z, z], axis=1),
                 "s2": np.concatenate([z, sin, z, z], axis=1)})


def _pad_heads(w, nope, rope):
    k = w.shape[0]
    w3 = w.reshape(k, MLA_HEADS, nope + rope)
    pad = jnp.zeros((k, MLA_HEADS, 2 * LANES - nope - rope), w.dtype)
    return jnp.concatenate([w3, pad], axis=2).reshape(k, MLA_HEADS * 2 * LANES)


def _tiles(seq, n_pages, dff):
    def fit(t, n=seq):
        while n % t:
            t //= 2
        return t
    return {"tok": fit(1024), "mix": fit(512), "ret_chunk": fit(256), "ffn": fit(512), "attn": fit(512), "attn_k": fit(512), "ff_chunk": dff,
            "pages": fit(64, n_pages)}


def kernel(x_prompt, x_sample, state_pool, state_ret, cache_ckv, cache_kpe, state_conv, page_table,
           w_in_even, pool_w, pool_scale, ret_gn_g, w_o_even,
           w_dq, q_norm_g, w_uq, w_dkv, kv_norm_g, w_uk, w_uv, w_o_mla,
           w_up, conv_w, conv_b, w_down, ln_mix_g, ln_mix_b, ln_ffn_g, ln_ffn_b):
    bp, lp, d = x_prompt.shape
    bs, ls, _ = x_sample.shape
    past_len = page_table.shape[1] * PAGE_SIZE
    depth = w_up.shape[0]
    dff = w_down.shape[1]
    tl = _tiles(lp, page_table.shape[1], dff)
    ns = bs * ls
    assert lp % tl["ret_chunk"] == 0 and tl["mix"] % tl["ret_chunk"] == 0
    assert dff % tl["ff_chunk"] == 0 and page_table.shape[1] % tl["pages"] == 0 and ls <= PAGE_SIZE

    xp = x_prompt.reshape(bp * lp, d)
    xs = jnp.swapaxes(x_sample, 0, 1).reshape(ns, d)
    pos_p = np.arange(lp)
    pos_s_rows = np.repeat(past_len + np.arange(ls), bs)

    wup_all = w_up.astype(BF16)
    wd_all = w_down.astype(BF16)
    outs = {k: [] for k in ("pool_p", "pool_s", "ret_p", "ret_s", "ckv_p", "ckv_s", "kpe_p", "kpe_s",
                            "conv_p", "conv_s")}
    for layer in range(depth):
        mixer_out = None
        if layer % 2 == 0:
            e = layer // 2
            w_in = w_in_even[e].astype(BF16)
            w_o = w_o_even[e].astype(BF16)
            pw = pool_w[e].astype(BF16)
            tp = _ret_tables(pos_p, tl["ret_chunk"], 1)
            xp, pst, rst = _even_layer_prompt(xp, bp, lp, w_in, tp, pw, pool_scale[e], ret_gn_g[e], w_o,
                                              ln_mix_g[layer], ln_mix_b[layer], tl["mix"])
            outs["pool_p"].append(pst)
            outs["ret_p"].append(rst)
            ts = _ret_tables(pos_s_rows, ls, bs)
            rb = np.arange(ns) % bs
            ts["dm"] = np.where((rb[:, None] == rb[None, :])[None], ts["dec"], np.float32(0.0))
            hs =_matmul(xs, w_in, ns)
            hist = jnp.swapaxes(state_pool[e], 0, 1)
            mix_s, hist_new, s_new = _even_mix_sample(hs, hist, state_ret[e], ts, pw, pool_scale[e],
                                                      ret_gn_g[e], bs, ls, past_len)
            outs["pool_s"].append(jnp.swapaxes(hist_new, 0, 1))
            outs["ret_s"].append(s_new)
            xs = _matmul_res_ln(mix_s, w_o, xs, ln_mix_g[layer], ln_mix_b[layer], ns)
        else:
            o = layer // 2
            hw = 2 * LANES
            w = {
                "dq": w_dq[o].astype(BF16),
                "qg": q_norm_g[o].reshape(1, -1),
                "uq": _pad_heads(w_uq[o], QK_NOPE, QK_ROPE).astype(BF16),
                "dkv": jnp.pad(w_dkv[o], ((0, 0), (0, KV_LORA + LANES - w_dkv.shape[2]))).astype(BF16),
                "kvg": kv_norm_g[o].reshape(1, -1),
                "uk": w_uk[o].reshape(KV_LORA, MLA_HEADS * QK_NOPE).astype(BF16),
                "uv": w_uv[o].reshape(KV_LORA, MLA_HEADS * V_DIM).astype(BF16),
                "ukt": jnp.transpose(w_uk[o], (1, 2, 0)).astype(BF16),
            }
            w_o = w_o_mla[o].astype(BF16)
            qp, kp, vp, ckv_p, kpe_p = _mla_proj(xp, _pe_tables(pos_p), w, tl["tok"], lp // tl["tok"], False)
            att = _flash_prompt(qp, kp, vp, bp, lp, tl["attn"], tl["attn_k"])
            outs["ckv_p"].append(ckv_p.reshape(bp, lp, KV_LORA))
            outs["kpe_p"].append(kpe_p.reshape(bp, lp, QK_ROPE))
            mixer_out = (att, w_o, ln_mix_g[layer], ln_mix_b[layer])
            ql, qpe, ckv_s, kpe_s = _mla_proj(xs, _pe_tables(pos_s_rows), w, ns, 1, True)
            rows = ls * MLA_HEADS

            def per_batch(a, width):
                return jnp.transpose(a.reshape(ls, bs, MLA_HEADS, width), (1, 0, 2, 3)).reshape(bs, rows, width)

            ql_b = per_batch(ql, KV_LORA)
            qp_b = per_batch(qpe, LANES)[:, :, 0:QK_ROPE]
            ckv_sb = jnp.swapaxes(ckv_s.reshape(ls, bs, KV_LORA), 0, 1)
            kpe_sb = jnp.swapaxes(kpe_s.reshape(ls, bs, QK_ROPE), 0, 1)
            cn = jnp.pad(ckv_sb, ((0, 0), (0, PAGE_SIZE - ls), (0, 0)))
            knt = jnp.swapaxes(jnp.pad(kpe_sb, ((0, 0), (0, PAGE_SIZE - ls), (0, 0))), 1, 2)
            o_lat = _decode_attention(page_table, ql_b, qp_b, cn, knt, cache_ckv,
                                      jnp.swapaxes(cache_kpe, 2, 3), o, tl["pages"])
            outs["ckv_s"].append(ckv_sb)
            outs["kpe_s"].append(kpe_sb)
            ol = jnp.transpose(o_lat.reshape(bs, ls, MLA_HEADS, KV_LORA), (2, 1, 0, 3)).reshape(MLA_HEADS, ns, KV_LORA)
            wuv3 = jnp.transpose(w_uv[o], (1, 0, 2)).astype(BF16)
            xs = _decode_out(ol, wuv3, w_o, xs, ln_mix_g[layer], ln_mix_b[layer])
        xp, st_p = _ffn_prompt(xp, bp, lp, layer, wup_all, wd_all, conv_w[layer], conv_b[layer],
                               ln_ffn_g[layer], ln_ffn_b[layer], tl["ffn"], tl["ff_chunk"], mixer_out)
        outs["conv_p"].append(st_p[:, SUBLANES - (CONV_W - 1):, :])
        st_s = jnp.swapaxes(state_conv[layer], 0, 1)
        xs, st_s_new = _ffn_sample(xs, st_s, layer, wup_all, wd_all, conv_w[layer], conv_b[layer],
                                   ln_ffn_g[layer], ln_ffn_b[layer], bs, ls, tl["ff_chunk"])
        outs["conv_s"].append(jnp.swapaxes(st_s_new, 0, 1))

    y_p = xp.reshape(bp, lp, d)
    y_s = jnp.swapaxes(xs.reshape(ls, bs, d), 0, 1)
    return (y_p, y_s,
            jnp.stack(outs["pool_p"]), jnp.stack(outs["pool_s"]),
            jnp.stack(outs["ret_p"]), jnp.stack(outs["ret_s"]),
            jnp.stack(outs["ckv_p"]), jnp.stack(outs["ckv_s"]),
            jnp.stack(outs["kpe_p"]), jnp.stack(outs["kpe_s"]),
            jnp.stack(outs["conv_p"]), jnp.stack(outs["conv_s"]))
```

```python
import functools

import jax
import jax.numpy as jnp
import numpy as np
from jax import lax
from jax.experimental import pallas as pl
from jax.experimental.pallas import tpu as pltpu

F32 = jnp.float32
BF16 = jnp.bfloat16

PAGE_SIZE = 128
POOL_WINDOWS = (2, 4, 8, 16)
POOL_HIST = max(POOL_WINDOWS) - 1
RET_HEADS = 4
RET_DK = 128
MLA_HEADS = 8
QK_NOPE = 128
QK_ROPE = 64
V_DIM = 128
KV_LORA = 256
CONV_W = 3
DEPTH = 2
ALPHA = (2.0 * DEPTH) ** 0.25
ROPE_THETA = 10000.0
LN_EPS = 1e-5
RMS_EPS = 1e-6
GN_EPS = 1e-6
MLA_SCALE = (QK_NOPE + QK_ROPE) ** -0.5
LOG2E = 1.4426950408889634
SCALE_LOG2E = MLA_SCALE * LOG2E

LANES = 128
SUBLANES = 8
MIB = 1024 * 1024


_VMEM_LIMIT_MIB = {
    "matmul": 4, "matmul_res_ln": 4, "even_layer_prompt": 16, "even_mix_sample": 20, "ffn_prompt": 41,
    "ffn_sample": 14, "mla_proj": 40, "mla_proj_decode": 4, "flash_prompt": 42, "decode_attention": 38,
    "decode_out": 4,
}


def _cparams(sem, call):
    return pltpu.CompilerParams(dimension_semantics=sem, vmem_limit_bytes=_VMEM_LIMIT_MIB[call] * MIB)


def _resident(shape):
    nd = len(shape)
    return pl.BlockSpec(shape, lambda *_: (0,) * nd, pipeline_mode=pl.Buffered(1))


def _resident_layer(stacked_shape, layer):
    nd = len(stacked_shape) - 1
    return pl.BlockSpec((None,) + tuple(stacked_shape[1:]), lambda *_: (layer,) + (0,) * nd,
                        pipeline_mode=pl.Buffered(1))


def _whole(shape):
    nd = len(shape)
    return pl.BlockSpec(shape, lambda *_: (0,) * nd)


def _dot(a, b):
    return jnp.dot(a, b, preferred_element_type=F32)


def _dot_nt(a, b):
    return lax.dot_general(a, b, (((1,), (1,)), ((), ())), preferred_element_type=F32)


def _dot_tn(a, b):
    return lax.dot_general(a, b, (((0,), (0,)), ((), ())), preferred_element_type=F32)


def _layer_norm(z, g, b):
    mu = jnp.mean(z, axis=-1, keepdims=True)
    d = z - mu
    var = jnp.mean(d * d, axis=-1, keepdims=True)
    return d * lax.rsqrt(var + LN_EPS) * g + b


def _silu(x):
    return x * jax.nn.sigmoid(x)


def _mm_body(x_ref, w_ref, o_ref):
    o_ref[...] = _dot(x_ref[...].astype(BF16), w_ref[...]).astype(o_ref.dtype)


def _matmul(x, w, tm, out_dtype=F32):
    m, k = x.shape
    n = w.shape[1]
    return pl.pallas_call(
        _mm_body,
        grid=(m // tm,),
        in_specs=[pl.BlockSpec((tm, k), lambda i: (i, 0)), _resident((k, n))],
        out_specs=pl.BlockSpec((tm, n), lambda i: (i, 0)),
        out_shape=jax.ShapeDtypeStruct((m, n), out_dtype),
        compiler_params=_cparams(("parallel",), "matmul"),
        name="matmul",
    )(x, w)


def _mm_ln_body(a_ref, w_ref, r_ref, g_ref, b_ref, o_ref):
    y = _dot(a_ref[...].astype(BF16), w_ref[...])
    o_ref[...] = _layer_norm(ALPHA * r_ref[...] + y, g_ref[...], b_ref[...])


def _matmul_res_ln(a, w, res, g, b, tm):
    m, k = a.shape
    n = w.shape[1]
    return pl.pallas_call(
        _mm_ln_body,
        grid=(m // tm,),
        in_specs=[pl.BlockSpec((tm, k), lambda i: (i, 0)), _resident((k, n)),
                  pl.BlockSpec((tm, n), lambda i: (i, 0)), _resident((1, n)), _resident((1, n))],
        out_specs=pl.BlockSpec((tm, n), lambda i: (i, 0)),
        out_shape=jax.ShapeDtypeStruct((m, n), F32),
        compiler_params=_cparams(("parallel",), "matmul_res_ln"),
        name="matmul_res_ln",
    )(a, w, res, g.reshape(1, n), b.reshape(1, n))


def _rope_full(x, c2, s2):
    return x * c2 + pltpu.roll(x, 64, axis=1) * s2


def _group_norm_gate(o, gate, gn_row):
    mu = jnp.mean(o, axis=-1, keepdims=True)
    d = o - mu
    var = jnp.mean(d * d, axis=-1, keepdims=True)
    return _silu(gate) * (d * lax.rsqrt(var + GN_EPS) * gn_row)


def _even_layer_prompt_body(x_ref, win_ref, c2_ref, s2_ref, dec_ref, qd_ref, kd_ref, gc_ref, pw_ref, ps_ref,
                            gn_ref, wo_ref, lg_ref, lb_ref, o_ref, pst_ref, rst_ref,
                            mix_ref, ext_ref, s_ref, *h_refs, tm, n_j, sub):
    j = pl.program_id(1)
    pd = len(POOL_WINDOWS) * LANES

    @pl.when(j == 0)
    def _():
        ext_ref[0:16, :] = jnp.zeros((16, pd), F32)
        s_ref[...] = jnp.zeros(s_ref.shape, F32)

    @pl.when(j > 0)
    def _():
        ext_ref[0:16, :] = ext_ref[tm:tm + 16, :]

    c = dec_ref.shape[1]
    k_scale = RET_DK ** -0.5
    nblk = 4 * LANES
    def project(r0):
        h_ref = h_refs[r0 // sub]
        xb = x_ref[r0:r0 + sub, :].astype(BF16)
        for c0 in range(0, h_ref.shape[1], nblk):
            h_ref[:, c0:c0 + nblk] = _dot(xb, win_ref[:, c0:c0 + nblk])

    project(0)
    for r0 in range(0, tm, sub):
        rb = slice(r0, r0 + sub)
        h_ref = h_refs[r0 // sub]
        if r0 + sub < tm:
            project(r0 + sub)
        ext_ref[16 + r0:16 + r0 + sub, :] = h_ref[:, 0:pd]

        pos = (j * tm + r0 + lax.broadcasted_iota(jnp.int32, (sub, 1), 0)).astype(F32)
        for g, w in enumerate(POOL_WINDOWS):
            cols = slice(g * LANES, (g + 1) * LANES)
            e = ext_ref[r0:r0 + sub + 16, cols]
            u = e[16:, :]
            s = 1
            while s < w:
                e = e + pltpu.roll(e, s, axis=0)
                s *= 2
            cnt = jnp.minimum(float(w), pos + 1.0)
            pooled = e[16:, :] / cnt - u
            mixed = _dot(pooled.astype(BF16), pw_ref[g]) * ps_ref[:, cols]
            mix_ref[rb, cols] = mixed.astype(BF16)

        for ci in range(r0 // c, (r0 + sub) // c):
            rows = slice(ci * c, (ci + 1) * c)
            lrows = slice(ci * c - r0, (ci + 1) * c - r0)
            c2 = c2_ref[rows, :]
            s2 = s2_ref[rows, :]
            for hd in range(RET_HEADS):
                def col(part, hd=hd):
                    return slice(pd + (part * RET_HEADS + hd) * LANES, pd + (part * RET_HEADS + hd + 1) * LANES)
                q = _rope_full(h_ref[lrows, col(0)], c2, s2)
                k = _rope_full(h_ref[lrows, col(1)], c2, s2) * k_scale
                vb = h_ref[lrows, col(2)].astype(BF16)
                gate = h_ref[lrows, col(3)]
                st = s_ref[hd]
                sc = _dot_nt(q.astype(BF16), k.astype(BF16)) * dec_ref[hd]
                o = _dot(sc.astype(BF16), vb)
                o = o + _dot((q * qd_ref[hd]).astype(BF16), st.astype(BF16))
                s_ref[hd] = gc_ref[hd] * st + _dot_tn((k * kd_ref[hd]).astype(BF16), vb)
                ret = _group_norm_gate(o, gate, gn_ref[:, hd * LANES:(hd + 1) * LANES])
                mix_ref[rows, pd + hd * LANES:pd + (hd + 1) * LANES] = ret.astype(BF16)

        y = _dot(mix_ref[rb, :], wo_ref[...])
        o_ref[rb, :] = _layer_norm(ALPHA * x_ref[rb, :] + y, lg_ref[...], lb_ref[...])

    @pl.when(j == n_j - 1)
    def _():
        pst_ref[0] = ext_ref[pl.ds(tm + 1, POOL_HIST), :]
        rst_ref[0] = s_ref[...]


def _even_layer_prompt(x, bsz, seq, w_in, tabs, pool_w, pool_scale, gn_g, w_o, ln_g, ln_b, tm):
    n_j = seq // tm
    d = x.shape[1]
    pd = pool_scale.shape[0]
    ed = pd + RET_HEADS * LANES
    chunk = tabs["dec"].shape[1]
    sub = min(tm, max(chunk, 2 * LANES))
    body = functools.partial(_even_layer_prompt_body, tm=tm, n_j=n_j, sub=sub)
    return pl.pallas_call(
        body,
        grid=(bsz, n_j),
        in_specs=[
            pl.BlockSpec((tm, d), lambda b, j: (b * n_j + j, 0)),
            _resident(w_in.shape),
            pl.BlockSpec((tm, LANES), lambda b, j: (j, 0)),
            pl.BlockSpec((tm, LANES), lambda b, j: (j, 0)),
            _resident((RET_HEADS, chunk, chunk)),
            _resident((RET_HEADS, chunk, LANES)),
            _resident((RET_HEADS, chunk, LANES)),
            _resident((RET_HEADS, 1, LANES)),
            _resident(pool_w.shape),
            _resident((1, pd)),
            _resident((1, RET_HEADS * LANES)),
            _resident(w_o.shape),
            _resident((1, d)),
            _resident((1, d)),
        ],
        out_specs=[
            pl.BlockSpec((tm, d), lambda b, j: (b * n_j + j, 0)),
            pl.BlockSpec((1, POOL_HIST, pd), lambda b, j: (b, 0, 0)),
            pl.BlockSpec((1, RET_HEADS, RET_DK, LANES), lambda b, j: (b, 0, 0, 0)),
        ],
        out_shape=[
            jax.ShapeDtypeStruct((bsz * seq, d), F32),
            jax.ShapeDtypeStruct((bsz, POOL_HIST, pd), F32),
            jax.ShapeDtypeStruct((bsz, RET_HEADS, RET_DK, LANES), F32),
        ],
        scratch_shapes=[pltpu.VMEM((tm, ed), BF16), pltpu.VMEM((tm + 16, pd), F32),
                        pltpu.VMEM((RET_HEADS, RET_DK, LANES), F32)]
        + [pltpu.VMEM((sub, w_in.shape[1]), F32)] * (tm // sub),
        compiler_params=_cparams(("parallel", "arbitrary"), "even_layer_prompt"),
        name="even_layer_prompt",
    )(x, w_in, tabs["c2"], tabs["s2"], tabs["dec"], tabs["qd"], tabs["kd"], tabs["gc"],
      pool_w, pool_scale.reshape(1, pd), gn_g.reshape(1, -1), w_o, ln_g.reshape(1, d), ln_b.reshape(1, d))


def _even_mix_sample_body(h_ref, hist_ref, s0_ref, c2_ref, s2_ref, dm_ref, qd_ref, kd_ref, gc_ref,
                          pw_ref, ps_ref, gn_ref, mix_ref, hist_o_ref, s_o_ref, oc_ref,
                          *, nb, ls, cnts):
    pd = len(POOL_WINDOWS) * LANES
    ext = [hist_ref[i] for i in range(POOL_HIST)]
    ext += [h_ref[l * nb:(l + 1) * nb, 0:pd] for l in range(ls)]
    for i in range(POOL_HIST):
        hist_o_ref[i] = ext[ls + i]
    for g, w in enumerate(POOL_WINDOWS):
        cols = slice(g * LANES, (g + 1) * LANES)
        outs = []
        for l in range(ls):
            top = POOL_HIST + l
            acc = ext[top][:, cols]
            for jj in range(1, w):
                acc = acc + ext[top - jj][:, cols]
            outs.append(acc / cnts[g][l] - ext[top][:, cols])
        pooled = jnp.concatenate(outs, axis=0)
        mixed = _dot(pooled.astype(BF16), pw_ref[g]) * ps_ref[:, cols]
        mix_ref[:, cols] = mixed.astype(BF16)

    rows_b = lax.broadcasted_iota(jnp.int32, (ls * nb, 1), 0) % nb
    k_scale = RET_DK ** -0.5
    c2 = c2_ref[...]
    s2 = s2_ref[...]
    qs, ks, vs = [], [], []
    for hd in range(RET_HEADS):
        def col(part, hd=hd):
            return slice(pd + (part * RET_HEADS + hd) * LANES, pd + (part * RET_HEADS + hd + 1) * LANES)
        q = _rope_full(h_ref[:, col(0)], c2, s2)
        k = _rope_full(h_ref[:, col(1)], c2, s2) * k_scale
        vb = h_ref[:, col(2)].astype(BF16)
        sc = _dot_nt(q.astype(BF16), k.astype(BF16)) * dm_ref[hd]
        oc_ref[hd] = _dot(sc.astype(BF16), vb)
        qs.append(q * qd_ref[hd])
        ks.append((k * kd_ref[hd]).T)
        vs.append(vb)

    cols_b = lax.broadcasted_iota(jnp.int32, (1, ls * nb), 1) % nb

    def per_batch(b, carry):
        sel = rows_b == b
        sel_t = cols_b == b
        for hd in range(RET_HEADS):
            st = s0_ref[b, hd]
            qm = jnp.where(sel, qs[hd], 0.0).astype(BF16)
            km_t = jnp.where(sel_t, ks[hd], 0.0).astype(BF16)
            oc_ref[hd] += _dot(qm, st.astype(BF16))
            s_o_ref[b, hd] = gc_ref[hd] * st + _dot(km_t, vs[hd])
        return carry

    lax.fori_loop(0, nb, per_batch, 0)

    for hd in range(RET_HEADS):
        gate = h_ref[:, pd + (3 * RET_HEADS + hd) * LANES:pd + (3 * RET_HEADS + hd + 1) * LANES]
        ret = _group_norm_gate(oc_ref[hd], gate, gn_ref[:, hd * LANES:(hd + 1) * LANES])
        mix_ref[:, pd + hd * LANES:pd + (hd + 1) * LANES] = ret.astype(BF16)


def _even_mix_sample(h, hist, s0, tabs, pool_w, pool_scale, gn_g, nb, ls, past_len):
    pd = pool_scale.shape[0]
    ed = pd + RET_HEADS * LANES
    cnts = tuple(tuple(float(min(w, past_len + l + 1)) for l in range(ls)) for w in POOL_WINDOWS)
    body = functools.partial(_even_mix_sample_body, nb=nb, ls=ls, cnts=cnts)
    n = ls * nb
    args = (h, hist, s0, tabs["c2"], tabs["s2"], tabs["dm"], tabs["qd"], tabs["kd"], tabs["gc"],
            pool_w, pool_scale.reshape(1, pd), gn_g.reshape(1, -1))
    return pl.pallas_call(
        body,
        grid=(1,),
        in_specs=[_resident(a.shape) for a in args],
        out_specs=[_whole((n, ed)), _whole(hist.shape), _whole(s0.shape)],
        out_shape=[jax.ShapeDtypeStruct((n, ed), BF16),
                   jax.ShapeDtypeStruct(hist.shape, F32),
                   jax.ShapeDtypeStruct(s0.shape, F32)],
        scratch_shapes=[pltpu.VMEM((RET_HEADS, n, LANES), F32)],
        compiler_params=_cparams(("arbitrary",), "even_mix_sample"),
        name="even_mix_sample",
    )(*args)


def _ffn_chunk(xb, wup_ref, wd_ref, cw_ref, cb_ref, c, tf, dff, shift_fn):
    cols = slice(c * tf, (c + 1) * tf)
    a = _dot(xb, wup_ref[:, cols])
    gate_in = _dot(xb, wup_ref[:, dff + c * tf:dff + (c + 1) * tf])
    a1, a2 = shift_fn(a, c)
    conv = cb_ref[:, cols] + cw_ref[0:1, cols] * a2
    conv = conv + cw_ref[1:2, cols] * a1
    conv = conv + cw_ref[2:3, cols] * a
    act = (_silu(conv) * gate_in).astype(BF16)
    return a, _dot(act, wd_ref[cols, :])


def _ffn_prompt_body(x_ref, *rest, tm, n_j, tf, dff, mixer_out):
    if mixer_out:
        a_ref, wo_ref, mg_ref, mb_ref = rest[:4]
        rest = rest[4:]
    wup_ref, wd_ref, cw_ref, cb_ref, g_ref, b_ref, o_ref, st_ref, carry_ref = rest
    j = pl.program_id(1)

    @pl.when(j == 0)
    def _():
        carry_ref[...] = jnp.zeros(carry_ref.shape, F32)

    x = x_ref[...]
    if mixer_out:
        x = _layer_norm(ALPHA * x + _dot(a_ref[...], wo_ref[...]), mg_ref[...], mb_ref[...])
    row = lax.broadcasted_iota(jnp.int32, (tm, tf), 0)

    def shift_fn(a, c):
        prev = carry_ref[:, c * tf:(c + 1) * tf]
        a1 = jnp.where(row == 0, prev[7:8, :], pltpu.roll(a, 1, axis=0))
        a2 = jnp.where(row == 0, prev[6:7, :], jnp.where(row == 1, prev[7:8, :], pltpu.roll(a, 2, axis=0)))
        return a1, a2

    xb = x.astype(BF16)
    acc = None
    for c in range(dff // tf):
        a, y = _ffn_chunk(xb, wup_ref, wd_ref, cw_ref, cb_ref, c, tf, dff, shift_fn)
        acc = y if acc is None else acc + y
        tail = a[tm - SUBLANES:tm, :]
        carry_ref[:, c * tf:(c + 1) * tf] = tail
        st_ref[0, :, c * tf:(c + 1) * tf] = tail
    o_ref[...] = _layer_norm(ALPHA * x + acc, g_ref[...], b_ref[...])


def _ffn_prompt(x, bsz, seq, layer, wup, wd, conv_w, conv_b, g, b, tm, tf, mixer_out=None):
    d = x.shape[1]
    dff = wd.shape[1]
    n_j = seq // tm
    body = functools.partial(_ffn_prompt_body, tm=tm, n_j=n_j, tf=tf, dff=dff, mixer_out=mixer_out is not None)
    row_tile = lambda bi, j: (bi * n_j + j, 0)
    pre_args, pre_specs = [], []
    if mixer_out is not None:
        a, w_o, mg, mb = mixer_out
        pre_args = [a, w_o, mg.reshape(1, d), mb.reshape(1, d)]
        pre_specs = [pl.BlockSpec((tm, a.shape[1]), row_tile), _resident(w_o.shape), _resident((1, d)), _resident((1, d))]
    return pl.pallas_call(
        body,
        grid=(bsz, n_j),
        in_specs=[pl.BlockSpec((tm, d), row_tile)] + pre_specs + [
                  _resident_layer(wup.shape, layer), _resident_layer(wd.shape, layer), _resident(conv_w.shape),
                  _resident((1, dff)), _resident((1, d)), _resident((1, d))],
        out_specs=[pl.BlockSpec((tm, d), row_tile),
                   pl.BlockSpec((1, SUBLANES, dff), lambda bi, j: (bi, 0, 0))],
        out_shape=[jax.ShapeDtypeStruct(x.shape, F32),
                   jax.ShapeDtypeStruct((bsz, SUBLANES, dff), F32)],
        scratch_shapes=[pltpu.VMEM((SUBLANES, dff), F32)],
        compiler_params=_cparams(("parallel", "arbitrary"), "ffn_prompt"),
        name="ffn_prompt",
    )(x, *pre_args, wup, wd, conv_w, conv_b.reshape(1, dff), g.reshape(1, d), b.reshape(1, d))


def _ffn_sample_body(x_ref, st_ref, wup_ref, wd_ref, cw_ref, cb_ref, g_ref, b_ref, o_ref, st_o_ref,
                     *, nb, ls, tf, dff):
    x = x_ref[...]
    xb = x.astype(BF16)
    nh = CONV_W - 1

    def shift_fn(a, c):
        cols = slice(c * tf, (c + 1) * tf)
        ext = [st_ref[i, :, cols] for i in range(nh)] + [a[l * nb:(l + 1) * nb, :] for l in range(ls)]
        a1 = jnp.concatenate([ext[nh + l - 1] for l in range(ls)], axis=0)
        a2 = jnp.concatenate([ext[nh + l - 2] for l in range(ls)], axis=0)
        for i in range(nh):
            st_o_ref[i, :, cols] = ext[ls + i]
        return a1, a2

    acc = None
    for c in range(dff // tf):
        _, y = _ffn_chunk(xb, wup_ref, wd_ref, cw_ref, cb_ref, c, tf, dff, shift_fn)
        acc = y if acc is None else acc + y
    o_ref[...] = _layer_norm(ALPHA * x + acc, g_ref[...], b_ref[...])


def _ffn_sample(x, st, layer, wup, wd, conv_w, conv_b, g, b, nb, ls, tf):
    d = x.shape[1]
    dff = wd.shape[1]
    body = functools.partial(_ffn_sample_body, nb=nb, ls=ls, tf=tf, dff=dff)
    args = (x, st, wup, wd, conv_w, conv_b.reshape(1, dff), g.reshape(1, d), b.reshape(1, d))
    in_specs = [_resident(a.shape) for a in args]
    in_specs[2] = _resident_layer(wup.shape, layer)
    in_specs[3] = _resident_layer(wd.shape, layer)
    return pl.pallas_call(
        body,
        grid=(1,),
        in_specs=in_specs,
        out_specs=[_whole(x.shape), _whole(st.shape)],
        out_shape=[jax.ShapeDtypeStruct(x.shape, F32), jax.ShapeDtypeStruct(st.shape, F32)],
        compiler_params=_cparams(("arbitrary",), "ffn_sample"),
        name="ffn_sample",
    )(*args)


def _rope_pe(blk, cc, s1, s2):
    return blk * cc + pltpu.roll(blk, 96, axis=1) * s1 + pltpu.roll(blk, 32, axis=1) * s2


def _rms_norm(x, g):
    ms = jnp.mean(x * x, axis=-1, keepdims=True)
    return x * lax.rsqrt(ms + RMS_EPS) * g


def _mla_proj_body(x_ref, cc_ref, s1_ref, s2_ref, wdq_ref, qg_ref, wuq_ref, wdkv_ref, kvg_ref, *rest,
                   decode, sub):
    hw = 2 * LANES
    for r0 in range(0, x_ref.shape[0], sub):
        rb = slice(r0, r0 + sub)
        cc = cc_ref[rb, :]
        s1 = s1_ref[rb, :]
        s2 = s2_ref[rb, :]
        xb = x_ref[rb, :].astype(BF16)
        cq = _rms_norm(_dot(xb, wdq_ref[...]), qg_ref[...])
        q = _dot(cq.astype(BF16), wuq_ref[...])
        kv = _dot(xb, wdkv_ref[...])
        ckv = _rms_norm(kv[:, 0:KV_LORA], kvg_ref[...])
        kpe = _rope_pe(kv[:, KV_LORA:KV_LORA + LANES], cc, s1, s2)
        if decode:
            wukt_ref, ql_ref, qp_ref, ckv_ref, kpe_ref = rest
            for h in range(MLA_HEADS):
                qn = q[:, h * hw:h * hw + LANES].astype(BF16)
                ql_ref[rb, h * KV_LORA:(h + 1) * KV_LORA] = _dot(qn, wukt_ref[h]).astype(BF16)
                qp = _rope_pe(q[:, h * hw + LANES:(h + 1) * hw], cc, s1, s2)
                qp_ref[rb, h * LANES:(h + 1) * LANES] = qp.astype(BF16)
        else:
            wuk_ref, wuv_ref, qo_ref, ko_ref, vo_ref, ckv_ref, kpe_ref = rest
            cb = ckv.astype(BF16)
            kn = _dot(cb, wuk_ref[...])
            vo_ref[rb, :] = _dot(cb, wuv_ref[...]).astype(BF16)
            kpb = kpe.astype(BF16)
            for h in range(MLA_HEADS):
                qo_ref[rb, h * hw:h * hw + LANES] = q[:, h * hw:h * hw + LANES].astype(BF16)
                qp = _rope_pe(q[:, h * hw + LANES:(h + 1) * hw], cc, s1, s2)
                qo_ref[rb, h * hw + LANES:(h + 1) * hw] = qp.astype(BF16)
                ko_ref[rb, h * hw:h * hw + LANES] = kn[:, h * LANES:(h + 1) * LANES].astype(BF16)
                ko_ref[rb, h * hw + LANES:(h + 1) * hw] = kpb
        ckv_ref[rb, :] = ckv
        kpe_ref[rb, :] = kpe[:, 0:QK_ROPE]


def _mla_proj(x, tabs, w, tm, n_pos_blocks, decode):
    t, d = x.shape
    hw = 2 * LANES
    body = functools.partial(_mla_proj_body, decode=decode, sub=tm)
    row = lambda i: (i, 0)
    tab = lambda i: (i % n_pos_blocks, 0)
    ins = [x, tabs["cc"], tabs["s1"], tabs["s2"], w["dq"], w["qg"], w["uq"], w["dkv"], w["kvg"]]
    in_specs = [pl.BlockSpec((tm, d), row)] + [pl.BlockSpec((tm, LANES), tab)] * 3
    in_specs += [_resident(a.shape) for a in ins[4:]]
    if decode:
        ins += [w["ukt"]]
        in_specs += [_resident(w["ukt"].shape)]
        outs = [(MLA_HEADS * KV_LORA, BF16), (MLA_HEADS * LANES, BF16)]
    else:
        ins += [w["uk"], w["uv"]]
        in_specs += [_resident(w["uk"].shape), _resident(w["uv"].shape)]
        outs = [(MLA_HEADS * hw, BF16), (MLA_HEADS * hw, BF16), (MLA_HEADS * V_DIM, BF16)]
    outs += [(KV_LORA, F32), (QK_ROPE, F32)]
    return pl.pallas_call(
        body,
        grid=(t // tm,),
        in_specs=in_specs,
        out_specs=[pl.BlockSpec((tm, n), row) for n, _ in outs],
        out_shape=[jax.ShapeDtypeStruct((t, n), dt) for n, dt in outs],
        compiler_params=_cparams(("parallel",), "mla_proj_decode" if decode else "mla_proj"),
        name="mla_proj_decode" if decode else "mla_proj",
    )(*ins)


def _online_softmax_update(s2, m_ref, l_ref, acc_ref, pv_fn, row_chunk):
    rows, width = s2.shape
    n = width // LANES
    aw = acc_ref.shape[-1] // LANES
    p_chunks, alphas = [], []
    for r0 in range(0, rows, row_chunk):
        rs_ = slice(r0, r0 + row_chunk)
        tiles = [s2[rs_, j * LANES:(j + 1) * LANES] for j in range(n)]
        mx = tiles[0]
        for t in tiles[1:]:
            mx = jnp.maximum(mx, t)
        m_prev = m_ref[rs_, :]
        m_new = jnp.maximum(m_prev, jnp.max(mx, axis=-1, keepdims=True))
        alpha = jnp.exp2(m_prev - m_new)
        ps = [jnp.exp2(t - m_new) for t in tiles]
        if l_ref is not None:
            tot = ps[0]
            for t in ps[1:]:
                tot = tot + t
            l_ref[rs_, :] = alpha * l_ref[rs_, :] + jnp.sum(tot, axis=-1, keepdims=True)
        m_ref[rs_, :] = m_new
        p_chunks.append((jnp.concatenate(ps, axis=1) if n > 1 else ps[0]).astype(BF16))
        alphas.append(alpha)
    p = jnp.concatenate(p_chunks, axis=0) if len(p_chunks) > 1 else p_chunks[0]
    pv = pv_fn(p)
    for i, r0 in enumerate(range(0, rows, row_chunk)):
        rs_ = slice(r0, r0 + row_chunk)
        a_w = alphas[i] if aw == 1 else jnp.concatenate([alphas[i]] * aw, axis=1)
        acc_ref[rs_, :] = a_w * acc_ref[rs_, :] + pv[rs_, :]


def _flash_body(q_ref, k_ref, v_ref, o_ref, m_ref, acc_ref, *, tq, tk, row_chunk):
    qi = pl.program_id(1)
    hw = 2 * LANES
    m_ref[...] = jnp.full(m_ref.shape, -jnp.inf, F32)
    acc_ref[...] = jnp.zeros(acc_ref.shape, F32)
    ones = jnp.ones((tk, LANES), BF16)

    def step(h, key0, diag_off):
        start = pl.multiple_of(key0, tk)
        q = q_ref[:, h * hw:(h + 1) * hw]
        k = k_ref[pl.ds(start, tk), h * hw:(h + 1) * hw]
        v1 = jnp.concatenate([v_ref[pl.ds(start, tk), h * V_DIM:(h + 1) * V_DIM], ones], axis=1)
        s = _dot_nt(q, k) * SCALE_LOG2E
        if diag_off is not None:
            row = lax.broadcasted_iota(jnp.int32, s.shape, 0)
            colm = lax.broadcasted_iota(jnp.int32, s.shape, 1)
            s = jnp.where(colm + diag_off <= row, s, -jnp.inf)
        _online_softmax_update(s, m_ref.at[h], None, acc_ref.at[h], lambda p: _dot(p, v1), row_chunk)

    def loop_body(kj, carry):
        for h in range(MLA_HEADS):
            step(h, kj * tk, None)
        return carry

    lax.fori_loop(0, qi * (tq // tk), loop_body, 0)
    for h in range(MLA_HEADS):
        for j in range(tq // tk):
            step(h, qi * tq + j * tk, j * tk)
        o_ref[:, h * V_DIM:(h + 1) * V_DIM] = (acc_ref[h, :, 0:V_DIM] / acc_ref[h, :, V_DIM:]).astype(BF16)


def _flash_prompt(qp, kp, vp, bsz, seq, tq, tk):
    nq = seq // tq
    body = functools.partial(_flash_body, tq=tq, tk=tk, row_chunk=min(tq, 64))
    return pl.pallas_call(
        body,
        grid=(bsz, nq),
        in_specs=[pl.BlockSpec((tq, qp.shape[1]), lambda b, i: (b * nq + i, 0)),
                  pl.BlockSpec((seq, kp.shape[1]), lambda b, i: (b, 0)),
                  pl.BlockSpec((seq, vp.shape[1]), lambda b, i: (b, 0))],
        out_specs=pl.BlockSpec((tq, vp.shape[1]), lambda b, i: (b * nq + i, 0)),
        out_shape=jax.ShapeDtypeStruct(vp.shape, BF16),
        scratch_shapes=[pltpu.VMEM((MLA_HEADS, tq, LANES), F32), pltpu.VMEM((MLA_HEADS, tq, V_DIM + LANES), F32)],
        compiler_params=_cparams(("parallel", "arbitrary"), "flash_prompt"),
        name="flash_prompt",
    )(qp, kp, vp)


def _decode_body(pt_ref, ql_ref, qp_ref, cn_ref, kn_ref, ckv_hbm, kpe_hbm, o_ref,
                 cbuf_ref, rbuf_ref, sem_ref, m_ref, l_ref, acc_ref, *, gp, ls, layer, n_split, n_b, n_g, n_buf):
    b = pl.program_id(0)
    g = pl.program_id(1)
    t = b * n_g + g
    slot = lax.rem(t, n_buf)
    ahead = n_buf - 1

    def page_copies(bb, grp, slot):
        cps = []
        for i in range(gp):
            pg = pt_ref[bb, grp * gp + i]
            keys = pl.ds(i * PAGE_SIZE, PAGE_SIZE)
            cps.append(pltpu.make_async_copy(ckv_hbm.at[layer, pg], cbuf_ref.at[slot, keys, :], sem_ref.at[slot, 0]))
            cps.append(pltpu.make_async_copy(kpe_hbm.at[layer, pg], rbuf_ref.at[slot, :, keys], sem_ref.at[slot, 1]))
        return cps

    def start(step):
        for cp in page_copies(lax.div(step, n_g), lax.rem(step, n_g), lax.rem(step, n_buf)):
            cp.start()

    @pl.when(t == 0)
    def _():
        for s in range(min(ahead, n_b * n_g)):
            start(jnp.int32(s))

    @pl.when(t + ahead < n_b * n_g)
    def _():
        start(t + ahead)

    @pl.when(g == 0)
    def _():
        m_ref[...] = jnp.full(m_ref.shape, -jnp.inf, F32)
        l_ref[...] = jnp.zeros(l_ref.shape, F32)
        acc_ref[...] = jnp.zeros(acc_ref.shape, F32)

    ql = ql_ref[0]
    qp = qp_ref[0]

    for cp in page_copies(b, g, slot):
        cp.wait()

    kc_len = gp * PAGE_SIZE // n_split
    kbs, scores = [], []
    for c in range(n_split):
        ks = slice(c * kc_len, (c + 1) * kc_len)
        kb = cbuf_ref[slot, ks, :].astype(BF16)
        scores.append((_dot_nt(ql, kb) + _dot(qp, rbuf_ref[slot, :, ks].astype(BF16))) * SCALE_LOG2E)
        kbs.append(kb)
    state = (m_ref.at[0], l_ref.at[0], acc_ref.at[0])
    n_rows = ql.shape[0]
    for c in range(n_split):
        _online_softmax_update(scores[c], *state, lambda p, kb=kbs[c]: _dot(p, kb), n_rows)

    @pl.when(g == n_g - 1)
    def _():
        kc = cn_ref[0].astype(BF16)
        s = (_dot_nt(ql, kc) + _dot(qp, kn_ref[0].astype(BF16))) * SCALE_LOG2E
        r = lax.broadcasted_iota(jnp.int32, s.shape, 0) // MLA_HEADS
        cidx = lax.broadcasted_iota(jnp.int32, s.shape, 1)
        s = jnp.where((cidx <= r) & (cidx < ls), s, -jnp.inf)
        _online_softmax_update(s, *state, lambda p: _dot(p, kc), n_rows)
        l_w = jnp.concatenate([l_ref[0]] * (KV_LORA // LANES), axis=1)
        o_ref[0] = acc_ref[0] / l_w


def _decode_attention(page_table, ql, qp, cn, knt, cache_ckv, cache_kpet, layer, gp):
    nb, rows, _ = ql.shape
    n_pages = page_table.shape[1]
    ls = rows // MLA_HEADS
    n_split = 4 if gp % 4 == 0 else 1
    n_g = n_pages // gp
    n_buf = 3
    body = functools.partial(_decode_body, gp=gp, ls=ls, layer=layer, n_split=n_split, n_b=nb, n_g=n_g, n_buf=n_buf)
    per_b = lambda b, g, pt: (b, 0, 0)
    in_specs = [pl.BlockSpec((1, rows, KV_LORA), per_b), pl.BlockSpec((1, rows, QK_ROPE), per_b),
                pl.BlockSpec((1, PAGE_SIZE, KV_LORA), per_b), pl.BlockSpec((1, QK_ROPE, PAGE_SIZE), per_b),
                pl.BlockSpec(memory_space=pl.ANY), pl.BlockSpec(memory_space=pl.ANY)]
    grid_spec = pltpu.PrefetchScalarGridSpec(
        num_scalar_prefetch=1,
        grid=(nb, n_g),
        in_specs=in_specs,
        out_specs=pl.BlockSpec((1, rows, KV_LORA), per_b),
        scratch_shapes=[pltpu.VMEM((n_buf, gp * PAGE_SIZE, KV_LORA), F32),
                        pltpu.VMEM((n_buf, QK_ROPE, gp * PAGE_SIZE), F32),
                        pltpu.SemaphoreType.DMA((n_buf, 2)),
                        pltpu.VMEM((1, rows, LANES), F32), pltpu.VMEM((1, rows, LANES), F32),
                        pltpu.VMEM((1, rows, KV_LORA), F32)],
    )
    return pl.pallas_call(
        body,
        grid_spec=grid_spec,
        out_shape=jax.ShapeDtypeStruct((nb, rows, KV_LORA), F32),
        compiler_params=_cparams(("arbitrary", "arbitrary"), "decode_attention"),
        name="decode_attention",
    )(page_table, ql, qp, cn, knt, cache_ckv, cache_kpet)


def _decode_out_body(ol_ref, wuv_ref, wo_ref, r_ref, g_ref, b_ref, o_ref):
    y = None
    for h in range(MLA_HEADS):
        oh = _dot(ol_ref[h].astype(BF16), wuv_ref[h]).astype(BF16)
        t = _dot(oh, wo_ref[h * V_DIM:(h + 1) * V_DIM, :])
        y = t if y is None else y + t
    o_ref[...] = _layer_norm(ALPHA * r_ref[...] + y, g_ref[...], b_ref[...])


def _decode_out(ol, wuv3, wo, res, g, b):
    d = res.shape[1]
    args = (ol, wuv3, wo, res, g.reshape(1, d), b.reshape(1, d))
    return pl.pallas_call(
        _decode_out_body,
        grid=(1,),
        in_specs=[_resident(a.shape) for a in args],
        out_specs=_whole(res.shape),
        out_shape=jax.ShapeDtypeStruct(res.shape, F32),
        compiler_params=_cparams(("arbitrary",), "decode_out"),
        name="decode_out",
    )(*args)


def _rope_angles(pos, half):
    inv_freq = ROPE_THETA ** (-np.arange(half, dtype=np.float64) / half)
    ang = np.asarray(pos, np.float64)[:, None] * inv_freq[None, :]
    return np.cos(ang), np.sin(ang)


def _f32(tabs):
    return {k: np.ascontiguousarray(v, dtype=np.float32) for k, v in tabs.items()}


def _ret_tables(pos, chunk, rows_per_pos):
    cos, sin = _rope_angles(pos, RET_DK // 2)
    log_gamma = np.log(1.0 - 2.0 ** (-5.0 - np.arange(RET_HEADS, dtype=np.float64)))
    idx = np.repeat(np.arange(chunk, dtype=np.float64), rows_per_pos)
    diff = idx[:, None] - idx[None, :]
    dec = np.where(diff >= 0, np.exp(np.maximum(diff, 0.0)[None] * log_gamma[:, None, None]), 0.0)
    q_dec = np.exp((idx + 1.0)[None, :] * log_gamma[:, None])
    k_dec = np.exp((chunk - 1.0 - idx)[None, :] * log_gamma[:, None])
    gc = np.exp(chunk * log_gamma)
    n = idx.shape[0]
    return _f32({
        "c2": np.concatenate([cos, cos], axis=1),
        "s2": np.concatenate([-sin, sin], axis=1),
        "dec": dec,
        "qd": np.broadcast_to(q_dec[:, :, None], (RET_HEADS, n, LANES)),
        "kd": np.broadcast_to(k_dec[:, :, None], (RET_HEADS, n, LANES)),
        "gc": np.broadcast_to(gc[:, None, None], (RET_HEADS, 1, LANES)),
    })


def _pe_tables(pos):
    cos, sin = _rope_angles(pos, QK_ROPE // 2)
    z = np.zeros_like(cos)
    return _f32({"cc": np.concatenate([cos, cos, z, z], axis=1),
                 "s1": np.concatenate([-sin, z, z, z], axis=1),
                 "s2": np.concatenate([z, sin, z, z], axis=1)})


def _pad_heads(w, nope, rope):
    k = w.shape[0]
    w3 = w.reshape(k, MLA_HEADS, nope + rope)
    pad = jnp.zeros((k, MLA_HEADS, 2 * LANES - nope - rope), w.dtype)
    return jnp.concatenate([w3, pad], axis=2).reshape(k, MLA_HEADS * 2 * LANES)


def _tiles(seq, n_pages, dff):
    def fit(t, n=seq):
        while n % t:
            t //= 2
        return t
    return {"tok": fit(1024), "mix": fit(512), "ret_chunk": fit(256), "ffn": fit(512), "attn": fit(512), "attn_k": fit(512), "ff_chunk": dff,
            "pages": fit(64, n_pages)}


def kernel(x_prompt, x_sample, state_pool, state_ret, cache_ckv, cache_kpe, state_conv, page_table,
           w_in_even, pool_w, pool_scale, ret_gn_g, w_o_even,
           w_dq, q_norm_g, w_uq, w_dkv, kv_norm_g, w_uk, w_uv, w_o_mla,
           w_up, conv_w, conv_b, w_down, ln_mix_g, ln_mix_b, ln_ffn_g, ln_ffn_b):
    bp, lp, d = x_prompt.shape
    bs, ls, _ = x_sample.shape
    past_len = page_table.shape[1] * PAGE_SIZE
    depth = w_up.shape[0]
    dff = w_down.shape[1]
    tl = _tiles(lp, page_table.shape[1], dff)
    ns = bs * ls
    assert lp % tl["ret_chunk"] == 0 and tl["mix"] % tl["ret_chunk"] == 0
    assert dff % tl["ff_chunk"] == 0 and page_table.shape[1] % tl["pages"] == 0 and ls <= PAGE_SIZE

    xp = x_prompt.reshape(bp * lp, d)
    xs = jnp.swapaxes(x_sample, 0, 1).reshape(ns, d)
    pos_p = np.arange(lp)
    pos_s_rows = np.repeat(past_len + np.arange(ls), bs)

    wup_all = w_up.astype(BF16)
    wd_all = w_down.astype(BF16)
    outs = {k: [] for k in ("pool_p", "pool_s", "ret_p", "ret_s", "ckv_p", "ckv_s", "kpe_p", "kpe_s",
                            "conv_p", "conv_s")}
    for layer in range(depth):
        mixer_out = None
        if layer % 2 == 0:
            e = layer // 2
            w_in = w_in_even[e].astype(BF16)
            w_o = w_o_even[e].astype(BF16)
            pw = pool_w[e].astype(BF16)
            tp = _ret_tables(pos_p, tl["ret_chunk"], 1)
            xp, pst, rst = _even_layer_prompt(xp, bp, lp, w_in, tp, pw, pool_scale[e], ret_gn_g[e], w_o,
                                              ln_mix_g[layer], ln_mix_b[layer], tl["mix"])
            outs["pool_p"].append(pst)
            outs["ret_p"].append(rst)
            ts = _ret_tables(pos_s_rows, ls, bs)
            rb = np.arange(ns) % bs
            ts["dm"] = np.where((rb[:, None] == rb[None, :])[None], ts["dec"], np.float32(0.0))
            hs =_matmul(xs, w_in, ns)
            hist = jnp.swapaxes(state_pool[e], 0, 1)
            mix_s, hist_new, s_new = _even_mix_sample(hs, hist, state_ret[e], ts, pw, pool_scale[e],
                                                      ret_gn_g[e], bs, ls, past_len)
            outs["pool_s"].append(jnp.swapaxes(hist_new, 0, 1))
            outs["ret_s"].append(s_new)
            xs = _matmul_res_ln(mix_s, w_o, xs, ln_mix_g[layer], ln_mix_b[layer], ns)
        else:
            o = layer // 2
            hw = 2 * LANES
            w = {
                "dq": w_dq[o].astype(BF16),
                "qg": q_norm_g[o].reshape(1, -1),
                "uq": _pad_heads(w_uq[o], QK_NOPE, QK_ROPE).astype(BF16),
                "dkv": jnp.pad(w_dkv[o], ((0, 0), (0, KV_LORA + LANES - w_dkv.shape[2]))).astype(BF16),
                "kvg": kv_norm_g[o].reshape(1, -1),
                "uk": w_uk[o].reshape(KV_LORA, MLA_HEADS * QK_NOPE).astype(BF16),
                "uv": w_uv[o].reshape(KV_LORA, MLA_HEADS * V_DIM).astype(BF16),
                "ukt": jnp.transpose(w_uk[o], (1, 2, 0)).astype(BF16),
            }
            w_o = w_o_mla[o].astype(BF16)
            qp, kp, vp, ckv_p, kpe_p = _mla_proj(xp, _pe_tables(pos_p), w, tl["tok"], lp // tl["tok"], False)
            att = _flash_prompt(qp, kp, vp, bp, lp, tl["attn"], tl["attn_k"])
            outs["ckv_p"].append(ckv_p.reshape(bp, lp, KV_LORA))
            outs["kpe_p"].append(kpe_p.reshape(bp, lp, QK_ROPE))
            mixer_out = (att, w_o, ln_mix_g[layer], ln_mix_b[layer])
            ql, qpe, ckv_s, kpe_s = _mla_proj(xs, _pe_tables(pos_s_rows), w, ns, 1, True)
            rows = ls * MLA_HEADS

            def per_batch(a, width):
                return jnp.transpose(a.reshape(ls, bs, MLA_HEADS, width), (1, 0, 2, 3)).reshape(bs, rows, width)

            ql_b = per_batch(ql, KV_LORA)
            qp_b = per_batch(qpe, LANES)[:, :, 0:QK_ROPE]
            ckv_sb = jnp.swapaxes(ckv_s.reshape(ls, bs, KV_LORA), 0, 1)
            kpe_sb = jnp.swapaxes(kpe_s.reshape(ls, bs, QK_ROPE), 0, 1)
            cn = jnp.pad(ckv_sb, ((0, 0), (0, PAGE_SIZE - ls), (0, 0)))
            knt = jnp.swapaxes(jnp.pad(kpe_sb, ((0, 0), (0, PAGE_SIZE - ls), (0, 0))), 1, 2)
            o_lat = _decode_attention(page_table, ql_b, qp_b, cn, knt, cache_ckv,
                                      jnp.swapaxes(cache_kpe, 2, 3), o, tl["pages"])
            outs["ckv_s"].append(ckv_sb)
            outs["kpe_s"].append(kpe_sb)
            ol = jnp.transpose(o_lat.reshape(bs, ls, MLA_HEADS, KV_LORA), (2, 1, 0, 3)).reshape(MLA_HEADS, ns, KV_LORA)
            wuv3 = jnp.transpose(w_uv[o], (1, 0, 2)).astype(BF16)
            xs = _decode_out(ol, wuv3, w_o, xs, ln_mix_g[layer], ln_mix_b[layer])
        xp, st_p = _ffn_prompt(xp, bp, lp, layer, wup_all, wd_all, conv_w[layer], conv_b[layer],
                               ln_ffn_g[layer], ln_ffn_b[layer], tl["ffn"], tl["ff_chunk"], mixer_out)
        outs["conv_p"].append(st_p[:, SUBLANES - (CONV_W - 1):, :])
        st_s = jnp.swapaxes(state_conv[layer], 0, 1)
        xs, st_s_new = _ffn_sample(xs, st_s, layer, wup_all, wd_all, conv_w[layer], conv_b[layer],
                                   ln_ffn_g[layer], ln_ffn_b[layer], bs, ls, tl["ff_chunk"])
        outs["conv_s"].append(jnp.swapaxes(st_s_new, 0, 1))

    y_p = xp.reshape(bp, lp, d)
    y_s = jnp.swapaxes(xs.reshape(ls, bs, d), 0, 1)
    return (y_p, y_s,
            jnp.stack(outs["pool_p"]), jnp.stack(outs["pool_s"]),
            jnp.stack(outs["ret_p"]), jnp.stack(outs["ret_s"]),
            jnp.stack(outs["ckv_p"]), jnp.stack(outs["ckv_s"]),
            jnp.stack(outs["kpe_p"]), jnp.stack(outs["kpe_s"]),
            jnp.stack(outs["conv_p"]), jnp.stack(outs["conv_s"]))
```

```python
import functools

import jax
import jax.numpy as jnp
import numpy as np
from jax import lax
from jax.experimental import pallas as pl
from jax.experimental.pallas import tpu as pltpu

F32 = jnp.float32
BF16 = jnp.bfloat16

PAGE_SIZE = 128
POOL_WINDOWS = (2, 4, 8, 16)
POOL_HIST = max(POOL_WINDOWS) - 1
RET_HEADS = 4
RET_DK = 128
MLA_HEADS = 8
QK_NOPE = 128
QK_ROPE = 64
V_DIM = 128
KV_LORA = 256
CONV_W = 3
DEPTH = 2
ALPHA = (2.0 * DEPTH) ** 0.25
ROPE_THETA = 10000.0
LN_EPS = 1e-5
RMS_EPS = 1e-6
GN_EPS = 1e-6
MLA_SCALE = (QK_NOPE + QK_ROPE) ** -0.5
LOG2E = 1.4426950408889634
SCALE_LOG2E = MLA_SCALE * LOG2E

LANES = 128
SUBLANES = 8
MIB = 1024 * 1024


_VMEM_LIMIT_MIB = {
    "matmul": 2, "matmul_res_ln": 2, "even_layer_prompt": 15, "even_mix_sample": 19, "ffn_prompt": 39,
    "ffn_sample": 14, "mla_proj": 38, "mla_proj_decode": 2, "flash_prompt": 40, "decode_attention": 36,
    "decode_out": 2,
}


def _cparams(sem, call):
    return pltpu.CompilerParams(dimension_semantics=sem, vmem_limit_bytes=_VMEM_LIMIT_MIB[call] * MIB)


def _resident(shape):
    nd = len(shape)
    return pl.BlockSpec(shape, lambda *_: (0,) * nd, pipeline_mode=pl.Buffered(1))


def _resident_layer(stacked_shape, layer):
    nd = len(stacked_shape) - 1
    return pl.BlockSpec((None,) + tuple(stacked_shape[1:]), lambda *_: (layer,) + (0,) * nd,
                        pipeline_mode=pl.Buffered(1))


def _whole(shape):
    nd = len(shape)
    return pl.BlockSpec(shape, lambda *_: (0,) * nd)


def _dot(a, b):
    return jnp.dot(a, b, preferred_element_type=F32)


def _dot_nt(a, b):
    return lax.dot_general(a, b, (((1,), (1,)), ((), ())), preferred_element_type=F32)


def _dot_tn(a, b):
    return lax.dot_general(a, b, (((0,), (0,)), ((), ())), preferred_element_type=F32)


def _layer_norm(z, g, b):
    mu = jnp.mean(z, axis=-1, keepdims=True)
    d = z - mu
    var = jnp.mean(d * d, axis=-1, keepdims=True)
    return d * lax.rsqrt(var + LN_EPS) * g + b


def _silu(x):
    return x * jax.nn.sigmoid(x)


def _mm_body(x_ref, w_ref, o_ref):
    o_ref[...] = _dot(x_ref[...].astype(BF16), w_ref[...]).astype(o_ref.dtype)


def _matmul(x, w, tm, out_dtype=F32):
    m, k = x.shape
    n = w.shape[1]
    return pl.pallas_call(
        _mm_body,
        grid=(m // tm,),
        in_specs=[pl.BlockSpec((tm, k), lambda i: (i, 0)), _resident((k, n))],
        out_specs=pl.BlockSpec((tm, n), lambda i: (i, 0)),
        out_shape=jax.ShapeDtypeStruct((m, n), out_dtype),
        compiler_params=_cparams(("parallel",), "matmul"),
        name="matmul",
    )(x, w)


def _mm_ln_body(a_ref, w_ref, r_ref, g_ref, b_ref, o_ref):
    y = _dot(a_ref[...].astype(BF16), w_ref[...])
    o_ref[...] = _layer_norm(ALPHA * r_ref[...] + y, g_ref[...], b_ref[...])


def _matmul_res_ln(a, w, res, g, b, tm):
    m, k = a.shape
    n = w.shape[1]
    return pl.pallas_call(
        _mm_ln_body,
        grid=(m // tm,),
        in_specs=[pl.BlockSpec((tm, k), lambda i: (i, 0)), _resident((k, n)),
                  pl.BlockSpec((tm, n), lambda i: (i, 0)), _resident((1, n)), _resident((1, n))],
        out_specs=pl.BlockSpec((tm, n), lambda i: (i, 0)),
        out_shape=jax.ShapeDtypeStruct((m, n), F32),
        compiler_params=_cparams(("parallel",), "matmul_res_ln"),
        name="matmul_res_ln",
    )(a, w, res, g.reshape(1, n), b.reshape(1, n))


def _rope_full(x, c2, s2):
    return x * c2 + pltpu.roll(x, 64, axis=1) * s2


def _group_norm_gate(o, gate, gn_row):
    mu = jnp.mean(o, axis=-1, keepdims=True)
    d = o - mu
    var = jnp.mean(d * d, axis=-1, keepdims=True)
    return _silu(gate) * (d * lax.rsqrt(var + GN_EPS) * gn_row)


def _even_layer_prompt_body(x_ref, win_ref, c2_ref, s2_ref, dec_ref, qd_ref, kd_ref, gc_ref, pw_ref, ps_ref,
                            gn_ref, wo_ref, lg_ref, lb_ref, o_ref, pst_ref, rst_ref,
                            mix_ref, ext_ref, s_ref, *h_refs, tm, n_j, sub):
    j = pl.program_id(1)
    pd = len(POOL_WINDOWS) * LANES

    @pl.when(j == 0)
    def _():
        ext_ref[0:16, :] = jnp.zeros((16, pd), F32)
        s_ref[...] = jnp.zeros(s_ref.shape, F32)

    @pl.when(j > 0)
    def _():
        ext_ref[0:16, :] = ext_ref[tm:tm + 16, :]

    c = dec_ref.shape[1]
    k_scale = RET_DK ** -0.5
    nblk = 4 * LANES
    def project(r0):
        h_ref = h_refs[r0 // sub]
        xb = x_ref[r0:r0 + sub, :].astype(BF16)
        for c0 in range(0, h_ref.shape[1], nblk):
            h_ref[:, c0:c0 + nblk] = _dot(xb, win_ref[:, c0:c0 + nblk])

    project(0)
    for r0 in range(0, tm, sub):
        rb = slice(r0, r0 + sub)
        h_ref = h_refs[r0 // sub]
        if r0 + sub < tm:
            project(r0 + sub)
        ext_ref[16 + r0:16 + r0 + sub, :] = h_ref[:, 0:pd]

        pos = (j * tm + r0 + lax.broadcasted_iota(jnp.int32, (sub, 1), 0)).astype(F32)
        for g, w in enumerate(POOL_WINDOWS):
            cols = slice(g * LANES, (g + 1) * LANES)
            e = ext_ref[r0:r0 + sub + 16, cols]
            u = e[16:, :]
            s = 1
            while s < w:
                e = e + pltpu.roll(e, s, axis=0)
                s *= 2
            cnt = jnp.minimum(float(w), pos + 1.0)
            pooled = e[16:, :] / cnt - u
            mixed = _dot(pooled.astype(BF16), pw_ref[g]) * ps_ref[:, cols]
            mix_ref[rb, cols] = mixed.astype(BF16)

        for ci in range(r0 // c, (r0 + sub) // c):
            rows = slice(ci * c, (ci + 1) * c)
            lrows = slice(ci * c - r0, (ci + 1) * c - r0)
            c2 = c2_ref[rows, :]
            s2 = s2_ref[rows, :]
            for hd in range(RET_HEADS):
                def col(part, hd=hd):
                    return slice(pd + (part * RET_HEADS + hd) * LANES, pd + (part * RET_HEADS + hd + 1) * LANES)
                q = _rope_full(h_ref[lrows, col(0)], c2, s2)
                k = _rope_full(h_ref[lrows, col(1)], c2, s2) * k_scale
                vb = h_ref[lrows, col(2)].astype(BF16)
                gate = h_ref[lrows, col(3)]
                st = s_ref[hd]
                sc = _dot_nt(q.astype(BF16), k.astype(BF16)) * dec_ref[hd]
                o = _dot(sc.astype(BF16), vb)
                o = o + _dot((q * qd_ref[hd]).astype(BF16), st.astype(BF16))
                s_ref[hd] = gc_ref[hd] * st + _dot_tn((k * kd_ref[hd]).astype(BF16), vb)
                ret = _group_norm_gate(o, gate, gn_ref[:, hd * LANES:(hd + 1) * LANES])
                mix_ref[rows, pd + hd * LANES:pd + (hd + 1) * LANES] = ret.astype(BF16)

        y = _dot(mix_ref[rb, :], wo_ref[...])
        o_ref[rb, :] = _layer_norm(ALPHA * x_ref[rb, :] + y, lg_ref[...], lb_ref[...])

    @pl.when(j == n_j - 1)
    def _():
        pst_ref[0] = ext_ref[pl.ds(tm + 1, POOL_HIST), :]
        rst_ref[0] = s_ref[...]


def _even_layer_prompt(x, bsz, seq, w_in, tabs, pool_w, pool_scale, gn_g, w_o, ln_g, ln_b, tm):
    n_j = seq // tm
    d = x.shape[1]
    pd = pool_scale.shape[0]
    ed = pd + RET_HEADS * LANES
    chunk = tabs["dec"].shape[1]
    sub = min(tm, max(chunk, 2 * LANES))
    body = functools.partial(_even_layer_prompt_body, tm=tm, n_j=n_j, sub=sub)
    return pl.pallas_call(
        body,
        grid=(bsz, n_j),
        in_specs=[
            pl.BlockSpec((tm, d), lambda b, j: (b * n_j + j, 0)),
            _resident(w_in.shape),
            pl.BlockSpec((tm, LANES), lambda b, j: (j, 0)),
            pl.BlockSpec((tm, LANES), lambda b, j: (j, 0)),
            _resident((RET_HEADS, chunk, chunk)),
            _resident((RET_HEADS, chunk, LANES)),
            _resident((RET_HEADS, chunk, LANES)),
            _resident((RET_HEADS, 1, LANES)),
            _resident(pool_w.shape),
            _resident((1, pd)),
            _resident((1, RET_HEADS * LANES)),
            _resident(w_o.shape),
            _resident((1, d)),
            _resident((1, d)),
        ],
        out_specs=[
            pl.BlockSpec((tm, d), lambda b, j: (b * n_j + j, 0)),
            pl.BlockSpec((1, POOL_HIST, pd), lambda b, j: (b, 0, 0)),
            pl.BlockSpec((1, RET_HEADS, RET_DK, LANES), lambda b, j: (b, 0, 0, 0)),
        ],
        out_shape=[
            jax.ShapeDtypeStruct((bsz * seq, d), F32),
            jax.ShapeDtypeStruct((bsz, POOL_HIST, pd), F32),
            jax.ShapeDtypeStruct((bsz, RET_HEADS, RET_DK, LANES), F32),
        ],
        scratch_shapes=[pltpu.VMEM((tm, ed), BF16), pltpu.VMEM((tm + 16, pd), F32),
                        pltpu.VMEM((RET_HEADS, RET_DK, LANES), F32)]
        + [pltpu.VMEM((sub, w_in.shape[1]), F32)] * (tm // sub),
        compiler_params=_cparams(("parallel", "arbitrary"), "even_layer_prompt"),
        name="even_layer_prompt",
    )(x, w_in, tabs["c2"], tabs["s2"], tabs["dec"], tabs["qd"], tabs["kd"], tabs["gc"],
      pool_w, pool_scale.reshape(1, pd), gn_g.reshape(1, -1), w_o, ln_g.reshape(1, d), ln_b.reshape(1, d))


def _even_mix_sample_body(h_ref, hist_ref, s0_ref, c2_ref, s2_ref, dm_ref, qd_ref, kd_ref, gc_ref,
                          pw_ref, ps_ref, gn_ref, mix_ref, hist_o_ref, s_o_ref, oc_ref,
                          *, nb, ls, cnts):
    pd = len(POOL_WINDOWS) * LANES
    ext = [hist_ref[i] for i in range(POOL_HIST)]
    ext += [h_ref[l * nb:(l + 1) * nb, 0:pd] for l in range(ls)]
    for i in range(POOL_HIST):
        hist_o_ref[i] = ext[ls + i]
    for g, w in enumerate(POOL_WINDOWS):
        cols = slice(g * LANES, (g + 1) * LANES)
        outs = []
        for l in range(ls):
            top = POOL_HIST + l
            acc = ext[top][:, cols]
            for jj in range(1, w):
                acc = acc + ext[top - jj][:, cols]
            outs.append(acc / cnts[g][l] - ext[top][:, cols])
        pooled = jnp.concatenate(outs, axis=0)
        mixed = _dot(pooled.astype(BF16), pw_ref[g]) * ps_ref[:, cols]
        mix_ref[:, cols] = mixed.astype(BF16)

    rows_b = lax.broadcasted_iota(jnp.int32, (ls * nb, 1), 0) % nb
    k_scale = RET_DK ** -0.5
    c2 = c2_ref[...]
    s2 = s2_ref[...]
    qs, ks, vs = [], [], []
    for hd in range(RET_HEADS):
        def col(part, hd=hd):
            return slice(pd + (part * RET_HEADS + hd) * LANES, pd + (part * RET_HEADS + hd + 1) * LANES)
        q = _rope_full(h_ref[:, col(0)], c2, s2)
        k = _rope_full(h_ref[:, col(1)], c2, s2) * k_scale
        vb = h_ref[:, col(2)].astype(BF16)
        sc = _dot_nt(q.astype(BF16), k.astype(BF16)) * dm_ref[hd]
        oc_ref[hd] = _dot(sc.astype(BF16), vb)
        qs.append(q * qd_ref[hd])
        ks.append((k * kd_ref[hd]).T)
        vs.append(vb)

    cols_b = lax.broadcasted_iota(jnp.int32, (1, ls * nb), 1) % nb

    def per_batch(b, carry):
        sel = rows_b == b
        sel_t = cols_b == b
        for hd in range(RET_HEADS):
            st = s0_ref[b, hd]
            qm = jnp.where(sel, qs[hd], 0.0).astype(BF16)
            km_t = jnp.where(sel_t, ks[hd], 0.0).astype(BF16)
            oc_ref[hd] += _dot(qm, st.astype(BF16))
            s_o_ref[b, hd] = gc_ref[hd] * st + _dot(km_t, vs[hd])
        return carry

    lax.fori_loop(0, nb, per_batch, 0)

    for hd in range(RET_HEADS):
        gate = h_ref[:, pd + (3 * RET_HEADS + hd) * LANES:pd + (3 * RET_HEADS + hd + 1) * LANES]
        ret = _group_norm_gate(oc_ref[hd], gate, gn_ref[:, hd * LANES:(hd + 1) * LANES])
        mix_ref[:, pd + hd * LANES:pd + (hd + 1) * LANES] = ret.astype(BF16)


def _even_mix_sample(h, hist, s0, tabs, pool_w, pool_scale, gn_g, nb, ls, past_len):
    pd = pool_scale.shape[0]
    ed = pd + RET_HEADS * LANES
    cnts = tuple(tuple(float(min(w, past_len + l + 1)) for l in range(ls)) for w in POOL_WINDOWS)
    body = functools.partial(_even_mix_sample_body, nb=nb, ls=ls, cnts=cnts)
    n = ls * nb
    args = (h, hist, s0, tabs["c2"], tabs["s2"], tabs["dm"], tabs["qd"], tabs["kd"], tabs["gc"],
            pool_w, pool_scale.reshape(1, pd), gn_g.reshape(1, -1))
    return pl.pallas_call(
        body,
        grid=(1,),
        in_specs=[_resident(a.shape) for a in args],
        out_specs=[_whole((n, ed)), _whole(hist.shape), _whole(s0.shape)],
        out_shape=[jax.ShapeDtypeStruct((n, ed), BF16),
                   jax.ShapeDtypeStruct(hist.shape, F32),
                   jax.ShapeDtypeStruct(s0.shape, F32)],
        scratch_shapes=[pltpu.VMEM((RET_HEADS, n, LANES), F32)],
        compiler_params=_cparams(("arbitrary",), "even_mix_sample"),
        name="even_mix_sample",
    )(*args)


def _ffn_chunk(xb, wup_ref, wd_ref, cw_ref, cb_ref, c, tf, dff, shift_fn):
    cols = slice(c * tf, (c + 1) * tf)
    a = _dot(xb, wup_ref[:, cols])
    gate_in = _dot(xb, wup_ref[:, dff + c * tf:dff + (c + 1) * tf])
    a1, a2 = shift_fn(a, c)
    conv = cb_ref[:, cols] + cw_ref[0:1, cols] * a2
    conv = conv + cw_ref[1:2, cols] * a1
    conv = conv + cw_ref[2:3, cols] * a
    act = (_silu(conv) * gate_in).astype(BF16)
    return a, _dot(act, wd_ref[cols, :])


def _ffn_prompt_body(x_ref, *rest, tm, n_j, tf, dff, mixer_out):
    if mixer_out:
        a_ref, wo_ref, mg_ref, mb_ref = rest[:4]
        rest = rest[4:]
    wup_ref, wd_ref, cw_ref, cb_ref, g_ref, b_ref, o_ref, st_ref, carry_ref = rest
    j = pl.program_id(1)

    @pl.when(j == 0)
    def _():
        carry_ref[...] = jnp.zeros(carry_ref.shape, F32)

    x = x_ref[...]
    if mixer_out:
        x = _layer_norm(ALPHA * x + _dot(a_ref[...], wo_ref[...]), mg_ref[...], mb_ref[...])
    row = lax.broadcasted_iota(jnp.int32, (tm, tf), 0)

    def shift_fn(a, c):
        prev = carry_ref[:, c * tf:(c + 1) * tf]
        a1 = jnp.where(row == 0, prev[7:8, :], pltpu.roll(a, 1, axis=0))
        a2 = jnp.where(row == 0, prev[6:7, :], jnp.where(row == 1, prev[7:8, :], pltpu.roll(a, 2, axis=0)))
        return a1, a2

    xb = x.astype(BF16)
    acc = None
    for c in range(dff // tf):
        a, y = _ffn_chunk(xb, wup_ref, wd_ref, cw_ref, cb_ref, c, tf, dff, shift_fn)
        acc = y if acc is None else acc + y
        tail = a[tm - SUBLANES:tm, :]
        carry_ref[:, c * tf:(c + 1) * tf] = tail
        st_ref[0, :, c * tf:(c + 1) * tf] = tail
    o_ref[...] = _layer_norm(ALPHA * x + acc, g_ref[...], b_ref[...])


def _ffn_prompt(x, bsz, seq, layer, wup, wd, conv_w, conv_b, g, b, tm, tf, mixer_out=None):
    d = x.shape[1]
    dff = wd.shape[1]
    n_j = seq // tm
    body = functools.partial(_ffn_prompt_body, tm=tm, n_j=n_j, tf=tf, dff=dff, mixer_out=mixer_out is not None)
    row_tile = lambda bi, j: (bi * n_j + j, 0)
    pre_args, pre_specs = [], []
    if mixer_out is not None:
        a, w_o, mg, mb = mixer_out
        pre_args = [a, w_o, mg.reshape(1, d), mb.reshape(1, d)]
        pre_specs = [pl.BlockSpec((tm, a.shape[1]), row_tile), _resident(w_o.shape), _resident((1, d)), _resident((1, d))]
    return pl.pallas_call(
        body,
        grid=(bsz, n_j),
        in_specs=[pl.BlockSpec((tm, d), row_tile)] + pre_specs + [
                  _resident_layer(wup.shape, layer), _resident_layer(wd.shape, layer), _resident(conv_w.shape),
                  _resident((1, dff)), _resident((1, d)), _resident((1, d))],
        out_specs=[pl.BlockSpec((tm, d), row_tile),
                   pl.BlockSpec((1, SUBLANES, dff), lambda bi, j: (bi, 0, 0))],
        out_shape=[jax.ShapeDtypeStruct(x.shape, F32),
                   jax.ShapeDtypeStruct((bsz, SUBLANES, dff), F32)],
        scratch_shapes=[pltpu.VMEM((SUBLANES, dff), F32)],
        compiler_params=_cparams(("parallel", "arbitrary"), "ffn_prompt"),
        name="ffn_prompt",
    )(x, *pre_args, wup, wd, conv_w, conv_b.reshape(1, dff), g.reshape(1, d), b.reshape(1, d))


def _ffn_sample_body(x_ref, st_ref, wup_ref, wd_ref, cw_ref, cb_ref, g_ref, b_ref, o_ref, st_o_ref,
                     *, nb, ls, tf, dff):
    x = x_ref[...]
    xb = x.astype(BF16)
    nh = CONV_W - 1

    def shift_fn(a, c):
        cols = slice(c * tf, (c + 1) * tf)
        ext = [st_ref[i, :, cols] for i in range(nh)] + [a[l * nb:(l + 1) * nb, :] for l in range(ls)]
        a1 = jnp.concatenate([ext[nh + l - 1] for l in range(ls)], axis=0)
        a2 = jnp.concatenate([ext[nh + l - 2] for l in range(ls)], axis=0)
        for i in range(nh):
            st_o_ref[i, :, cols] = ext[ls + i]
        return a1, a2

    acc = None
    for c in range(dff // tf):
        _, y = _ffn_chunk(xb, wup_ref, wd_ref, cw_ref, cb_ref, c, tf, dff, shift_fn)
        acc = y if acc is None else acc + y
    o_ref[...] = _layer_norm(ALPHA * x + acc, g_ref[...], b_ref[...])


def _ffn_sample(x, st, layer, wup, wd, conv_w, conv_b, g, b, nb, ls, tf):
    d = x.shape[1]
    dff = wd.shape[1]
    body = functools.partial(_ffn_sample_body, nb=nb, ls=ls, tf=tf, dff=dff)
    args = (x, st, wup, wd, conv_w, conv_b.reshape(1, dff), g.reshape(1, d), b.reshape(1, d))
    in_specs = [_resident(a.shape) for a in args]
    in_specs[2] = _resident_layer(wup.shape, layer)
    in_specs[3] = _resident_layer(wd.shape, layer)
    return pl.pallas_call(
        body,
        grid=(1,),
        in_specs=in_specs,
        out_specs=[_whole(x.shape), _whole(st.shape)],
        out_shape=[jax.ShapeDtypeStruct(x.shape, F32), jax.ShapeDtypeStruct(st.shape, F32)],
        compiler_params=_cparams(("arbitrary",), "ffn_sample"),
        name="ffn_sample",
    )(*args)


def _rope_pe(blk, cc, s1, s2):
    return blk * cc + pltpu.roll(blk, 96, axis=1) * s1 + pltpu.roll(blk, 32, axis=1) * s2


def _rms_norm(x, g):
    ms = jnp.mean(x * x, axis=-1, keepdims=True)
    return x * lax.rsqrt(ms + RMS_EPS) * g


def _mla_proj_body(x_ref, cc_ref, s1_ref, s2_ref, wdq_ref, qg_ref, wuq_ref, wdkv_ref, kvg_ref, *rest,
                   decode, sub):
    hw = 2 * LANES
    for r0 in range(0, x_ref.shape[0], sub):
        rb = slice(r0, r0 + sub)
        cc = cc_ref[rb, :]
        s1 = s1_ref[rb, :]
        s2 = s2_ref[rb, :]
        xb = x_ref[rb, :].astype(BF16)
        cq = _rms_norm(_dot(xb, wdq_ref[...]), qg_ref[...])
        q = _dot(cq.astype(BF16), wuq_ref[...])
        kv = _dot(xb, wdkv_ref[...])
        ckv = _rms_norm(kv[:, 0:KV_LORA], kvg_ref[...])
        kpe = _rope_pe(kv[:, KV_LORA:KV_LORA + LANES], cc, s1, s2)
        if decode:
            wukt_ref, ql_ref, qp_ref, ckv_ref, kpe_ref = rest
            for h in range(MLA_HEADS):
                qn = q[:, h * hw:h * hw + LANES].astype(BF16)
                ql_ref[rb, h * KV_LORA:(h + 1) * KV_LORA] = _dot(qn, wukt_ref[h]).astype(BF16)
                qp = _rope_pe(q[:, h * hw + LANES:(h + 1) * hw], cc, s1, s2)
                qp_ref[rb, h * LANES:(h + 1) * LANES] = qp.astype(BF16)
        else:
            wuk_ref, wuv_ref, qo_ref, ko_ref, vo_ref, ckv_ref, kpe_ref = rest
            cb = ckv.astype(BF16)
            kn = _dot(cb, wuk_ref[...])
            vo_ref[rb, :] = _dot(cb, wuv_ref[...]).astype(BF16)
            kpb = kpe.astype(BF16)
            for h in range(MLA_HEADS):
                qo_ref[rb, h * hw:h * hw + LANES] = q[:, h * hw:h * hw + LANES].astype(BF16)
                qp = _rope_pe(q[:, h * hw + LANES:(h + 1) * hw], cc, s1, s2)
                qo_ref[rb, h * hw + LANES:(h + 1) * hw] = qp.astype(BF16)
                ko_ref[rb, h * hw:h * hw + LANES] = kn[:, h * LANES:(h + 1) * LANES].astype(BF16)
                ko_ref[rb, h * hw + LANES:(h + 1) * hw] = kpb
        ckv_ref[rb, :] = ckv
        kpe_ref[rb, :] = kpe[:, 0:QK_ROPE]


def _mla_proj(x, tabs, w, tm, n_pos_blocks, decode):
    t, d = x.shape
    hw = 2 * LANES
    body = functools.partial(_mla_proj_body, decode=decode, sub=tm)
    row = lambda i: (i, 0)
    tab = lambda i: (i % n_pos_blocks, 0)
    ins = [x, tabs["cc"], tabs["s1"], tabs["s2"], w["dq"], w["qg"], w["uq"], w["dkv"], w["kvg"]]
    in_specs = [pl.BlockSpec((tm, d), row)] + [pl.BlockSpec((tm, LANES), tab)] * 3
    in_specs += [_resident(a.shape) for a in ins[4:]]
    if decode:
        ins += [w["ukt"]]
        in_specs += [_resident(w["ukt"].shape)]
        outs = [(MLA_HEADS * KV_LORA, BF16), (MLA_HEADS * LANES, BF16)]
    else:
        ins += [w["uk"], w["uv"]]
        in_specs += [_resident(w["uk"].shape), _resident(w["uv"].shape)]
        outs = [(MLA_HEADS * hw, BF16), (MLA_HEADS * hw, BF16), (MLA_HEADS * V_DIM, BF16)]
    outs += [(KV_LORA, F32), (QK_ROPE, F32)]
    return pl.pallas_call(
        body,
        grid=(t // tm,),
        in_specs=in_specs,
        out_specs=[pl.BlockSpec((tm, n), row) for n, _ in outs],
        out_shape=[jax.ShapeDtypeStruct((t, n), dt) for n, dt in outs],
        compiler_params=_cparams(("parallel",), "mla_proj_decode" if decode else "mla_proj"),
        name="mla_proj_decode" if decode else "mla_proj",
    )(*ins)


def _online_softmax_update(s2, m_ref, l_ref, acc_ref, pv_fn, row_chunk):
    rows, width = s2.shape
    n = width // LANES
    aw = acc_ref.shape[-1] // LANES
    p_chunks, alphas = [], []
    for r0 in range(0, rows, row_chunk):
        rs_ = slice(r0, r0 + row_chunk)
        tiles = [s2[rs_, j * LANES:(j + 1) * LANES] for j in range(n)]
        mx = tiles[0]
        for t in tiles[1:]:
            mx = jnp.maximum(mx, t)
        m_prev = m_ref[rs_, :]
        m_new = jnp.maximum(m_prev, jnp.max(mx, axis=-1, keepdims=True))
        alpha = jnp.exp2(m_prev - m_new)
        ps = [jnp.exp2(t - m_new) for t in tiles]
        if l_ref is not None:
            tot = ps[0]
            for t in ps[1:]:
                tot = tot + t
            l_ref[rs_, :] = alpha * l_ref[rs_, :] + jnp.sum(tot, axis=-1, keepdims=True)
        m_ref[rs_, :] = m_new
        p_chunks.append((jnp.concatenate(ps, axis=1) if n > 1 else ps[0]).astype(BF16))
        alphas.append(alpha)
    p = jnp.concatenate(p_chunks, axis=0) if len(p_chunks) > 1 else p_chunks[0]
    pv = pv_fn(p)
    for i, r0 in enumerate(range(0, rows, row_chunk)):
        rs_ = slice(r0, r0 + row_chunk)
        a_w = alphas[i] if aw == 1 else jnp.concatenate([alphas[i]] * aw, axis=1)
        acc_ref[rs_, :] = a_w * acc_ref[rs_, :] + pv[rs_, :]


def _flash_body(q_ref, k_ref, v_ref, o_ref, m_ref, acc_ref, *, tq, tk, row_chunk):
    qi = pl.program_id(1)
    hw = 2 * LANES
    m_ref[...] = jnp.full(m_ref.shape, -jnp.inf, F32)
    acc_ref[...] = jnp.zeros(acc_ref.shape, F32)
    ones = jnp.ones((tk, LANES), BF16)

    def step(h, key0, diag_off):
        start = pl.multiple_of(key0, tk)
        q = q_ref[:, h * hw:(h + 1) * hw]
        k = k_ref[pl.ds(start, tk), h * hw:(h + 1) * hw]
        v1 = jnp.concatenate([v_ref[pl.ds(start, tk), h * V_DIM:(h + 1) * V_DIM], ones], axis=1)
        s = _dot_nt(q, k) * SCALE_LOG2E
        if diag_off is not None:
            row = lax.broadcasted_iota(jnp.int32, s.shape, 0)
            colm = lax.broadcasted_iota(jnp.int32, s.shape, 1)
            s = jnp.where(colm + diag_off <= row, s, -jnp.inf)
        _online_softmax_update(s, m_ref.at[h], None, acc_ref.at[h], lambda p: _dot(p, v1), row_chunk)

    def loop_body(kj, carry):
        for h in range(MLA_HEADS):
            step(h, kj * tk, None)
        return carry

    lax.fori_loop(0, qi * (tq // tk), loop_body, 0)
    for h in range(MLA_HEADS):
        for j in range(tq // tk):
            step(h, qi * tq + j * tk, j * tk)
        o_ref[:, h * V_DIM:(h + 1) * V_DIM] = (acc_ref[h, :, 0:V_DIM] / acc_ref[h, :, V_DIM:]).astype(BF16)


def _flash_prompt(qp, kp, vp, bsz, seq, tq, tk):
    nq = seq // tq
    body = functools.partial(_flash_body, tq=tq, tk=tk, row_chunk=min(tq, 64))
    return pl.pallas_call(
        body,
        grid=(bsz, nq),
        in_specs=[pl.BlockSpec((tq, qp.shape[1]), lambda b, i: (b * nq + i, 0)),
                  pl.BlockSpec((seq, kp.shape[1]), lambda b, i: (b, 0)),
                  pl.BlockSpec((seq, vp.shape[1]), lambda b, i: (b, 0))],
        out_specs=pl.BlockSpec((tq, vp.shape[1]), lambda b, i: (b * nq + i, 0)),
        out_shape=jax.ShapeDtypeStruct(vp.shape, BF16),
        scratch_shapes=[pltpu.VMEM((MLA_HEADS, tq, LANES), F32), pltpu.VMEM((MLA_HEADS, tq, V_DIM + LANES), F32)],
        compiler_params=_cparams(("parallel", "arbitrary"), "flash_prompt"),
        name="flash_prompt",
    )(qp, kp, vp)


def _decode_body(pt_ref, ql_ref, qp_ref, cn_ref, kn_ref, ckv_hbm, kpe_hbm, o_ref,
                 cbuf_ref, rbuf_ref, sem_ref, m_ref, l_ref, acc_ref, *, gp, ls, layer, n_split, n_b, n_g, n_buf):
    b = pl.program_id(0)
    g = pl.program_id(1)
    t = b * n_g + g
    slot = lax.rem(t, n_buf)
    ahead = n_buf - 1

    def page_copies(bb, grp, slot):
        cps = []
        for i in range(gp):
            pg = pt_ref[bb, grp * gp + i]
            keys = pl.ds(i * PAGE_SIZE, PAGE_SIZE)
            cps.append(pltpu.make_async_copy(ckv_hbm.at[layer, pg], cbuf_ref.at[slot, keys, :], sem_ref.at[slot, 0]))
            cps.append(pltpu.make_async_copy(kpe_hbm.at[layer, pg], rbuf_ref.at[slot, :, keys], sem_ref.at[slot, 1]))
        return cps

    def start(step):
        for cp in page_copies(lax.div(step, n_g), lax.rem(step, n_g), lax.rem(step, n_buf)):
            cp.start()

    @pl.when(t == 0)
    def _():
        for s in range(min(ahead, n_b * n_g)):
            start(jnp.int32(s))

    @pl.when(t + ahead < n_b * n_g)
    def _():
        start(t + ahead)

    @pl.when(g == 0)
    def _():
        m_ref[...] = jnp.full(m_ref.shape, -jnp.inf, F32)
        l_ref[...] = jnp.zeros(l_ref.shape, F32)
        acc_ref[...] = jnp.zeros(acc_ref.shape, F32)

    ql = ql_ref[0]
    qp = qp_ref[0]

    for cp in page_copies(b, g, slot):
        cp.wait()

    kc_len = gp * PAGE_SIZE // n_split
    kbs, scores = [], []
    for c in range(n_split):
        ks = slice(c * kc_len, (c + 1) * kc_len)
        kb = cbuf_ref[slot, ks, :].astype(BF16)
        scores.append((_dot_nt(ql, kb) + _dot(qp, rbuf_ref[slot, :, ks].astype(BF16))) * SCALE_LOG2E)
        kbs.append(kb)
    state = (m_ref.at[0], l_ref.at[0], acc_ref.at[0])
    n_rows = ql.shape[0]
    for c in range(n_split):
        _online_softmax_update(scores[c], *state, lambda p, kb=kbs[c]: _dot(p, kb), n_rows)

    @pl.when(g == n_g - 1)
    def _():
        kc = cn_ref[0].astype(BF16)
        s = (_dot_nt(ql, kc) + _dot(qp, kn_ref[0].astype(BF16))) * SCALE_LOG2E
        r = lax.broadcasted_iota(jnp.int32, s.shape, 0) // MLA_HEADS
        cidx = lax.broadcasted_iota(jnp.int32, s.shape, 1)
        s = jnp.where((cidx <= r) & (cidx < ls), s, -jnp.inf)
        _online_softmax_update(s, *state, lambda p: _dot(p, kc), n_rows)
        l_w = jnp.concatenate([l_ref[0]] * (KV_LORA // LANES), axis=1)
        o_ref[0] = acc_ref[0] / l_w


def _decode_attention(page_table, ql, qp, cn, knt, cache_ckv, cache_kpet, layer, gp):
    nb, rows, _ = ql.shape
    n_pages = page_table.shape[1]
    ls = rows // MLA_HEADS
    n_split = 4 if gp % 4 == 0 else 1
    n_g = n_pages // gp
    n_buf = 3
    body = functools.partial(_decode_body, gp=gp, ls=ls, layer=layer, n_split=n_split, n_b=nb, n_g=n_g, n_buf=n_buf)
    per_b = lambda b, g, pt: (b, 0, 0)
    in_specs = [pl.BlockSpec((1, rows, KV_LORA), per_b), pl.BlockSpec((1, rows, QK_ROPE), per_b),
                pl.BlockSpec((1, PAGE_SIZE, KV_LORA), per_b), pl.BlockSpec((1, QK_ROPE, PAGE_SIZE), per_b),
                pl.BlockSpec(memory_space=pl.ANY), pl.BlockSpec(memory_space=pl.ANY)]
    grid_spec = pltpu.PrefetchScalarGridSpec(
        num_scalar_prefetch=1,
        grid=(nb, n_g),
        in_specs=in_specs,
        out_specs=pl.BlockSpec((1, rows, KV_LORA), per_b),
        scratch_shapes=[pltpu.VMEM((n_buf, gp * PAGE_SIZE, KV_LORA), F32),
                        pltpu.VMEM((n_buf, QK_ROPE, gp * PAGE_SIZE), F32),
                        pltpu.SemaphoreType.DMA((n_buf, 2)),
                        pltpu.VMEM((1, rows, LANES), F32), pltpu.VMEM((1, rows, LANES), F32),
                        pltpu.VMEM((1, rows, KV_LORA), F32)],
    )
    return pl.pallas_call(
        body,
        grid_spec=grid_spec,
        out_shape=jax.ShapeDtypeStruct((nb, rows, KV_LORA), F32),
        compiler_params=_cparams(("arbitrary", "arbitrary"), "decode_attention"),
        name="decode_attention",
    )(page_table, ql, qp, cn, knt, cache_ckv, cache_kpet)


def _decode_out_body(ol_ref, wuv_ref, wo_ref, r_ref, g_ref, b_ref, o_ref):
    y = None
    for h in range(MLA_HEADS):
        oh = _dot(ol_ref[h].astype(BF16), wuv_ref[h]).astype(BF16)
        t = _dot(oh, wo_ref[h * V_DIM:(h + 1) * V_DIM, :])
        y = t if y is None else y + t
    o_ref[...] = _layer_norm(ALPHA * r_ref[...] + y, g_ref[...], b_ref[...])


def _decode_out(ol, wuv3, wo, res, g, b):
    d = res.shape[1]
    args = (ol, wuv3, wo, res, g.reshape(1, d), b.reshape(1, d))
    return pl.pallas_call(
        _decode_out_body,
        grid=(1,),
        in_specs=[_resident(a.shape) for a in args],
        out_specs=_whole(res.shape),
        out_shape=jax.ShapeDtypeStruct(res.shape, F32),
        compiler_params=_cparams(("arbitrary",), "decode_out"),
        name="decode_out",
    )(*args)


def _rope_angles(pos, half):
    inv_freq = ROPE_THETA ** (-np.arange(half, dtype=np.float64) / half)
    ang = np.asarray(pos, np.float64)[:, None] * inv_freq[None, :]
    return np.cos(ang), np.sin(ang)


def _f32(tabs):
    return {k: np.ascontiguousarray(v, dtype=np.float32) for k, v in tabs.items()}


def _ret_tables(pos, chunk, rows_per_pos):
    cos, sin = _rope_angles(pos, RET_DK // 2)
    log_gamma = np.log(1.0 - 2.0 ** (-5.0 - np.arange(RET_HEADS, dtype=np.float64)))
    idx = np.repeat(np.arange(chunk, dtype=np.float64), rows_per_pos)
    diff = idx[:, None] - idx[None, :]
    dec = np.where(diff >= 0, np.exp(np.maximum(diff, 0.0)[None] * log_gamma[:, None, None]), 0.0)
    q_dec = np.exp((idx + 1.0)[None, :] * log_gamma[:, None])
    k_dec = np.exp((chunk - 1.0 - idx)[None, :] * log_gamma[:, None])
    gc = np.exp(chunk * log_gamma)
    n = idx.shape[0]
    return _f32({
        "c2": np.concatenate([cos, cos], axis=1),
        "s2": np.concatenate([-sin, sin], axis=1),
        "dec": dec,
        "qd": np.broadcast_to(q_dec[:, :, None], (RET_HEADS, n, LANES)),
        "kd": np.broadcast_to(k_dec[:, :, None], (RET_HEADS, n, LANES)),
        "gc": np.broadcast_to(gc[:, None, None], (RET_HEADS, 1, LANES)),
    })


def _pe_tables(pos):
    cos, sin = _rope_angles(pos, QK_ROPE // 2)
    z = np.zeros_like(cos)
    return _f32({"cc": np.concatenate([cos, cos, z, z], axis=1),
                 "s1": np.concatenate([-sin, z, z, z], axis=1),
                 "s2": np.concatenate([z, sin, z, z], axis=1)})


def _pad_heads(w, nope, rope):
    k = w.shape[0]
    w3 = w.reshape(k, MLA_HEADS, nope + rope)
    pad = jnp.zeros((k, MLA_HEADS, 2 * LANES - nope - rope), w.dtype)
    return jnp.concatenate([w3, pad], axis=2).reshape(k, MLA_HEADS * 2 * LANES)


def _tiles(seq, n_pages, dff):
    def fit(t, n=seq):
        while n % t:
            t //= 2
        return t
    return {"tok": fit(1024), "mix": fit(512), "ret_chunk": fit(256), "ffn": fit(512), "attn": fit(512), "attn_k": fit(512), "ff_chunk": dff,
            "pages": fit(64, n_pages)}


def kernel(x_prompt, x_sample, state_pool, state_ret, cache_ckv, cache_kpe, state_conv, page_table,
           w_in_even, pool_w, pool_scale, ret_gn_g, w_o_even,
           w_dq, q_norm_g, w_uq, w_dkv, kv_norm_g, w_uk, w_uv, w_o_mla,
           w_up, conv_w, conv_b, w_down, ln_mix_g, ln_mix_b, ln_ffn_g, ln_ffn_b):
    bp, lp, d = x_prompt.shape
    bs, ls, _ = x_sample.shape
    past_len = page_table.shape[1] * PAGE_SIZE
    depth = w_up.shape[0]
    dff = w_down.shape[1]
    tl = _tiles(lp, page_table.shape[1], dff)
    ns = bs * ls
    assert lp % tl["ret_chunk"] == 0 and tl["mix"] % tl["ret_chunk"] == 0
    assert dff % tl["ff_chunk"] == 0 and page_table.shape[1] % tl["pages"] == 0 and ls <= PAGE_SIZE

    xp = x_prompt.reshape(bp * lp, d)
    xs = jnp.swapaxes(x_sample, 0, 1).reshape(ns, d)
    pos_p = np.arange(lp)
    pos_s_rows = np.repeat(past_len + np.arange(ls), bs)

    wup_all = w_up.astype(BF16)
    wd_all = w_down.astype(BF16)
    outs = {k: [] for k in ("pool_p", "pool_s", "ret_p", "ret_s", "ckv_p", "ckv_s", "kpe_p", "kpe_s",
                            "conv_p", "conv_s")}
    for layer in range(depth):
        mixer_out = None
        if layer % 2 == 0:
            e = layer // 2
            w_in = w_in_even[e].astype(BF16)
            w_o = w_o_even[e].astype(BF16)
            pw = pool_w[e].astype(BF16)
            tp = _ret_tables(pos_p, tl["ret_chunk"], 1)
            xp, pst, rst = _even_layer_prompt(xp, bp, lp, w_in, tp, pw, pool_scale[e], ret_gn_g[e], w_o,
                                              ln_mix_g[layer], ln_mix_b[layer], tl["mix"])
            outs["pool_p"].append(pst)
            outs["ret_p"].append(rst)
            ts = _ret_tables(pos_s_rows, ls, bs)
            rb = np.arange(ns) % bs
            ts["dm"] = np.where((rb[:, None] == rb[None, :])[None], ts["dec"], np.float32(0.0))
            hs =_matmul(xs, w_in, ns)
            hist = jnp.swapaxes(state_pool[e], 0, 1)
            mix_s, hist_new, s_new = _even_mix_sample(hs, hist, state_ret[e], ts, pw, pool_scale[e],
                                                      ret_gn_g[e], bs, ls, past_len)
            outs["pool_s"].append(jnp.swapaxes(hist_new, 0, 1))
            outs["ret_s"].append(s_new)
            xs = _matmul_res_ln(mix_s, w_o, xs, ln_mix_g[layer], ln_mix_b[layer], ns)
        else:
            o = layer // 2
            hw = 2 * LANES
            w = {
                "dq": w_dq[o].astype(BF16),
                "qg": q_norm_g[o].reshape(1, -1),
                "uq": _pad_heads(w_uq[o], QK_NOPE, QK_ROPE).astype(BF16),
                "dkv": jnp.pad(w_dkv[o], ((0, 0), (0, KV_LORA + LANES - w_dkv.shape[2]))).astype(BF16),
                "kvg": kv_norm_g[o].reshape(1, -1),
                "uk": w_uk[o].reshape(KV_LORA, MLA_HEADS * QK_NOPE).astype(BF16),
                "uv": w_uv[o].reshape(KV_LORA, MLA_HEADS * V_DIM).astype(BF16),
                "ukt": jnp.transpose(w_uk[o], (1, 2, 0)).astype(BF16),
            }
            w_o = w_o_mla[o].astype(BF16)
            qp, kp, vp, ckv_p, kpe_p = _mla_proj(xp, _pe_tables(pos_p), w, tl["tok"], lp // tl["tok"], False)
            att = _flash_prompt(qp, kp, vp, bp, lp, tl["attn"], tl["attn_k"])
            outs["ckv_p"].append(ckv_p.reshape(bp, lp, KV_LORA))
            outs["kpe_p"].append(kpe_p.reshape(bp, lp, QK_ROPE))
            mixer_out = (att, w_o, ln_mix_g[layer], ln_mix_b[layer])
            ql, qpe, ckv_s, kpe_s = _mla_proj(xs, _pe_tables(pos_s_rows), w, ns, 1, True)
            rows = ls * MLA_HEADS

            def per_batch(a, width):
                return jnp.transpose(a.reshape(ls, bs, MLA_HEADS, width), (1, 0, 2, 3)).reshape(bs, rows, width)

            ql_b = per_batch(ql, KV_LORA)
            qp_b = per_batch(qpe, LANES)[:, :, 0:QK_ROPE]
            ckv_sb = jnp.swapaxes(ckv_s.reshape(ls, bs, KV_LORA), 0, 1)
            kpe_sb = jnp.swapaxes(kpe_s.reshape(ls, bs, QK_ROPE), 0, 1)
            cn = jnp.pad(ckv_sb, ((0, 0), (0, PAGE_SIZE - ls), (0, 0)))
            knt = jnp.swapaxes(jnp.pad(kpe_sb, ((0, 0), (0, PAGE_SIZE - ls), (0, 0))), 1, 2)
            o_lat = _decode_attention(page_table, ql_b, qp_b, cn, knt, cache_ckv,
                                      jnp.swapaxes(cache_kpe, 2, 3), o, tl["pages"])
            outs["ckv_s"].append(ckv_sb)
            outs["kpe_s"].append(kpe_sb)
            ol = jnp.transpose(o_lat.reshape(bs, ls, MLA_HEADS, KV_LORA), (2, 1, 0, 3)).reshape(MLA_HEADS, ns, KV_LORA)
            wuv3 = jnp.transpose(w_uv[o], (1, 0, 2)).astype(BF16)
            xs = _decode_out(ol, wuv3, w_o, xs, ln_mix_g[layer], ln_mix_b[layer])
        xp, st_p = _ffn_prompt(xp, bp, lp, layer, wup_all, wd_all, conv_w[layer], conv_b[layer],
                               ln_ffn_g[layer], ln_ffn_b[layer], tl["ffn"], tl["ff_chunk"], mixer_out)
        outs["conv_p"].append(st_p[:, SUBLANES - (CONV_W - 1):, :])
        st_s = jnp.swapaxes(state_conv[layer], 0, 1)
        xs, st_s_new = _ffn_sample(xs, st_s, layer, wup_all, wd_all, conv_w[layer], conv_b[layer],
                                   ln_ffn_g[layer], ln_ffn_b[layer], bs, ls, tl["ff_chunk"])
        outs["conv_s"].append(jnp.swapaxes(st_s_new, 0, 1))

    y_p = xp.reshape(bp, lp, d)
    y_s = jnp.swapaxes(xs.reshape(ls, bs, d), 0, 1)
    return (y_p, y_s,
            jnp.stack(outs["pool_p"]), jnp.stack(outs["pool_s"]),
            jnp.stack(outs["ret_p"]), jnp.stack(outs["ret_s"]),
            jnp.stack(outs["ckv_p"]), jnp.stack(outs["ckv_s"]),
            jnp.stack(outs["kpe_p"]), jnp.stack(outs["kpe_s"]),
            jnp.stack(outs["conv_p"]), jnp.stack(outs["conv_s"]))
```
